```python
import jax, jax.numpy as jnp
from jax import lax
import numpy as np

D_MODEL = 1024
BATCH = 1
SEQ = 16384
DEPTH = 2
DEC_BATCH = 32
DEC_SEQ = 16
PAST_LEN = 1024

CHUNK = 64
N_EVEN = (DEPTH + 1) // 2
N_ODD = DEPTH // 2
HA = 8
DHA = 64
WA = HA * DHA
BAND_CHUNKS = 8
BAND_PAST = BAND_CHUNKS * CHUNK
REL_CLIP = 256
HB = 8
DHB = 64
WB = HB * DHB
QBLK = 128
FORGET_BIAS_INIT = 3.0
HG = 8
DKG = 128
DVG = 128
WG = HG * DKG
CONV = 4
MEM = 256
HX = 4
DHX = D_MODEL // HX
D_FF = 4 * D_MODEL
EPS = 1e-6
f32 = jnp.float32

kernel_name = 'streaming_hybrid_band_fox_gdn_step'


def rmsnorm(x, g):
    xf = x.astype(f32)
    y = xf * lax.rsqrt(jnp.mean(xf * xf, axis=-1, keepdims=True) + EPS)
    return (y * g.astype(f32)).astype(x.dtype)


def macaron_half(x, g, wg, wu, wd):
    h = rmsnorm(x, g)
    return x + 0.5 * ((jax.nn.silu(h @ wg) * (h @ wu)) @ wd)


def rel_bias(table, rel):
    return table[:, jnp.clip(rel, -REL_CLIP, REL_CLIP) + REL_CLIP].astype(f32)


def band_attn_prompt(q, k, v, table):
    b, s = q.shape[:2]
    n = s // CHUNK
    nk = (BAND_CHUNKS + 1) * CHUNK
    qc = q.reshape(b, n, CHUNK, HA, DHA)
    pad = jnp.zeros((b, BAND_PAST, HA, DHA), k.dtype)
    kc = jnp.concatenate([pad, k], axis=1).reshape(b, n + BAND_CHUNKS, CHUNK, HA, DHA)
    vc = jnp.concatenate([pad.astype(v.dtype), v], axis=1).reshape(b, n + BAND_CHUNKS, CHUNK, HA, DHA)
    kband = jnp.concatenate([kc[:, j:j + n] for j in range(BAND_CHUNKS + 1)], axis=2)
    vband = jnp.concatenate([vc[:, j:j + n] for j in range(BAND_CHUNKS + 1)], axis=2)
    qi = jnp.arange(CHUNK)
    ki = jnp.arange(nk)
    bias = rel_bias(table, qi[:, None] + BAND_PAST - ki[None, :])
    kpos = (jnp.arange(n)[:, None] - BAND_CHUNKS) * CHUNK + ki[None, :]
    valid = kpos >= 0
    sc = jnp.einsum('bnqhd,bnkhd->bnhqk', qc, kband).astype(f32) * (DHA ** -0.5) + bias
    sc = jnp.where(valid[None, :, None, None, :], sc, -jnp.inf)
    p = jax.nn.softmax(sc, axis=-1).astype(vband.dtype)
    o = jnp.einsum('bnhqk,bnkhd->bnqhd', p, vband)
    return o.reshape(b, s, HA, DHA)


def band_attn_sample(q, k, v, ck, cv, table):
    a = ck.shape[1]
    m = q.shape[1]
    K = jnp.concatenate([ck, k], axis=1)
    V = jnp.concatenate([cv, v], axis=1)
    kpos = jnp.concatenate([jnp.arange(a) - a, jnp.arange(m)])
    bias = rel_bias(table, jnp.arange(m)[:, None] - kpos[None, :])
    sc = jnp.einsum('bqhd,bkhd->bhqk', q, K).astype(f32) * (DHA ** -0.5) + bias
    p = jax.nn.softmax(sc, axis=-1).astype(V.dtype)
    return jnp.einsum('bhqk,bkhd->bqhd', p, V)


def fox_prompt(q, k, v, logf):
    b, s = q.shape[:2]
    nb = s // QBLK
    Ft = jnp.cumsum(logf, axis=1).transpose(0, 2, 1)
    qb = q.reshape(b, nb, QBLK, HB, DHB).transpose(1, 0, 2, 3, 4)
    Fq = Ft.reshape(b, HB, nb, QBLK).transpose(2, 0, 1, 3)
    kpos = jnp.arange(s)

    def block(args):
        qi, Fi, i0 = args
        qpos = i0 + jnp.arange(QBLK)
        sc = jnp.einsum('bqhd,bkhd->bhqk', qi, k).astype(f32) * (DHB ** -0.5)
        sc = sc + Fi[..., :, None] - Ft[..., None, :]
        sc = jnp.where(kpos[None, :] <= qpos[:, None], sc, -jnp.inf)
        p = jax.nn.softmax(sc, axis=-1).astype(v.dtype)
        return jnp.einsum('bhqk,bkhd->bqhd', p, v)

    o = lax.map(block, (qb, Fq, jnp.arange(nb) * QBLK))
    return o.transpose(1, 0, 2, 3, 4).reshape(b, s, HB, DHB)


def fox_sample(q, k, v, logf, ck, cv, clogf):
    P = ck.shape[1]
    m = q.shape[1]
    K = jnp.concatenate([ck, k], axis=1)
    V = jnp.concatenate([cv, v], axis=1)
    Ft = jnp.cumsum(jnp.concatenate([clogf.astype(f32), logf], axis=1), axis=1).transpose(0, 2, 1)
    sc = jnp.einsum('bqhd,bkhd->bhqk', q, K).astype(f32) * (DHB ** -0.5)
    sc = sc + Ft[..., P:, None] - Ft[..., None, :]
    causal = jnp.arange(P + m)[None, :] <= (P + jnp.arange(m))[:, None]
    sc = jnp.where(causal, sc, -jnp.inf)
    p = jax.nn.softmax(sc, axis=-1).astype(V.dtype)
    return jnp.einsum('bhqk,bkhd->bqhd', p, V)


def ab_split(h, w_in, b_f):
    b, L, _ = h.shape
    p = h @ w_in
    heads_a = lambda t: t.reshape(b, L, HA, DHA)
    heads_b = lambda t: t.reshape(b, L, HB, DHB)
    qa = heads_a(p[..., 0:WA])
    ka = heads_a(p[..., WA:2 * WA])
    va = heads_a(p[..., 2 * WA:3 * WA])
    o = 3 * WA
    qb = heads_b(p[..., o:o + WB])
    kb = heads_b(p[..., o + WB:o + 2 * WB])
    vb = heads_b(p[..., o + 2 * WB:o + 3 * WB])
    logf = jax.nn.log_sigmoid(p[..., o + 3 * WB:].astype(f32) + b_f.astype(f32))
    return qa, ka, va, qb, kb, vb, logf


def ab_prompt(h, w_in, b_f, table, w_o):
    b, L, _ = h.shape
    qa, ka, va, qb, kb, vb, logf = ab_split(h, w_in, b_f)
    oa = band_attn_prompt(qa, ka, va, table)
    ob = fox_prompt(qb, kb, vb, logf)
    y = jnp.concatenate([oa.reshape(b, L, WA), ob.reshape(b, L, WB)], axis=-1) @ w_o
    keep = min(BAND_PAST, L)
    return y, (ka[:, L - keep:], va[:, L - keep:], kb, vb, logf)


def ab_sample(h, w_in, b_f, table, w_o, ca_k, ca_v, cb_k, cb_v, cb_logf):
    b, L, _ = h.shape
    qa, ka, va, qb, kb, vb, logf = ab_split(h, w_in, b_f)
    oa = band_attn_sample(qa, ka, va, ca_k, ca_v, table)
    ob = fox_sample(qb, kb, vb, logf, cb_k, cb_v, cb_logf)
    y = jnp.concatenate([oa.reshape(b, L, WA), ob.reshape(b, L, WB)], axis=-1) @ w_o
    return y, (ka, va, kb, vb, logf)


def l2norm(x):
    return x * lax.rsqrt(jnp.sum(x * x, axis=-1, keepdims=True) + EPS)


def gated_delta_chunked(q, k, v, g, beta, S0, chunk):
    b, L, H, dk = q.shape
    dv = v.shape[-1]
    n = L // chunk
    C = chunk
    q = (q * (dk ** -0.5)).reshape(b, n, C, H, dk)
    k = k.reshape(b, n, C, H, dk)
    v = v.reshape(b, n, C, H, dv)
    beta = beta.reshape(b, n, C, H)
    gc = jnp.cumsum(g.reshape(b, n, C, H), axis=2)
    gct = gc.transpose(0, 1, 3, 2)
    tril = jnp.tril(jnp.ones((C, C), bool))
    strict = jnp.tril(jnp.ones((C, C), bool), -1)
    Lm = jnp.exp(jnp.where(tril, gct[..., :, None] - gct[..., None, :], -jnp.inf))
    kb = k * beta[..., None]
    M = jnp.where(strict, jnp.einsum('bnihd,bnjhd->bnhij', kb, k) * Lm, 0.0)
    eye = jnp.eye(C, dtype=f32)
    T = lax.linalg.triangular_solve(eye + M, jnp.broadcast_to(eye, M.shape), left_side=True, lower=True, unit_diagonal=True)
    u = jnp.einsum('bnhij,bnjhd->bnihd', T, v * beta[..., None])
    w = jnp.einsum('bnhij,bnjhd->bnihd', T, kb * jnp.exp(gc)[..., None])
    Aqk = jnp.where(tril, jnp.einsum('bnihd,bnjhd->bnhij', q, k) * Lm, 0.0)
    qg = q * jnp.exp(gc)[..., None]
    glast = gc[:, :, -1]
    kg = k * jnp.exp(glast[:, :, None, :] - gc)[..., None]
    xs = tuple(jnp.moveaxis(t, 1, 0) for t in (w, u, qg, kg, Aqk, glast))

    def step(S, inp):
        w_i, u_i, qg_i, kg_i, a_i, gl_i = inp
        vnew = u_i - jnp.einsum('bchk,bhkv->bchv', w_i, S)
        o = jnp.einsum('bchk,bhkv->bchv', qg_i, S) + jnp.einsum('bhij,bjhv->bihv', a_i, vnew)
        S = S * jnp.exp(gl_i)[..., None, None] + jnp.einsum('bchk,bchv->bhkv', kg_i, vnew)
        return S, o

    S, o = lax.scan(step, S0, xs)
    return jnp.moveaxis(o, 0, 1).reshape(b, L, H, dv), S


def gdn_mix(h, w_in, conv_w, a_log, dt_bias, norm_g, w_o, conv_state, S0, chunk):
    b, L, _ = h.shape
    p = h @ w_in
    qkv = p[..., :3 * WG]
    z = p[..., 3 * WG:4 * WG]
    a = p[..., 4 * WG:4 * WG + HG]
    bl = p[..., 4 * WG + HG:]
    xpad = jnp.concatenate([conv_state.astype(qkv.dtype), qkv], axis=1)
    conv = xpad[:, 0:L] * conv_w[0]
    for j in range(1, CONV):
        conv = conv + xpad[:, j:j + L] * conv_w[j]
    new_conv = xpad[:, xpad.shape[1] - (CONV - 1):]
    act = jax.nn.silu(conv.astype(f32))
    q = l2norm(act[..., :WG].reshape(b, L, HG, DKG))
    k = l2norm(act[..., WG:2 * WG].reshape(b, L, HG, DKG))
    v = act[..., 2 * WG:].reshape(b, L, HG, DVG)
    g = -jnp.exp(a_log.astype(f32)) * jax.nn.softplus(a.astype(f32) + dt_bias.astype(f32))
    beta = jax.nn.sigmoid(bl.astype(f32))
    o, S = gated_delta_chunked(q, k, v, g, beta, S0.astype(f32), chunk)
    o = rmsnorm(o, norm_g) * jax.nn.silu(z.reshape(b, L, HG, DVG).astype(f32))
    y = o.reshape(b, L, WG).astype(h.dtype) @ w_o
    return y, S, new_conv


def mem_kv(mem, g, wk, wv):
    b = mem.shape[0]
    m = rmsnorm(mem, g)
    return (m @ wk).reshape(b, MEM, HX, DHX), (m @ wv).reshape(b, MEM, HX, DHX)


def cross_attn(h, wq, wo, mk, mv):
    b, L, _ = h.shape
    q = (h @ wq).reshape(b, L, HX, DHX)
    sc = jnp.einsum('bqhd,bkhd->bhqk', q, mk).astype(f32) * (DHX ** -0.5)
    p = jax.nn.softmax(sc, axis=-1).astype(mv.dtype)
    o = jnp.einsum('bhqk,bkhd->bqhd', p, mv).reshape(b, L, HX * DHX)
    return o @ wo


def setup_inputs(seed: int = 0) -> dict:
    key = jax.random.key(seed)
    ks = iter(jax.random.split(key, 48))
    D = D_MODEL

    def nrm(shape, scale):
        return scale * jax.random.normal(next(ks), shape, f32)

    a_cache = min(BAND_PAST, PAST_LEN)
    dt = jnp.exp(jax.random.uniform(next(ks), (N_ODD, HG), f32, minval=float(np.log(1e-3)), maxval=float(np.log(1e-1))))
    return {
        'x_prompt': nrm((BATCH, SEQ, D), 1.0),
        'x_sample': nrm((DEC_BATCH, DEC_SEQ, D), 1.0),
        'mem_prompt': nrm((BATCH, MEM, D), 1.0),
        'cache_a_k': nrm((N_EVEN, DEC_BATCH, a_cache, HA, DHA), 1.0),
        'cache_a_v': nrm((N_EVEN, DEC_BATCH, a_cache, HA, DHA), 1.0),
        'cache_b_k': nrm((N_EVEN, DEC_BATCH, PAST_LEN, HB, DHB), 1.0),
        'cache_b_v': nrm((N_EVEN, DEC_BATCH, PAST_LEN, HB, DHB), 1.0),
        'cache_b_logf': jax.nn.log_sigmoid(FORGET_BIAS_INIT + nrm((N_EVEN, DEC_BATCH, PAST_LEN, HB), 1.0)),
        'state_gdn': nrm((N_ODD, DEC_BATCH, HG, DKG, DVG), 0.05),
        'state_gdn_conv': nrm((N_ODD, DEC_BATCH, CONV - 1, 3 * WG), 1.0),
        'cache_mem_k': nrm((DEPTH, DEC_BATCH, MEM, HX, DHX), 1.0),
        'cache_mem_v': nrm((DEPTH, DEC_BATCH, MEM, HX, DHX), 1.0),
        'norm_g': 1.0 + nrm((DEPTH, 4, D), 0.02),
        'mem_norm_g': 1.0 + nrm((DEPTH, D), 0.02),
        'final_norm_g': 1.0 + nrm((D,), 0.02),
        'ffn_w_gate': nrm((DEPTH, 2, D, D_FF), D ** -0.5),
        'ffn_w_up': nrm((DEPTH, 2, D, D_FF), D ** -0.5),
        'ffn_w_down': nrm((DEPTH, 2, D_FF, D), D_FF ** -0.5),
        'xa_w_q': nrm((DEPTH, D, HX * DHX), D ** -0.5),
        'xa_w_k': nrm((DEPTH, D, HX * DHX), D ** -0.5),
        'xa_w_v': nrm((DEPTH, D, HX * DHX), D ** -0.5),
        'xa_w_o': nrm((DEPTH, HX * DHX, D), (HX * DHX) ** -0.5),
        'ab_w_in': nrm((N_EVEN, D, 3 * WA + 3 * WB + HB), D ** -0.5),
        'ab_b_f': FORGET_BIAS_INIT + nrm((N_EVEN, HB), 0.5),
        'ab_rel_bias': nrm((N_EVEN, HA, 2 * REL_CLIP + 1), 0.5),
        'ab_w_o': nrm((N_EVEN, WA + WB, D), (WA + WB) ** -0.5),
        'gdn_w_in': nrm((N_ODD, D, 4 * WG + 2 * HG), D ** -0.5),
        'gdn_conv_w': nrm((N_ODD, CONV, 3 * WG), CONV ** -0.5),
        'gdn_a_log': jnp.log(jax.random.uniform(next(ks), (N_ODD, HG), f32, minval=1.0, maxval=16.0)),
        'gdn_dt_bias': dt + jnp.log(-jnp.expm1(-dt)),
        'gdn_norm_g': 1.0 + nrm((N_ODD, DVG), 0.02),
        'gdn_w_o': nrm((N_ODD, WG, D), WG ** -0.5),
    }


def reference(x_prompt, x_sample, mem_prompt, cache_a_k, cache_a_v, cache_b_k, cache_b_v, cache_b_logf,
              state_gdn, state_gdn_conv, cache_mem_k, cache_mem_v, norm_g, mem_norm_g, final_norm_g,
              ffn_w_gate, ffn_w_up, ffn_w_down, xa_w_q, xa_w_k, xa_w_v, xa_w_o,
              ab_w_in, ab_b_f, ab_rel_bias, ab_w_o,
              gdn_w_in, gdn_conv_w, gdn_a_log, gdn_dt_bias, gdn_norm_g, gdn_w_o):
    xp, xs = x_prompt, x_sample
    bp = xp.shape[0]
    akp, avp, bkp, bvp, blp, sgp, scp, mkp, mvp = [], [], [], [], [], [], [], [], []
    aks, avs, bks, bvs, bls, sgs, scs = [], [], [], [], [], [], []
    for l in range(DEPTH):
        xp = macaron_half(xp, norm_g[l, 0], ffn_w_gate[l, 0], ffn_w_up[l, 0], ffn_w_down[l, 0])
        xs = macaron_half(xs, norm_g[l, 0], ffn_w_gate[l, 0], ffn_w_up[l, 0], ffn_w_down[l, 0])
        hp = rmsnorm(xp, norm_g[l, 1])
        hs = rmsnorm(xs, norm_g[l, 1])
        if l % 2 == 0:
            e = l // 2
            yp, (ka, va, kb, vb, lf) = ab_prompt(hp, ab_w_in[e], ab_b_f[e], ab_rel_bias[e], ab_w_o[e])
            ys, (ka2, va2, kb2, vb2, lf2) = ab_sample(hs, ab_w_in[e], ab_b_f[e], ab_rel_bias[e], ab_w_o[e],
                                                      cache_a_k[e], cache_a_v[e], cache_b_k[e], cache_b_v[e], cache_b_logf[e])
            akp.append(ka); avp.append(va); bkp.append(kb); bvp.append(vb); blp.append(lf)
            aks.append(ka2); avs.append(va2); bks.append(kb2); bvs.append(vb2); bls.append(lf2)
        else:
            o = l // 2
            yp, Sp, cvp = gdn_mix(hp, gdn_w_in[o], gdn_conv_w[o], gdn_a_log[o], gdn_dt_bias[o], gdn_norm_g[o], gdn_w_o[o],
                                  jnp.zeros((bp, CONV - 1, 3 * WG), hp.dtype), jnp.zeros((bp, HG, DKG, DVG), f32), CHUNK)
            ys, Ss, cvs = gdn_mix(hs, gdn_w_in[o], gdn_conv_w[o], gdn_a_log[o], gdn_dt_bias[o], gdn_norm_g[o], gdn_w_o[o],
                                  state_gdn_conv[o], state_gdn[o], hs.shape[1])
            sgp.append(Sp); scp.append(cvp); sgs.append(Ss); scs.append(cvs)
        xp = xp + yp
        xs = xs + ys
        mk, mv = mem_kv(mem_prompt, mem_norm_g[l], xa_w_k[l], xa_w_v[l])
        mkp.append(mk); mvp.append(mv)
        xp = xp + cross_attn(rmsnorm(xp, norm_g[l, 2]), xa_w_q[l], xa_w_o[l], mk, mv)
        xs = xs + cross_attn(rmsnorm(xs, norm_g[l, 2]), xa_w_q[l], xa_w_o[l], cache_mem_k[l], cache_mem_v[l])
        xp = macaron_half(xp, norm_g[l, 3], ffn_w_gate[l, 1], ffn_w_up[l, 1], ffn_w_down[l, 1])
        xs = macaron_half(xs, norm_g[l, 3], ffn_w_gate[l, 1], ffn_w_up[l, 1], ffn_w_down[l, 1])
    y_prompt = rmsnorm(xp, final_norm_g)
    y_sample = rmsnorm(xs, final_norm_g)
    return (y_prompt, y_sample,
            jnp.stack(akp), jnp.stack(avp), jnp.stack(bkp), jnp.stack(bvp), jnp.stack(blp),
            jnp.stack(sgp), jnp.stack(scp), jnp.stack(mkp), jnp.stack(mvp),
            jnp.stack(aks), jnp.stack(avs), jnp.stack(bks), jnp.stack(bvs), jnp.stack(bls),
            jnp.stack(sgs), jnp.stack(scs))
```

```python
import functools

import numpy as np
import jax
import jax.numpy as jnp
from jax import lax
from jax.experimental import pallas as pl
from jax.experimental.pallas import tpu as pltpu

f32 = jnp.float32
bf16 = jnp.bfloat16
EPS = 1e-6
NEG = -1e30

V7X_VMEM_BYTES = 64 * 1024 * 1024
V7X_LANES = 128
MIB = 1024 * 1024

CHUNK = 64
BAND_CHUNKS = 8
REL_CLIP = 256
CONV = 4
MEM = 256


def _params(sem, est_bytes):
    limit = int(min(V7X_VMEM_BYTES - 8 * MIB, max(32 * MIB, est_bytes + 8 * MIB)))
    return pltpu.CompilerParams(dimension_semantics=sem, vmem_limit_bytes=limit)


def _rms(x, g):
    ms = jnp.mean(x * x, axis=-1, keepdims=True)
    return x * lax.rsqrt(ms + EPS) * g


def _silu(x):
    return x / (1.0 + jnp.exp(-x))


def _dot(a, b):
    return jnp.dot(a, b, preferred_element_type=f32)


def _dot_nt(a, b):
    return lax.dot_general(a, b, (((1,), (1,)), ((), ())), preferred_element_type=f32)


def _dot_hi(a, b):
    return jnp.dot(a, b, preferred_element_type=f32, precision=lax.Precision.HIGHEST)


def _ffn_body(*refs, nj, final):
    if final:
        x_ref, g_ref, wg_ref, wu_ref, wd_ref, gf_ref, o_ref, h_scr, acc = refs
    else:
        x_ref, g_ref, wg_ref, wu_ref, wd_ref, o_ref, h_scr, acc = refs
    j = pl.program_id(1)

    @pl.when(j == 0)
    def _():
        h_scr[...] = _rms(x_ref[...], g_ref[...]).astype(bf16)
        acc[...] = jnp.zeros_like(acc)

    h = h_scr[...]
    a = _silu(_dot(h, wg_ref[...])) * _dot(h, wu_ref[...])
    acc[...] += _dot(a.astype(bf16), wd_ref[...])

    @pl.when(j == nj - 1)
    def _():
        y = x_ref[...] + 0.5 * acc[...]
        if final:
            y = _rms(y, gf_ref[...])
        o_ref[...] = y


def _ffn(x, g, wg, wu, wd, final_g=None):
    T, D = x.shape
    FF = wg.shape[1]
    TM = min(512, T)
    TF = min(1024, FF)
    nj = FF // TF
    final = final_g is not None
    in_specs = [pl.BlockSpec((TM, D), lambda i, j: (i, 0)),
                pl.BlockSpec((1, D), lambda i, j: (0, 0)),
                pl.BlockSpec((D, TF), lambda i, j: (0, j)),
                pl.BlockSpec((D, TF), lambda i, j: (0, j)),
                pl.BlockSpec((TF, D), lambda i, j: (j, 0))]
    args = [x, g.reshape(1, D), wg, wu, wd]
    if final:
        in_specs.append(pl.BlockSpec((1, D), lambda i, j: (0, 0)))
        args.append(final_g.reshape(1, D))
    est = 2 * (2 * TM * D * 4) + 2 * 3 * D * TF * 2 + TM * D * 6 + 3 * TM * TF * 4
    return pl.pallas_call(
        functools.partial(_ffn_body, nj=nj, final=final),
        grid=(T // TM, nj),
        in_specs=in_specs,
        out_specs=pl.BlockSpec((TM, D), lambda i, j: (i, 0)),
        out_shape=jax.ShapeDtypeStruct((T, D), f32),
        scratch_shapes=[pltpu.VMEM((TM, D), bf16), pltpu.VMEM((TM, D), f32)],
        compiler_params=_params(("parallel", "arbitrary"), est),
        name="ffn",
    )(*args)


def _log_sigmoid(x):
    return jnp.minimum(x, 0.0) - jnp.log(1.0 + jnp.exp(-jnp.abs(x)))


def _norm_proj_body(*refs, outs, has_bias):
    x_ref, g_ref, w_ref = refs[:3]
    pos = 3
    b_ref = None
    if has_bias:
        b_ref = refs[pos]
        pos += 1
    o_refs = refs[pos:]
    h = _rms(x_ref[...], g_ref[...]).astype(bf16)
    cache = {}
    for o_ref, (off, n, dt, scale, act) in zip(o_refs, outs):
        if (off, n) not in cache:
            cache[(off, n)] = _dot(h, w_ref[:, off:off + n])
        r = cache[(off, n)]
        if act == "logsig":
            r = _log_sigmoid(r + b_ref[...])
        if scale != 1.0:
            r = r * scale
        o_ref[...] = r.astype(dt)


def _norm_proj(x, g, w, outs, bias=None):
    T, D = x.shape
    N = w.shape[1]
    TM = min(512, T)
    in_specs = [pl.BlockSpec((TM, D), lambda i: (i, 0)),
                pl.BlockSpec((1, D), lambda i: (0, 0)),
                pl.BlockSpec((D, N), lambda i: (0, 0))]
    args = [x, g.reshape(1, D), w]
    if bias is not None:
        in_specs.append(pl.BlockSpec((1, bias.shape[-1]), lambda i: (0, 0)))
        args.append(bias)
    out_specs = [pl.BlockSpec((TM, n), lambda i: (i, 0)) for (_, n, _, _, _) in outs]
    out_shape = [jax.ShapeDtypeStruct((T, n), dt) for (_, n, dt, _, _) in outs]
    est = 2 * TM * D * 4 + 2 * D * N * 2 + sum(2 * TM * n * 4 for (_, n, _, _, _) in outs) + TM * N * 4
    return pl.pallas_call(
        functools.partial(_norm_proj_body, outs=tuple(outs), has_bias=bias is not None),
        grid=(T // TM,),
        in_specs=in_specs,
        out_specs=out_specs,
        out_shape=out_shape,
        compiler_params=_params(("parallel",), est),
        name="norm_proj",
    )(*args)


def _proj_res_body(*refs, nparts):
    x_ref = refs[0]
    o_ref = refs[1 + 2 * nparts]
    y = x_ref[...]
    for p in range(nparts):
        y = y + _dot(refs[1 + 2 * p][...], refs[2 + 2 * p][...])
    o_ref[...] = y


def _proj_res(x, parts):
    T, D = x.shape
    TM = min(512, T)
    in_specs = [pl.BlockSpec((TM, D), lambda i: (i, 0))]
    args = [x]
    est = 4 * TM * D * 4
    for a, w in parts:
        K = a.shape[1]
        in_specs += [pl.BlockSpec((TM, K), lambda i: (i, 0)), pl.BlockSpec((K, D), lambda i: (0, 0))]
        args += [a, w]
        est += 2 * TM * K * 2 + 2 * K * D * 2
    return pl.pallas_call(
        functools.partial(_proj_res_body, nparts=len(parts)),
        grid=(T // TM,),
        in_specs=in_specs,
        out_specs=pl.BlockSpec((TM, D), lambda i: (i, 0)),
        out_shape=jax.ShapeDtypeStruct((T, D), f32),
        compiler_params=_params(("parallel",), est),
        name="proj_res",
    )(*args)


def _cumsum_body(x_ref, f_ref, ft_ref, carry, *, nsub):
    @pl.when(pl.program_id(0) == 0)
    def _():
        carry[...] = jnp.zeros_like(carry)

    r = lax.broadcasted_iota(jnp.int32, (V7X_LANES, V7X_LANES), 0)
    c = lax.broadcasted_iota(jnp.int32, (V7X_LANES, V7X_LANES), 1)
    tri = (r >= c).astype(f32)
    run = carry[...]
    ncol = x_ref.shape[1] // V7X_LANES
    for sb in range(nsub):
        rows = slice(sb * V7X_LANES, (sb + 1) * V7X_LANES)
        blk = _dot_hi(tri, x_ref[rows, :]) + run
        f_ref[rows, :] = blk
        for cb in range(ncol):
            cols = slice(cb * V7X_LANES, (cb + 1) * V7X_LANES)
            ft_ref[cols, rows] = blk[:, cols].T
        run = blk[V7X_LANES - 1:V7X_LANES, :]
    carry[...] = run


def _cumsum_rows(x):
    L, N = x.shape
    nsub = 8 if L % (8 * V7X_LANES) == 0 else 1
    TB = nsub * V7X_LANES
    return pl.pallas_call(
        functools.partial(_cumsum_body, nsub=nsub),
        grid=(L // TB,),
        in_specs=[pl.BlockSpec((TB, N), lambda i: (i, 0))],
        out_specs=[pl.BlockSpec((TB, N), lambda i: (i, 0)), pl.BlockSpec((N, TB), lambda i: (0, i))],
        out_shape=[jax.ShapeDtypeStruct((L, N), f32), jax.ShapeDtypeStruct((N, L), f32)],
        scratch_shapes=[pltpu.VMEM((1, N), f32)],
        compiler_params=_params(("arbitrary",), 8 * TB * N * 4),
        name="cumsum_rows",
    )(x)


BAND_TQ = 256
BAND_NB = 3


def _band_body(q_ref, k0_ref, k1_ref, k2_ref, v0_ref, v1_ref, v2_ref, b_ref, o_ref, *, nheads):
    i = pl.program_id(0)
    TQ = BAND_TQ
    lane = lax.broadcasted_iota(jnp.int32, (TQ, V7X_LANES), 1)
    lo = lane < 64
    krefs = (k0_ref, k1_ref, k2_ref)
    vrefs = (v0_ref, v1_ref, v2_ref)
    pens = [jnp.where(i - (BAND_NB - 1) + b < 0, NEG, 0.0).astype(f32) for b in range(BAND_NB)]
    for hp in range(nheads // 2):
        cols = slice(hp * V7X_LANES, (hp + 1) * V7X_LANES)
        qp = q_ref[:, cols]
        kp = [kr[:, cols] for kr in krefs]
        vp = [vr[:, cols] for vr in vrefs]
        res = []
        for half in range(2):
            h = 2 * hp + half
            qm = jnp.where(lo if half == 0 else jnp.logical_not(lo), qp, jnp.zeros_like(qp))
            s = [_dot_nt(qm, kp[b]) + b_ref[h, :, b * TQ:(b + 1) * TQ] + pens[b] for b in range(BAND_NB)]
            m = jnp.max(s[0], axis=-1, keepdims=True)
            for b in range(1, BAND_NB):
                m = jnp.maximum(m, jnp.max(s[b], axis=-1, keepdims=True))
            p = [jnp.exp(sb - m) for sb in s]
            l = p[0].sum(axis=-1, keepdims=True)
            o = _dot(p[0].astype(bf16), vp[0])
            for b in range(1, BAND_NB):
                l = l + p[b].sum(axis=-1, keepdims=True)
                o = o + _dot(p[b].astype(bf16), vp[b])
            res.append(o / l)
        o_ref[:, cols] = jnp.where(lo, res[0], res[1]).astype(bf16)


def _band_bias(table):
    qi = np.arange(BAND_TQ)[:, None]
    ki = np.arange(BAND_NB * BAND_TQ)[None, :]
    rel = np.clip(qi + BAND_CHUNKS * CHUNK - ki, -REL_CLIP, REL_CLIP) + REL_CLIP
    qc = qi // CHUNK
    kc = ki // CHUNK
    valid = (kc >= qc) & (kc <= qc + BAND_CHUNKS)
    bias = table[:, rel].astype(f32)
    return jnp.where(valid[None], bias, NEG)


def _band_prompt(q, k, v, bias):
    L, W = q.shape
    TQ = BAND_TQ
    H = bias.shape[0]
    kspec = [pl.BlockSpec((TQ, W), functools.partial(lambda i, d: (jnp.maximum(i - d, 0), 0), d=d))
             for d in (2, 1, 0)]
    est = 2 * 7 * TQ * W * 2 + 2 * H * TQ * BAND_NB * TQ * 4 + 16 * TQ * TQ * 4
    return pl.pallas_call(
        functools.partial(_band_body, nheads=H),
        grid=(L // TQ,),
        in_specs=[pl.BlockSpec((TQ, W), lambda i: (i, 0))] + kspec + kspec
                 + [pl.BlockSpec((H, TQ, BAND_NB * TQ), lambda i: (0, 0, 0))],
        out_specs=pl.BlockSpec((TQ, W), lambda i: (i, 0)),
        out_shape=jax.ShapeDtypeStruct((L, W), bf16),
        compiler_params=_params(("parallel",), est),
        name="band_prompt",
    )(q, k, k, k, v, v, v, bias)


FOX_T = 512


def _fox_body(qi_ref, kj_ref, q_ref, k_ref, v_ref, fq_ref, fk_ref, o_ref, m_scr, l_scr, acc_scr, *, nheads):
    s_id = pl.program_id(0)
    i = qi_ref[s_id]
    j = kj_ref[s_id]
    T = FOX_T
    nrep = T // V7X_LANES

    @pl.when(j == 0)
    def _():
        m_scr[...] = jnp.full_like(m_scr, NEG)
        l_scr[...] = jnp.zeros_like(l_scr)
        acc_scr[...] = jnp.zeros_like(acc_scr)

    def step(masked):
        lane = lax.broadcasted_iota(jnp.int32, (T, V7X_LANES), 1)
        lo = lane < 64
        if masked:
            causal = lax.broadcasted_iota(jnp.int32, (T, T), 0) >= lax.broadcasted_iota(jnp.int32, (T, T), 1)
        for hp in range(nheads // 2):
            cols = slice(hp * V7X_LANES, (hp + 1) * V7X_LANES)
            qp = q_ref[:, cols]
            kp = k_ref[:, cols]
            vp = v_ref[:, cols]
            for half in range(2):
                h = 2 * hp + half
                qm = jnp.where(lo if half == 0 else jnp.logical_not(lo), qp, jnp.zeros_like(qp))
                s = _dot_nt(qm, kp) + (fq_ref[:, h:h + 1] - fk_ref[h:h + 1, :])
                if masked:
                    s = jnp.where(causal, s, NEG)
                m_prev = m_scr[h]
                m_next = jnp.maximum(m_prev, jnp.max(s, axis=-1, keepdims=True))
                p = jnp.exp(s - jnp.tile(m_next, (1, nrep)))
                alpha = jnp.exp(m_prev - m_next)
                l_scr[h] = alpha * l_scr[h] + p.sum(axis=-1, keepdims=True)
                acc_scr[h] = alpha * acc_scr[h] + _dot(p.astype(bf16), vp)
                m_scr[h] = m_next

    @pl.when(j < i)
    def _():
        step(False)

    @pl.when(j == i)
    def _():
        step(True)
        lane = lax.broadcasted_iota(jnp.int32, (T, V7X_LANES), 1)
        lo = lane < 64
        for hp in range(nheads // 2):
            cols = slice(hp * V7X_LANES, (hp + 1) * V7X_LANES)
            o0 = acc_scr[2 * hp] / l_scr[2 * hp]
            o1 = acc_scr[2 * hp + 1] / l_scr[2 * hp + 1]
            o_ref[:, cols] = jnp.where(lo, o0, o1).astype(bf16)


def _fox_prompt(q, k, v, F, FT, nheads):
    L, W = q.shape
    T = FOX_T
    n = L // T
    pairs = [(i, j) for i in range(n) for j in range(i + 1)]
    qi = jnp.asarray(np.array([p[0] for p in pairs], np.int32))
    kj = jnp.asarray(np.array([p[1] for p in pairs], np.int32))
    grid_spec = pltpu.PrefetchScalarGridSpec(
        num_scalar_prefetch=2,
        grid=(len(pairs),),
        in_specs=[pl.BlockSpec((T, W), lambda s, qi, kj: (qi[s], 0)),
                  pl.BlockSpec((T, W), lambda s, qi, kj: (kj[s], 0)),
                  pl.BlockSpec((T, W), lambda s, qi, kj: (kj[s], 0)),
                  pl.BlockSpec((T, V7X_LANES), lambda s, qi, kj: (qi[s], 0)),
                  pl.BlockSpec((8, T), lambda s, qi, kj: (0, kj[s]))],
        out_specs=pl.BlockSpec((T, W), lambda s, qi, kj: (qi[s], 0)),
        scratch_shapes=[pltpu.VMEM((nheads, T, V7X_LANES), f32)] * 3,
    )
    est = 2 * 4 * T * W * 2 + 3 * nheads * T * V7X_LANES * 4 + 8 * T * T * 4
    return pl.pallas_call(
        functools.partial(_fox_body, nheads=nheads),
        grid_spec=grid_spec,
        out_shape=jax.ShapeDtypeStruct((L, W), bf16),
        compiler_params=_params(("arbitrary",), est),
        name="fox_prompt",
    )(qi, kj, q, k, v, F, FT)


def _cross_body(q_ref, k_ref, v_ref, o_ref, *, nheads, dh):
    for h in range(nheads):
        cols = slice(h * dh, (h + 1) * dh)
        s = _dot_nt(q_ref[:, cols], k_ref[:, cols])
        m = jnp.max(s, axis=-1, keepdims=True)
        p = jnp.exp(s - m)
        l = p.sum(axis=-1, keepdims=True)
        o_ref[:, cols] = (_dot(p.astype(bf16), v_ref[:, cols]) / l).astype(bf16)


def _cross_prompt(q, mk, mv, nheads):
    L, W = q.shape
    S = mk.shape[0]
    TM = min(512, L)
    return pl.pallas_call(
        functools.partial(_cross_body, nheads=nheads, dh=W // nheads),
        grid=(L // TM,),
        in_specs=[pl.BlockSpec((TM, W), lambda i: (i, 0)),
                  pl.BlockSpec((S, W), lambda i: (0, 0)),
                  pl.BlockSpec((S, W), lambda i: (0, 0))],
        out_specs=pl.BlockSpec((TM, W), lambda i: (i, 0)),
        out_shape=jax.ShapeDtypeStruct((L, W), bf16),
        compiler_params=_params(("parallel",), 8 * TM * W * 2 + 8 * TM * S * 4),
        name="cross_prompt",
    )(q, mk, mv)


def _sample_attn_body(*refs, nheads, mode, has_new):
    it = iter(refs)
    q_ref = next(it)
    kc_ref = next(it)
    vc_ref = next(it)
    kn_ref = vn_ref = None
    if has_new:
        kn_ref = next(it)
        vn_ref = next(it)
    if mode == "rel":
        bc_ref = next(it)
        bn_ref = next(it)
    elif mode == "fox":
        fq_ref = next(it)
        fkc_ref = next(it)
        fkn_ref = next(it)
    o_ref = next(it)

    m_q, W = q_ref.shape
    dh = W // nheads
    R = nheads * m_q
    q = q_ref[...].astype(f32)
    qt = jnp.concatenate([q] * nheads, axis=0)
    hrow = lax.broadcasted_iota(jnp.int32, (R, W), 0) // m_q
    hlane = lax.broadcasted_iota(jnp.int32, (R, W), 1) // dh
    qbd = jnp.where(hrow == hlane, qt, 0.0).astype(bf16)

    def rows_of(f_ref):
        n = f_ref.shape[-1]
        return jnp.concatenate([jnp.broadcast_to(f_ref[h:h + 1, :], (m_q, n)) for h in range(nheads)], axis=0)

    sc = _dot_nt(qbd, kc_ref[...].astype(bf16))
    if mode == "rel":
        sc = sc + bc_ref[...]
    elif mode == "fox":
        sc = sc + (fq_ref[...] - rows_of(fkc_ref))
    m = jnp.max(sc, axis=-1, keepdims=True)
    if has_new:
        sn = _dot_nt(qbd, kn_ref[...].astype(bf16))
        if mode == "rel":
            sn = sn + bn_ref[...]
        elif mode == "fox":
            sn = sn + (fq_ref[...] - rows_of(fkn_ref))
            qpos = lax.broadcasted_iota(jnp.int32, (R, m_q), 0) % m_q
            kpos = lax.broadcasted_iota(jnp.int32, (R, m_q), 1)
            sn = jnp.where(kpos <= qpos, sn, NEG)
        m = jnp.maximum(m, jnp.max(sn, axis=-1, keepdims=True))
    pc = jnp.exp(sc - m)
    l = pc.sum(axis=-1, keepdims=True)
    o = _dot(pc.astype(bf16), vc_ref[...].astype(bf16))
    if has_new:
        pn = jnp.exp(sn - m)
        l = l + pn.sum(axis=-1, keepdims=True)
        o = o + _dot(pn.astype(bf16), vn_ref[...].astype(bf16))
    o = o / l
    hl = lax.broadcasted_iota(jnp.int32, (m_q, W), 1) // dh
    out = jnp.zeros((m_q, W), f32)
    for h in range(nheads):
        out = out + jnp.where(hl == h, o[h * m_q:(h + 1) * m_q, :], 0.0)
    o_ref[...] = out.astype(bf16)


def _sample_attn(q, kc, vc, nheads, mode="none", kn=None, vn=None, extra=()):
    B, m_q, W = q.shape
    P = kc.shape[1]
    R = nheads * m_q
    has_new = kn is not None
    per_b = lambda *shape: pl.BlockSpec((None,) + shape, lambda b: (b,) + (0,) * len(shape))
    shared = lambda *shape: pl.BlockSpec(shape, lambda b: (0,) * len(shape))
    in_specs = [per_b(m_q, W), per_b(P, W), per_b(P, W)]
    args = [q, kc, vc]
    if has_new:
        in_specs += [per_b(m_q, W), per_b(m_q, W)]
        args += [kn, vn]
    if mode == "rel":
        in_specs += [shared(R, P), shared(R, m_q)]
    elif mode == "fox":
        in_specs += [per_b(R, 1), per_b(nheads, P), per_b(nheads, m_q)]
    args += list(extra)
    est = 2 * 2 * P * W * 4 + 2 * P * W * 2 + 6 * R * P * 4 + 4 * R * W * 4
    return pl.pallas_call(
        functools.partial(_sample_attn_body, nheads=nheads, mode=mode, has_new=has_new),
        grid=(B,),
        in_specs=in_specs,
        out_specs=per_b(m_q, W),
        out_shape=jax.ShapeDtypeStruct((B, m_q, W), bf16),
        compiler_params=_params(("parallel",), est),
        name="sample_attn_" + mode,
    )(*args)


GDN_HALO = 8


def _softplus(x):
    return jnp.maximum(x, 0.0) + jnp.log(1.0 + jnp.exp(-jnp.abs(x)))


def _unit_lower_inverse(M, C):
    r = lax.broadcasted_iota(jnp.int32, (C, C), 0)
    c = lax.broadcasted_iota(jnp.int32, (C, C), 1)
    X = jnp.where(r == c, 1.0, 0.0).astype(f32) - M
    Pw = _dot_hi(M, M)
    e = 2
    while e < C:
        X = X + _dot_hi(X, Pw)
        e *= 2
        if e < C:
            Pw = _dot_hi(Pw, Pw)
    return X


def _gdn_body(x_ref, ab_ref, z_ref, cs_ref, s0_ref, cw_ref, alog_ref, dtb_ref, ng_ref,
              o_ref, sf_ref, cf_ref, xbuf, s_scr, *, C, nchunks, nheads, dk):
    c = pl.program_id(1)
    H0 = GDN_HALO
    W = nheads * dk

    @pl.when(c == 0)
    def _():
        xbuf[H0 - (CONV - 1):H0, :] = cs_ref[...]
        s_scr[...] = s0_ref[...]

    xbuf[H0:H0 + C, :] = x_ref[...]
    cw = cw_ref[...]
    conv = xbuf[H0 - 3:H0 - 3 + C, :] * cw[0:1, :]
    for jj in range(1, CONV):
        conv = conv + xbuf[H0 - 3 + jj:H0 - 3 + jj + C, :] * cw[jj:jj + 1, :]
    tail = xbuf[H0 + C - (CONV - 1):H0 + C, :]
    xbuf[H0 - (CONV - 1):H0, :] = tail

    @pl.when(c == nchunks - 1)
    def _():
        cf_ref[...] = tail

    act = _silu(conv)
    ab = ab_ref[...]
    gfull = -jnp.exp(alog_ref[...]) * _softplus(ab + dtb_ref[...])
    bfull = 1.0 / (1.0 + jnp.exp(-ab))
    r = lax.broadcasted_iota(jnp.int32, (C, C), 0)
    cc = lax.broadcasted_iota(jnp.int32, (C, C), 1)
    tri = r >= cc
    strict = r > cc
    gc = _dot_hi(tri.astype(f32), gfull)
    gcT = gc.T
    egc = jnp.exp(gc)
    z = z_ref[...]
    ng = ng_ref[...]

    for h in range(nheads):
        cols = slice(h * dk, (h + 1) * dk)
        qh = act[:, h * dk:(h + 1) * dk]
        kh = act[:, W + h * dk:W + (h + 1) * dk]
        vh = act[:, 2 * W + h * dk:2 * W + (h + 1) * dk]
        qh = qh * lax.rsqrt(jnp.sum(qh * qh, axis=-1, keepdims=True) + EPS) * (dk ** -0.5)
        kh = kh * lax.rsqrt(jnp.sum(kh * kh, axis=-1, keepdims=True) + EPS)
        gcol = gc[:, h:h + 1]
        grow = gcT[h:h + 1, :]
        Lm = jnp.where(tri, jnp.exp(jnp.where(tri, gcol - grow, 0.0)), 0.0)
        bcol = bfull[:, nheads + h:nheads + h + 1]
        ecol = egc[:, h:h + 1]
        kb = kh * bcol
        kbf = kh.astype(bf16)
        M = jnp.where(strict, _dot_nt(kb.astype(bf16), kbf) * Lm, 0.0)
        Tm = _unit_lower_inverse(M, C).astype(bf16)
        u = _dot(Tm, (vh * bcol).astype(bf16))
        w = _dot(Tm, (kb * ecol).astype(bf16))
        Aqk = jnp.where(tri, _dot_nt(qh.astype(bf16), kbf) * Lm, 0.0)
        qg = qh * ecol
        glast = gc[C - 1:C, h:h + 1]
        kg = kh * jnp.exp(glast - gcol)
        S = s_scr[h]
        Sb = S.astype(bf16)
        vnew = u - _dot(w.astype(bf16), Sb)
        vnb = vnew.astype(bf16)
        o = _dot(qg.astype(bf16), Sb) + _dot(Aqk.astype(bf16), vnb)
        s_scr[h] = S * jnp.exp(glast) + _dot(kg.T.astype(bf16), vnb)
        y = _rms(o, ng) * _silu(z[:, cols])
        o_ref[:, cols] = y.astype(bf16)

    @pl.when(c == nchunks - 1)
    def _():
        sf_ref[...] = s_scr[...]


def _gdn_core(qkv, ab, z, conv_state, S0, conv_w, a_log, dt_bias, norm_g, C):
    T = qkv.shape[0]
    B, H, dk, dv = S0.shape
    W = H * dk
    nchunks = T // (B * C)
    pad = lambda vec: jnp.zeros((1, V7X_LANES), f32).at[0, :H].set(vec.astype(f32))
    row = lambda width: pl.BlockSpec((C, width), lambda b, c: (b * nchunks + c, 0))
    est = 2 * C * (3 * W + W + V7X_LANES) * 4 + 3 * H * dk * dv * 4 * 2 + 12 * C * 3 * W * 4
    return pl.pallas_call(
        functools.partial(_gdn_body, C=C, nchunks=nchunks, nheads=H, dk=dk),
        grid=(B, nchunks),
        in_specs=[row(3 * W), row(V7X_LANES), row(W),
                  pl.BlockSpec((None, CONV - 1, 3 * W), lambda b, c: (b, 0, 0)),
                  pl.BlockSpec((None, H, dk, dv), lambda b, c: (b, 0, 0, 0)),
                  pl.BlockSpec((CONV, 3 * W), lambda b, c: (0, 0)),
                  pl.BlockSpec((1, V7X_LANES), lambda b, c: (0, 0)),
                  pl.BlockSpec((1, V7X_LANES), lambda b, c: (0, 0)),
                  pl.BlockSpec((1, dv), lambda b, c: (0, 0))],
        out_specs=[row(W),
                   pl.BlockSpec((None, H, dk, dv), lambda b, c: (b, 0, 0, 0)),
                   pl.BlockSpec((None, CONV - 1, 3 * W), lambda b, c: (b, 0, 0))],
        out_shape=[jax.ShapeDtypeStruct((T, W), bf16),
                   jax.ShapeDtypeStruct((B, H, dk, dv), f32),
                   jax.ShapeDtypeStruct((B, CONV - 1, 3 * W), f32)],
        scratch_shapes=[pltpu.VMEM((GDN_HALO + C, 3 * W), f32), pltpu.VMEM((H, dk, dv), f32)],
        compiler_params=_params(("parallel", "arbitrary"), est),
        name="gdn_core",
    )(qkv, ab, z, conv_state, S0, conv_w, pad(a_log), pad(dt_bias), norm_g.reshape(1, dv))


def kernel(x_prompt, x_sample, mem_prompt, cache_a_k, cache_a_v, cache_b_k, cache_b_v, cache_b_logf, state_gdn, state_gdn_conv, cache_mem_k, cache_mem_v, norm_g, mem_norm_g, final_norm_g, ffn_w_gate, ffn_w_up, ffn_w_down, xa_w_q, xa_w_k, xa_w_v, xa_w_o, ab_w_in, ab_b_f, ab_rel_bias, ab_w_o, gdn_w_in, gdn_conv_w, gdn_a_log, gdn_dt_bias, gdn_norm_g, gdn_w_o):
    depth = norm_g.shape[0]
    BP, SEQ, D = x_prompt.shape
    BS, MS, _ = x_sample.shape
    assert BP == 1
    HA = HB = ab_b_f.shape[1]
    WA = WB = (ab_w_in.shape[2] - HB) // 6
    DHA = WA // HA
    HG = gdn_a_log.shape[1]
    WG = gdn_w_o.shape[1]
    DKG = WG // HG
    HX = cache_mem_k.shape[3]
    DHX = cache_mem_k.shape[4]
    WX = HX * DHX
    PA = cache_a_k.shape[2]
    PB = cache_b_k.shape[2]
    LANES = V7X_LANES

    xp = x_prompt.reshape(SEQ, D)
    xs = x_sample.reshape(BS * MS, D)

    wgate = ffn_w_gate.astype(bf16)
    wup = ffn_w_up.astype(bf16)
    wdown = ffn_w_down.astype(bf16)
    wxq = xa_w_q.astype(bf16)
    wxo = xa_w_o.astype(bf16)
    wxkv = jnp.concatenate([xa_w_k, xa_w_v], axis=-1).astype(bf16)

    outs = {k: [] for k in ("akp", "avp", "bkp", "bvp", "blp", "sgp", "scp", "mkp", "mvp",
                            "aks", "avs", "bks", "bvs", "bls", "sgs", "scs")}

    for l in range(depth):
        xp = _ffn(xp, norm_g[l, 0], wgate[l, 0], wup[l, 0], wdown[l, 0])
        xs = _ffn(xs, norm_g[l, 0], wgate[l, 0], wup[l, 0], wdown[l, 0])

        if l % 2 == 0:
            e = l // 2
            w_in = ab_w_in[e]
            nq = 3 * WA + 3 * WB
            w_pad = jnp.concatenate([w_in, jnp.zeros((D, LANES - HB), f32)], axis=-1).astype(bf16)
            b_pad = jnp.zeros((1, LANES), f32).at[0, :HB].set(ab_b_f[e].astype(f32))
            spec = [(0, WA, bf16, DHA ** -0.5, None),
                    (WA, WA, f32, 1.0, None), (WA, WA, bf16, 1.0, None),
                    (2 * WA, WA, f32, 1.0, None), (2 * WA, WA, bf16, 1.0, None),
                    (3 * WA, WB, bf16, (WB // HB) ** -0.5, None),
                    (3 * WA + WB, WB, f32, 1.0, None), (3 * WA + WB, WB, bf16, 1.0, None),
                    (3 * WA + 2 * WB, WB, f32, 1.0, None), (3 * WA + 2 * WB, WB, bf16, 1.0, None),
                    (nq, LANES, f32, 1.0, "logsig")]
            wo = ab_w_o[e].astype(bf16)

            qa, ka, kab, va, vab, qb, kb, kbb, vb, vbb, lf = _norm_proj(xp, norm_g[l, 1], w_pad, spec, bias=b_pad)
            oa = _band_prompt(qa, kab, vab, _band_bias(ab_rel_bias[e]))
            F, FT = _cumsum_rows(lf)
            ob = _fox_prompt(qb, kbb, vbb, F, FT, HB)
            xp = _proj_res(xp, [(oa, wo[:WA]), (ob, wo[WA:])])
            keep = min(BAND_CHUNKS * CHUNK, SEQ)
            outs["akp"].append(ka[SEQ - keep:].reshape(1, keep, HA, DHA))
            outs["avp"].append(va[SEQ - keep:].reshape(1, keep, HA, DHA))
            outs["bkp"].append(kb.reshape(1, SEQ, HB, WB // HB))
            outs["bvp"].append(vb.reshape(1, SEQ, HB, WB // HB))
            outs["blp"].append(lf[:, :HB].reshape(1, SEQ, HB))

            qa, ka, _, va, _, qb, kb, _, vb, _, lf = _norm_proj(xs, norm_g[l, 1], w_pad, spec, bias=b_pad)
            table = ab_rel_bias[e]
            kpos = jnp.concatenate([jnp.arange(PA) - PA, jnp.arange(MS)])
            rel = jnp.clip(jnp.arange(MS)[:, None] - kpos[None, :], -REL_CLIP, REL_CLIP) + REL_CLIP
            bias_s = table[:, rel].astype(f32).reshape(HA * MS, PA + MS)
            oa = _sample_attn(qa.reshape(BS, MS, WA), cache_a_k[e].reshape(BS, PA, WA), cache_a_v[e].reshape(BS, PA, WA),
                              HA, "rel", ka.reshape(BS, MS, WA), va.reshape(BS, MS, WA),
                              extra=(bias_s[:, :PA], bias_s[:, PA:]))
            lfn = lf[:, :HB].reshape(BS, MS, HB)
            lcat = jnp.concatenate([cache_b_logf[e].astype(f32), lfn], axis=1)
            LP = -(-(PB + MS) // LANES) * LANES
            lcat = jnp.pad(lcat.transpose(1, 0, 2).reshape(PB + MS, BS * HB), ((0, LP - PB - MS), (0, 0)))
            _, FTs = _cumsum_rows(lcat)
            FTs = FTs.reshape(BS, HB, LP)
            fq = FTs[:, :, PB:PB + MS].reshape(BS, HB * MS, 1)
            ob = _sample_attn(qb.reshape(BS, MS, WB), cache_b_k[e].reshape(BS, PB, WB), cache_b_v[e].reshape(BS, PB, WB),
                              HB, "fox", kb.reshape(BS, MS, WB), vb.reshape(BS, MS, WB),
                              extra=(fq, FTs[:, :, :PB], FTs[:, :, PB:PB + MS]))
            xs = _proj_res(xs, [(oa.reshape(BS * MS, WA), wo[:WA]), (ob.reshape(BS * MS, WB), wo[WA:])])
            outs["aks"].append(ka.reshape(BS, MS, HA, DHA))
            outs["avs"].append(va.reshape(BS, MS, HA, DHA))
            outs["bks"].append(kb.reshape(BS, MS, HB, WB // HB))
            outs["bvs"].append(vb.reshape(BS, MS, HB, WB // HB))
            outs["bls"].append(lfn)
        else:
            o = l // 2
            w_in = gdn_w_in[o]
            w_pad = jnp.concatenate([w_in, jnp.zeros((D, LANES - 2 * HG), f32)], axis=-1).astype(bf16)
            spec = [(0, 3 * WG, f32, 1.0, None), (3 * WG, WG, f32, 1.0, None), (4 * WG, LANES, f32, 1.0, None)]
            wo = gdn_w_o[o].astype(bf16)

            qkv, z, ab = _norm_proj(xp, norm_g[l, 1], w_pad, spec)
            y, Sp, cvp = _gdn_core(qkv, ab, z, jnp.zeros((1, CONV - 1, 3 * WG), f32), jnp.zeros((1, HG, DKG, DKG), f32),
                                   gdn_conv_w[o], gdn_a_log[o], gdn_dt_bias[o], gdn_norm_g[o], CHUNK)
            xp = _proj_res(xp, [(y, wo)])
            outs["sgp"].append(Sp)
            outs["scp"].append(cvp)

            qkv, z, ab = _norm_proj(xs, norm_g[l, 1], w_pad, spec)
            y, Ss, cvs = _gdn_core(qkv, ab, z, state_gdn_conv[o].astype(f32), state_gdn[o].astype(f32),
                                   gdn_conv_w[o], gdn_a_log[o], gdn_dt_bias[o], gdn_norm_g[o], MS)
            xs = _proj_res(xs, [(y, wo)])
            outs["sgs"].append(Ss)
            outs["scs"].append(cvs)

        mk, mkb, mv, mvb = _norm_proj(mem_prompt.reshape(MEM, D), mem_norm_g[l], wxkv[l],
                                      [(0, WX, f32, 1.0, None), (0, WX, bf16, 1.0, None),
                                       (WX, WX, f32, 1.0, None), (WX, WX, bf16, 1.0, None)])
        outs["mkp"].append(mk.reshape(1, MEM, HX, DHX))
        outs["mvp"].append(mv.reshape(1, MEM, HX, DHX))
        qspec = [(0, WX, bf16, DHX ** -0.5, None)]
        (q,) = _norm_proj(xp, norm_g[l, 2], wxq[l], qspec)
        xp = _proj_res(xp, [(_cross_prompt(q, mkb, mvb, HX), wxo[l])])
        (q,) = _norm_proj(xs, norm_g[l, 2], wxq[l], qspec)
        oc = _sample_attn(q.reshape(BS, MS, WX), cache_mem_k[l].reshape(BS, MEM, WX), cache_mem_v[l].reshape(BS, MEM, WX), HX)
        xs = _proj_res(xs, [(oc.reshape(BS * MS, WX), wxo[l])])

        fin = final_norm_g if l == depth - 1 else None
        xp = _ffn(xp, norm_g[l, 3], wgate[l, 1], wup[l, 1], wdown[l, 1], fin)
        xs = _ffn(xs, norm_g[l, 3], wgate[l, 1], wup[l, 1], wdown[l, 1], fin)

    st = lambda k: jnp.stack(outs[k])
    return (xp.reshape(BP, SEQ, D), xs.reshape(BS, MS, D),
            st("akp"), st("avp"), st("bkp"), st("bvp"), st("blp"),
            st("sgp"), st("scp"), st("mkp"), st("mvp"),
            st("aks"), st("avs"), st("bks"), st("bvs"), st("bls"),
            st("sgs"), st("scs"))
```

```python
import functools

import numpy as np
import jax
import jax.numpy as jnp
from jax import lax
from jax.experimental import pallas as pl
from jax.experimental.pallas import tpu as pltpu

f32 = jnp.float32
bf16 = jnp.bfloat16
EPS = 1e-6
NEG = -1e30

V7X_VMEM_BYTES = 64 * 1024 * 1024
V7X_LANES = 128
MIB = 1024 * 1024

CHUNK = 64
BAND_CHUNKS = 8
REL_CLIP = 256
CONV = 4
MEM = 256


def _params(sem, est_bytes):
    limit = int(min(V7X_VMEM_BYTES - 8 * MIB, max(32 * MIB, est_bytes + 8 * MIB)))
    return pltpu.CompilerParams(dimension_semantics=sem, vmem_limit_bytes=limit)


def _rms(x, g):
    ms = jnp.mean(x * x, axis=-1, keepdims=True)
    return x * lax.rsqrt(ms + EPS) * g


def _silu(x):
    return x / (1.0 + jnp.exp(-x))


def _dot(a, b):
    return jnp.dot(a, b, preferred_element_type=f32)


def _dot_nt(a, b):
    return lax.dot_general(a, b, (((1,), (1,)), ((), ())), preferred_element_type=f32)


def _dot_hi(a, b):
    return jnp.dot(a, b, preferred_element_type=f32, precision=lax.Precision.HIGHEST)


def _ffn_body(*refs, nj, final):
    if final:
        x_ref, g_ref, wg_ref, wu_ref, wd_ref, gf_ref, o_ref, h_scr, acc = refs
    else:
        x_ref, g_ref, wg_ref, wu_ref, wd_ref, o_ref, h_scr, acc = refs
    j = pl.program_id(1)

    @pl.when(j == 0)
    def _():
        h_scr[...] = _rms(x_ref[...], g_ref[...]).astype(bf16)
        acc[...] = jnp.zeros_like(acc)

    h = h_scr[...]
    a = _silu(_dot(h, wg_ref[...])) * _dot(h, wu_ref[...])
    acc[...] += _dot(a.astype(bf16), wd_ref[...])

    @pl.when(j == nj - 1)
    def _():
        y = x_ref[...] + 0.5 * acc[...]
        if final:
            y = _rms(y, gf_ref[...])
        o_ref[...] = y


def _ffn(x, g, wg, wu, wd, final_g=None):
    T, D = x.shape
    FF = wg.shape[1]
    TM = min(512, T)
    TF = min(1024, FF)
    nj = FF // TF
    final = final_g is not None
    in_specs = [pl.BlockSpec((TM, D), lambda i, j: (i, 0)),
                pl.BlockSpec((1, D), lambda i, j: (0, 0)),
                pl.BlockSpec((D, TF), lambda i, j: (0, j)),
                pl.BlockSpec((D, TF), lambda i, j: (0, j)),
                pl.BlockSpec((TF, D), lambda i, j: (j, 0))]
    args = [x, g.reshape(1, D), wg, wu, wd]
    if final:
        in_specs.append(pl.BlockSpec((1, D), lambda i, j: (0, 0)))
        args.append(final_g.reshape(1, D))
    est = 2 * (2 * TM * D * 4) + 2 * 3 * D * TF * 2 + TM * D * 6 + 3 * TM * TF * 4
    return pl.pallas_call(
        functools.partial(_ffn_body, nj=nj, final=final),
        grid=(T // TM, nj),
        in_specs=in_specs,
        out_specs=pl.BlockSpec((TM, D), lambda i, j: (i, 0)),
        out_shape=jax.ShapeDtypeStruct((T, D), f32),
        scratch_shapes=[pltpu.VMEM((TM, D), bf16), pltpu.VMEM((TM, D), f32)],
        compiler_params=_params(("parallel", "arbitrary"), est),
        name="ffn",
    )(*args)


def _log_sigmoid(x):
    return jnp.minimum(x, 0.0) - jnp.log(1.0 + jnp.exp(-jnp.abs(x)))


def _norm_proj_body(*refs, outs, has_bias):
    x_ref, g_ref, w_ref = refs[:3]
    pos = 3
    b_ref = None
    if has_bias:
        b_ref = refs[pos]
        pos += 1
    o_refs = refs[pos:]
    h = _rms(x_ref[...], g_ref[...]).astype(bf16)
    cache = {}
    for o_ref, (off, n, dt, scale, act) in zip(o_refs, outs):
        if (off, n) not in cache:
            cache[(off, n)] = _dot(h, w_ref[:, off:off + n])
        r = cache[(off, n)]
        if act == "logsig":
            r = _log_sigmoid(r + b_ref[...])
        if scale != 1.0:
            r = r * scale
        o_ref[...] = r.astype(dt)


def _norm_proj(x, g, w, outs, bias=None):
    T, D = x.shape
    N = w.shape[1]
    TM = min(512, T)
    in_specs = [pl.BlockSpec((TM, D), lambda i: (i, 0)),
                pl.BlockSpec((1, D), lambda i: (0, 0)),
                pl.BlockSpec((D, N), lambda i: (0, 0))]
    args = [x, g.reshape(1, D), w]
    if bias is not None:
        in_specs.append(pl.BlockSpec((1, bias.shape[-1]), lambda i: (0, 0)))
        args.append(bias)
    out_specs = [pl.BlockSpec((TM, n), lambda i: (i, 0)) for (_, n, _, _, _) in outs]
    out_shape = [jax.ShapeDtypeStruct((T, n), dt) for (_, n, dt, _, _) in outs]
    est = 2 * TM * D * 4 + 2 * D * N * 2 + sum(2 * TM * n * 4 for (_, n, _, _, _) in outs) + TM * N * 4
    return pl.pallas_call(
        functools.partial(_norm_proj_body, outs=tuple(outs), has_bias=bias is not None),
        grid=(T // TM,),
        in_specs=in_specs,
        out_specs=out_specs,
        out_shape=out_shape,
        compiler_params=_params(("parallel",), est),
        name="norm_proj",
    )(*args)


def _proj_res_body(*refs, nparts):
    x_ref = refs[0]
    o_ref = refs[1 + 2 * nparts]
    y = x_ref[...]
    for p in range(nparts):
        y = y + _dot(refs[1 + 2 * p][...], refs[2 + 2 * p][...])
    o_ref[...] = y


def _proj_res(x, parts):
    T, D = x.shape
    TM = min(512, T)
    in_specs = [pl.BlockSpec((TM, D), lambda i: (i, 0))]
    args = [x]
    est = 4 * TM * D * 4
    for a, w in parts:
        K = a.shape[1]
        in_specs += [pl.BlockSpec((TM, K), lambda i: (i, 0)), pl.BlockSpec((K, D), lambda i: (0, 0))]
        args += [a, w]
        est += 2 * TM * K * 2 + 2 * K * D * 2
    return pl.pallas_call(
        functools.partial(_proj_res_body, nparts=len(parts)),
        grid=(T // TM,),
        in_specs=in_specs,
        out_specs=pl.BlockSpec((TM, D), lambda i: (i, 0)),
        out_shape=jax.ShapeDtypeStruct((T, D), f32),
        compiler_params=_params(("parallel",), est),
        name="proj_res",
    )(*args)


def _cumsum_body(x_ref, f_ref, ft_ref, carry, *, nsub):
    @pl.when(pl.program_id(0) == 0)
    def _():
        carry[...] = jnp.zeros_like(carry)

    r = lax.broadcasted_iota(jnp.int32, (V7X_LANES, V7X_LANES), 0)
    c = lax.broadcasted_iota(jnp.int32, (V7X_LANES, V7X_LANES), 1)
    tri = (r >= c).astype(f32)
    run = carry[...]
    ncol = x_ref.shape[1] // V7X_LANES
    for sb in range(nsub):
        rows = slice(sb * V7X_LANES, (sb + 1) * V7X_LANES)
        blk = _dot_hi(tri, x_ref[rows, :]) + run
        f_ref[rows, :] = blk
        for cb in range(ncol):
            cols = slice(cb * V7X_LANES, (cb + 1) * V7X_LANES)
            ft_ref[cols, rows] = blk[:, cols].T
        run = blk[V7X_LANES - 1:V7X_LANES, :]
    carry[...] = run


def _cumsum_rows(x):
    L, N = x.shape
    nsub = 8 if L % (8 * V7X_LANES) == 0 else 1
    TB = nsub * V7X_LANES
    return pl.pallas_call(
        functools.partial(_cumsum_body, nsub=nsub),
        grid=(L // TB,),
        in_specs=[pl.BlockSpec((TB, N), lambda i: (i, 0))],
        out_specs=[pl.BlockSpec((TB, N), lambda i: (i, 0)), pl.BlockSpec((N, TB), lambda i: (0, i))],
        out_shape=[jax.ShapeDtypeStruct((L, N), f32), jax.ShapeDtypeStruct((N, L), f32)],
        scratch_shapes=[pltpu.VMEM((1, N), f32)],
        compiler_params=_params(("arbitrary",), 8 * TB * N * 4),
        name="cumsum_rows",
    )(x)


BAND_TQ = 256
BAND_NB = 3


def _band_body(q_ref, k0_ref, k1_ref, k2_ref, v0_ref, v1_ref, v2_ref, b_ref, o_ref, *, nheads):
    i = pl.program_id(0)
    TQ = BAND_TQ
    lane = lax.broadcasted_iota(jnp.int32, (TQ, V7X_LANES), 1)
    lo = lane < 64
    krefs = (k0_ref, k1_ref, k2_ref)
    vrefs = (v0_ref, v1_ref, v2_ref)
    pens = [jnp.where(i - (BAND_NB - 1) + b < 0, NEG, 0.0).astype(f32) for b in range(BAND_NB)]
    for hp in range(nheads // 2):
        cols = slice(hp * V7X_LANES, (hp + 1) * V7X_LANES)
        qp = q_ref[:, cols]
        kp = [kr[:, cols] for kr in krefs]
        vp = [vr[:, cols] for vr in vrefs]
        res = []
        for half in range(2):
            h = 2 * hp + half
            qm = jnp.where(lo if half == 0 else jnp.logical_not(lo), qp, jnp.zeros_like(qp))
            s = [_dot_nt(qm, kp[b]) + b_ref[h, :, b * TQ:(b + 1) * TQ] + pens[b] for b in range(BAND_NB)]
            m = jnp.max(s[0], axis=-1, keepdims=True)
            for b in range(1, BAND_NB):
                m = jnp.maximum(m, jnp.max(s[b], axis=-1, keepdims=True))
            p = [jnp.exp(sb - m) for sb in s]
            l = p[0].sum(axis=-1, keepdims=True)
            o = _dot(p[0].astype(bf16), vp[0])
            for b in range(1, BAND_NB):
                l = l + p[b].sum(axis=-1, keepdims=True)
                o = o + _dot(p[b].astype(bf16), vp[b])
            res.append(o / l)
        o_ref[:, cols] = jnp.where(lo, res[0], res[1]).astype(bf16)


def _toeplitz_bias(table, nq, nk, off):
    H = table.shape[0]
    m = np.arange(nq + nk - 1)
    idx = np.clip(off + nq - 1 - m, -REL_CLIP, REL_CLIP) + REL_CLIP
    w = jnp.concatenate([table[:, idx].astype(f32), jnp.zeros((H, 1), f32)], axis=-1)
    skew = jnp.tile(w, (1, nq))[:, :nq * (nq + nk - 1)].reshape(H, nq, nq + nk - 1)
    return skew[:, :, nq - 1:]


def _band_bias(table):
    qc = np.arange(BAND_TQ)[:, None] // CHUNK
    kc = np.arange(BAND_NB * BAND_TQ)[None, :] // CHUNK
    valid = (kc >= qc) & (kc <= qc + BAND_CHUNKS)
    bias = _toeplitz_bias(table, BAND_TQ, BAND_NB * BAND_TQ, BAND_CHUNKS * CHUNK)
    return jnp.where(valid[None], bias, NEG)


def _band_prompt(q, k, v, bias):
    L, W = q.shape
    TQ = BAND_TQ
    H = bias.shape[0]
    kspec = [pl.BlockSpec((TQ, W), functools.partial(lambda i, d: (jnp.maximum(i - d, 0), 0), d=d))
             for d in (2, 1, 0)]
    est = 2 * 7 * TQ * W * 2 + 2 * H * TQ * BAND_NB * TQ * 4 + 16 * TQ * TQ * 4
    return pl.pallas_call(
        functools.partial(_band_body, nheads=H),
        grid=(L // TQ,),
        in_specs=[pl.BlockSpec((TQ, W), lambda i: (i, 0))] + kspec + kspec
                 + [pl.BlockSpec((H, TQ, BAND_NB * TQ), lambda i: (0, 0, 0))],
        out_specs=pl.BlockSpec((TQ, W), lambda i: (i, 0)),
        out_shape=jax.ShapeDtypeStruct((L, W), bf16),
        compiler_params=_params(("parallel",), est),
        name="band_prompt",
    )(q, k, k, k, v, v, v, bias)


FOX_T = 512
FOX_VROWS = 80
FOX_QPIECE = 64
FOX_KPIECE = 67
LOG2E = 1.4426950408889634


def _fox_pack_body(q_ref, k_ref, v_ref, f_ref, e_ref, qa_ref, ka_ref, vt_ref, *, nheads, dh):
    T, W2 = q_ref.shape
    a = (f_ref[...] - f_ref[0:1, :]) * LOG2E
    hi = a.astype(bf16)
    r1 = a - hi.astype(f32)
    mid = r1.astype(bf16)
    lo = (r1 - mid.astype(f32)).astype(bf16)
    aug = _dot(jnp.concatenate([hi, mid, lo], axis=1), e_ref[...])
    lane = lax.broadcasted_iota(jnp.int32, (T, W2), 1) % V7X_LANES
    ones_q = jnp.where((lane >= FOX_KPIECE) & (lane < FOX_KPIECE + 3), 1.0, 0.0)
    ones_k = jnp.where((lane >= FOX_QPIECE) & (lane < FOX_QPIECE + 3), 1.0, 0.0)
    qa_ref[...] = (q_ref[...].astype(f32) + aug[:, :W2] + ones_q).astype(bf16)
    ka_ref[...] = (k_ref[...].astype(f32) + aug[:, W2:] + ones_k).astype(bf16)
    vT = v_ref[...].astype(f32).T
    pad = FOX_VROWS - dh
    ones_row = jnp.where(lax.broadcasted_iota(jnp.int32, (pad, T), 0) == 0, 1.0, 0.0).astype(bf16)
    for h in range(nheads):
        vt_ref[h, 0:dh, :] = vT[h * dh:(h + 1) * dh, :].astype(bf16)
        vt_ref[h, dh:FOX_VROWS, :] = ones_row


def _fox_pack(q, k, v, F, nheads):
    L, W2 = q.shape
    W = v.shape[1]
    dh = W // nheads
    T = FOX_T
    e = np.zeros((3 * V7X_LANES, 2 * W2), np.float32)
    for h in range(nheads):
        for t in range(3):
            e[t * V7X_LANES + h, h * V7X_LANES + FOX_QPIECE + t] = 1.0
            e[t * V7X_LANES + h, W2 + h * V7X_LANES + FOX_KPIECE + t] = -1.0
    est = 2 * (4 * T * W2 * 2 + T * W * 2 + nheads * FOX_VROWS * T * 2) + 6 * T * W2 * 4
    return pl.pallas_call(
        functools.partial(_fox_pack_body, nheads=nheads, dh=dh),
        grid=(L // T,),
        in_specs=[pl.BlockSpec((T, W2), lambda i: (i, 0)),
                  pl.BlockSpec((T, W2), lambda i: (i, 0)),
                  pl.BlockSpec((T, W), lambda i: (i, 0)),
                  pl.BlockSpec((T, V7X_LANES), lambda i: (i, 0)),
                  pl.BlockSpec((3 * V7X_LANES, 2 * W2), lambda i: (0, 0))],
        out_specs=[pl.BlockSpec((T, W2), lambda i: (i, 0)),
                   pl.BlockSpec((T, W2), lambda i: (i, 0)),
                   pl.BlockSpec((nheads, FOX_VROWS, T), lambda i: (0, 0, i))],
        out_shape=[jax.ShapeDtypeStruct((L, W2), bf16), jax.ShapeDtypeStruct((L, W2), bf16),
                   jax.ShapeDtypeStruct((nheads, FOX_VROWS, L), bf16)],
        compiler_params=_params(("parallel",), est),
        name="fox_pack",
    )(q, k, v, F, jnp.asarray(e, bf16))


def _fox_body(qi_ref, kj_ref, qa_ref, ka_ref, vt_ref, d_ref, o_ref, m_scr, acc_scr, *, nheads, dh):
    s_id = pl.program_id(0)
    i = qi_ref[s_id]
    j = kj_ref[s_id]
    T = FOX_T

    @pl.when(j == 0)
    def _():
        m_scr[...] = jnp.full_like(m_scr, NEG)
        acc_scr[...] = jnp.zeros_like(acc_scr)

    def step(masked):
        if masked:
            keep = lax.broadcasted_iota(jnp.int32, (T, T), 0) <= lax.broadcasted_iota(jnp.int32, (T, T), 1)
        for h in range(nheads):
            cols = slice(h * V7X_LANES, (h + 1) * V7X_LANES)
            st = _dot_nt(ka_ref[:, cols], qa_ref[:, cols])
            if masked:
                st = jnp.where(keep, st, NEG)
            d = d_ref[s_id * nheads + h]
            m_prev = m_scr[h:h + 1, :]
            m_new = jnp.maximum(m_prev, jnp.max(st, axis=0, keepdims=True) + d)
            p = jnp.exp2(st - (m_new - d))
            alpha = jnp.exp2(m_prev - m_new)
            acc_scr[h] = acc_scr[h] * alpha + _dot(vt_ref[h], p.astype(bf16))
            m_scr[h:h + 1, :] = m_new

    @pl.when(j < i)
    def _():
        step(False)

    @pl.when(j == i)
    def _():
        step(True)
        for hp in range(nheads // 2):
            a0 = acc_scr[2 * hp]
            a1 = acc_scr[2 * hp + 1]
            o2 = jnp.concatenate([a0[0:dh, :] / a0[dh:dh + 1, :], a1[0:dh, :] / a1[dh:dh + 1, :]], axis=0)
            o_ref[:, hp * V7X_LANES:(hp + 1) * V7X_LANES] = o2.T.astype(bf16)


def _fox_prompt(qa, ka, vt, F, nheads):
    L, W2 = qa.shape
    T = FOX_T
    n = L // T
    dh = V7X_LANES // 2
    W = nheads * dh
    pairs = [(i, j) for i in range(n) for j in range(i + 1)]
    qi = np.array([p[0] for p in pairs], np.int32)
    kj = np.array([p[1] for p in pairs], np.int32)
    fs = F[::T, :nheads]
    d = ((fs[qi] - fs[kj]) * LOG2E).reshape(-1)
    grid_spec = pltpu.PrefetchScalarGridSpec(
        num_scalar_prefetch=2,
        grid=(len(pairs),),
        in_specs=[pl.BlockSpec((T, W2), lambda s, qi, kj: (qi[s], 0)),
                  pl.BlockSpec((T, W2), lambda s, qi, kj: (kj[s], 0)),
                  pl.BlockSpec((nheads, FOX_VROWS, T), lambda s, qi, kj: (0, 0, kj[s])),
                  pl.BlockSpec(memory_space=pltpu.SMEM)],
        out_specs=pl.BlockSpec((T, W), lambda s, qi, kj: (qi[s], 0)),
        scratch_shapes=[pltpu.VMEM((nheads, T), f32), pltpu.VMEM((nheads, FOX_VROWS, T), f32)],
    )
    est = 2 * (2 * T * W2 * 2 + nheads * FOX_VROWS * T * 2 + T * W * 2) + nheads * FOX_VROWS * T * 4 + 24 * T * T * 4
    return pl.pallas_call(
        functools.partial(_fox_body, nheads=nheads, dh=dh),
        grid_spec=grid_spec,
        out_shape=jax.ShapeDtypeStruct((L, W), bf16),
        compiler_params=_params(("arbitrary",), est),
        name="fox_prompt",
    )(jnp.asarray(qi), jnp.asarray(kj), qa, ka, vt, d)


def _cross_body(q_ref, k_ref, v_ref, o_ref, *, nheads, dh):
    for h in range(nheads):
        cols = slice(h * dh, (h + 1) * dh)
        s = _dot_nt(q_ref[:, cols], k_ref[:, cols])
        m = jnp.max(s, axis=-1, keepdims=True)
        p = jnp.exp(s - m)
        l = p.sum(axis=-1, keepdims=True)
        o_ref[:, cols] = (_dot(p.astype(bf16), v_ref[:, cols]) / l).astype(bf16)


def _cross_prompt(q, mk, mv, nheads):
    L, W = q.shape
    S = mk.shape[0]
    TM = min(512, L)
    return pl.pallas_call(
        functools.partial(_cross_body, nheads=nheads, dh=W // nheads),
        grid=(L // TM,),
        in_specs=[pl.BlockSpec((TM, W), lambda i: (i, 0)),
                  pl.BlockSpec((S, W), lambda i: (0, 0)),
                  pl.BlockSpec((S, W), lambda i: (0, 0))],
        out_specs=pl.BlockSpec((TM, W), lambda i: (i, 0)),
        out_shape=jax.ShapeDtypeStruct((L, W), bf16),
        compiler_params=_params(("parallel",), 8 * TM * W * 2 + 8 * TM * S * 4),
        name="cross_prompt",
    )(q, mk, mv)


def _sample_attn_body(*refs, nheads, mode, has_new):
    it = iter(refs)
    q_ref = next(it)
    kc_ref = next(it)
    vc_ref = next(it)
    kn_ref = vn_ref = None
    if has_new:
        kn_ref = next(it)
        vn_ref = next(it)
    if mode == "rel":
        bc_ref = next(it)
        bn_ref = next(it)
    elif mode == "fox":
        fq_ref = next(it)
        fkc_ref = next(it)
        fkn_ref = next(it)
    o_ref = next(it)

    m_q, W = q_ref.shape
    dh = W // nheads
    R = nheads * m_q
    q = q_ref[...].astype(f32)
    qt = jnp.concatenate([q] * nheads, axis=0)
    hrow = lax.broadcasted_iota(jnp.int32, (R, W), 0) // m_q
    hlane = lax.broadcasted_iota(jnp.int32, (R, W), 1) // dh
    qbd = jnp.where(hrow == hlane, qt, 0.0).astype(bf16)

    def rows_of(f_ref):
        n = f_ref.shape[-1]
        return jnp.concatenate([jnp.broadcast_to(f_ref[h:h + 1, :], (m_q, n)) for h in range(nheads)], axis=0)

    sc = _dot_nt(qbd, kc_ref[...].astype(bf16))
    if mode == "rel":
        sc = sc + bc_ref[...]
    elif mode == "fox":
        sc = sc + (fq_ref[...] - rows_of(fkc_ref))
    m = jnp.max(sc, axis=-1, keepdims=True)
    if has_new:
        sn = _dot_nt(qbd, kn_ref[...].astype(bf16))
        if mode == "rel":
            sn = sn + bn_ref[...]
        elif mode == "fox":
            sn = sn + (fq_ref[...] - rows_of(fkn_ref))
            qpos = lax.broadcasted_iota(jnp.int32, (R, m_q), 0) % m_q
            kpos = lax.broadcasted_iota(jnp.int32, (R, m_q), 1)
            sn = jnp.where(kpos <= qpos, sn, NEG)
        m = jnp.maximum(m, jnp.max(sn, axis=-1, keepdims=True))
    pc = jnp.exp(sc - m)
    l = pc.sum(axis=-1, keepdims=True)
    o = _dot(pc.astype(bf16), vc_ref[...].astype(bf16))
    if has_new:
        pn = jnp.exp(sn - m)
        l = l + pn.sum(axis=-1, keepdims=True)
        o = o + _dot(pn.astype(bf16), vn_ref[...].astype(bf16))
    o = o / l
    hl = lax.broadcasted_iota(jnp.int32, (m_q, W), 1) // dh
    out = jnp.zeros((m_q, W), f32)
    for h in range(nheads):
        out = out + jnp.where(hl == h, o[h * m_q:(h + 1) * m_q, :], 0.0)
    o_ref[...] = out.astype(bf16)


def _sample_attn(q, kc, vc, nheads, mode="none", kn=None, vn=None, extra=()):
    B, m_q, W = q.shape
    P = kc.shape[1]
    R = nheads * m_q
    has_new = kn is not None
    per_b = lambda *shape: pl.BlockSpec((None,) + shape, lambda b: (b,) + (0,) * len(shape))
    shared = lambda *shape: pl.BlockSpec(shape, lambda b: (0,) * len(shape))
    in_specs = [per_b(m_q, W), per_b(P, W), per_b(P, W)]
    args = [q, kc, vc]
    if has_new:
        in_specs += [per_b(m_q, W), per_b(m_q, W)]
        args += [kn, vn]
    if mode == "rel":
        in_specs += [shared(R, P), shared(R, m_q)]
    elif mode == "fox":
        in_specs += [per_b(R, 1), per_b(nheads, P), per_b(nheads, m_q)]
    args += list(extra)
    est = 2 * 2 * P * W * 4 + 2 * P * W * 2 + 6 * R * P * 4 + 4 * R * W * 4
    return pl.pallas_call(
        functools.partial(_sample_attn_body, nheads=nheads, mode=mode, has_new=has_new),
        grid=(B,),
        in_specs=in_specs,
        out_specs=per_b(m_q, W),
        out_shape=jax.ShapeDtypeStruct((B, m_q, W), bf16),
        compiler_params=_params(("parallel",), est),
        name="sample_attn_" + mode,
    )(*args)


GDN_HALO = 8


def _softplus(x):
    return jnp.maximum(x, 0.0) + jnp.log(1.0 + jnp.exp(-jnp.abs(x)))


def _dot3(a, b):
    ah = a.astype(bf16)
    al = (a - ah.astype(f32)).astype(bf16)
    bh = b.astype(bf16)
    bl = (b - bh.astype(f32)).astype(bf16)
    return _dot(ah, bh) + (_dot(ah, bl) + _dot(al, bh))


def _unit_lower_inverse(Ms, C):
    r = lax.broadcasted_iota(jnp.int32, (C, C), 0)
    c = lax.broadcasted_iota(jnp.int32, (C, C), 1)
    eye = jnp.where(r == c, 1.0, 0.0).astype(f32)
    Xs = [eye - M for M in Ms]
    Pws = [_dot3(M, M) for M in Ms]
    e = 2
    while e < C:
        Xs = [X + _dot3(X, Pw) for X, Pw in zip(Xs, Pws)]
        e *= 2
        if e < C:
            Pws = [_dot3(Pw, Pw) for Pw in Pws]
    return Xs


def _gdn_body(x_ref, ab_ref, z_ref, cs_ref, s0_ref, cw_ref, alog_ref, dtb_ref, ng_ref,
              o_ref, sf_ref, cf_ref, xbuf, s_scr, *, C, nchunks, nheads, dk):
    c = pl.program_id(1)
    H0 = GDN_HALO
    W = nheads * dk

    @pl.when(c == 0)
    def _():
        xbuf[H0 - (CONV - 1):H0, :] = cs_ref[...]
        s_scr[...] = s0_ref[...]

    xbuf[H0:H0 + C, :] = x_ref[...]
    cw = cw_ref[...]
    conv = xbuf[H0 - 3:H0 - 3 + C, :] * cw[0:1, :]
    for jj in range(1, CONV):
        conv = conv + xbuf[H0 - 3 + jj:H0 - 3 + jj + C, :] * cw[jj:jj + 1, :]
    tail = xbuf[H0 + C - (CONV - 1):H0 + C, :]
    xbuf[H0 - (CONV - 1):H0, :] = tail

    @pl.when(c == nchunks - 1)
    def _():
        cf_ref[...] = tail

    act = _silu(conv)
    ab = ab_ref[...]
    gfull = -jnp.exp(alog_ref[...]) * _softplus(ab + dtb_ref[...])
    bfull = 1.0 / (1.0 + jnp.exp(-ab))
    r = lax.broadcasted_iota(jnp.int32, (C, C), 0)
    cc = lax.broadcasted_iota(jnp.int32, (C, C), 1)
    tri = r >= cc
    strict = r > cc
    gc = _dot_hi(tri.astype(f32), gfull)
    gcT = gc.T
    egc = jnp.exp(gc)
    z = z_ref[...]
    ng = ng_ref[...]

    hs = range(nheads)
    qs = [act[:, h * dk:(h + 1) * dk] for h in hs]
    ks = [act[:, W + h * dk:W + (h + 1) * dk] for h in hs]
    vs = [act[:, 2 * W + h * dk:2 * W + (h + 1) * dk] for h in hs]
    qs = [q * lax.rsqrt(jnp.sum(q * q, axis=-1, keepdims=True) + EPS) * (dk ** -0.5) for q in qs]
    ks = [k * lax.rsqrt(jnp.sum(k * k, axis=-1, keepdims=True) + EPS) for k in ks]
    gcols = [gc[:, h:h + 1] for h in hs]
    Lms = [jnp.where(tri, jnp.exp(jnp.where(tri, gcols[h] - gcT[h:h + 1, :], 0.0)), 0.0) for h in hs]
    bcols = [bfull[:, nheads + h:nheads + h + 1] for h in hs]
    ecols = [egc[:, h:h + 1] for h in hs]
    glasts = [gc[C - 1:C, h:h + 1] for h in hs]
    kbs = [ks[h] * bcols[h] for h in hs]
    kbfs = [k.astype(bf16) for k in ks]
    Ms = [jnp.where(strict, _dot_nt(kbs[h].astype(bf16), kbfs[h]) * Lms[h], 0.0) for h in hs]
    Aqks = [jnp.where(tri, _dot_nt(qs[h].astype(bf16), kbfs[h]) * Lms[h], 0.0).astype(bf16) for h in hs]
    Tms = [X.astype(bf16) for X in _unit_lower_inverse(Ms, C)]
    us = [_dot(Tms[h], (vs[h] * bcols[h]).astype(bf16)) for h in hs]
    ws = [_dot(Tms[h], (kbs[h] * ecols[h]).astype(bf16)).astype(bf16) for h in hs]
    qgs = [(qs[h] * ecols[h]).astype(bf16) for h in hs]
    kgTs = [(ks[h] * jnp.exp(glasts[h] - gcols[h])).T.astype(bf16) for h in hs]
    Ss = [s_scr[h] for h in hs]
    Sbs = [S.astype(bf16) for S in Ss]
    vnbs = [(us[h] - _dot(ws[h], Sbs[h])).astype(bf16) for h in hs]
    os_ = [_dot(qgs[h], Sbs[h]) + _dot(Aqks[h], vnbs[h]) for h in hs]
    Sn = [Ss[h] * jnp.exp(glasts[h]) + _dot(kgTs[h], vnbs[h]) for h in hs]
    for h in hs:
        s_scr[h] = Sn[h]
    ys = [(_rms(os_[h], ng) * _silu(z[:, h * dk:(h + 1) * dk])).astype(bf16) for h in hs]
    o_ref[...] = jnp.concatenate(ys, axis=-1)

    @pl.when(c == nchunks - 1)
    def _():
        sf_ref[...] = s_scr[...]


def _gdn_core(qkv, ab, z, conv_state, S0, conv_w, a_log, dt_bias, norm_g, C):
    T = qkv.shape[0]
    B, H, dk, dv = S0.shape
    W = H * dk
    nchunks = T // (B * C)
    pad = lambda vec: jnp.zeros((1, V7X_LANES), f32).at[0, :H].set(vec.astype(f32))
    row = lambda width: pl.BlockSpec((C, width), lambda b, c: (b * nchunks + c, 0))
    est = 2 * C * (3 * W + W + V7X_LANES) * 4 + 3 * H * dk * dv * 4 * 2 + 12 * C * 3 * W * 4
    return pl.pallas_call(
        functools.partial(_gdn_body, C=C, nchunks=nchunks, nheads=H, dk=dk),
        grid=(B, nchunks),
        in_specs=[row(3 * W), row(V7X_LANES), row(W),
                  pl.BlockSpec((None, CONV - 1, 3 * W), lambda b, c: (b, 0, 0)),
                  pl.BlockSpec((None, H, dk, dv), lambda b, c: (b, 0, 0, 0)),
                  pl.BlockSpec((CONV, 3 * W), lambda b, c: (0, 0)),
                  pl.BlockSpec((1, V7X_LANES), lambda b, c: (0, 0)),
                  pl.BlockSpec((1, V7X_LANES), lambda b, c: (0, 0)),
                  pl.BlockSpec((1, dv), lambda b, c: (0, 0))],
        out_specs=[row(W),
                   pl.BlockSpec((None, H, dk, dv), lambda b, c: (b, 0, 0, 0)),
                   pl.BlockSpec((None, CONV - 1, 3 * W), lambda b, c: (b, 0, 0))],
        out_shape=[jax.ShapeDtypeStruct((T, W), bf16),
                   jax.ShapeDtypeStruct((B, H, dk, dv), f32),
                   jax.ShapeDtypeStruct((B, CONV - 1, 3 * W), f32)],
        scratch_shapes=[pltpu.VMEM((GDN_HALO + C, 3 * W), f32), pltpu.VMEM((H, dk, dv), f32)],
        compiler_params=_params(("parallel", "arbitrary"), est),
        name="gdn_core",
    )(qkv, ab, z, conv_state, S0, conv_w, pad(a_log), pad(dt_bias), norm_g.reshape(1, dv))


def kernel(x_prompt, x_sample, mem_prompt, cache_a_k, cache_a_v, cache_b_k, cache_b_v, cache_b_logf, state_gdn, state_gdn_conv, cache_mem_k, cache_mem_v, norm_g, mem_norm_g, final_norm_g, ffn_w_gate, ffn_w_up, ffn_w_down, xa_w_q, xa_w_k, xa_w_v, xa_w_o, ab_w_in, ab_b_f, ab_rel_bias, ab_w_o, gdn_w_in, gdn_conv_w, gdn_a_log, gdn_dt_bias, gdn_norm_g, gdn_w_o):
    depth = norm_g.shape[0]
    BP, SEQ, D = x_prompt.shape
    BS, MS, _ = x_sample.shape
    assert BP == 1
    HA = HB = ab_b_f.shape[1]
    WA = WB = (ab_w_in.shape[2] - HB) // 6
    DHA = WA // HA
    HG = gdn_a_log.shape[1]
    WG = gdn_w_o.shape[1]
    DKG = WG // HG
    HX = cache_mem_k.shape[3]
    DHX = cache_mem_k.shape[4]
    WX = HX * DHX
    PA = cache_a_k.shape[2]
    PB = cache_b_k.shape[2]
    LANES = V7X_LANES

    xp = x_prompt.reshape(SEQ, D)
    xs = x_sample.reshape(BS * MS, D)

    wgate = ffn_w_gate.astype(bf16)
    wup = ffn_w_up.astype(bf16)
    wdown = ffn_w_down.astype(bf16)
    wxq = xa_w_q.astype(bf16)
    wxo = xa_w_o.astype(bf16)
    wxkv = jnp.concatenate([xa_w_k, xa_w_v], axis=-1).astype(bf16)

    outs = {k: [] for k in ("akp", "avp", "bkp", "bvp", "blp", "sgp", "scp", "mkp", "mvp",
                            "aks", "avs", "bks", "bvs", "bls", "sgs", "scs")}

    for l in range(depth):
        xp = _ffn(xp, norm_g[l, 0], wgate[l, 0], wup[l, 0], wdown[l, 0])
        xs = _ffn(xs, norm_g[l, 0], wgate[l, 0], wup[l, 0], wdown[l, 0])

        if l % 2 == 0:
            e = l // 2
            w_in = ab_w_in[e]
            nq = 3 * WA + 3 * WB
            w_pad = jnp.concatenate([w_in, jnp.zeros((D, LANES - HB), f32)], axis=-1).astype(bf16)
            b_pad = jnp.zeros((1, LANES), f32).at[0, :HB].set(ab_b_f[e].astype(f32))
            spec = [(0, WA, bf16, DHA ** -0.5, None),
                    (WA, WA, f32, 1.0, None), (WA, WA, bf16, 1.0, None),
                    (2 * WA, WA, f32, 1.0, None), (2 * WA, WA, bf16, 1.0, None),
                    (3 * WA, WB, bf16, (WB // HB) ** -0.5, None),
                    (3 * WA + WB, WB, f32, 1.0, None), (3 * WA + WB, WB, bf16, 1.0, None),
                    (3 * WA + 2 * WB, WB, f32, 1.0, None), (3 * WA + 2 * WB, WB, bf16, 1.0, None),
                    (nq, LANES, f32, 1.0, "logsig")]
            wo = ab_w_o[e].astype(bf16)

            qa, ka, kab, va, vab = _norm_proj(xp, norm_g[l, 1], w_pad[:, :3 * WA], spec[:5])
            oa = _band_prompt(qa, kab, vab, _band_bias(ab_rel_bias[e]))
            DHB = WB // HB
            spread = lambda w: jnp.pad(w.reshape(D, HB, DHB), ((0, 0), (0, 0), (0, LANES - DHB))).reshape(D, HB * LANES)
            wq_b = w_in[:, 3 * WA:3 * WA + WB]
            wk_b = w_in[:, 3 * WA + WB:3 * WA + 2 * WB]
            w_b = jnp.concatenate([spread(wq_b), wk_b, spread(wk_b), w_in[:, 3 * WA + 2 * WB:nq],
                                   w_pad[:, nq:]], axis=-1).astype(bf16)
            W2 = HB * LANES
            spec_b = [(0, W2, bf16, DHB ** -0.5 * LOG2E, None),
                      (W2, WB, f32, 1.0, None), (W2 + WB, W2, bf16, 1.0, None),
                      (2 * W2 + WB, WB, f32, 1.0, None), (2 * W2 + WB, WB, bf16, 1.0, None),
                      (2 * W2 + 2 * WB, LANES, f32, 1.0, "logsig")]
            qbs, kb, kbs, vb, vbb, lf = _norm_proj(xp, norm_g[l, 1], w_b, spec_b, bias=b_pad)
            F, _ = _cumsum_rows(lf)
            qaug, kaug, vt = _fox_pack(qbs, kbs, vbb, F, HB)
            ob = _fox_prompt(qaug, kaug, vt, F, HB)
            xp = _proj_res(xp, [(oa, wo[:WA]), (ob, wo[WA:])])
            keep = min(BAND_CHUNKS * CHUNK, SEQ)
            outs["akp"].append(ka[SEQ - keep:].reshape(1, keep, HA, DHA))
            outs["avp"].append(va[SEQ - keep:].reshape(1, keep, HA, DHA))
            outs["bkp"].append(kb.reshape(1, SEQ, HB, WB // HB))
            outs["bvp"].append(vb.reshape(1, SEQ, HB, WB // HB))
            outs["blp"].append(lf[:, :HB].reshape(1, SEQ, HB))

            qa, ka, _, va, _, qb, kb, _, vb, _, lf = _norm_proj(xs, norm_g[l, 1], w_pad, spec, bias=b_pad)
            bias_s = _toeplitz_bias(ab_rel_bias[e], MS, PA + MS, PA).reshape(HA * MS, PA + MS)
            oa = _sample_attn(qa.reshape(BS, MS, WA), cache_a_k[e].reshape(BS, PA, WA), cache_a_v[e].reshape(BS, PA, WA),
                              HA, "rel", ka.reshape(BS, MS, WA), va.reshape(BS, MS, WA),
                              extra=(bias_s[:, :PA], bias_s[:, PA:]))
            lfn = lf[:, :HB].reshape(BS, MS, HB)
            lcat = jnp.concatenate([cache_b_logf[e].astype(f32), lfn], axis=1)
            LP = -(-(PB + MS) // LANES) * LANES
            lcat = jnp.pad(lcat.transpose(1, 0, 2).reshape(PB + MS, BS * HB), ((0, LP - PB - MS), (0, 0)))
            _, FTs = _cumsum_rows(lcat)
            FTs = FTs.reshape(BS, HB, LP)
            fq = FTs[:, :, PB:PB + MS].reshape(BS, HB * MS, 1)
            ob = _sample_attn(qb.reshape(BS, MS, WB), cache_b_k[e].reshape(BS, PB, WB), cache_b_v[e].reshape(BS, PB, WB),
                              HB, "fox", kb.reshape(BS, MS, WB), vb.reshape(BS, MS, WB),
                              extra=(fq, FTs[:, :, :PB], FTs[:, :, PB:PB + MS]))
            xs = _proj_res(xs, [(oa.reshape(BS * MS, WA), wo[:WA]), (ob.reshape(BS * MS, WB), wo[WA:])])
            outs["aks"].append(ka.reshape(BS, MS, HA, DHA))
            outs["avs"].append(va.reshape(BS, MS, HA, DHA))
            outs["bks"].append(kb.reshape(BS, MS, HB, WB // HB))
            outs["bvs"].append(vb.reshape(BS, MS, HB, WB // HB))
            outs["bls"].append(lfn)
        else:
            o = l // 2
            w_in = gdn_w_in[o]
            w_pad = jnp.concatenate([w_in, jnp.zeros((D, LANES - 2 * HG), f32)], axis=-1).astype(bf16)
            spec = [(0, 3 * WG, f32, 1.0, None), (3 * WG, WG, f32, 1.0, None), (4 * WG, LANES, f32, 1.0, None)]
            wo = gdn_w_o[o].astype(bf16)

            qkv, z, ab = _norm_proj(xp, norm_g[l, 1], w_pad, spec)
            y, Sp, cvp = _gdn_core(qkv, ab, z, jnp.zeros((1, CONV - 1, 3 * WG), f32), jnp.zeros((1, HG, DKG, DKG), f32),
                                   gdn_conv_w[o], gdn_a_log[o], gdn_dt_bias[o], gdn_norm_g[o], CHUNK)
            xp = _proj_res(xp, [(y, wo)])
            outs["sgp"].append(Sp)
            outs["scp"].append(cvp)

            qkv, z, ab = _norm_proj(xs, norm_g[l, 1], w_pad, spec)
            y, Ss, cvs = _gdn_core(qkv, ab, z, state_gdn_conv[o].astype(f32), state_gdn[o].astype(f32),
                                   gdn_conv_w[o], gdn_a_log[o], gdn_dt_bias[o], gdn_norm_g[o], MS)
            xs = _proj_res(xs, [(y, wo)])
            outs["sgs"].append(Ss)
            outs["scs"].append(cvs)

        mk, mkb, mv, mvb = _norm_proj(mem_prompt.reshape(MEM, D), mem_norm_g[l], wxkv[l],
                                      [(0, WX, f32, 1.0, None), (0, WX, bf16, 1.0, None),
                                       (WX, WX, f32, 1.0, None), (WX, WX, bf16, 1.0, None)])
        outs["mkp"].append(mk.reshape(1, MEM, HX, DHX))
        outs["mvp"].append(mv.reshape(1, MEM, HX, DHX))
        qspec = [(0, WX, bf16, DHX ** -0.5, None)]
        (q,) = _norm_proj(xp, norm_g[l, 2], wxq[l], qspec)
        xp = _proj_res(xp, [(_cross_prompt(q, mkb, mvb, HX), wxo[l])])
        (q,) = _norm_proj(xs, norm_g[l, 2], wxq[l], qspec)
        oc = _sample_attn(q.reshape(BS, MS, WX), cache_mem_k[l].reshape(BS, MEM, WX), cache_mem_v[l].reshape(BS, MEM, WX), HX)
        xs = _proj_res(xs, [(oc.reshape(BS * MS, WX), wxo[l])])

        fin = final_norm_g if l == depth - 1 else None
        xp = _ffn(xp, norm_g[l, 3], wgate[l, 1], wup[l, 1], wdown[l, 1], fin)
        xs = _ffn(xs, norm_g[l, 3], wgate[l, 1], wup[l, 1], wdown[l, 1], fin)

    st = lambda k: jnp.stack(outs[k])
    return (xp.reshape(BP, SEQ, D), xs.reshape(BS, MS, D),
            st("akp"), st("avp"), st("bkp"), st("bvp"), st("blp"),
            st("sgp"), st("scp"), st("mkp"), st("mvp"),
            st("aks"), st("avs"), st("bks"), st("bvs"), st("bls"),
            st("sgs"), st("scs"))
```

```python
import functools

import numpy as np
import jax
import jax.numpy as jnp
from jax import lax
from jax.experimental import pallas as pl
from jax.experimental.pallas import tpu as pltpu

f32 = jnp.float32
bf16 = jnp.bfloat16
EPS = 1e-6
NEG = -1e30

V7X_VMEM_BYTES = 64 * 1024 * 1024
V7X_LANES = 128
MIB = 1024 * 1024

CHUNK = 64
BAND_CHUNKS = 8
REL_CLIP = 256
CONV = 4
MEM = 256


def _params(sem, est_bytes):
    limit = int(min(V7X_VMEM_BYTES - 8 * MIB, max(32 * MIB, est_bytes + 8 * MIB)))
    return pltpu.CompilerParams(dimension_semantics=sem, vmem_limit_bytes=limit)


def _rms(x, g):
    ms = jnp.mean(x * x, axis=-1, keepdims=True)
    return x * lax.rsqrt(ms + EPS) * g


def _silu(x):
    return x / (1.0 + jnp.exp(-x))


def _dot(a, b):
    return jnp.dot(a, b, preferred_element_type=f32)


def _dot_nt(a, b):
    return lax.dot_general(a, b, (((1,), (1,)), ((), ())), preferred_element_type=f32)


def _dot_hi(a, b):
    return jnp.dot(a, b, preferred_element_type=f32, precision=lax.Precision.HIGHEST)


def _ffn_body(*refs, nj, final, has_res):
    x_ref, g_ref, wg_ref, wu_ref, wd_ref = refs[:5]
    pos = 5
    if has_res:
        a_ref, wo_ref = refs[pos:pos + 2]
        pos += 2
    if final:
        gf_ref = refs[pos]
        pos += 1
    o_ref, h_scr, acc = refs[pos:]
    j = pl.program_id(1)

    @pl.when(j == 0)
    def _():
        x = x_ref[...]
        if has_res:
            x = x + _dot(a_ref[...], wo_ref[...])
        o_ref[...] = x
        h_scr[...] = _rms(x, g_ref[...]).astype(bf16)
        acc[...] = jnp.zeros_like(acc)

    h = h_scr[...]
    a = _silu(_dot(h, wg_ref[...])) * _dot(h, wu_ref[...])
    acc[...] += _dot(a.astype(bf16), wd_ref[...])

    @pl.when(j == nj - 1)
    def _():
        y = o_ref[...] + 0.5 * acc[...]
        if final:
            y = _rms(y, gf_ref[...])
        o_ref[...] = y


def _ffn(x, g, wg, wu, wd, final_g=None, res=None):
    T, D = x.shape
    FF = wg.shape[1]
    TM = min(1024, T)
    TF = min(1024, FF)
    nj = FF // TF
    final = final_g is not None
    in_specs = [pl.BlockSpec((TM, D), lambda i, j: (i, 0)),
                pl.BlockSpec((1, D), lambda i, j: (0, 0)),
                pl.BlockSpec((D, TF), lambda i, j: (0, j)),
                pl.BlockSpec((D, TF), lambda i, j: (0, j)),
                pl.BlockSpec((TF, D), lambda i, j: (j, 0))]
    args = [x, g.reshape(1, D), wg, wu, wd]
    est = 2 * (2 * TM * D * 4) + 2 * 3 * D * TF * 2 + TM * D * 6 + 3 * TM * TF * 4
    if res is not None:
        a, wo = res
        K = a.shape[1]
        in_specs += [pl.BlockSpec((TM, K), lambda i, j: (i, 0)), pl.BlockSpec((K, D), lambda i, j: (0, 0))]
        args += [a, wo]
        est += 2 * TM * K * 2 + 2 * K * D * 2
    if final:
        in_specs.append(pl.BlockSpec((1, D), lambda i, j: (0, 0)))
        args.append(final_g.reshape(1, D))
    return pl.pallas_call(
        functools.partial(_ffn_body, nj=nj, final=final, has_res=res is not None),
        grid=(T // TM, nj),
        in_specs=in_specs,
        out_specs=pl.BlockSpec((TM, D), lambda i, j: (i, 0)),
        out_shape=jax.ShapeDtypeStruct((T, D), f32),
        scratch_shapes=[pltpu.VMEM((TM, D), bf16), pltpu.VMEM((TM, D), f32)],
        compiler_params=_params(("parallel", "arbitrary"), est),
        name="ffn",
    )(*args)


def _log_sigmoid(x):
    return jnp.minimum(x, 0.0) - jnp.log(1.0 + jnp.exp(-jnp.abs(x)))


def _norm_proj_body(*refs, outs, has_bias, nres):
    x_ref, g_ref, w_ref = refs[:3]
    pos = 3
    b_ref = None
    if has_bias:
        b_ref = refs[pos]
        pos += 1
    x = x_ref[...]
    for p in range(nres):
        x = x + _dot(refs[pos][...], refs[pos + 1][...])
        pos += 2
    o_refs = refs[pos:]
    if nres:
        o_refs[0][...] = x
        o_refs = o_refs[1:]
    h = _rms(x, g_ref[...]).astype(bf16)
    cache = {}
    for o_ref, (off, n, dt, scale, act) in zip(o_refs, outs):
        if (off, n) not in cache:
            cache[(off, n)] = _dot(h, w_ref[:, off:off + n])
        r = cache[(off, n)]
        if act == "logsig":
            r = _log_sigmoid(r + b_ref[...])
        if scale != 1.0:
            r = r * scale
        o_ref[...] = r.astype(dt)


def _norm_proj(x, g, w, outs, bias=None, res=()):
    T, D = x.shape
    N = w.shape[1]
    TM = min(512, T)
    in_specs = [pl.BlockSpec((TM, D), lambda i: (i, 0)),
                pl.BlockSpec((1, D), lambda i: (0, 0)),
                pl.BlockSpec((D, N), lambda i: (0, 0))]
    args = [x, g.reshape(1, D), w]
    if bias is not None:
        in_specs.append(pl.BlockSpec((1, bias.shape[-1]), lambda i: (0, 0)))
        args.append(bias)
    est = 2 * TM * D * 4 + 2 * D * N * 2 + sum(2 * TM * n * 4 for (_, n, _, _, _) in outs) + TM * N * 4
    for a, wr in res:
        K = a.shape[1]
        in_specs += [pl.BlockSpec((TM, K), lambda i: (i, 0)), pl.BlockSpec((K, D), lambda i: (0, 0))]
        args += [a, wr]
        est += 2 * TM * K * 2 + 2 * K * D * 2
    out_specs = [pl.BlockSpec((TM, n), lambda i: (i, 0)) for (_, n, _, _, _) in outs]
    out_shape = [jax.ShapeDtypeStruct((T, n), dt) for (_, n, dt, _, _) in outs]
    if res:
        out_specs.insert(0, pl.BlockSpec((TM, D), lambda i: (i, 0)))
        out_shape.insert(0, jax.ShapeDtypeStruct((T, D), f32))
        est += 2 * TM * D * 4
    return pl.pallas_call(
        functools.partial(_norm_proj_body, outs=tuple(outs), has_bias=bias is not None, nres=len(res)),
        grid=(T // TM,),
        in_specs=in_specs,
        out_specs=out_specs,
        out_shape=out_shape,
        compiler_params=_params(("parallel",), est),
        name="norm_proj",
    )(*args)


def _cumsum_body(x_ref, f_ref, ft_ref, carry, *, nsub):
    @pl.when(pl.program_id(0) == 0)
    def _():
        carry[...] = jnp.zeros_like(carry)

    r = lax.broadcasted_iota(jnp.int32, (V7X_LANES, V7X_LANES), 0)
    c = lax.broadcasted_iota(jnp.int32, (V7X_LANES, V7X_LANES), 1)
    tri = (r >= c).astype(f32)
    run = carry[...]
    ncol = x_ref.shape[1] // V7X_LANES
    for sb in range(nsub):
        rows = slice(sb * V7X_LANES, (sb + 1) * V7X_LANES)
        blk = _dot_hi(tri, x_ref[rows, :]) + run
        f_ref[rows, :] = blk
        for cb in range(ncol):
            cols = slice(cb * V7X_LANES, (cb + 1) * V7X_LANES)
            ft_ref[cols, rows] = blk[:, cols].T
        run = blk[V7X_LANES - 1:V7X_LANES, :]
    carry[...] = run


def _cumsum_rows(x):
    L, N = x.shape
    nsub = 8 if L % (8 * V7X_LANES) == 0 else 1
    TB = nsub * V7X_LANES
    return pl.pallas_call(
        functools.partial(_cumsum_body, nsub=nsub),
        grid=(L // TB,),
        in_specs=[pl.BlockSpec((TB, N), lambda i: (i, 0))],
        out_specs=[pl.BlockSpec((TB, N), lambda i: (i, 0)), pl.BlockSpec((N, TB), lambda i: (0, i))],
        out_shape=[jax.ShapeDtypeStruct((L, N), f32), jax.ShapeDtypeStruct((N, L), f32)],
        scratch_shapes=[pltpu.VMEM((1, N), f32)],
        compiler_params=_params(("arbitrary",), 8 * TB * N * 4),
        name="cumsum_rows",
    )(x)


BAND_TQ = 256
BAND_NB = 3


def _band_body(q_ref, k0_ref, k1_ref, k2_ref, v0_ref, v1_ref, v2_ref, b_ref, o_ref, *, nheads):
    i = pl.program_id(0)
    TQ = BAND_TQ
    lane = lax.broadcasted_iota(jnp.int32, (TQ, V7X_LANES), 1)
    lo = lane < 64
    krefs = (k0_ref, k1_ref, k2_ref)
    vrefs = (v0_ref, v1_ref, v2_ref)
    pens = [jnp.where(i - (BAND_NB - 1) + b < 0, NEG, 0.0).astype(f32) for b in range(BAND_NB)]
    for hp in range(nheads // 2):
        cols = slice(hp * V7X_LANES, (hp + 1) * V7X_LANES)
        qp = q_ref[:, cols]
        kp = [kr[:, cols] for kr in krefs]
        vp = [vr[:, cols] for vr in vrefs]
        res = []
        for half in range(2):
            h = 2 * hp + half
            qm = jnp.where(lo if half == 0 else jnp.logical_not(lo), qp, jnp.zeros_like(qp))
            s = [_dot_nt(qm, kp[b]) + b_ref[h, :, b * TQ:(b + 1) * TQ] + pens[b] for b in range(BAND_NB)]
            m = jnp.max(s[0], axis=-1, keepdims=True)
            for b in range(1, BAND_NB):
                m = jnp.maximum(m, jnp.max(s[b], axis=-1, keepdims=True))
            p = [jnp.exp(sb - m) for sb in s]
            l = p[0].sum(axis=-1, keepdims=True)
            o = _dot(p[0].astype(bf16), vp[0])
            for b in range(1, BAND_NB):
                l = l + p[b].sum(axis=-1, keepdims=True)
                o = o + _dot(p[b].astype(bf16), vp[b])
            res.append(o / l)
        o_ref[:, cols] = jnp.where(lo, res[0], res[1]).astype(bf16)


def _toeplitz_bias(table, nq, nk, off):
    H = table.shape[0]
    m = np.arange(nq + nk - 1)
    idx = np.clip(off + nq - 1 - m, -REL_CLIP, REL_CLIP) + REL_CLIP
    w = jnp.concatenate([table[:, idx].astype(f32), jnp.zeros((H, 1), f32)], axis=-1)
    skew = jnp.tile(w, (1, nq))[:, :nq * (nq + nk - 1)].reshape(H, nq, nq + nk - 1)
    return skew[:, :, nq - 1:]


def _band_bias(table):
    qc = np.arange(BAND_TQ)[:, None] // CHUNK
    kc = np.arange(BAND_NB * BAND_TQ)[None, :] // CHUNK
    valid = (kc >= qc) & (kc <= qc + BAND_CHUNKS)
    bias = _toeplitz_bias(table, BAND_TQ, BAND_NB * BAND_TQ, BAND_CHUNKS * CHUNK)
    return jnp.where(valid[None], bias, NEG)


def _band_prompt(q, k, v, bias):
    L, W = q.shape
    TQ = BAND_TQ
    H = bias.shape[0]
    kspec = [pl.BlockSpec((TQ, W), functools.partial(lambda i, d: (jnp.maximum(i - d, 0), 0), d=d))
             for d in (2, 1, 0)]
    est = 2 * 7 * TQ * W * 2 + 2 * H * TQ * BAND_NB * TQ * 4 + 16 * TQ * TQ * 4
    return pl.pallas_call(
        functools.partial(_band_body, nheads=H),
        grid=(L // TQ,),
        in_specs=[pl.BlockSpec((TQ, W), lambda i: (i, 0))] + kspec + kspec
                 + [pl.BlockSpec((H, TQ, BAND_NB * TQ), lambda i: (0, 0, 0))],
        out_specs=pl.BlockSpec((TQ, W), lambda i: (i, 0)),
        out_shape=jax.ShapeDtypeStruct((L, W), bf16),
        compiler_params=_params(("parallel",), est),
        name="band_prompt",
    )(q, k, k, k, v, v, v, bias)


FOX_T = 512
FOX_VROWS = 80
FOX_QPIECE = 64
FOX_KPIECE = 67
LOG2E = 1.4426950408889634


def _fox_pack_body(q_ref, k_ref, v_ref, f_ref, e_ref, qa_ref, ka_ref, vt_ref, *, nheads, dh):
    T, W2 = q_ref.shape
    a = (f_ref[...] - f_ref[0:1, :]) * LOG2E
    hi = a.astype(bf16)
    r1 = a - hi.astype(f32)
    mid = r1.astype(bf16)
    lo = (r1 - mid.astype(f32)).astype(bf16)
    aug = _dot(jnp.concatenate([hi, mid, lo], axis=1), e_ref[...])
    lane = lax.broadcasted_iota(jnp.int32, (T, W2), 1) % V7X_LANES
    ones_q = jnp.where((lane >= FOX_KPIECE) & (lane < FOX_KPIECE + 3), 1.0, 0.0)
    ones_k = jnp.where((lane >= FOX_QPIECE) & (lane < FOX_QPIECE + 3), 1.0, 0.0)
    qa_ref[...] = (q_ref[...].astype(f32) + aug[:, :W2] + ones_q).astype(bf16)
    ka_ref[...] = (k_ref[...].astype(f32) + aug[:, W2:] + ones_k).astype(bf16)
    vT = v_ref[...].astype(f32).T
    pad = FOX_VROWS - dh
    ones_row = jnp.where(lax.broadcasted_iota(jnp.int32, (pad, T), 0) == 0, 1.0, 0.0).astype(bf16)
    for h in range(nheads):
        vt_ref[h, 0:dh, :] = vT[h * dh:(h + 1) * dh, :].astype(bf16)
        vt_ref[h, dh:FOX_VROWS, :] = ones_row


def _fox_pack(q, k, v, F, nheads):
    L, W2 = q.shape
    W = v.shape[1]
    dh = W // nheads
    T = FOX_T
    e = np.zeros((3 * V7X_LANES, 2 * W2), np.float32)
    for h in range(nheads):
        for t in range(3):
            e[t * V7X_LANES + h, h * V7X_LANES + FOX_QPIECE + t] = 1.0
            e[t * V7X_LANES + h, W2 + h * V7X_LANES + FOX_KPIECE + t] = -1.0
    est = 2 * (4 * T * W2 * 2 + T * W * 2 + nheads * FOX_VROWS * T * 2) + 6 * T * W2 * 4
    return pl.pallas_call(
        functools.partial(_fox_pack_body, nheads=nheads, dh=dh),
        grid=(L // T,),
        in_specs=[pl.BlockSpec((T, W2), lambda i: (i, 0)),
                  pl.BlockSpec((T, W2), lambda i: (i, 0)),
                  pl.BlockSpec((T, W), lambda i: (i, 0)),
                  pl.BlockSpec((T, V7X_LANES), lambda i: (i, 0)),
                  pl.BlockSpec((3 * V7X_LANES, 2 * W2), lambda i: (0, 0))],
        out_specs=[pl.BlockSpec((T, W2), lambda i: (i, 0)),
                   pl.BlockSpec((T, W2), lambda i: (i, 0)),
                   pl.BlockSpec((nheads, FOX_VROWS, T), lambda i: (0, 0, i))],
        out_shape=[jax.ShapeDtypeStruct((L, W2), bf16), jax.ShapeDtypeStruct((L, W2), bf16),
                   jax.ShapeDtypeStruct((nheads, FOX_VROWS, L), bf16)],
        compiler_params=_params(("parallel",), est),
        name="fox_pack",
    )(q, k, v, F, jnp.asarray(e, bf16))


def _fox_body(qi_ref, kj_ref, qa_ref, ka_ref, vt_ref, d_ref, o_ref, m_scr, acc_scr, *, nheads, dh):
    s_id = pl.program_id(0)
    i = qi_ref[s_id]
    j = kj_ref[s_id]
    T = FOX_T

    @pl.when(j == 0)
    def _():
        m_scr[...] = jnp.full_like(m_scr, NEG)
        acc_scr[...] = jnp.zeros_like(acc_scr)

    def step(masked):
        if masked:
            keep = lax.broadcasted_iota(jnp.int32, (T, T), 0) <= lax.broadcasted_iota(jnp.int32, (T, T), 1)
        def scores(h):
            cols = slice(h * V7X_LANES, (h + 1) * V7X_LANES)
            st = _dot_nt(ka_ref[:, cols], qa_ref[:, cols])
            return jnp.where(keep, st, NEG) if masked else st

        ahead = 2
        pending = [scores(h) for h in range(ahead)]
        for h in range(nheads):
            st = pending.pop(0)
            if h + ahead < nheads:
                pending.append(scores(h + ahead))
            d = d_ref[s_id * nheads + h]
            m_prev = m_scr[h:h + 1, :]
            m_new = jnp.maximum(m_prev, jnp.max(st, axis=0, keepdims=True) + d)
            p = jnp.exp2(st - (m_new - d))
            alpha = jnp.exp2(m_prev - m_new)
            acc_scr[h] = acc_scr[h] * alpha + _dot(vt_ref[h], p.astype(bf16))
            m_scr[h:h + 1, :] = m_new

    @pl.when(j < i)
    def _():
        step(False)

    @pl.when(j == i)
    def _():
        step(True)
        for hp in range(nheads // 2):
            a0 = acc_scr[2 * hp]
            a1 = acc_scr[2 * hp + 1]
            o2 = jnp.concatenate([a0[0:dh, :] / a0[dh:dh + 1, :], a1[0:dh, :] / a1[dh:dh + 1, :]], axis=0)
            o_ref[:, hp * V7X_LANES:(hp + 1) * V7X_LANES] = o2.T.astype(bf16)


def _fox_prompt(qa, ka, vt, F, nheads):
    L, W2 = qa.shape
    T = FOX_T
    n = L // T
    dh = V7X_LANES // 2
    W = nheads * dh
    pairs = [(i, j) for i in range(n) for j in range(i + 1)]
    qi = np.array([p[0] for p in pairs], np.int32)
    kj = np.array([p[1] for p in pairs], np.int32)
    fs = F[::T, :nheads]
    d = ((fs[qi] - fs[kj]) * LOG2E).reshape(-1)
    grid_spec = pltpu.PrefetchScalarGridSpec(
        num_scalar_prefetch=2,
        grid=(len(pairs),),
        in_specs=[pl.BlockSpec((T, W2), lambda s, qi, kj: (qi[s], 0)),
                  pl.BlockSpec((T, W2), lambda s, qi, kj: (kj[s], 0)),
                  pl.BlockSpec((nheads, FOX_VROWS, T), lambda s, qi, kj: (0, 0, kj[s])),
                  pl.BlockSpec(memory_space=pltpu.SMEM)],
        out_specs=pl.BlockSpec((T, W), lambda s, qi, kj: (qi[s], 0)),
        scratch_shapes=[pltpu.VMEM((nheads, T), f32), pltpu.VMEM((nheads, FOX_VROWS, T), f32)],
    )
    est = 2 * (2 * T * W2 * 2 + nheads * FOX_VROWS * T * 2 + T * W * 2) + nheads * FOX_VROWS * T * 4 + 24 * T * T * 4
    return pl.pallas_call(
        functools.partial(_fox_body, nheads=nheads, dh=dh),
        grid_spec=grid_spec,
        out_shape=jax.ShapeDtypeStruct((L, W), bf16),
        compiler_params=_params(("arbitrary",), est),
        name="fox_prompt",
    )(jnp.asarray(qi), jnp.asarray(kj), qa, ka, vt, d)


def _cross_body(q_ref, k_ref, v_ref, o_ref, *, nheads, dh):
    for h in range(nheads):
        cols = slice(h * dh, (h + 1) * dh)
        s = _dot_nt(q_ref[:, cols], k_ref[:, cols])
        m = jnp.max(s, axis=-1, keepdims=True)
        p = jnp.exp(s - m)
        l = p.sum(axis=-1, keepdims=True)
        o_ref[:, cols] = (_dot(p.astype(bf16), v_ref[:, cols]) / l).astype(bf16)


def _cross_prompt(q, mk, mv, nheads):
    L, W = q.shape
    S = mk.shape[0]
    TM = min(512, L)
    return pl.pallas_call(
        functools.partial(_cross_body, nheads=nheads, dh=W // nheads),
        grid=(L // TM,),
        in_specs=[pl.BlockSpec((TM, W), lambda i: (i, 0)),
                  pl.BlockSpec((S, W), lambda i: (0, 0)),
                  pl.BlockSpec((S, W), lambda i: (0, 0))],
        out_specs=pl.BlockSpec((TM, W), lambda i: (i, 0)),
        out_shape=jax.ShapeDtypeStruct((L, W), bf16),
        compiler_params=_params(("parallel",), 8 * TM * W * 2 + 8 * TM * S * 4),
        name="cross_prompt",
    )(q, mk, mv)


def _sample_attn_body(*refs, nheads, mode, has_new):
    it = iter(refs)
    q_ref = next(it)
    kc_ref = next(it)
    vc_ref = next(it)
    kn_ref = vn_ref = None
    if has_new:
        kn_ref = next(it)
        vn_ref = next(it)
    if mode == "rel":
        bc_ref = next(it)
        bn_ref = next(it)
    elif mode == "fox":
        fq_ref = next(it)
        fkc_ref = next(it)
        fkn_ref = next(it)
    o_ref = next(it)

    m_q, W = q_ref.shape
    dh = W // nheads
    R = nheads * m_q
    q = q_ref[...].astype(f32)
    qt = jnp.concatenate([q] * nheads, axis=0)
    hrow = lax.broadcasted_iota(jnp.int32, (R, W), 0) // m_q
    hlane = lax.broadcasted_iota(jnp.int32, (R, W), 1) // dh
    qbd = jnp.where(hrow == hlane, qt, 0.0).astype(bf16)

    def rows_of(f_ref):
        n = f_ref.shape[-1]
        return jnp.concatenate([jnp.broadcast_to(f_ref[h:h + 1, :], (m_q, n)) for h in range(nheads)], axis=0)

    sc = _dot_nt(qbd, kc_ref[...].astype(bf16))
    if mode == "rel":
        sc = sc + bc_ref[...]
    elif mode == "fox":
        sc = sc + (fq_ref[...] - rows_of(fkc_ref))
    m = jnp.max(sc, axis=-1, keepdims=True)
    if has_new:
        sn = _dot_nt(qbd, kn_ref[...].astype(bf16))
        if mode == "rel":
            sn = sn + bn_ref[...]
        elif mode == "fox":
            sn = sn + (fq_ref[...] - rows_of(fkn_ref))
            qpos = lax.broadcasted_iota(jnp.int32, (R, m_q), 0) % m_q
            kpos = lax.broadcasted_iota(jnp.int32, (R, m_q), 1)
            sn = jnp.where(kpos <= qpos, sn, NEG)
        m = jnp.maximum(m, jnp.max(sn, axis=-1, keepdims=True))
    pc = jnp.exp(sc - m)
    l = pc.sum(axis=-1, keepdims=True)
    o = _dot(pc.astype(bf16), vc_ref[...].astype(bf16))
    if has_new:
        pn = jnp.exp(sn - m)
        l = l + pn.sum(axis=-1, keepdims=True)
        o = o + _dot(pn.astype(bf16), vn_ref[...].astype(bf16))
    o = o / l
    hl = lax.broadcasted_iota(jnp.int32, (m_q, W), 1) // dh
    out = jnp.zeros((m_q, W), f32)
    for h in range(nheads):
        out = out + jnp.where(hl == h, o[h * m_q:(h + 1) * m_q, :], 0.0)
    o_ref[...] = out.astype(bf16)


def _sample_attn(q, kc, vc, nheads, mode="none", kn=None, vn=None, extra=()):
    B, m_q, W = q.shape
    P = kc.shape[1]
    R = nheads * m_q
    has_new = kn is not None
    per_b = lambda *shape: pl.BlockSpec((None,) + shape, lambda b: (b,) + (0,) * len(shape))
    shared = lambda *shape: pl.BlockSpec(shape, lambda b: (0,) * len(shape))
    in_specs = [per_b(m_q, W), per_b(P, W), per_b(P, W)]
    args = [q, kc, vc]
    if has_new:
        in_specs += [per_b(m_q, W), per_b(m_q, W)]
        args += [kn, vn]
    if mode == "rel":
        in_specs += [shared(R, P), shared(R, m_q)]
    elif mode == "fox":
        in_specs += [per_b(R, 1), per_b(nheads, P), per_b(nheads, m_q)]
    args += list(extra)
    est = 2 * 2 * P * W * 4 + 2 * P * W * 2 + 6 * R * P * 4 + 4 * R * W * 4
    return pl.pallas_call(
        functools.partial(_sample_attn_body, nheads=nheads, mode=mode, has_new=has_new),
        grid=(B,),
        in_specs=in_specs,
        out_specs=per_b(m_q, W),
        out_shape=jax.ShapeDtypeStruct((B, m_q, W), bf16),
        compiler_params=_params(("parallel",), est),
        name="sample_attn_" + mode,
    )(*args)


GDN_HALO = 8


def _softplus(x):
    return jnp.maximum(x, 0.0) + jnp.log(1.0 + jnp.exp(-jnp.abs(x)))


def _split2(a):
    hi = a.astype(bf16)
    return hi, (a - hi.astype(f32)).astype(bf16)


def _split3(a):
    hi = a.astype(bf16)
    r1 = a - hi.astype(f32)
    mid = r1.astype(bf16)
    return hi, mid, (r1 - mid.astype(f32)).astype(bf16)


def _gdn_body(x_ref, ab_ref, z_ref, cs_ref, s0_ref, cw_ref, alog_ref, dtb_ref, ng_ref,
              o_ref, sf_ref, cf_ref, xbuf, s_scr, *, C, NS, nsteps, nheads, dk):
    c = pl.program_id(1)
    H0 = GDN_HALO
    W = nheads * dk
    R = NS * C
    G = V7X_LANES // C
    NG = nheads // G
    GW = G * dk

    @pl.when(c == 0)
    def _():
        xbuf[H0 - (CONV - 1):H0, :] = cs_ref[...]
        s_scr[...] = s0_ref[...]

    xbuf[H0:H0 + R, :] = x_ref[...]
    cw = cw_ref[...]
    conv = xbuf[H0 - 3:H0 - 3 + R, :] * cw[0:1, :]
    for jj in range(1, CONV):
        conv = conv + xbuf[H0 - 3 + jj:H0 - 3 + jj + R, :] * cw[jj:jj + 1, :]
    tail = xbuf[H0 + R - (CONV - 1):H0 + R, :]
    xbuf[H0 - (CONV - 1):H0, :] = tail

    @pl.when(c == nsteps - 1)
    def _():
        cf_ref[...] = tail

    act = _silu(conv)
    ab = ab_ref[...]
    gfull = -jnp.exp(alog_ref[...]) * _softplus(ab + dtb_ref[...])
    bfull = 1.0 / (1.0 + jnp.exp(-ab))
    z = z_ref[...]
    ng = ng_ref[...]
    hs = range(nheads)

    qn = [act[:, h * dk:(h + 1) * dk] for h in hs]
    kn = [act[:, W + h * dk:W + (h + 1) * dk] for h in hs]
    qn = [q * lax.rsqrt(jnp.sum(q * q, axis=-1, keepdims=True) + EPS) * (dk ** -0.5) for q in qn]
    kn = [k * lax.rsqrt(jnp.sum(k * k, axis=-1, keepdims=True) + EPS) for k in kn]
    vn = [act[:, 2 * W + h * dk:2 * W + (h + 1) * dk] for h in hs]

    r_cc = lax.broadcasted_iota(jnp.int32, (C, C), 0)
    c_cc = lax.broadcasted_iota(jnp.int32, (C, C), 1)
    tri_f = (r_cc >= c_cc).astype(f32)
    ri = lax.broadcasted_iota(jnp.int32, (C, V7X_LANES), 0)
    jl = lax.broadcasted_iota(jnp.int32, (C, V7X_LANES), 1) % C
    lblk = lax.broadcasted_iota(jnp.int32, (C, V7X_LANES), 1) // C
    tri_g = ri >= jl
    strict_g = ri > jl
    eye_g = ri == jl
    r128 = lax.broadcasted_iota(jnp.int32, (V7X_LANES, V7X_LANES), 0)
    l128 = lax.broadcasted_iota(jnp.int32, (V7X_LANES, V7X_LANES), 1)
    same_blk = (r128 // C) == (l128 // C)
    wide_blk = (lax.broadcasted_iota(jnp.int32, (V7X_LANES, GW), 0) // C
                == lax.broadcasted_iota(jnp.int32, (V7X_LANES, GW), 1) // dk)

    def bd_sq(p):
        return jnp.where(same_blk, jnp.concatenate([p] * G, axis=0), jnp.zeros((), p.dtype))

    def bd_wide(xs):
        row = jnp.concatenate(xs, axis=1)
        return jnp.where(wide_blk, jnp.concatenate([row] * G, axis=0), 0.0).astype(bf16)

    def gmm3(xg, pg):
        xh, xl = _split2(xg)
        ph, plo = _split2(pg)
        bh = bd_sq(ph)
        return _dot(xh, bh) + (_dot(xh, bd_sq(plo)) + _dot(xl, bh))

    prep = []
    for s in range(NS):
        sl = slice(s * C, (s + 1) * C)
        gc = _dot_hi(tri_f, gfull[sl])
        egc = jnp.exp(gc)
        kdec = jnp.exp(gc[C - 1:C, :] - gc)
        gcT = jnp.concatenate([gc] * G, axis=0).T
        bcols = [bfull[sl, nheads + h:nheads + h + 1] for h in hs]
        ecols = [egc[:, h:h + 1] for h in hs]
        kbs = [kn[h][sl] * bcols[h] for h in hs]
        Ms, Aqks = [], []
        for grp in range(NG):
            heads = range(grp * G, (grp + 1) * G)
            gcol = jnp.broadcast_to(gc[:, grp * G:grp * G + 1], (C, V7X_LANES))
            grow = gcT[grp * G:grp * G + 1, :]
            for g in range(1, G):
                h = grp * G + g
                gcol = jnp.where(lblk == g, gc[:, h:h + 1], gcol)
                grow = jnp.where(lblk[0:1, :] == g, gcT[h:h + 1, :], grow)
            Lm = jnp.where(tri_g, jnp.exp(jnp.where(tri_g, gcol - grow, 0.0)), 0.0)
            kbd = bd_wide([kn[h][sl] for h in heads])
            kb_row = jnp.concatenate([kbs[h] for h in heads], axis=1).astype(bf16)
            q_row = jnp.concatenate([qn[h][sl] for h in heads], axis=1).astype(bf16)
            Ms.append(jnp.where(strict_g, _dot_nt(kb_row, kbd) * Lm, 0.0))
            Aqks.append(jnp.where(tri_g, _dot_nt(q_row, kbd) * Lm, 0.0).astype(bf16))
        Xs = [jnp.where(eye_g, 1.0, 0.0) - M for M in Ms]
        Pws = [gmm3(M, M) for M in Ms]
        e = 2
        while e < C:
            Xs = [X + gmm3(X, Pw) for X, Pw in zip(Xs, Pws)]
            e *= 2
            if e < C:
                Pws = [gmm3(Pw, Pw) for Pw in Pws]
        us, ws = [], []
        for grp in range(NG):
            heads = range(grp * G, (grp + 1) * G)
            Tm = Xs[grp].astype(bf16)
            U = _dot(Tm, bd_wide([vn[h][sl] * bcols[h] for h in heads]))
            Wm = _dot(Tm, bd_wide([kbs[h] * ecols[h] for h in heads]))
            for g, h in enumerate(heads):
                us.append(U[:, g * dk:(g + 1) * dk])
                ws.append(Wm[:, g * dk:(g + 1) * dk].astype(bf16))
        qgs = [(qn[h][sl] * ecols[h]).astype(bf16) for h in hs]
        kgTs = [(kn[h][sl] * kdec[:, h:h + 1]).T.astype(bf16) for h in hs]
        decs = [egc[C - 1:C, h:h + 1] for h in hs]
        prep.append((us, ws, Aqks, qgs, kgTs, decs))

    Ss = [s_scr[h] for h in hs]
    for s in range(NS):
        us, ws, Aqks, qgs, kgTs, decs = prep[s]
        Sbs = [S.astype(bf16) for S in Ss]
        vnews = [us[h] - _dot(ws[h], Sbs[h]) for h in hs]
        vnbs = [v.astype(bf16) for v in vnews]
        intra = [_dot(Aqks[grp], bd_wide([vnews[h] for h in range(grp * G, (grp + 1) * G)])) for grp in range(NG)]
        os_ = [_dot(qgs[h], Sbs[h]) + intra[h // G][:, (h % G) * dk:(h % G + 1) * dk] for h in hs]
        Ss = [Ss[h] * decs[h] + _dot(kgTs[h], vnbs[h]) for h in hs]
        zs = z[s * C:(s + 1) * C, :]
        ys = [(_rms(os_[h], ng) * _silu(zs[:, h * dk:(h + 1) * dk])).astype(bf16) for h in hs]
        o_ref[s * C:(s + 1) * C, :] = jnp.concatenate(ys, axis=-1)
    for h in hs:
        s_scr[h] = Ss[h]

    @pl.when(c == nsteps - 1)
    def _():
        sf_ref[...] = s_scr[...]


def _gdn_core(qkv, ab, z, conv_state, S0, conv_w, a_log, dt_bias, norm_g, C, NS):
    T = qkv.shape[0]
    B, H, dk, dv = S0.shape
    W = H * dk
    R = NS * C
    nsteps = T // (B * R)
    pad = lambda vec: jnp.zeros((1, V7X_LANES), f32).at[0, :H].set(vec.astype(f32))
    row = lambda width: pl.BlockSpec((R, width), lambda b, c: (b * nsteps + c, 0))
    est = 2 * R * (3 * W + W + V7X_LANES) * 4 + 3 * H * dk * dv * 4 * 2 + 12 * R * 3 * W * 4
    return pl.pallas_call(
        functools.partial(_gdn_body, C=C, NS=NS, nsteps=nsteps, nheads=H, dk=dk),
        grid=(B, nsteps),
        in_specs=[row(3 * W), row(V7X_LANES), row(W),
                  pl.BlockSpec((None, CONV - 1, 3 * W), lambda b, c: (b, 0, 0)),
                  pl.BlockSpec((None, H, dk, dv), lambda b, c: (b, 0, 0, 0)),
                  pl.BlockSpec((CONV, 3 * W), lambda b, c: (0, 0)),
                  pl.BlockSpec((1, V7X_LANES), lambda b, c: (0, 0)),
                  pl.BlockSpec((1, V7X_LANES), lambda b, c: (0, 0)),
                  pl.BlockSpec((1, dv), lambda b, c: (0, 0))],
        out_specs=[row(W),
                   pl.BlockSpec((None, H, dk, dv), lambda b, c: (b, 0, 0, 0)),
                   pl.BlockSpec((None, CONV - 1, 3 * W), lambda b, c: (b, 0, 0))],
        out_shape=[jax.ShapeDtypeStruct((T, W), bf16),
                   jax.ShapeDtypeStruct((B, H, dk, dv), f32),
                   jax.ShapeDtypeStruct((B, CONV - 1, 3 * W), f32)],
        scratch_shapes=[pltpu.VMEM((GDN_HALO + R, 3 * W), f32), pltpu.VMEM((H, dk, dv), f32)],
        compiler_params=_params(("parallel", "arbitrary"), est),
        name="gdn_core",
    )(qkv, ab, z, conv_state, S0, conv_w, pad(a_log), pad(dt_bias), norm_g.reshape(1, dv))


def kernel(x_prompt, x_sample, mem_prompt, cache_a_k, cache_a_v, cache_b_k, cache_b_v, cache_b_logf, state_gdn, state_gdn_conv, cache_mem_k, cache_mem_v, norm_g, mem_norm_g, final_norm_g, ffn_w_gate, ffn_w_up, ffn_w_down, xa_w_q, xa_w_k, xa_w_v, xa_w_o, ab_w_in, ab_b_f, ab_rel_bias, ab_w_o, gdn_w_in, gdn_conv_w, gdn_a_log, gdn_dt_bias, gdn_norm_g, gdn_w_o):
    depth = norm_g.shape[0]
    BP, SEQ, D = x_prompt.shape
    BS, MS, _ = x_sample.shape
    assert BP == 1
    HA = HB = ab_b_f.shape[1]
    WA = WB = (ab_w_in.shape[2] - HB) // 6
    DHA = WA // HA
    HG = gdn_a_log.shape[1]
    WG = gdn_w_o.shape[1]
    DKG = WG // HG
    HX = cache_mem_k.shape[3]
    DHX = cache_mem_k.shape[4]
    WX = HX * DHX
    PA = cache_a_k.shape[2]
    PB = cache_b_k.shape[2]
    LANES = V7X_LANES

    xp = x_prompt.reshape(SEQ, D)
    xs = x_sample.reshape(BS * MS, D)

    wgate = ffn_w_gate.astype(bf16)
    wup = ffn_w_up.astype(bf16)
    wdown = ffn_w_down.astype(bf16)
    wxq = xa_w_q.astype(bf16)
    wxo = xa_w_o.astype(bf16)
    wxkv = jnp.concatenate([xa_w_k, xa_w_v], axis=-1).astype(bf16)

    outs = {k: [] for k in ("akp", "avp", "bkp", "bvp", "blp", "sgp", "scp", "mkp", "mvp",
                            "aks", "avs", "bks", "bvs", "bls", "sgs", "scs")}

    for l in range(depth):
        xp = _ffn(xp, norm_g[l, 0], wgate[l, 0], wup[l, 0], wdown[l, 0])
        xs = _ffn(xs, norm_g[l, 0], wgate[l, 0], wup[l, 0], wdown[l, 0])

        if l % 2 == 0:
            e = l // 2
            w_in = ab_w_in[e]
            nq = 3 * WA + 3 * WB
            w_pad = jnp.concatenate([w_in, jnp.zeros((D, LANES - HB), f32)], axis=-1).astype(bf16)
            b_pad = jnp.zeros((1, LANES), f32).at[0, :HB].set(ab_b_f[e].astype(f32))
            spec = [(0, WA, bf16, DHA ** -0.5, None),
                    (WA, WA, f32, 1.0, None), (WA, WA, bf16, 1.0, None),
                    (2 * WA, WA, f32, 1.0, None), (2 * WA, WA, bf16, 1.0, None),
                    (3 * WA, WB, bf16, (WB // HB) ** -0.5, None),
                    (3 * WA + WB, WB, f32, 1.0, None), (3 * WA + WB, WB, bf16, 1.0, None),
                    (3 * WA + 2 * WB, WB, f32, 1.0, None), (3 * WA + 2 * WB, WB, bf16, 1.0, None),
                    (nq, LANES, f32, 1.0, "logsig")]
            wo = ab_w_o[e].astype(bf16)

            qa, ka, kab, va, vab = _norm_proj(xp, norm_g[l, 1], w_pad[:, :3 * WA], spec[:5])
            oa = _band_prompt(qa, kab, vab, _band_bias(ab_rel_bias[e]))
            DHB = WB // HB
            spread = lambda w: jnp.pad(w.reshape(D, HB, DHB), ((0, 0), (0, 0), (0, LANES - DHB))).reshape(D, HB * LANES)
            wq_b = w_in[:, 3 * WA:3 * WA + WB]
            wk_b = w_in[:, 3 * WA + WB:3 * WA + 2 * WB]
            w_b = jnp.concatenate([spread(wq_b), wk_b, spread(wk_b), w_in[:, 3 * WA + 2 * WB:nq],
                                   w_pad[:, nq:]], axis=-1).astype(bf16)
            W2 = HB * LANES
            spec_b = [(0, W2, bf16, DHB ** -0.5 * LOG2E, None),
                      (W2, WB, f32, 1.0, None), (W2 + WB, W2, bf16, 1.0, None),
                      (2 * W2 + WB, WB, f32, 1.0, None), (2 * W2 + WB, WB, bf16, 1.0, None),
                      (2 * W2 + 2 * WB, LANES, f32, 1.0, "logsig")]
            qbs, kb, kbs, vb, vbb, lf = _norm_proj(xp, norm_g[l, 1], w_b, spec_b, bias=b_pad)
            F, _ = _cumsum_rows(lf)
            qaug, kaug, vt = _fox_pack(qbs, kbs, vbb, F, HB)
            ob = _fox_prompt(qaug, kaug, vt, F, HB)
            res_p = [(oa, wo[:WA]), (ob, wo[WA:])]
            keep = min(BAND_CHUNKS * CHUNK, SEQ)
            outs["akp"].append(ka[SEQ - keep:].reshape(1, keep, HA, DHA))
            outs["avp"].append(va[SEQ - keep:].reshape(1, keep, HA, DHA))
            outs["bkp"].append(kb.reshape(1, SEQ, HB, WB // HB))
            outs["bvp"].append(vb.reshape(1, SEQ, HB, WB // HB))
            outs["blp"].append(lf[:, :HB].reshape(1, SEQ, HB))

            qa, ka, _, va, _, qb, kb, _, vb, _, lf = _norm_proj(xs, norm_g[l, 1], w_pad, spec, bias=b_pad)
            bias_s = _toeplitz_bias(ab_rel_bias[e], MS, PA + MS, PA).reshape(HA * MS, PA + MS)
            oa = _sample_attn(qa.reshape(BS, MS, WA), cache_a_k[e].reshape(BS, PA, WA), cache_a_v[e].reshape(BS, PA, WA),
                              HA, "rel", ka.reshape(BS, MS, WA), va.reshape(BS, MS, WA),
                              extra=(bias_s[:, :PA], bias_s[:, PA:]))
            lfn = lf[:, :HB].reshape(BS, MS, HB)
            lcat = jnp.concatenate([cache_b_logf[e].astype(f32), lfn], axis=1)
            LP = -(-(PB + MS) // LANES) * LANES
            lcat = jnp.pad(lcat.transpose(1, 0, 2).reshape(PB + MS, BS * HB), ((0, LP - PB - MS), (0, 0)))
            _, FTs = _cumsum_rows(lcat)
            FTs = FTs.reshape(BS, HB, LP)
            fq = FTs[:, :, PB:PB + MS].reshape(BS, HB * MS, 1)
            ob = _sample_attn(qb.reshape(BS, MS, WB), cache_b_k[e].reshape(BS, PB, WB), cache_b_v[e].reshape(BS, PB, WB),
                              HB, "fox", kb.reshape(BS, MS, WB), vb.reshape(BS, MS, WB),
                              extra=(fq, FTs[:, :, :PB], FTs[:, :, PB:PB + MS]))
            res_s = [(oa.reshape(BS * MS, WA), wo[:WA]), (ob.reshape(BS * MS, WB), wo[WA:])]
            outs["aks"].append(ka.reshape(BS, MS, HA, DHA))
            outs["avs"].append(va.reshape(BS, MS, HA, DHA))
            outs["bks"].append(kb.reshape(BS, MS, HB, WB // HB))
            outs["bvs"].append(vb.reshape(BS, MS, HB, WB // HB))
            outs["bls"].append(lfn)
        else:
            o = l // 2
            w_in = gdn_w_in[o]
            w_pad = jnp.concatenate([w_in, jnp.zeros((D, LANES - 2 * HG), f32)], axis=-1).astype(bf16)
            spec = [(0, 3 * WG, f32, 1.0, None), (3 * WG, WG, f32, 1.0, None), (4 * WG, LANES, f32, 1.0, None)]
            wo = gdn_w_o[o].astype(bf16)

            qkv, z, ab = _norm_proj(xp, norm_g[l, 1], w_pad, spec)
            y, Sp, cvp = _gdn_core(qkv, ab, z, jnp.zeros((1, CONV - 1, 3 * WG), f32), jnp.zeros((1, HG, DKG, DKG), f32),
                                   gdn_conv_w[o], gdn_a_log[o], gdn_dt_bias[o], gdn_norm_g[o], CHUNK, 2)
            res_p = [(y, wo)]
            outs["sgp"].append(Sp)
            outs["scp"].append(cvp)

            qkv, z, ab = _norm_proj(xs, norm_g[l, 1], w_pad, spec)
            y, Ss, cvs = _gdn_core(qkv, ab, z, state_gdn_conv[o].astype(f32), state_gdn[o].astype(f32),
                                   gdn_conv_w[o], gdn_a_log[o], gdn_dt_bias[o], gdn_norm_g[o], MS, 1)
            res_s = [(y, wo)]
            outs["sgs"].append(Ss)
            outs["scs"].append(cvs)

        mk, mkb, mv, mvb = _norm_proj(mem_prompt.reshape(MEM, D), mem_norm_g[l], wxkv[l],
                                      [(0, WX, f32, 1.0, None), (0, WX, bf16, 1.0, None),
                                       (WX, WX, f32, 1.0, None), (WX, WX, bf16, 1.0, None)])
        outs["mkp"].append(mk.reshape(1, MEM, HX, DHX))
        outs["mvp"].append(mv.reshape(1, MEM, HX, DHX))
        qspec = [(0, WX, bf16, DHX ** -0.5, None)]
        xp, q = _norm_proj(xp, norm_g[l, 2], wxq[l], qspec, res=res_p)
        ocp = _cross_prompt(q, mkb, mvb, HX)
        xs, q = _norm_proj(xs, norm_g[l, 2], wxq[l], qspec, res=res_s)
        ocs = _sample_attn(q.reshape(BS, MS, WX), cache_mem_k[l].reshape(BS, MEM, WX), cache_mem_v[l].reshape(BS, MEM, WX), HX)

        fin = final_norm_g if l == depth - 1 else None
        xp = _ffn(xp, norm_g[l, 3], wgate[l, 1], wup[l, 1], wdown[l, 1], fin, res=(ocp, wxo[l]))
        xs = _ffn(xs, norm_g[l, 3], wgate[l, 1], wup[l, 1], wdown[l, 1], fin, res=(ocs.reshape(BS * MS, WX), wxo[l]))

    st = lambda k: jnp.stack(outs[k])
    return (xp.reshape(BP, SEQ, D), xs.reshape(BS, MS, D),
            st("akp"), st("avp"), st("bkp"), st("bvp"), st("blp"),
            st("sgp"), st("scp"), st("mkp"), st("mvp"),
            st("aks"), st("avs"), st("bks"), st("bvs"), st("bls"),
            st("sgs"), st("scs"))
```

```python
import functools

import numpy as np
import jax
import jax.numpy as jnp
from jax import lax
from jax.experimental import pallas as pl
from jax.experimental.pallas import tpu as pltpu

f32 = jnp.float32
bf16 = jnp.bfloat16
EPS = 1e-6
NEG = -1e30

V7X_VMEM_BYTES = 64 * 1024 * 1024
V7X_LANES = 128
MIB = 1024 * 1024

CHUNK = 64
BAND_CHUNKS = 8
REL_CLIP = 256
CONV = 4
MEM = 256


def _params(sem, est_bytes):
    limit = int(min(V7X_VMEM_BYTES - 8 * MIB, max(32 * MIB, est_bytes + 8 * MIB)))
    return pltpu.CompilerParams(dimension_semantics=sem, vmem_limit_bytes=limit)


def _rms(x, g):
    ms = jnp.mean(x * x, axis=-1, keepdims=True)
    return x * lax.rsqrt(ms + EPS) * g


def _silu(x):
    return x / (1.0 + jnp.exp(-x))


def _dot(a, b):
    return jnp.dot(a, b, preferred_element_type=f32)


def _dot_nt(a, b):
    return lax.dot_general(a, b, (((1,), (1,)), ((), ())), preferred_element_type=f32)


def _dot_hi(a, b):
    return jnp.dot(a, b, preferred_element_type=f32, precision=lax.Precision.HIGHEST)


def _ffn_body(*refs, nj, final, has_res):
    x_ref, g_ref, wg_ref, wu_ref, wd_ref = refs[:5]
    pos = 5
    if has_res:
        a_ref, wo_ref = refs[pos:pos + 2]
        pos += 2
    if final:
        gf_ref = refs[pos]
        pos += 1
    o_ref, h_scr, acc = refs[pos:]
    j = pl.program_id(1)

    @pl.when(j == 0)
    def _():
        x = x_ref[...]
        if has_res:
            x = x + _dot(a_ref[...], wo_ref[...])
        o_ref[...] = x
        h_scr[...] = _rms(x, g_ref[...]).astype(bf16)
        acc[...] = jnp.zeros_like(acc)

    h = h_scr[...]
    a = _silu(_dot(h, wg_ref[...])) * _dot(h, wu_ref[...])
    acc[...] += _dot(a.astype(bf16), wd_ref[...])

    @pl.when(j == nj - 1)
    def _():
        y = o_ref[...] + 0.5 * acc[...]
        if final:
            y = _rms(y, gf_ref[...])
        o_ref[...] = y


def _ffn(x, g, wg, wu, wd, final_g=None, res=None):
    T, D = x.shape
    FF = wg.shape[1]
    TM = min(1024, T)
    TF = min(1024, FF)
    nj = FF // TF
    final = final_g is not None
    in_specs = [pl.BlockSpec((TM, D), lambda i, j: (i, 0)),
                pl.BlockSpec((1, D), lambda i, j: (0, 0)),
                pl.BlockSpec((D, TF), lambda i, j: (0, j)),
                pl.BlockSpec((D, TF), lambda i, j: (0, j)),
                pl.BlockSpec((TF, D), lambda i, j: (j, 0))]
    args = [x, g.reshape(1, D), wg, wu, wd]
    est = 2 * (2 * TM * D * 4) + 2 * 3 * D * TF * 2 + TM * D * 6 + 3 * TM * TF * 4
    if res is not None:
        a, wo = res
        K = a.shape[1]
        in_specs += [pl.BlockSpec((TM, K), lambda i, j: (i, 0)), pl.BlockSpec((K, D), lambda i, j: (0, 0))]
        args += [a, wo]
        est += 2 * TM * K * 2 + 2 * K * D * 2
    if final:
        in_specs.append(pl.BlockSpec((1, D), lambda i, j: (0, 0)))
        args.append(final_g.reshape(1, D))
    return pl.pallas_call(
        functools.partial(_ffn_body, nj=nj, final=final, has_res=res is not None),
        grid=(T // TM, nj),
        in_specs=in_specs,
        out_specs=pl.BlockSpec((TM, D), lambda i, j: (i, 0)),
        out_shape=jax.ShapeDtypeStruct((T, D), f32),
        scratch_shapes=[pltpu.VMEM((TM, D), bf16), pltpu.VMEM((TM, D), f32)],
        compiler_params=_params(("parallel", "arbitrary"), est),
        name="ffn",
    )(*args)


def _log_sigmoid(x):
    return jnp.minimum(x, 0.0) - jnp.log(1.0 + jnp.exp(-jnp.abs(x)))


def _norm_proj_body(*refs, outs, has_bias, nres):
    x_ref, g_ref, w_ref = refs[:3]
    pos = 3
    b_ref = None
    if has_bias:
        b_ref = refs[pos]
        pos += 1
    x = x_ref[...]
    for p in range(nres):
        x = x + _dot(refs[pos][...], refs[pos + 1][...])
        pos += 2
    o_refs = refs[pos:]
    if nres:
        o_refs[0][...] = x
        o_refs = o_refs[1:]
    h = _rms(x, g_ref[...]).astype(bf16)
    cache = {}
    for o_ref, (off, n, dt, scale, act) in zip(o_refs, outs):
        if (off, n) not in cache:
            cache[(off, n)] = _dot(h, w_ref[:, off:off + n])
        r = cache[(off, n)]
        if act == "logsig":
            r = _log_sigmoid(r + b_ref[...])
        if scale != 1.0:
            r = r * scale
        o_ref[...] = r.astype(dt)


def _norm_proj(x, g, w, outs, bias=None, res=()):
    T, D = x.shape
    N = w.shape[1]
    TM = min(512, T)
    in_specs = [pl.BlockSpec((TM, D), lambda i: (i, 0)),
                pl.BlockSpec((1, D), lambda i: (0, 0)),
                pl.BlockSpec((D, N), lambda i: (0, 0))]
    args = [x, g.reshape(1, D), w]
    if bias is not None:
        in_specs.append(pl.BlockSpec((1, bias.shape[-1]), lambda i: (0, 0)))
        args.append(bias)
    est = 2 * TM * D * 4 + 2 * D * N * 2 + sum(2 * TM * n * 4 for (_, n, _, _, _) in outs) + TM * N * 4
    for a, wr in res:
        K = a.shape[1]
        in_specs += [pl.BlockSpec((TM, K), lambda i: (i, 0)), pl.BlockSpec((K, D), lambda i: (0, 0))]
        args += [a, wr]
        est += 2 * TM * K * 2 + 2 * K * D * 2
    out_specs = [pl.BlockSpec((TM, n), lambda i: (i, 0)) for (_, n, _, _, _) in outs]
    out_shape = [jax.ShapeDtypeStruct((T, n), dt) for (_, n, dt, _, _) in outs]
    if res:
        out_specs.insert(0, pl.BlockSpec((TM, D), lambda i: (i, 0)))
        out_shape.insert(0, jax.ShapeDtypeStruct((T, D), f32))
        est += 2 * TM * D * 4
    return pl.pallas_call(
        functools.partial(_norm_proj_body, outs=tuple(outs), has_bias=bias is not None, nres=len(res)),
        grid=(T // TM,),
        in_specs=in_specs,
        out_specs=out_specs,
        out_shape=out_shape,
        compiler_params=_params(("parallel",), est),
        name="norm_proj",
    )(*args)


def _cumsum_body(x_ref, f_ref, ft_ref, carry, *, nsub):
    @pl.when(pl.program_id(0) == 0)
    def _():
        carry[...] = jnp.zeros_like(carry)

    r = lax.broadcasted_iota(jnp.int32, (V7X_LANES, V7X_LANES), 0)
    c = lax.broadcasted_iota(jnp.int32, (V7X_LANES, V7X_LANES), 1)
    tri = (r >= c).astype(f32)
    run = carry[...]
    ncol = x_ref.shape[1] // V7X_LANES
    for sb in range(nsub):
        rows = slice(sb * V7X_LANES, (sb + 1) * V7X_LANES)
        blk = _dot_hi(tri, x_ref[rows, :]) + run
        f_ref[rows, :] = blk
        for cb in range(ncol):
            cols = slice(cb * V7X_LANES, (cb + 1) * V7X_LANES)
            ft_ref[cols, rows] = blk[:, cols].T
        run = blk[V7X_LANES - 1:V7X_LANES, :]
    carry[...] = run


def _cumsum_rows(x):
    L, N = x.shape
    nsub = 8 if L % (8 * V7X_LANES) == 0 else 1
    TB = nsub * V7X_LANES
    return pl.pallas_call(
        functools.partial(_cumsum_body, nsub=nsub),
        grid=(L // TB,),
        in_specs=[pl.BlockSpec((TB, N), lambda i: (i, 0))],
        out_specs=[pl.BlockSpec((TB, N), lambda i: (i, 0)), pl.BlockSpec((N, TB), lambda i: (0, i))],
        out_shape=[jax.ShapeDtypeStruct((L, N), f32), jax.ShapeDtypeStruct((N, L), f32)],
        scratch_shapes=[pltpu.VMEM((1, N), f32)],
        compiler_params=_params(("arbitrary",), 8 * TB * N * 4),
        name="cumsum_rows",
    )(x)


BAND_TQ = 256
BAND_NB = 3


def _band_body(q_ref, k0_ref, k1_ref, k2_ref, v0_ref, v1_ref, v2_ref, b_ref, o_ref, *, nheads):
    i = pl.program_id(0)
    TQ = BAND_TQ
    lane = lax.broadcasted_iota(jnp.int32, (TQ, V7X_LANES), 1)
    lo = lane < 64
    krefs = (k0_ref, k1_ref, k2_ref)
    vrefs = (v0_ref, v1_ref, v2_ref)
    pens = [jnp.where(i - (BAND_NB - 1) + b < 0, NEG, 0.0).astype(f32) for b in range(BAND_NB)]
    for hp in range(nheads // 2):
        cols = slice(hp * V7X_LANES, (hp + 1) * V7X_LANES)
        qp = q_ref[:, cols]
        kp = [kr[:, cols] for kr in krefs]
        vp = [vr[:, cols] for vr in vrefs]
        res = []
        for half in range(2):
            h = 2 * hp + half
            qm = jnp.where(lo if half == 0 else jnp.logical_not(lo), qp, jnp.zeros_like(qp))
            s = [_dot_nt(qm, kp[b]) + b_ref[h, :, b * TQ:(b + 1) * TQ] + pens[b] for b in range(BAND_NB)]
            m = jnp.max(s[0], axis=-1, keepdims=True)
            for b in range(1, BAND_NB):
                m = jnp.maximum(m, jnp.max(s[b], axis=-1, keepdims=True))
            p = [jnp.exp(sb - m) for sb in s]
            l = p[0].sum(axis=-1, keepdims=True)
            o = _dot(p[0].astype(bf16), vp[0])
            for b in range(1, BAND_NB):
                l = l + p[b].sum(axis=-1, keepdims=True)
                o = o + _dot(p[b].astype(bf16), vp[b])
            res.append(o / l)
        o_ref[:, cols] = jnp.where(lo, res[0], res[1]).astype(bf16)


def _toeplitz_bias(table, nq, nk, off):
    H = table.shape[0]
    m = np.arange(nq + nk - 1)
    idx = np.clip(off + nq - 1 - m, -REL_CLIP, REL_CLIP) + REL_CLIP
    w = jnp.concatenate([table[:, idx].astype(f32), jnp.zeros((H, 1), f32)], axis=-1)
    skew = jnp.tile(w, (1, nq))[:, :nq * (nq + nk - 1)].reshape(H, nq, nq + nk - 1)
    return skew[:, :, nq - 1:]


def _band_bias(table):
    qc = np.arange(BAND_TQ)[:, None] // CHUNK
    kc = np.arange(BAND_NB * BAND_TQ)[None, :] // CHUNK
    valid = (kc >= qc) & (kc <= qc + BAND_CHUNKS)
    bias = _toeplitz_bias(table, BAND_TQ, BAND_NB * BAND_TQ, BAND_CHUNKS * CHUNK)
    return jnp.where(valid[None], bias, NEG)


def _band_prompt(q, k, v, bias):
    L, W = q.shape
    TQ = BAND_TQ
    H = bias.shape[0]
    kspec = [pl.BlockSpec((TQ, W), functools.partial(lambda i, d: (jnp.maximum(i - d, 0), 0), d=d))
             for d in (2, 1, 0)]
    est = 2 * 7 * TQ * W * 2 + 2 * H * TQ * BAND_NB * TQ * 4 + 16 * TQ * TQ * 4
    return pl.pallas_call(
        functools.partial(_band_body, nheads=H),
        grid=(L // TQ,),
        in_specs=[pl.BlockSpec((TQ, W), lambda i: (i, 0))] + kspec + kspec
                 + [pl.BlockSpec((H, TQ, BAND_NB * TQ), lambda i: (0, 0, 0))],
        out_specs=pl.BlockSpec((TQ, W), lambda i: (i, 0)),
        out_shape=jax.ShapeDtypeStruct((L, W), bf16),
        compiler_params=_params(("parallel",), est),
        name="band_prompt",
    )(q, k, k, k, v, v, v, bias)


FOX_T = 512
FOX_VROWS = 80
FOX_QPIECE = 64
FOX_KPIECE = 67
LOG2E = 1.4426950408889634


def _fox_pack_body(q_ref, k_ref, v_ref, f_ref, e_ref, qa_ref, ka_ref, vt_ref, *, nheads, dh):
    T, W2 = q_ref.shape
    a = (f_ref[...] - f_ref[0:1, :]) * LOG2E
    hi = a.astype(bf16)
    r1 = a - hi.astype(f32)
    mid = r1.astype(bf16)
    lo = (r1 - mid.astype(f32)).astype(bf16)
    aug = _dot(jnp.concatenate([hi, mid, lo], axis=1), e_ref[...])
    lane = lax.broadcasted_iota(jnp.int32, (T, W2), 1) % V7X_LANES
    ones_q = jnp.where((lane >= FOX_KPIECE) & (lane < FOX_KPIECE + 3), 1.0, 0.0)
    ones_k = jnp.where((lane >= FOX_QPIECE) & (lane < FOX_QPIECE + 3), 1.0, 0.0)
    qa_ref[...] = (q_ref[...].astype(f32) + aug[:, :W2] + ones_q).astype(bf16)
    ka_ref[...] = (k_ref[...].astype(f32) + aug[:, W2:] + ones_k).astype(bf16)
    vT = v_ref[...].astype(f32).T
    pad = FOX_VROWS - dh
    ones_row = jnp.where(lax.broadcasted_iota(jnp.int32, (pad, T), 0) == 0, 1.0, 0.0).astype(bf16)
    for h in range(nheads):
        vt_ref[h, 0:dh, :] = vT[h * dh:(h + 1) * dh, :].astype(bf16)
        vt_ref[h, dh:FOX_VROWS, :] = ones_row


def _fox_pack(q, k, v, F, nheads):
    L, W2 = q.shape
    W = v.shape[1]
    dh = W // nheads
    T = FOX_T
    e = np.zeros((3 * V7X_LANES, 2 * W2), np.float32)
    for h in range(nheads):
        for t in range(3):
            e[t * V7X_LANES + h, h * V7X_LANES + FOX_QPIECE + t] = 1.0
            e[t * V7X_LANES + h, W2 + h * V7X_LANES + FOX_KPIECE + t] = -1.0
    est = 2 * (4 * T * W2 * 2 + T * W * 2 + nheads * FOX_VROWS * T * 2) + 6 * T * W2 * 4
    return pl.pallas_call(
        functools.partial(_fox_pack_body, nheads=nheads, dh=dh),
        grid=(L // T,),
        in_specs=[pl.BlockSpec((T, W2), lambda i: (i, 0)),
                  pl.BlockSpec((T, W2), lambda i: (i, 0)),
                  pl.BlockSpec((T, W), lambda i: (i, 0)),
                  pl.BlockSpec((T, V7X_LANES), lambda i: (i, 0)),
                  pl.BlockSpec((3 * V7X_LANES, 2 * W2), lambda i: (0, 0))],
        out_specs=[pl.BlockSpec((T, W2), lambda i: (i, 0)),
                   pl.BlockSpec((T, W2), lambda i: (i, 0)),
                   pl.BlockSpec((nheads, FOX_VROWS, T), lambda i: (0, 0, i))],
        out_shape=[jax.ShapeDtypeStruct((L, W2), bf16), jax.ShapeDtypeStruct((L, W2), bf16),
                   jax.ShapeDtypeStruct((nheads, FOX_VROWS, L), bf16)],
        compiler_params=_params(("parallel",), est),
        name="fox_pack",
    )(q, k, v, F, jnp.asarray(e, bf16))


def _fox_body(qi_ref, kj_ref, qa_ref, ka_ref, vt_ref, d_ref, o_ref, m_scr, acc_scr, *, nheads, dh):
    s_id = pl.program_id(0)
    i = qi_ref[s_id]
    j = kj_ref[s_id]
    T = FOX_T

    @pl.when(j == 0)
    def _():
        m_scr[...] = jnp.full_like(m_scr, NEG)
        acc_scr[...] = jnp.zeros_like(acc_scr)

    def step(masked):
        if masked:
            keep = lax.broadcasted_iota(jnp.int32, (T, T), 0) <= lax.broadcasted_iota(jnp.int32, (T, T), 1)
        def scores(h):
            cols = slice(h * V7X_LANES, (h + 1) * V7X_LANES)
            st = _dot_nt(ka_ref[:, cols], qa_ref[:, cols])
            return jnp.where(keep, st, NEG) if masked else st

        ahead = 2
        pending = [scores(h) for h in range(ahead)]
        for h in range(nheads):
            st = pending.pop(0)
            if h + ahead < nheads:
                pending.append(scores(h + ahead))
            d = d_ref[s_id * nheads + h]
            m_prev = m_scr[h:h + 1, :]
            m_new = jnp.maximum(m_prev, jnp.max(st, axis=0, keepdims=True) + d)
            p = jnp.exp2(st - (m_new - d))
            alpha = jnp.exp2(m_prev - m_new)
            acc_scr[h] = acc_scr[h] * alpha + _dot(vt_ref[h], p.astype(bf16))
            m_scr[h:h + 1, :] = m_new

    @pl.when(j < i)
    def _():
        step(False)

    @pl.when(j == i)
    def _():
        step(True)
        for hp in range(nheads // 2):
            a0 = acc_scr[2 * hp]
            a1 = acc_scr[2 * hp + 1]
            o2 = jnp.concatenate([a0[0:dh, :] / a0[dh:dh + 1, :], a1[0:dh, :] / a1[dh:dh + 1, :]], axis=0)
            o_ref[:, hp * V7X_LANES:(hp + 1) * V7X_LANES] = o2.T.astype(bf16)


def _fox_prompt(qa, ka, vt, F, nheads):
    L, W2 = qa.shape
    T = FOX_T
    n = L // T
    dh = V7X_LANES // 2
    W = nheads * dh
    pairs = [(i, j) for i in range(n) for j in range(i + 1)]
    qi = np.array([p[0] for p in pairs], np.int32)
    kj = np.array([p[1] for p in pairs], np.int32)
    fs = F[::T, :nheads]
    d = ((fs[qi] - fs[kj]) * LOG2E).reshape(-1)
    grid_spec = pltpu.PrefetchScalarGridSpec(
        num_scalar_prefetch=2,
        grid=(len(pairs),),
        in_specs=[pl.BlockSpec((T, W2), lambda s, qi, kj: (qi[s], 0)),
                  pl.BlockSpec((T, W2), lambda s, qi, kj: (kj[s], 0)),
                  pl.BlockSpec((nheads, FOX_VROWS, T), lambda s, qi, kj: (0, 0, kj[s])),
                  pl.BlockSpec(memory_space=pltpu.SMEM)],
        out_specs=pl.BlockSpec((T, W), lambda s, qi, kj: (qi[s], 0)),
        scratch_shapes=[pltpu.VMEM((nheads, T), f32), pltpu.VMEM((nheads, FOX_VROWS, T), f32)],
    )
    est = 2 * (2 * T * W2 * 2 + nheads * FOX_VROWS * T * 2 + T * W * 2) + nheads * FOX_VROWS * T * 4 + 24 * T * T * 4
    return pl.pallas_call(
        functools.partial(_fox_body, nheads=nheads, dh=dh),
        grid_spec=grid_spec,
        out_shape=jax.ShapeDtypeStruct((L, W), bf16),
        compiler_params=_params(("arbitrary",), est),
        name="fox_prompt",
    )(jnp.asarray(qi), jnp.asarray(kj), qa, ka, vt, d)


def _cross_body(q_ref, k_ref, v_ref, o_ref, *, nheads, dh):
    for h in range(nheads):
        cols = slice(h * dh, (h + 1) * dh)
        s = _dot_nt(q_ref[:, cols], k_ref[:, cols])
        m = jnp.max(s, axis=-1, keepdims=True)
        p = jnp.exp(s - m)
        l = p.sum(axis=-1, keepdims=True)
        o_ref[:, cols] = (_dot(p.astype(bf16), v_ref[:, cols]) / l).astype(bf16)


def _cross_prompt(q, mk, mv, nheads):
    L, W = q.shape
    S = mk.shape[0]
    TM = min(512, L)
    return pl.pallas_call(
        functools.partial(_cross_body, nheads=nheads, dh=W // nheads),
        grid=(L // TM,),
        in_specs=[pl.BlockSpec((TM, W), lambda i: (i, 0)),
                  pl.BlockSpec((S, W), lambda i: (0, 0)),
                  pl.BlockSpec((S, W), lambda i: (0, 0))],
        out_specs=pl.BlockSpec((TM, W), lambda i: (i, 0)),
        out_shape=jax.ShapeDtypeStruct((L, W), bf16),
        compiler_params=_params(("parallel",), 8 * TM * W * 2 + 8 * TM * S * 4),
        name="cross_prompt",
    )(q, mk, mv)


def _sample_attn_body(*refs, nheads, mode, has_new):
    it = iter(refs)
    q_ref = next(it)
    kc_ref = next(it)
    vc_ref = next(it)
    kn_ref = vn_ref = None
    if has_new:
        kn_ref = next(it)
        vn_ref = next(it)
    if mode == "rel":
        bc_ref = next(it)
        bn_ref = next(it)
    elif mode == "fox":
        fq_ref = next(it)
        fkc_ref = next(it)
        fkn_ref = next(it)
    o_ref = next(it)

    m_q, W = q_ref.shape
    dh = W // nheads
    npc = dh // kc_ref.shape[1]
    P = kc_ref.shape[0] // (nheads * npc)
    hs = range(nheads)

    def head_rows(ref, h):
        pieces = [ref[pl.ds(h * npc + j, P, stride=nheads * npc), :] for j in range(npc)]
        return (pieces[0] if npc == 1 else jnp.concatenate(pieces, axis=-1)).astype(bf16)

    q = q_ref[...]
    qh = [q[:, h * dh:(h + 1) * dh] for h in hs]
    kc = [head_rows(kc_ref, h) for h in hs]
    sc = [_dot_nt(qh[h], kc[h]) for h in hs]
    if mode == "rel":
        sc = [sc[h] + bc_ref[h] for h in hs]
    elif mode == "fox":
        fq = fq_ref[...]
        sc = [sc[h] + (fq[:, h:h + 1] - fkc_ref[h:h + 1, :]) for h in hs]
    m = [jnp.max(s, axis=-1, keepdims=True) for s in sc]
    if has_new:
        kn = kn_ref[...]
        sn = [_dot_nt(qh[h], kn[:, h * dh:(h + 1) * dh].astype(bf16)) for h in hs]
        if mode == "rel":
            sn = [sn[h] + bn_ref[h] for h in hs]
        elif mode == "fox":
            causal = (lax.broadcasted_iota(jnp.int32, (m_q, m_q), 1)
                      <= lax.broadcasted_iota(jnp.int32, (m_q, m_q), 0))
            sn = [jnp.where(causal, sn[h] + (fq[:, h:h + 1] - fkn_ref[h:h + 1, :]), NEG) for h in hs]
        m = [jnp.maximum(m[h], jnp.max(sn[h], axis=-1, keepdims=True)) for h in hs]
    pc = [jnp.exp(sc[h] - m[h]) for h in hs]
    l = [p.sum(axis=-1, keepdims=True) for p in pc]
    vc = [head_rows(vc_ref, h) for h in hs]
    o = [_dot(pc[h].astype(bf16), vc[h]) for h in hs]
    if has_new:
        vn = vn_ref[...]
        pn = [jnp.exp(sn[h] - m[h]) for h in hs]
        l = [l[h] + pn[h].sum(axis=-1, keepdims=True) for h in hs]
        o = [o[h] + _dot(pn[h].astype(bf16), vn[:, h * dh:(h + 1) * dh].astype(bf16)) for h in hs]
    o_ref[...] = jnp.concatenate([o[h] / l[h] for h in hs], axis=-1).astype(bf16)


def _sample_attn(q, kc, vc, nheads, mode="none", kn=None, vn=None, extra=()):
    B, m_q, W = q.shape
    P, dh = kc.shape[1], kc.shape[3]
    lw = min(dh, V7X_LANES)
    nrows = P * nheads * (dh // lw)
    kc = kc.reshape(B, nrows, lw)
    vc = vc.reshape(B, nrows, lw)
    has_new = kn is not None
    per_b = lambda *shape: pl.BlockSpec((None,) + shape, lambda b: (b,) + (0,) * len(shape))
    shared = lambda *shape: pl.BlockSpec(shape, lambda b: (0,) * len(shape))
    in_specs = [per_b(m_q, W), per_b(nrows, lw), per_b(nrows, lw)]
    args = [q, kc, vc]
    if has_new:
        in_specs += [per_b(m_q, W), per_b(m_q, W)]
        args += [kn, vn]
    if mode == "rel":
        in_specs += [shared(nheads, m_q, P), shared(nheads, m_q, m_q)]
    elif mode == "fox":
        in_specs += [per_b(m_q, nheads), per_b(nheads, P), per_b(nheads, m_q)]
    args += list(extra)
    est = 2 * 2 * P * nheads * max(dh, V7X_LANES) * 4 + 4 * P * W * 2 + 8 * nheads * m_q * P * 4
    return pl.pallas_call(
        functools.partial(_sample_attn_body, nheads=nheads, mode=mode, has_new=has_new),
        grid=(B,),
        in_specs=in_specs,
        out_specs=per_b(m_q, W),
        out_shape=jax.ShapeDtypeStruct((B, m_q, W), bf16),
        compiler_params=_params(("parallel",), est),
        name="sample_attn_" + mode,
    )(*args)


GDN_HALO = 8


def _softplus(x):
    return jnp.maximum(x, 0.0) + jnp.log(1.0 + jnp.exp(-jnp.abs(x)))


def _split2(a):
    hi = a.astype(bf16)
    return hi, (a - hi.astype(f32)).astype(bf16)


def _split3(a):
    hi = a.astype(bf16)
    r1 = a - hi.astype(f32)
    mid = r1.astype(bf16)
    return hi, mid, (r1 - mid.astype(f32)).astype(bf16)


def _gdn_body(x_ref, ab_ref, z_ref, cs_ref, s0_ref, cw_ref, alog_ref, dtb_ref, ng_ref,
              o_ref, sf_ref, cf_ref, xbuf, s_scr, *, C, NS, nsteps, nheads, dk):
    c = pl.program_id(1)
    H0 = GDN_HALO
    W = nheads * dk
    R = NS * C
    G = V7X_LANES // C
    NG = nheads // G
    GW = G * dk

    @pl.when(c == 0)
    def _():
        xbuf[H0 - (CONV - 1):H0, :] = cs_ref[...]
        s_scr[...] = s0_ref[...]

    xbuf[H0:H0 + R, :] = x_ref[...]
    cw = cw_ref[...]
    conv = xbuf[H0 - 3:H0 - 3 + R, :] * cw[0:1, :]
    for jj in range(1, CONV):
        conv = conv + xbuf[H0 - 3 + jj:H0 - 3 + jj + R, :] * cw[jj:jj + 1, :]
    tail = xbuf[H0 + R - (CONV - 1):H0 + R, :]
    xbuf[H0 - (CONV - 1):H0, :] = tail

    @pl.when(c == nsteps - 1)
    def _():
        cf_ref[...] = tail

    act = _silu(conv)
    ab = ab_ref[...]
    gfull = -jnp.exp(alog_ref[...]) * _softplus(ab + dtb_ref[...])
    bfull = 1.0 / (1.0 + jnp.exp(-ab))
    z = z_ref[...]
    ng = ng_ref[...]
    hs = range(nheads)

    qn = [act[:, h * dk:(h + 1) * dk] for h in hs]
    kn = [act[:, W + h * dk:W + (h + 1) * dk] for h in hs]
    qn = [q * lax.rsqrt(jnp.sum(q * q, axis=-1, keepdims=True) + EPS) * (dk ** -0.5) for q in qn]
    kn = [k * lax.rsqrt(jnp.sum(k * k, axis=-1, keepdims=True) + EPS) for k in kn]
    vn = [act[:, 2 * W + h * dk:2 * W + (h + 1) * dk] for h in hs]

    r_cc = lax.broadcasted_iota(jnp.int32, (C, C), 0)
    c_cc = lax.broadcasted_iota(jnp.int32, (C, C), 1)
    tri_f = (r_cc >= c_cc).astype(f32)
    ri = lax.broadcasted_iota(jnp.int32, (C, V7X_LANES), 0)
    jl = lax.broadcasted_iota(jnp.int32, (C, V7X_LANES), 1) % C
    lblk = lax.broadcasted_iota(jnp.int32, (C, V7X_LANES), 1) // C
    tri_g = ri >= jl
    strict_g = ri > jl
    eye_g = ri == jl
    r128 = lax.broadcasted_iota(jnp.int32, (V7X_LANES, V7X_LANES), 0)
    l128 = lax.broadcasted_iota(jnp.int32, (V7X_LANES, V7X_LANES), 1)
    same_blk = (r128 // C) == (l128 // C)
    wide_blk = (lax.broadcasted_iota(jnp.int32, (V7X_LANES, GW), 0) // C
                == lax.broadcasted_iota(jnp.int32, (V7X_LANES, GW), 1) // dk)

    def bd_sq(p):
        return jnp.where(same_blk, jnp.concatenate([p] * G, axis=0), jnp.zeros((), p.dtype))

    def bd_wide(xs):
        row = jnp.concatenate(xs, axis=1)
        return jnp.where(wide_blk, jnp.concatenate([row] * G, axis=0), 0.0).astype(bf16)

    def gmm3(xg, pg):
        xh, xl = _split2(xg)
        ph, plo = _split2(pg)
        bh = bd_sq(ph)
        return _dot(xh, bh) + (_dot(xh, bd_sq(plo)) + _dot(xl, bh))

    prep = []
    for s in range(NS):
        sl = slice(s * C, (s + 1) * C)
        gc = _dot_hi(tri_f, gfull[sl])
        egc = jnp.exp(gc)
        kdec = jnp.exp(gc[C - 1:C, :] - gc)
        gcT = jnp.concatenate([gc] * G, axis=0).T
        bcols = [bfull[sl, nheads + h:nheads + h + 1] for h in hs]
        ecols = [egc[:, h:h + 1] for h in hs]
        kbs = [kn[h][sl] * bcols[h] for h in hs]
        Ms, Aqks = [], []
        for grp in range(NG):
            heads = range(grp * G, (grp + 1) * G)
            gcol = jnp.broadcast_to(gc[:, grp * G:grp * G + 1], (C, V7X_LANES))
            grow = gcT[grp * G:grp * G + 1, :]
            for g in range(1, G):
                h = grp * G + g
                gcol = jnp.where(lblk == g, gc[:, h:h + 1], gcol)
                grow = jnp.where(lblk[0:1, :] == g, gcT[h:h + 1, :], grow)
            Lm = jnp.where(tri_g, jnp.exp(jnp.where(tri_g, gcol - grow, 0.0)), 0.0)
            kbd = bd_wide([kn[h][sl] for h in heads])
            kb_row = jnp.concatenate([kbs[h] for h in heads], axis=1).astype(bf16)
            q_row = jnp.concatenate([qn[h][sl] for h in heads], axis=1).astype(bf16)
            Ms.append(jnp.where(strict_g, _dot_nt(kb_row, kbd) * Lm, 0.0))
            Aqks.append(jnp.where(tri_g, _dot_nt(q_row, kbd) * Lm, 0.0).astype(bf16))
        Xs = [jnp.where(eye_g, 1.0, 0.0) - M for M in Ms]
        Pws = [gmm3(M, M) for M in Ms]
        e = 2
        while e < C:
            Xs = [X + gmm3(X, Pw) for X, Pw in zip(Xs, Pws)]
            e *= 2
            if e < C:
                Pws = [gmm3(Pw, Pw) for Pw in Pws]
        us, ws = [], []
        for grp in range(NG):
            heads = range(grp * G, (grp + 1) * G)
            Tm = Xs[grp].astype(bf16)
            U = _dot(Tm, bd_wide([vn[h][sl] * bcols[h] for h in heads]))
            Wm = _dot(Tm, bd_wide([kbs[h] * ecols[h] for h in heads]))
            for g, h in enumerate(heads):
                us.append(U[:, g * dk:(g + 1) * dk])
                ws.append(Wm[:, g * dk:(g + 1) * dk].astype(bf16))
        qgs = [(qn[h][sl] * ecols[h]).astype(bf16) for h in hs]
        kgTs = [(kn[h][sl] * kdec[:, h:h + 1]).T.astype(bf16) for h in hs]
        decs = [egc[C - 1:C, h:h + 1] for h in hs]
        prep.append((us, ws, Aqks, qgs, kgTs, decs))

    Ss = [s_scr[h] for h in hs]
    for s in range(NS):
        us, ws, Aqks, qgs, kgTs, decs = prep[s]
        Sbs = [S.astype(bf16) for S in Ss]
        vnews = [us[h] - _dot(ws[h], Sbs[h]) for h in hs]
        vnbs = [v.astype(bf16) for v in vnews]
        intra = [_dot(Aqks[grp], bd_wide([vnews[h] for h in range(grp * G, (grp + 1) * G)])) for grp in range(NG)]
        os_ = [_dot(qgs[h], Sbs[h]) + intra[h // G][:, (h % G) * dk:(h % G + 1) * dk] for h in hs]
        Ss = [Ss[h] * decs[h] + _dot(kgTs[h], vnbs[h]) for h in hs]
        zs = z[s * C:(s + 1) * C, :]
        ys = [(_rms(os_[h], ng) * _silu(zs[:, h * dk:(h + 1) * dk])).astype(bf16) for h in hs]
        o_ref[s * C:(s + 1) * C, :] = jnp.concatenate(ys, axis=-1)
    for h in hs:
        s_scr[h] = Ss[h]

    @pl.when(c == nsteps - 1)
    def _():
        sf_ref[...] = s_scr[...]


def _gdn_core(qkv, ab, z, conv_state, S0, conv_w, a_log, dt_bias, norm_g, C, NS):
    T = qkv.shape[0]
    B, H, dk, dv = S0.shape
    W = H * dk
    R = NS * C
    nsteps = T // (B * R)
    pad = lambda vec: jnp.zeros((1, V7X_LANES), f32).at[0, :H].set(vec.astype(f32))
    row = lambda width: pl.BlockSpec((R, width), lambda b, c: (b * nsteps + c, 0))
    est = 2 * R * (3 * W + W + V7X_LANES) * 4 + 3 * H * dk * dv * 4 * 2 + 12 * R * 3 * W * 4
    return pl.pallas_call(
        functools.partial(_gdn_body, C=C, NS=NS, nsteps=nsteps, nheads=H, dk=dk),
        grid=(B, nsteps),
        in_specs=[row(3 * W), row(V7X_LANES), row(W),
                  pl.BlockSpec((None, CONV - 1, 3 * W), lambda b, c: (b, 0, 0)),
                  pl.BlockSpec((None, H, dk, dv), lambda b, c: (b, 0, 0, 0)),
                  pl.BlockSpec((CONV, 3 * W), lambda b, c: (0, 0)),
                  pl.BlockSpec((1, V7X_LANES), lambda b, c: (0, 0)),
                  pl.BlockSpec((1, V7X_LANES), lambda b, c: (0, 0)),
                  pl.BlockSpec((1, dv), lambda b, c: (0, 0))],
        out_specs=[row(W),
                   pl.BlockSpec((None, H, dk, dv), lambda b, c: (b, 0, 0, 0)),
                   pl.BlockSpec((None, CONV - 1, 3 * W), lambda b, c: (b, 0, 0))],
        out_shape=[jax.ShapeDtypeStruct((T, W), bf16),
                   jax.ShapeDtypeStruct((B, H, dk, dv), f32),
                   jax.ShapeDtypeStruct((B, CONV - 1, 3 * W), f32)],
        scratch_shapes=[pltpu.VMEM((GDN_HALO + R, 3 * W), f32), pltpu.VMEM((H, dk, dv), f32)],
        compiler_params=_params(("parallel", "arbitrary"), est),
        name="gdn_core",
    )(qkv, ab, z, conv_state, S0, conv_w, pad(a_log), pad(dt_bias), norm_g.reshape(1, dv))


def kernel(x_prompt, x_sample, mem_prompt, cache_a_k, cache_a_v, cache_b_k, cache_b_v, cache_b_logf, state_gdn, state_gdn_conv, cache_mem_k, cache_mem_v, norm_g, mem_norm_g, final_norm_g, ffn_w_gate, ffn_w_up, ffn_w_down, xa_w_q, xa_w_k, xa_w_v, xa_w_o, ab_w_in, ab_b_f, ab_rel_bias, ab_w_o, gdn_w_in, gdn_conv_w, gdn_a_log, gdn_dt_bias, gdn_norm_g, gdn_w_o):
    depth = norm_g.shape[0]
    BP, SEQ, D = x_prompt.shape
    BS, MS, _ = x_sample.shape
    assert BP == 1
    HA = HB = ab_b_f.shape[1]
    WA = WB = (ab_w_in.shape[2] - HB) // 6
    DHA = WA // HA
    HG = gdn_a_log.shape[1]
    WG = gdn_w_o.shape[1]
    DKG = WG // HG
    HX = cache_mem_k.shape[3]
    DHX = cache_mem_k.shape[4]
    WX = HX * DHX
    PA = cache_a_k.shape[2]
    PB = cache_b_k.shape[2]
    LANES = V7X_LANES

    xp = x_prompt.reshape(SEQ, D)
    xs = x_sample.reshape(BS * MS, D)

    wgate = ffn_w_gate.astype(bf16)
    wup = ffn_w_up.astype(bf16)
    wdown = ffn_w_down.astype(bf16)
    wxq = xa_w_q.astype(bf16)
    wxo = xa_w_o.astype(bf16)
    wxkv = jnp.concatenate([xa_w_k, xa_w_v], axis=-1).astype(bf16)

    outs = {k: [] for k in ("akp", "avp", "bkp", "bvp", "blp", "sgp", "scp", "mkp", "mvp",
                            "aks", "avs", "bks", "bvs", "bls", "sgs", "scs")}

    for l in range(depth):
        xp = _ffn(xp, norm_g[l, 0], wgate[l, 0], wup[l, 0], wdown[l, 0])
        xs = _ffn(xs, norm_g[l, 0], wgate[l, 0], wup[l, 0], wdown[l, 0])

        if l % 2 == 0:
            e = l // 2
            w_in = ab_w_in[e]
            nq = 3 * WA + 3 * WB
            w_pad = jnp.concatenate([w_in, jnp.zeros((D, LANES - HB), f32)], axis=-1).astype(bf16)
            b_pad = jnp.zeros((1, LANES), f32).at[0, :HB].set(ab_b_f[e].astype(f32))
            spec = [(0, WA, bf16, DHA ** -0.5, None),
                    (WA, WA, f32, 1.0, None), (WA, WA, bf16, 1.0, None),
                    (2 * WA, WA, f32, 1.0, None), (2 * WA, WA, bf16, 1.0, None),
                    (3 * WA, WB, bf16, (WB // HB) ** -0.5, None),
                    (3 * WA + WB, WB, f32, 1.0, None), (3 * WA + WB, WB, bf16, 1.0, None),
                    (3 * WA + 2 * WB, WB, f32, 1.0, None), (3 * WA + 2 * WB, WB, bf16, 1.0, None),
                    (nq, LANES, f32, 1.0, "logsig")]
            wo = ab_w_o[e].astype(bf16)

            qa, ka, kab, va, vab = _norm_proj(xp, norm_g[l, 1], w_pad[:, :3 * WA], spec[:5])
            oa = _band_prompt(qa, kab, vab, _band_bias(ab_rel_bias[e]))
            DHB = WB // HB
            spread = lambda w: jnp.pad(w.reshape(D, HB, DHB), ((0, 0), (0, 0), (0, LANES - DHB))).reshape(D, HB * LANES)
            wq_b = w_in[:, 3 * WA:3 * WA + WB]
            wk_b = w_in[:, 3 * WA + WB:3 * WA + 2 * WB]
            w_b = jnp.concatenate([spread(wq_b), wk_b, spread(wk_b), w_in[:, 3 * WA + 2 * WB:nq],
                                   w_pad[:, nq:]], axis=-1).astype(bf16)
            W2 = HB * LANES
            spec_b = [(0, W2, bf16, DHB ** -0.5 * LOG2E, None),
                      (W2, WB, f32, 1.0, None), (W2 + WB, W2, bf16, 1.0, None),
                      (2 * W2 + WB, WB, f32, 1.0, None), (2 * W2 + WB, WB, bf16, 1.0, None),
                      (2 * W2 + 2 * WB, LANES, f32, 1.0, "logsig")]
            qbs, kb, kbs, vb, vbb, lf = _norm_proj(xp, norm_g[l, 1], w_b, spec_b, bias=b_pad)
            F, _ = _cumsum_rows(lf)
            qaug, kaug, vt = _fox_pack(qbs, kbs, vbb, F, HB)
            ob = _fox_prompt(qaug, kaug, vt, F, HB)
            res_p = [(oa, wo[:WA]), (ob, wo[WA:])]
            keep = min(BAND_CHUNKS * CHUNK, SEQ)
            outs["akp"].append(ka[SEQ - keep:].reshape(1, keep, HA, DHA))
            outs["avp"].append(va[SEQ - keep:].reshape(1, keep, HA, DHA))
            outs["bkp"].append(kb.reshape(1, SEQ, HB, WB // HB))
            outs["bvp"].append(vb.reshape(1, SEQ, HB, WB // HB))
            outs["blp"].append(lf[:, :HB].reshape(1, SEQ, HB))

            qa, ka, _, va, _, qb, kb, _, vb, _, lf = _norm_proj(xs, norm_g[l, 1], w_pad, spec, bias=b_pad)
            bias_s = _toeplitz_bias(ab_rel_bias[e], MS, PA + MS, PA)
            oa = _sample_attn(qa.reshape(BS, MS, WA), cache_a_k[e], cache_a_v[e],
                              HA, "rel", ka.reshape(BS, MS, WA), va.reshape(BS, MS, WA),
                              extra=(bias_s[:, :, :PA], bias_s[:, :, PA:]))
            lfn = lf[:, :HB].reshape(BS, MS, HB)
            lcat = jnp.concatenate([cache_b_logf[e].astype(f32), lfn], axis=1)
            LP = -(-(PB + MS) // LANES) * LANES
            lcat = jnp.pad(lcat.transpose(1, 0, 2).reshape(PB + MS, BS * HB), ((0, LP - PB - MS), (0, 0)))
            _, FTs = _cumsum_rows(lcat)
            FTs = FTs.reshape(BS, HB, LP)
            fq = FTs[:, :, PB:PB + MS].transpose(0, 2, 1)
            ob = _sample_attn(qb.reshape(BS, MS, WB), cache_b_k[e], cache_b_v[e],
                              HB, "fox", kb.reshape(BS, MS, WB), vb.reshape(BS, MS, WB),
                              extra=(fq, FTs[:, :, :PB], FTs[:, :, PB:PB + MS]))
            res_s = [(oa.reshape(BS * MS, WA), wo[:WA]), (ob.reshape(BS * MS, WB), wo[WA:])]
            outs["aks"].append(ka.reshape(BS, MS, HA, DHA))
            outs["avs"].append(va.reshape(BS, MS, HA, DHA))
            outs["bks"].append(kb.reshape(BS, MS, HB, WB // HB))
            outs["bvs"].append(vb.reshape(BS, MS, HB, WB // HB))
            outs["bls"].append(lfn)
        else:
            o = l // 2
            w_in = gdn_w_in[o]
            w_pad = jnp.concatenate([w_in, jnp.zeros((D, LANES - 2 * HG), f32)], axis=-1).astype(bf16)
            spec = [(0, 3 * WG, f32, 1.0, None), (3 * WG, WG, f32, 1.0, None), (4 * WG, LANES, f32, 1.0, None)]
            wo = gdn_w_o[o].astype(bf16)

            qkv, z, ab = _norm_proj(xp, norm_g[l, 1], w_pad, spec)
            y, Sp, cvp = _gdn_core(qkv, ab, z, jnp.zeros((1, CONV - 1, 3 * WG), f32), jnp.zeros((1, HG, DKG, DKG), f32),
                                   gdn_conv_w[o], gdn_a_log[o], gdn_dt_bias[o], gdn_norm_g[o], CHUNK, 2)
            res_p = [(y, wo)]
            outs["sgp"].append(Sp)
            outs["scp"].append(cvp)

            qkv, z, ab = _norm_proj(xs, norm_g[l, 1], w_pad, spec)
            y, Ss, cvs = _gdn_core(qkv, ab, z, state_gdn_conv[o].astype(f32), state_gdn[o].astype(f32),
                                   gdn_conv_w[o], gdn_a_log[o], gdn_dt_bias[o], gdn_norm_g[o], MS, 1)
            res_s = [(y, wo)]
            outs["sgs"].append(Ss)
            outs["scs"].append(cvs)

        mk, mkb, mv, mvb = _norm_proj(mem_prompt.reshape(MEM, D), mem_norm_g[l], wxkv[l],
                                      [(0, WX, f32, 1.0, None), (0, WX, bf16, 1.0, None),
                                       (WX, WX, f32, 1.0, None), (WX, WX, bf16, 1.0, None)])
        outs["mkp"].append(mk.reshape(1, MEM, HX, DHX))
        outs["mvp"].append(mv.reshape(1, MEM, HX, DHX))
        qspec = [(0, WX, bf16, DHX ** -0.5, None)]
        xp, q = _norm_proj(xp, norm_g[l, 2], wxq[l], qspec, res=res_p)
        ocp = _cross_prompt(q, mkb, mvb, HX)
        xs, q = _norm_proj(xs, norm_g[l, 2], wxq[l], qspec, res=res_s)
        ocs = _sample_attn(q.reshape(BS, MS, WX), cache_mem_k[l], cache_mem_v[l], HX)

        fin = final_norm_g if l == depth - 1 else None
        xp = _ffn(xp, norm_g[l, 3], wgate[l, 1], wup[l, 1], wdown[l, 1], fin, res=(ocp, wxo[l]))
        xs = _ffn(xs, norm_g[l, 3], wgate[l, 1], wup[l, 1], wdown[l, 1], fin, res=(ocs.reshape(BS * MS, WX), wxo[l]))

    st = lambda k: jnp.stack(outs[k])
    return (xp.reshape(BP, SEQ, D), xs.reshape(BS, MS, D),
            st("akp"), st("avp"), st("bkp"), st("bvp"), st("blp"),
            st("sgp"), st("scp"), st("mkp"), st("mvp"),
            st("aks"), st("avs"), st("bks"), st("bvs"), st("bls"),
            st("sgs"), st("scs"))
```

```python
import functools

import numpy as np
import jax
import jax.numpy as jnp
from jax import lax
from jax.experimental import pallas as pl
from jax.experimental.pallas import tpu as pltpu

f32 = jnp.float32
bf16 = jnp.bfloat16
EPS = 1e-6
NEG = -1e30

V7X_VMEM_BYTES = 64 * 1024 * 1024
V7X_LANES = 128
MIB = 1024 * 1024

CHUNK = 64
BAND_CHUNKS = 8
REL_CLIP = 256
CONV = 4
MEM = 256


def _params(sem, est_bytes):
    limit = int(min(V7X_VMEM_BYTES - 8 * MIB, max(32 * MIB, est_bytes + 8 * MIB)))
    return pltpu.CompilerParams(dimension_semantics=sem, vmem_limit_bytes=limit)


def _rms(x, g):
    ms = jnp.mean(x * x, axis=-1, keepdims=True)
    return x * lax.rsqrt(ms + EPS) * g


def _silu(x):
    return x / (1.0 + jnp.exp(-x))


def _dot(a, b):
    return jnp.dot(a, b, preferred_element_type=f32)


def _dot_nt(a, b):
    return lax.dot_general(a, b, (((1,), (1,)), ((), ())), preferred_element_type=f32)


def _dot_hi(a, b):
    return jnp.dot(a, b, preferred_element_type=f32, precision=lax.Precision.HIGHEST)


def _ffn_body(*refs, nj, final, has_res):
    x_ref, g_ref, wg_ref, wu_ref, wd_ref = refs[:5]
    pos = 5
    if has_res:
        a_ref, wo_ref = refs[pos:pos + 2]
        pos += 2
    if final:
        gf_ref = refs[pos]
        pos += 1
    o_ref, h_scr, acc = refs[pos:]
    j = pl.program_id(1)

    @pl.when(j == 0)
    def _():
        x = x_ref[...]
        if has_res:
            x = x + _dot(a_ref[...], wo_ref[...])
        o_ref[...] = x
        h_scr[...] = _rms(x, g_ref[...]).astype(bf16)
        acc[...] = jnp.zeros_like(acc)

    h = h_scr[...]
    a = _silu(_dot(h, wg_ref[...])) * _dot(h, wu_ref[...])
    acc[...] += _dot(a.astype(bf16), wd_ref[...])

    @pl.when(j == nj - 1)
    def _():
        y = o_ref[...] + 0.5 * acc[...]
        if final:
            y = _rms(y, gf_ref[...])
        o_ref[...] = y


def _ffn(x, g, wg, wu, wd, final_g=None, res=None):
    T, D = x.shape
    FF = wg.shape[1]
    TM = min(1024, T)
    TF = min(1024, FF)
    nj = FF // TF
    final = final_g is not None
    in_specs = [pl.BlockSpec((TM, D), lambda i, j: (i, 0)),
                pl.BlockSpec((1, D), lambda i, j: (0, 0)),
                pl.BlockSpec((D, TF), lambda i, j: (0, j)),
                pl.BlockSpec((D, TF), lambda i, j: (0, j)),
                pl.BlockSpec((TF, D), lambda i, j: (j, 0))]
    args = [x, g.reshape(1, D), wg, wu, wd]
    est = 2 * (2 * TM * D * 4) + 2 * 3 * D * TF * 2 + TM * D * 6 + 3 * TM * TF * 4
    if res is not None:
        a, wo = res
        K = a.shape[1]
        in_specs += [pl.BlockSpec((TM, K), lambda i, j: (i, 0)), pl.BlockSpec((K, D), lambda i, j: (0, 0))]
        args += [a, wo]
        est += 2 * TM * K * 2 + 2 * K * D * 2
    if final:
        in_specs.append(pl.BlockSpec((1, D), lambda i, j: (0, 0)))
        args.append(final_g.reshape(1, D))
    return pl.pallas_call(
        functools.partial(_ffn_body, nj=nj, final=final, has_res=res is not None),
        grid=(T // TM, nj),
        in_specs=in_specs,
        out_specs=pl.BlockSpec((TM, D), lambda i, j: (i, 0)),
        out_shape=jax.ShapeDtypeStruct((T, D), f32),
        scratch_shapes=[pltpu.VMEM((TM, D), bf16), pltpu.VMEM((TM, D), f32)],
        compiler_params=_params(("parallel", "arbitrary"), est),
        name="ffn",
    )(*args)


def _log_sigmoid(x):
    return jnp.minimum(x, 0.0) - jnp.log(1.0 + jnp.exp(-jnp.abs(x)))


def _norm_proj_body(*refs, outs, has_bias, nres):
    x_ref, g_ref, w_ref = refs[:3]
    pos = 3
    b_ref = None
    if has_bias:
        b_ref = refs[pos]
        pos += 1
    x = x_ref[...]
    for p in range(nres):
        x = x + _dot(refs[pos][...], refs[pos + 1][...])
        pos += 2
    o_refs = refs[pos:]
    if nres:
        o_refs[0][...] = x
        o_refs = o_refs[1:]
    h = _rms(x, g_ref[...]).astype(bf16)
    cache = {}
    for o_ref, (off, n, dt, scale, act) in zip(o_refs, outs):
        if (off, n) not in cache:
            cache[(off, n)] = _dot(h, w_ref[:, off:off + n])
        r = cache[(off, n)]
        if act == "logsig":
            r = _log_sigmoid(r + b_ref[...])
        if scale != 1.0:
            r = r * scale
        o_ref[...] = r.astype(dt)


def _norm_proj(x, g, w, outs, bias=None, res=()):
    T, D = x.shape
    N = w.shape[1]
    TM = min(512, T)
    in_specs = [pl.BlockSpec((TM, D), lambda i: (i, 0)),
                pl.BlockSpec((1, D), lambda i: (0, 0)),
                pl.BlockSpec((D, N), lambda i: (0, 0))]
    args = [x, g.reshape(1, D), w]
    if bias is not None:
        in_specs.append(pl.BlockSpec((1, bias.shape[-1]), lambda i: (0, 0)))
        args.append(bias)
    est = 2 * TM * D * 4 + 2 * D * N * 2 + sum(2 * TM * n * 4 for (_, n, _, _, _) in outs) + TM * N * 4
    for a, wr in res:
        K = a.shape[1]
        in_specs += [pl.BlockSpec((TM, K), lambda i: (i, 0)), pl.BlockSpec((K, D), lambda i: (0, 0))]
        args += [a, wr]
        est += 2 * TM * K * 2 + 2 * K * D * 2
    out_specs = [pl.BlockSpec((TM, n), lambda i: (i, 0)) for (_, n, _, _, _) in outs]
    out_shape = [jax.ShapeDtypeStruct((T, n), dt) for (_, n, dt, _, _) in outs]
    if res:
        out_specs.insert(0, pl.BlockSpec((TM, D), lambda i: (i, 0)))
        out_shape.insert(0, jax.ShapeDtypeStruct((T, D), f32))
        est += 2 * TM * D * 4
    return pl.pallas_call(
        functools.partial(_norm_proj_body, outs=tuple(outs), has_bias=bias is not None, nres=len(res)),
        grid=(T // TM,),
        in_specs=in_specs,
        out_specs=out_specs,
        out_shape=out_shape,
        compiler_params=_params(("parallel",), est),
        name="norm_proj",
    )(*args)


def _cumsum_body(x_ref, f_ref, ft_ref, carry, *, nsub):
    @pl.when(pl.program_id(0) == 0)
    def _():
        carry[...] = jnp.zeros_like(carry)

    r = lax.broadcasted_iota(jnp.int32, (V7X_LANES, V7X_LANES), 0)
    c = lax.broadcasted_iota(jnp.int32, (V7X_LANES, V7X_LANES), 1)
    tri = (r >= c).astype(f32)
    run = carry[...]
    ncol = x_ref.shape[1] // V7X_LANES
    for sb in range(nsub):
        rows = slice(sb * V7X_LANES, (sb + 1) * V7X_LANES)
        blk = _dot_hi(tri, x_ref[rows, :]) + run
        f_ref[rows, :] = blk
        for cb in range(ncol):
            cols = slice(cb * V7X_LANES, (cb + 1) * V7X_LANES)
            ft_ref[cols, rows] = blk[:, cols].T
        run = blk[V7X_LANES - 1:V7X_LANES, :]
    carry[...] = run


def _cumsum_rows(x):
    L, N = x.shape
    nsub = 8 if L % (8 * V7X_LANES) == 0 else 1
    TB = nsub * V7X_LANES
    return pl.pallas_call(
        functools.partial(_cumsum_body, nsub=nsub),
        grid=(L // TB,),
        in_specs=[pl.BlockSpec((TB, N), lambda i: (i, 0))],
        out_specs=[pl.BlockSpec((TB, N), lambda i: (i, 0)), pl.BlockSpec((N, TB), lambda i: (0, i))],
        out_shape=[jax.ShapeDtypeStruct((L, N), f32), jax.ShapeDtypeStruct((N, L), f32)],
        scratch_shapes=[pltpu.VMEM((1, N), f32)],
        compiler_params=_params(("arbitrary",), 8 * TB * N * 4),
        name="cumsum_rows",
    )(x)


BAND_TQ = 256
BAND_NB = 3


def _band_body(q_ref, k0_ref, k1_ref, k2_ref, v0_ref, v1_ref, v2_ref, b_ref, o_ref, *, nheads):
    i = pl.program_id(0)
    TQ = BAND_TQ
    lane = lax.broadcasted_iota(jnp.int32, (TQ, V7X_LANES), 1)
    lo = lane < 64
    krefs = (k0_ref, k1_ref, k2_ref)
    vrefs = (v0_ref, v1_ref, v2_ref)
    pens = [jnp.where(i - (BAND_NB - 1) + b < 0, NEG, 0.0).astype(f32) for b in range(BAND_NB)]
    for hp in range(nheads // 2):
        cols = slice(hp * V7X_LANES, (hp + 1) * V7X_LANES)
        qp = q_ref[:, cols]
        kp = [kr[:, cols] for kr in krefs]
        vp = [vr[:, cols] for vr in vrefs]
        res = []
        for half in range(2):
            h = 2 * hp + half
            qm = jnp.where(lo if half == 0 else jnp.logical_not(lo), qp, jnp.zeros_like(qp))
            s = [_dot_nt(qm, kp[b]) + b_ref[h, :, b * TQ:(b + 1) * TQ] + pens[b] for b in range(BAND_NB)]
            m = jnp.max(s[0], axis=-1, keepdims=True)
            for b in range(1, BAND_NB):
                m = jnp.maximum(m, jnp.max(s[b], axis=-1, keepdims=True))
            p = [jnp.exp(sb - m) for sb in s]
            l = p[0].sum(axis=-1, keepdims=True)
            o = _dot(p[0].astype(bf16), vp[0])
            for b in range(1, BAND_NB):
                l = l + p[b].sum(axis=-1, keepdims=True)
                o = o + _dot(p[b].astype(bf16), vp[b])
            res.append(o / l)
        o_ref[:, cols] = jnp.where(lo, res[0], res[1]).astype(bf16)


def _toeplitz_bias(table, nq, nk, off):
    H = table.shape[0]
    m = np.arange(nq + nk - 1)
    idx = np.clip(off + nq - 1 - m, -REL_CLIP, REL_CLIP) + REL_CLIP
    w = jnp.concatenate([table[:, idx].astype(f32), jnp.zeros((H, 1), f32)], axis=-1)
    skew = jnp.tile(w, (1, nq))[:, :nq * (nq + nk - 1)].reshape(H, nq, nq + nk - 1)
    return skew[:, :, nq - 1:]


def _band_bias(table):
    qc = np.arange(BAND_TQ)[:, None] // CHUNK
    kc = np.arange(BAND_NB * BAND_TQ)[None, :] // CHUNK
    valid = (kc >= qc) & (kc <= qc + BAND_CHUNKS)
    bias = _toeplitz_bias(table, BAND_TQ, BAND_NB * BAND_TQ, BAND_CHUNKS * CHUNK)
    return jnp.where(valid[None], bias, NEG)


def _band_prompt(q, k, v, bias):
    L, W = q.shape
    TQ = BAND_TQ
    H = bias.shape[0]
    kspec = [pl.BlockSpec((TQ, W), functools.partial(lambda i, d: (jnp.maximum(i - d, 0), 0), d=d))
             for d in (2, 1, 0)]
    est = 2 * 7 * TQ * W * 2 + 2 * H * TQ * BAND_NB * TQ * 4 + 16 * TQ * TQ * 4
    return pl.pallas_call(
        functools.partial(_band_body, nheads=H),
        grid=(L // TQ,),
        in_specs=[pl.BlockSpec((TQ, W), lambda i: (i, 0))] + kspec + kspec
                 + [pl.BlockSpec((H, TQ, BAND_NB * TQ), lambda i: (0, 0, 0))],
        out_specs=pl.BlockSpec((TQ, W), lambda i: (i, 0)),
        out_shape=jax.ShapeDtypeStruct((L, W), bf16),
        compiler_params=_params(("parallel",), est),
        name="band_prompt",
    )(q, k, k, k, v, v, v, bias)


FOX_T = 1024
FOX_VROWS = 80
FOX_QPIECE = 64
FOX_KPIECE = 67
LOG2E = 1.4426950408889634


def _fox_pack_body(q_ref, k_ref, v_ref, f_ref, e_ref, qa_ref, ka_ref, vt_ref, *, nheads, dh):
    T, W2 = q_ref.shape
    a = (f_ref[...] - f_ref[0:1, :]) * LOG2E
    hi = a.astype(bf16)
    r1 = a - hi.astype(f32)
    mid = r1.astype(bf16)
    lo = (r1 - mid.astype(f32)).astype(bf16)
    aug = _dot(jnp.concatenate([hi, mid, lo], axis=1), e_ref[...])
    lane = lax.broadcasted_iota(jnp.int32, (T, W2), 1) % V7X_LANES
    ones_q = jnp.where((lane >= FOX_KPIECE) & (lane < FOX_KPIECE + 3), 1.0, 0.0)
    ones_k = jnp.where((lane >= FOX_QPIECE) & (lane < FOX_QPIECE + 3), 1.0, 0.0)
    qa_ref[...] = (q_ref[...].astype(f32) + aug[:, :W2] + ones_q).astype(bf16)
    ka_ref[...] = (k_ref[...].astype(f32) + aug[:, W2:] + ones_k).astype(bf16)
    vT = v_ref[...].astype(f32).T
    pad = FOX_VROWS - dh
    ones_row = jnp.where(lax.broadcasted_iota(jnp.int32, (pad, T), 0) == 0, 1.0, 0.0).astype(bf16)
    for h in range(nheads):
        vt_ref[h, 0:dh, :] = vT[h * dh:(h + 1) * dh, :].astype(bf16)
        vt_ref[h, dh:FOX_VROWS, :] = ones_row


def _fox_pack(q, k, v, F, nheads):
    L, W2 = q.shape
    W = v.shape[1]
    dh = W // nheads
    T = FOX_T
    e = np.zeros((3 * V7X_LANES, 2 * W2), np.float32)
    for h in range(nheads):
        for t in range(3):
            e[t * V7X_LANES + h, h * V7X_LANES + FOX_QPIECE + t] = 1.0
            e[t * V7X_LANES + h, W2 + h * V7X_LANES + FOX_KPIECE + t] = -1.0
    est = 2 * (4 * T * W2 * 2 + T * W * 2 + nheads * FOX_VROWS * T * 2) + 6 * T * W2 * 4
    return pl.pallas_call(
        functools.partial(_fox_pack_body, nheads=nheads, dh=dh),
        grid=(L // T,),
        in_specs=[pl.BlockSpec((T, W2), lambda i: (i, 0)),
                  pl.BlockSpec((T, W2), lambda i: (i, 0)),
                  pl.BlockSpec((T, W), lambda i: (i, 0)),
                  pl.BlockSpec((T, V7X_LANES), lambda i: (i, 0)),
                  pl.BlockSpec((3 * V7X_LANES, 2 * W2), lambda i: (0, 0))],
        out_specs=[pl.BlockSpec((T, W2), lambda i: (i, 0)),
                   pl.BlockSpec((T, W2), lambda i: (i, 0)),
                   pl.BlockSpec((nheads, FOX_VROWS, T), lambda i: (0, 0, i))],
        out_shape=[jax.ShapeDtypeStruct((L, W2), bf16), jax.ShapeDtypeStruct((L, W2), bf16),
                   jax.ShapeDtypeStruct((nheads, FOX_VROWS, L), bf16)],
        compiler_params=_params(("parallel",), est),
        name="fox_pack",
    )(q, k, v, F, jnp.asarray(e, bf16))


def _fox_body(qi_ref, kj_ref, qa_ref, ka_ref, vt_ref, d_ref, o_ref, m_scr, acc_scr, *, nheads, dh):
    s_id = pl.program_id(0)
    i = qi_ref[s_id]
    j = kj_ref[s_id]
    T = FOX_T

    @pl.when(j == 0)
    def _():
        m_scr[...] = jnp.full_like(m_scr, NEG)
        acc_scr[...] = jnp.zeros_like(acc_scr)

    def step(masked):
        if masked:
            keep = lax.broadcasted_iota(jnp.int32, (T, T), 0) <= lax.broadcasted_iota(jnp.int32, (T, T), 1)
        def scores(h):
            cols = slice(h * V7X_LANES, (h + 1) * V7X_LANES)
            st = _dot_nt(ka_ref[:, cols], qa_ref[:, cols])
            return jnp.where(keep, st, NEG) if masked else st

        ahead = 2
        pending = [scores(h) for h in range(ahead)]
        for h in range(nheads):
            st = pending.pop(0)
            if h + ahead < nheads:
                pending.append(scores(h + ahead))
            d = d_ref[s_id * nheads + h]
            m_prev = m_scr[h:h + 1, :]
            m_new = jnp.maximum(m_prev, jnp.max(st, axis=0, keepdims=True) + d)
            p = jnp.exp2(st - (m_new - d))
            alpha = jnp.exp2(m_prev - m_new)
            acc_scr[h] = acc_scr[h] * alpha + _dot(vt_ref[h], p.astype(bf16))
            m_scr[h:h + 1, :] = m_new

    @pl.when(j < i)
    def _():
        step(False)

    @pl.when(j == i)
    def _():
        step(True)
        for hp in range(nheads // 2):
            a0 = acc_scr[2 * hp]
            a1 = acc_scr[2 * hp + 1]
            o2 = jnp.concatenate([a0[0:dh, :] / a0[dh:dh + 1, :], a1[0:dh, :] / a1[dh:dh + 1, :]], axis=0)
            o_ref[:, hp * V7X_LANES:(hp + 1) * V7X_LANES] = o2.T.astype(bf16)


def _fox_prompt(qa, ka, vt, F, nheads):
    L, W2 = qa.shape
    T = FOX_T
    n = L // T
    dh = V7X_LANES // 2
    W = nheads * dh
    pairs = [(i, j) for i in range(n) for j in range(i + 1)]
    qi = np.array([p[0] for p in pairs], np.int32)
    kj = np.array([p[1] for p in pairs], np.int32)
    fs = F[::T, :nheads]
    d = ((fs[qi] - fs[kj]) * LOG2E).reshape(-1)
    grid_spec = pltpu.PrefetchScalarGridSpec(
        num_scalar_prefetch=2,
        grid=(len(pairs),),
        in_specs=[pl.BlockSpec((T, W2), lambda s, qi, kj: (qi[s], 0)),
                  pl.BlockSpec((T, W2), lambda s, qi, kj: (kj[s], 0)),
                  pl.BlockSpec((nheads, FOX_VROWS, T), lambda s, qi, kj: (0, 0, kj[s])),
                  pl.BlockSpec(memory_space=pltpu.SMEM)],
        out_specs=pl.BlockSpec((T, W), lambda s, qi, kj: (qi[s], 0)),
        scratch_shapes=[pltpu.VMEM((nheads, T), f32), pltpu.VMEM((nheads, FOX_VROWS, T), f32)],
    )
    est = 2 * (2 * T * W2 * 2 + nheads * FOX_VROWS * T * 2 + T * W * 2) + nheads * FOX_VROWS * T * 4 + 24 * T * T * 4
    return pl.pallas_call(
        functools.partial(_fox_body, nheads=nheads, dh=dh),
        grid_spec=grid_spec,
        out_shape=jax.ShapeDtypeStruct((L, W), bf16),
        compiler_params=_params(("arbitrary",), est),
        name="fox_prompt",
    )(jnp.asarray(qi), jnp.asarray(kj), qa, ka, vt, d)


def _cross_body(q_ref, k_ref, v_ref, o_ref, *, nheads, dh):
    for h in range(nheads):
        cols = slice(h * dh, (h + 1) * dh)
        s = _dot_nt(q_ref[:, cols], k_ref[:, cols])
        m = jnp.max(s, axis=-1, keepdims=True)
        p = jnp.exp(s - m)
        l = p.sum(axis=-1, keepdims=True)
        o_ref[:, cols] = (_dot(p.astype(bf16), v_ref[:, cols]) / l).astype(bf16)


def _cross_prompt(q, mk, mv, nheads):
    L, W = q.shape
    S = mk.shape[0]
    TM = min(512, L)
    return pl.pallas_call(
        functools.partial(_cross_body, nheads=nheads, dh=W // nheads),
        grid=(L // TM,),
        in_specs=[pl.BlockSpec((TM, W), lambda i: (i, 0)),
                  pl.BlockSpec((S, W), lambda i: (0, 0)),
                  pl.BlockSpec((S, W), lambda i: (0, 0))],
        out_specs=pl.BlockSpec((TM, W), lambda i: (i, 0)),
        out_shape=jax.ShapeDtypeStruct((L, W), bf16),
        compiler_params=_params(("parallel",), 8 * TM * W * 2 + 8 * TM * S * 4),
        name="cross_prompt",
    )(q, mk, mv)


def _sample_attn_body(*refs, nheads, mode):
    q_ref, kc_ref, vc_ref, kn_ref, vn_ref = refs[:5]
    if mode == "rel":
        bc_ref, bn_ref, o_ref = refs[5:]
    else:
        fq_ref, fkc_ref, fkn_ref, o_ref = refs[5:]

    m_q, W = q_ref.shape
    dh = W // nheads
    R = nheads * m_q
    q = q_ref[...].astype(f32)
    qt = jnp.concatenate([q] * nheads, axis=0)
    hrow = lax.broadcasted_iota(jnp.int32, (R, W), 0) // m_q
    hlane = lax.broadcasted_iota(jnp.int32, (R, W), 1) // dh
    qbd = jnp.where(hrow == hlane, qt, 0.0).astype(bf16)

    def rows_of(f_ref):
        n = f_ref.shape[-1]
        return jnp.concatenate([jnp.broadcast_to(f_ref[h:h + 1, :], (m_q, n)) for h in range(nheads)], axis=0)

    sc = _dot_nt(qbd, kc_ref[...].astype(bf16))
    sn = _dot_nt(qbd, kn_ref[...].astype(bf16))
    if mode == "rel":
        sc = sc + bc_ref[...]
        sn = sn + bn_ref[...]
    else:
        sc = sc + (fq_ref[...] - rows_of(fkc_ref))
        sn = sn + (fq_ref[...] - rows_of(fkn_ref))
        qpos = lax.broadcasted_iota(jnp.int32, (R, m_q), 0) % m_q
        kpos = lax.broadcasted_iota(jnp.int32, (R, m_q), 1)
        sn = jnp.where(kpos <= qpos, sn, NEG)
    m = jnp.maximum(jnp.max(sc, axis=-1, keepdims=True), jnp.max(sn, axis=-1, keepdims=True))
    pc = jnp.exp(sc - m)
    pn = jnp.exp(sn - m)
    l = pc.sum(axis=-1, keepdims=True) + pn.sum(axis=-1, keepdims=True)
    o = _dot(pc.astype(bf16), vc_ref[...].astype(bf16)) + _dot(pn.astype(bf16), vn_ref[...].astype(bf16))
    o = o / l
    hl = lax.broadcasted_iota(jnp.int32, (m_q, W), 1) // dh
    out = jnp.zeros((m_q, W), f32)
    for h in range(nheads):
        out = out + jnp.where(hl == h, o[h * m_q:(h + 1) * m_q, :], 0.0)
    o_ref[...] = out.astype(bf16)


def _sample_attn(q, kc, vc, nheads, mode, kn, vn, extra):
    B, m_q, W = q.shape
    P = kc.shape[1]
    R = nheads * m_q
    per_b = lambda *shape: pl.BlockSpec((None,) + shape, lambda b: (b,) + (0,) * len(shape))
    shared = lambda *shape: pl.BlockSpec(shape, lambda b: (0,) * len(shape))
    in_specs = [per_b(m_q, W), per_b(P, W), per_b(P, W), per_b(m_q, W), per_b(m_q, W)]
    if mode == "rel":
        in_specs += [shared(R, P), shared(R, m_q)]
    else:
        in_specs += [per_b(R, 1), per_b(nheads, P), per_b(nheads, m_q)]
    est = 2 * 2 * P * W * 4 + 2 * P * W * 2 + 6 * R * P * 4 + 4 * R * W * 4
    return pl.pallas_call(
        functools.partial(_sample_attn_body, nheads=nheads, mode=mode),
        grid=(B,),
        in_specs=in_specs,
        out_specs=per_b(m_q, W),
        out_shape=jax.ShapeDtypeStruct((B, m_q, W), bf16),
        compiler_params=_params(("parallel",), est),
        name="sample_attn_" + mode,
    )(q, kc, vc, kn, vn, *extra)


def _cross_sample_body(q_ref, kc_ref, vc_ref, o_ref, *, nheads):
    m_q, W = q_ref.shape
    dh = W // nheads
    npc = dh // kc_ref.shape[1]
    P = kc_ref.shape[0] // (nheads * npc)
    hs = range(nheads)

    def head_rows(ref, h):
        pieces = [ref[pl.ds(h * npc + j, P, stride=nheads * npc), :] for j in range(npc)]
        return jnp.concatenate(pieces, axis=-1).astype(bf16)

    q = q_ref[...]
    sc = [_dot_nt(q[:, h * dh:(h + 1) * dh], head_rows(kc_ref, h)) for h in hs]
    m = [jnp.max(s, axis=-1, keepdims=True) for s in sc]
    pc = [jnp.exp(sc[h] - m[h]) for h in hs]
    l = [p.sum(axis=-1, keepdims=True) for p in pc]
    o = [_dot(pc[h].astype(bf16), head_rows(vc_ref, h)) / l[h] for h in hs]
    o_ref[...] = jnp.concatenate(o, axis=-1).astype(bf16)


def _cross_sample(q, kc_all, vc_all, layer):
    B, m_q, W = q.shape
    nl, _, P, nheads, dh = kc_all.shape
    nrows = P * nheads * (dh // V7X_LANES)
    kc_all = kc_all.reshape(nl, B, nrows, V7X_LANES)
    vc_all = vc_all.reshape(nl, B, nrows, V7X_LANES)
    cache = pl.BlockSpec((None, None, nrows, V7X_LANES), lambda b: (layer, b, 0, 0))
    qo = pl.BlockSpec((None, m_q, W), lambda b: (b, 0, 0))
    est = 2 * 2 * nrows * V7X_LANES * 4 + 4 * P * W * 2 + 8 * nheads * m_q * P * 4
    return pl.pallas_call(
        functools.partial(_cross_sample_body, nheads=nheads),
        grid=(B,),
        in_specs=[qo, cache, cache],
        out_specs=qo,
        out_shape=jax.ShapeDtypeStruct((B, m_q, W), bf16),
        compiler_params=_params(("parallel",), est),
        name="cross_sample",
    )(q, kc_all, vc_all)


GDN_HALO = 8


def _softplus(x):
    return jnp.maximum(x, 0.0) + jnp.log(1.0 + jnp.exp(-jnp.abs(x)))


def _split2(a):
    hi = a.astype(bf16)
    return hi, (a - hi.astype(f32)).astype(bf16)


def _split3(a):
    hi = a.astype(bf16)
    r1 = a - hi.astype(f32)
    mid = r1.astype(bf16)
    return hi, mid, (r1 - mid.astype(f32)).astype(bf16)


def _gdn_body(x_ref, ab_ref, z_ref, cs_ref, s0_ref, cw_ref, alog_ref, dtb_ref, ng_ref,
              o_ref, sf_ref, cf_ref, xbuf, s_scr, *, C, NS, nsteps, nheads, dk):
    c = pl.program_id(1)
    H0 = GDN_HALO
    W = nheads * dk
    R = NS * C
    G = V7X_LANES // C
    NG = nheads // G
    GW = G * dk

    @pl.when(c == 0)
    def _():
        xbuf[H0 - (CONV - 1):H0, :] = cs_ref[...]
        s_scr[...] = s0_ref[...]

    xbuf[H0:H0 + R, :] = x_ref[...]
    cw = cw_ref[...]
    conv = xbuf[H0 - 3:H0 - 3 + R, :] * cw[0:1, :]
    for jj in range(1, CONV):
        conv = conv + xbuf[H0 - 3 + jj:H0 - 3 + jj + R, :] * cw[jj:jj + 1, :]
    tail = xbuf[H0 + R - (CONV - 1):H0 + R, :]
    xbuf[H0 - (CONV - 1):H0, :] = tail

    @pl.when(c == nsteps - 1)
    def _():
        cf_ref[...] = tail

    act = _silu(conv)
    ab = ab_ref[...]
    gfull = -jnp.exp(alog_ref[...]) * _softplus(ab + dtb_ref[...])
    bfull = 1.0 / (1.0 + jnp.exp(-ab))
    z = z_ref[...]
    ng = ng_ref[...]
    hs = range(nheads)

    qn = [act[:, h * dk:(h + 1) * dk] for h in hs]
    kn = [act[:, W + h * dk:W + (h + 1) * dk] for h in hs]
    qn = [q * lax.rsqrt(jnp.sum(q * q, axis=-1, keepdims=True) + EPS) * (dk ** -0.5) for q in qn]
    kn = [k * lax.rsqrt(jnp.sum(k * k, axis=-1, keepdims=True) + EPS) for k in kn]
    vn = [act[:, 2 * W + h * dk:2 * W + (h + 1) * dk] for h in hs]

    r_cc = lax.broadcasted_iota(jnp.int32, (C, C), 0)
    c_cc = lax.broadcasted_iota(jnp.int32, (C, C), 1)
    tri_f = (r_cc >= c_cc).astype(f32)
    ri = lax.broadcasted_iota(jnp.int32, (C, V7X_LANES), 0)
    jl = lax.broadcasted_iota(jnp.int32, (C, V7X_LANES), 1) % C
    lblk = lax.broadcasted_iota(jnp.int32, (C, V7X_LANES), 1) // C
    tri_g = ri >= jl
    strict_g = ri > jl
    eye_g = ri == jl
    r128 = lax.broadcasted_iota(jnp.int32, (V7X_LANES, V7X_LANES), 0)
    l128 = lax.broadcasted_iota(jnp.int32, (V7X_LANES, V7X_LANES), 1)
    same_blk = (r128 // C) == (l128 // C)
    wide_blk = (lax.broadcasted_iota(jnp.int32, (V7X_LANES, GW), 0) // C
                == lax.broadcasted_iota(jnp.int32, (V7X_LANES, GW), 1) // dk)

    def bd_sq(p):
        return jnp.where(same_blk, jnp.concatenate([p] * G, axis=0), jnp.zeros((), p.dtype))

    def bd_wide(xs):
        row = jnp.concatenate(xs, axis=1)
        return jnp.where(wide_blk, jnp.concatenate([row] * G, axis=0), 0.0).astype(bf16)

    def gmm3(xg, pg):
        xh, xl = _split2(xg)
        ph, plo = _split2(pg)
        bh = bd_sq(ph)
        return _dot(xh, bh) + (_dot(xh, bd_sq(plo)) + _dot(xl, bh))

    prep = []
    for s in range(NS):
        sl = slice(s * C, (s + 1) * C)
        gc = _dot_hi(tri_f, gfull[sl])
        egc = jnp.exp(gc)
        kdec = jnp.exp(gc[C - 1:C, :] - gc)
        gcT = jnp.concatenate([gc] * G, axis=0).T
        bcols = [bfull[sl, nheads + h:nheads + h + 1] for h in hs]
        ecols = [egc[:, h:h + 1] for h in hs]
        kbs = [kn[h][sl] * bcols[h] for h in hs]
        Ms, Aqks = [], []
        for grp in range(NG):
            heads = range(grp * G, (grp + 1) * G)
            gcol = jnp.broadcast_to(gc[:, grp * G:grp * G + 1], (C, V7X_LANES))
            grow = gcT[grp * G:grp * G + 1, :]
            for g in range(1, G):
                h = grp * G + g
                gcol = jnp.where(lblk == g, gc[:, h:h + 1], gcol)
                grow = jnp.where(lblk[0:1, :] == g, gcT[h:h + 1, :], grow)
            Lm = jnp.where(tri_g, jnp.exp(jnp.where(tri_g, gcol - grow, 0.0)), 0.0)
            kbd = bd_wide([kn[h][sl] for h in heads])
            kb_row = jnp.concatenate([kbs[h] for h in heads], axis=1).astype(bf16)
            q_row = jnp.concatenate([qn[h][sl] for h in heads], axis=1).astype(bf16)
            Ms.append(jnp.where(strict_g, _dot_nt(kb_row, kbd) * Lm, 0.0))
            Aqks.append(jnp.where(tri_g, _dot_nt(q_row, kbd) * Lm, 0.0).astype(bf16))
        Xs = [jnp.where(eye_g, 1.0, 0.0) - M for M in Ms]
        Pws = [gmm3(M, M) for M in Ms]
        e = 2
        while e < C:
            Xs = [X + gmm3(X, Pw) for X, Pw in zip(Xs, Pws)]
            e *= 2
            if e < C:
                Pws = [gmm3(Pw, Pw) for Pw in Pws]
        us, ws = [], []
        for grp in range(NG):
            heads = range(grp * G, (grp + 1) * G)
            Tm = Xs[grp].astype(bf16)
            U = _dot(Tm, bd_wide([vn[h][sl] * bcols[h] for h in heads]))
            Wm = _dot(Tm, bd_wide([kbs[h] * ecols[h] for h in heads]))
            for g, h in enumerate(heads):
                us.append(U[:, g * dk:(g + 1) * dk])
                ws.append(Wm[:, g * dk:(g + 1) * dk].astype(bf16))
        qgs = [(qn[h][sl] * ecols[h]).astype(bf16) for h in hs]
        kgTs = [(kn[h][sl] * kdec[:, h:h + 1]).T.astype(bf16) for h in hs]
        decs = [egc[C - 1:C, h:h + 1] for h in hs]
        prep.append((us, ws, Aqks, qgs, kgTs, decs))

    Ss = [s_scr[h] for h in hs]
    for s in range(NS):
        us, ws, Aqks, qgs, kgTs, decs = prep[s]
        Sbs = [S.astype(bf16) for S in Ss]
        vnews = [us[h] - _dot(ws[h], Sbs[h]) for h in hs]
        vnbs = [v.astype(bf16) for v in vnews]
        intra = [_dot(Aqks[grp], bd_wide([vnews[h] for h in range(grp * G, (grp + 1) * G)])) for grp in range(NG)]
        os_ = [_dot(qgs[h], Sbs[h]) + intra[h // G][:, (h % G) * dk:(h % G + 1) * dk] for h in hs]
        Ss = [Ss[h] * decs[h] + _dot(kgTs[h], vnbs[h]) for h in hs]
        zs = z[s * C:(s + 1) * C, :]
        ys = [(_rms(os_[h], ng) * _silu(zs[:, h * dk:(h + 1) * dk])).astype(bf16) for h in hs]
        o_ref[s * C:(s + 1) * C, :] = jnp.concatenate(ys, axis=-1)
    for h in hs:
        s_scr[h] = Ss[h]

    @pl.when(c == nsteps - 1)
    def _():
        sf_ref[...] = s_scr[...]


def _gdn_core(qkv, ab, z, conv_state, S0, conv_w, a_log, dt_bias, norm_g, C, NS):
    T = qkv.shape[0]
    B, H, dk, dv = S0.shape
    W = H * dk
    R = NS * C
    nsteps = T // (B * R)
    pad = lambda vec: jnp.zeros((1, V7X_LANES), f32).at[0, :H].set(vec.astype(f32))
    row = lambda width: pl.BlockSpec((R, width), lambda b, c: (b * nsteps + c, 0))
    est = 2 * R * (3 * W + W + V7X_LANES) * 4 + 3 * H * dk * dv * 4 * 2 + 12 * R * 3 * W * 4
    return pl.pallas_call(
        functools.partial(_gdn_body, C=C, NS=NS, nsteps=nsteps, nheads=H, dk=dk),
        grid=(B, nsteps),
        in_specs=[row(3 * W), row(V7X_LANES), row(W),
                  pl.BlockSpec((None, CONV - 1, 3 * W), lambda b, c: (b, 0, 0)),
                  pl.BlockSpec((None, H, dk, dv), lambda b, c: (b, 0, 0, 0)),
                  pl.BlockSpec((CONV, 3 * W), lambda b, c: (0, 0)),
                  pl.BlockSpec((1, V7X_LANES), lambda b, c: (0, 0)),
                  pl.BlockSpec((1, V7X_LANES), lambda b, c: (0, 0)),
                  pl.BlockSpec((1, dv), lambda b, c: (0, 0))],
        out_specs=[row(W),
                   pl.BlockSpec((None, H, dk, dv), lambda b, c: (b, 0, 0, 0)),
                   pl.BlockSpec((None, CONV - 1, 3 * W), lambda b, c: (b, 0, 0))],
        out_shape=[jax.ShapeDtypeStruct((T, W), bf16),
                   jax.ShapeDtypeStruct((B, H, dk, dv), f32),
                   jax.ShapeDtypeStruct((B, CONV - 1, 3 * W), f32)],
        scratch_shapes=[pltpu.VMEM((GDN_HALO + R, 3 * W), f32), pltpu.VMEM((H, dk, dv), f32)],
        compiler_params=_params(("parallel", "arbitrary"), est),
        name="gdn_core",
    )(qkv, ab, z, conv_state, S0, conv_w, pad(a_log), pad(dt_bias), norm_g.reshape(1, dv))


def kernel(x_prompt, x_sample, mem_prompt, cache_a_k, cache_a_v, cache_b_k, cache_b_v, cache_b_logf, state_gdn, state_gdn_conv, cache_mem_k, cache_mem_v, norm_g, mem_norm_g, final_norm_g, ffn_w_gate, ffn_w_up, ffn_w_down, xa_w_q, xa_w_k, xa_w_v, xa_w_o, ab_w_in, ab_b_f, ab_rel_bias, ab_w_o, gdn_w_in, gdn_conv_w, gdn_a_log, gdn_dt_bias, gdn_norm_g, gdn_w_o):
    depth = norm_g.shape[0]
    BP, SEQ, D = x_prompt.shape
    BS, MS, _ = x_sample.shape
    assert BP == 1
    HA = HB = ab_b_f.shape[1]
    WA = WB = (ab_w_in.shape[2] - HB) // 6
    DHA = WA // HA
    HG = gdn_a_log.shape[1]
    WG = gdn_w_o.shape[1]
    DKG = WG // HG
    HX = cache_mem_k.shape[3]
    DHX = cache_mem_k.shape[4]
    WX = HX * DHX
    PA = cache_a_k.shape[2]
    PB = cache_b_k.shape[2]
    LANES = V7X_LANES

    xp = x_prompt.reshape(SEQ, D)
    xs = x_sample.reshape(BS * MS, D)

    wgate = ffn_w_gate.astype(bf16)
    wup = ffn_w_up.astype(bf16)
    wdown = ffn_w_down.astype(bf16)
    wxq = xa_w_q.astype(bf16)
    wxo = xa_w_o.astype(bf16)
    wxkv = jnp.concatenate([xa_w_k, xa_w_v], axis=-1).astype(bf16)

    outs = {k: [] for k in ("akp", "avp", "bkp", "bvp", "blp", "sgp", "scp", "mkp", "mvp",
                            "aks", "avs", "bks", "bvs", "bls", "sgs", "scs")}

    for l in range(depth):
        xp = _ffn(xp, norm_g[l, 0], wgate[l, 0], wup[l, 0], wdown[l, 0])
        xs = _ffn(xs, norm_g[l, 0], wgate[l, 0], wup[l, 0], wdown[l, 0])

        if l % 2 == 0:
            e = l // 2
            w_in = ab_w_in[e]
            nq = 3 * WA + 3 * WB
            w_pad = jnp.concatenate([w_in, jnp.zeros((D, LANES - HB), f32)], axis=-1).astype(bf16)
            b_pad = jnp.zeros((1, LANES), f32).at[0, :HB].set(ab_b_f[e].astype(f32))
            spec = [(0, WA, bf16, DHA ** -0.5, None),
                    (WA, WA, f32, 1.0, None), (WA, WA, bf16, 1.0, None),
                    (2 * WA, WA, f32, 1.0, None), (2 * WA, WA, bf16, 1.0, None),
                    (3 * WA, WB, bf16, (WB // HB) ** -0.5, None),
                    (3 * WA + WB, WB, f32, 1.0, None), (3 * WA + WB, WB, bf16, 1.0, None),
                    (3 * WA + 2 * WB, WB, f32, 1.0, None), (3 * WA + 2 * WB, WB, bf16, 1.0, None),
                    (nq, LANES, f32, 1.0, "logsig")]
            wo = ab_w_o[e].astype(bf16)

            qa, ka, kab, va, vab = _norm_proj(xp, norm_g[l, 1], w_pad[:, :3 * WA], spec[:5])
            oa = _band_prompt(qa, kab, vab, _band_bias(ab_rel_bias[e]))
            DHB = WB // HB
            spread = lambda w: jnp.pad(w.reshape(D, HB, DHB), ((0, 0), (0, 0), (0, LANES - DHB))).reshape(D, HB * LANES)
            wq_b = w_in[:, 3 * WA:3 * WA + WB]
            wk_b = w_in[:, 3 * WA + WB:3 * WA + 2 * WB]
            w_b = jnp.concatenate([spread(wq_b), wk_b, spread(wk_b), w_in[:, 3 * WA + 2 * WB:nq],
                                   w_pad[:, nq:]], axis=-1).astype(bf16)
            W2 = HB * LANES
            spec_b = [(0, W2, bf16, DHB ** -0.5 * LOG2E, None),
                      (W2, WB, f32, 1.0, None), (W2 + WB, W2, bf16, 1.0, None),
                      (2 * W2 + WB, WB, f32, 1.0, None), (2 * W2 + WB, WB, bf16, 1.0, None),
                      (2 * W2 + 2 * WB, LANES, f32, 1.0, "logsig")]
            qbs, kb, kbs, vb, vbb, lf = _norm_proj(xp, norm_g[l, 1], w_b, spec_b, bias=b_pad)
            F, _ = _cumsum_rows(lf)
            qaug, kaug, vt = _fox_pack(qbs, kbs, vbb, F, HB)
            ob = _fox_prompt(qaug, kaug, vt, F, HB)
            res_p = [(oa, wo[:WA]), (ob, wo[WA:])]
            keep = min(BAND_CHUNKS * CHUNK, SEQ)
            outs["akp"].append(ka[SEQ - keep:].reshape(1, keep, HA, DHA))
            outs["avp"].append(va[SEQ - keep:].reshape(1, keep, HA, DHA))
            outs["bkp"].append(kb.reshape(1, SEQ, HB, WB // HB))
            outs["bvp"].append(vb.reshape(1, SEQ, HB, WB // HB))
            outs["blp"].append(lf[:, :HB].reshape(1, SEQ, HB))

            qa, ka, _, va, _, qb, kb, _, vb, _, lf = _norm_proj(xs, norm_g[l, 1], w_pad, spec, bias=b_pad)
            bias_s = _toeplitz_bias(ab_rel_bias[e], MS, PA + MS, PA).reshape(HA * MS, PA + MS)
            oa = _sample_attn(qa.reshape(BS, MS, WA), cache_a_k[e].reshape(BS, PA, WA), cache_a_v[e].reshape(BS, PA, WA),
                              HA, "rel", ka.reshape(BS, MS, WA), va.reshape(BS, MS, WA),
                              extra=(bias_s[:, :PA], bias_s[:, PA:]))
            lfn = lf[:, :HB].reshape(BS, MS, HB)
            lcat = jnp.concatenate([cache_b_logf[e].astype(f32), lfn], axis=1)
            LP = -(-(PB + MS) // LANES) * LANES
            lcat = jnp.pad(lcat.transpose(1, 0, 2).reshape(PB + MS, BS * HB), ((0, LP - PB - MS), (0, 0)))
            _, FTs = _cumsum_rows(lcat)
            FTs = FTs.reshape(BS, HB, LP)
            fq = FTs[:, :, PB:PB + MS].reshape(BS, HB * MS, 1)
            ob = _sample_attn(qb.reshape(BS, MS, WB), cache_b_k[e].reshape(BS, PB, WB), cache_b_v[e].reshape(BS, PB, WB),
                              HB, "fox", kb.reshape(BS, MS, WB), vb.reshape(BS, MS, WB),
                              extra=(fq, FTs[:, :, :PB], FTs[:, :, PB:PB + MS]))
            res_s = [(oa.reshape(BS * MS, WA), wo[:WA]), (ob.reshape(BS * MS, WB), wo[WA:])]
            outs["aks"].append(ka.reshape(BS, MS, HA, DHA))
            outs["avs"].append(va.reshape(BS, MS, HA, DHA))
            outs["bks"].append(kb.reshape(BS, MS, HB, WB // HB))
            outs["bvs"].append(vb.reshape(BS, MS, HB, WB // HB))
            outs["bls"].append(lfn)
        else:
            o = l // 2
            w_in = gdn_w_in[o]
            w_pad = jnp.concatenate([w_in, jnp.zeros((D, LANES - 2 * HG), f32)], axis=-1).astype(bf16)
            spec = [(0, 3 * WG, f32, 1.0, None), (3 * WG, WG, f32, 1.0, None), (4 * WG, LANES, f32, 1.0, None)]
            wo = gdn_w_o[o].astype(bf16)

            qkv, z, ab = _norm_proj(xp, norm_g[l, 1], w_pad, spec)
            y, Sp, cvp = _gdn_core(qkv, ab, z, jnp.zeros((1, CONV - 1, 3 * WG), f32), jnp.zeros((1, HG, DKG, DKG), f32),
                                   gdn_conv_w[o], gdn_a_log[o], gdn_dt_bias[o], gdn_norm_g[o], CHUNK, 4)
            res_p = [(y, wo)]
            outs["sgp"].append(Sp)
            outs["scp"].append(cvp)

            qkv, z, ab = _norm_proj(xs, norm_g[l, 1], w_pad, spec)
            y, Ss, cvs = _gdn_core(qkv, ab, z, state_gdn_conv[o].astype(f32), state_gdn[o].astype(f32),
                                   gdn_conv_w[o], gdn_a_log[o], gdn_dt_bias[o], gdn_norm_g[o], MS, 1)
            res_s = [(y, wo)]
            outs["sgs"].append(Ss)
            outs["scs"].append(cvs)

        mk, mkb, mv, mvb = _norm_proj(mem_prompt.reshape(MEM, D), mem_norm_g[l], wxkv[l],
                                      [(0, WX, f32, 1.0, None), (0, WX, bf16, 1.0, None),
                                       (WX, WX, f32, 1.0, None), (WX, WX, bf16, 1.0, None)])
        outs["mkp"].append(mk.reshape(1, MEM, HX, DHX))
        outs["mvp"].append(mv.reshape(1, MEM, HX, DHX))
        qspec = [(0, WX, bf16, DHX ** -0.5, None)]
        xp, q = _norm_proj(xp, norm_g[l, 2], wxq[l], qspec, res=res_p)
        ocp = _cross_prompt(q, mkb, mvb, HX)
        xs, q = _norm_proj(xs, norm_g[l, 2], wxq[l], qspec, res=res_s)
        ocs = _cross_sample(q.reshape(BS, MS, WX), cache_mem_k, cache_mem_v, l)

        fin = final_norm_g if l == depth - 1 else None
        xp = _ffn(xp, norm_g[l, 3], wgate[l, 1], wup[l, 1], wdown[l, 1], fin, res=(ocp, wxo[l]))
        xs = _ffn(xs, norm_g[l, 3], wgate[l, 1], wup[l, 1], wdown[l, 1], fin, res=(ocs.reshape(BS * MS, WX), wxo[l]))

    st = lambda k: jnp.stack(outs[k])
    return (xp.reshape(BP, SEQ, D), xs.reshape(BS, MS, D),
            st("akp"), st("avp"), st("bkp"), st("bvp"), st("blp"),
            st("sgp"), st("scp"), st("mkp"), st("mvp"),
            st("aks"), st("avs"), st("bks"), st("bvs"), st("bls"),
            st("sgs"), st("scs"))
```

```python
import functools

import numpy as np
import jax
import jax.numpy as jnp
from jax import lax
from jax.experimental import pallas as pl
from jax.experimental.pallas import tpu as pltpu

f32 = jnp.float32
bf16 = jnp.bfloat16
EPS = 1e-6
NEG = -1e30

V7X_VMEM_BYTES = 64 * 1024 * 1024
V7X_LANES = 128
MIB = 1024 * 1024

CHUNK = 64
BAND_CHUNKS = 8
REL_CLIP = 256
CONV = 4
MEM = 256


def _params(sem, est_bytes, fuse_inputs=None):
    limit = int(min(V7X_VMEM_BYTES - 8 * MIB, max(32 * MIB, est_bytes + 8 * MIB)))
    return pltpu.CompilerParams(dimension_semantics=sem, vmem_limit_bytes=limit, allow_input_fusion=fuse_inputs)


def _rms(x, g):
    ms = jnp.mean(x * x, axis=-1, keepdims=True)
    return x * lax.rsqrt(ms + EPS) * g


def _silu(x):
    h = 0.5 * x
    return h + h * jnp.tanh(h)


def _dot(a, b):
    return jnp.dot(a, b, preferred_element_type=f32)


def _dot_nt(a, b):
    return lax.dot_general(a, b, (((1,), (1,)), ((), ())), preferred_element_type=f32)


def _dot_hi(a, b):
    return jnp.dot(a, b, preferred_element_type=f32, precision=lax.Precision.HIGHEST)


def _ffn_body(*refs, nj, final, has_res):
    x_ref, g_ref, wg_ref, wu_ref, wd_ref = refs[:5]
    pos = 5
    if has_res:
        a_ref, wo_ref = refs[pos:pos + 2]
        pos += 2
    if final:
        gf_ref = refs[pos]
        pos += 1
    o_ref, h_scr, acc = refs[pos:]
    j = pl.program_id(1)

    @pl.when(j == 0)
    def _():
        x = x_ref[...]
        if has_res:
            x = x + _dot(a_ref[...], wo_ref[...])
        o_ref[...] = x
        h_scr[...] = _rms(x, g_ref[...]).astype(bf16)
        acc[...] = jnp.zeros_like(acc)

    h = h_scr[...]
    a = _silu(_dot(h, wg_ref[...])) * _dot(h, wu_ref[...])
    acc[...] += _dot(a.astype(bf16), wd_ref[...])

    @pl.when(j == nj - 1)
    def _():
        y = o_ref[...] + 0.5 * acc[...]
        if final:
            y = _rms(y, gf_ref[...])
        o_ref[...] = y


def _ffn(x, g, wg, wu, wd, final_g=None, res=None):
    T, D = x.shape
    FF = wg.shape[1]
    TM = min(1024, T)
    TF = min(1024, FF)
    nj = FF // TF
    final = final_g is not None
    in_specs = [pl.BlockSpec((TM, D), lambda i, j: (i, 0)),
                pl.BlockSpec((1, D), lambda i, j: (0, 0)),
                pl.BlockSpec((D, TF), lambda i, j: (0, j)),
                pl.BlockSpec((D, TF), lambda i, j: (0, j)),
                pl.BlockSpec((TF, D), lambda i, j: (j, 0))]
    args = [x, g.reshape(1, D), wg, wu, wd]
    est = 2 * (2 * TM * D * 4) + 2 * 3 * D * TF * 2 + TM * D * 6 + 3 * TM * TF * 4
    if res is not None:
        a, wo = res
        K = a.shape[1]
        in_specs += [pl.BlockSpec((TM, K), lambda i, j: (i, 0)), pl.BlockSpec((K, D), lambda i, j: (0, 0))]
        args += [a, wo]
        est += 2 * TM * K * 2 + 2 * K * D * 2
    if final:
        in_specs.append(pl.BlockSpec((1, D), lambda i, j: (0, 0)))
        args.append(final_g.reshape(1, D))
    return pl.pallas_call(
        functools.partial(_ffn_body, nj=nj, final=final, has_res=res is not None),
        grid=(T // TM, nj),
        in_specs=in_specs,
        out_specs=pl.BlockSpec((TM, D), lambda i, j: (i, 0)),
        out_shape=jax.ShapeDtypeStruct((T, D), f32),
        scratch_shapes=[pltpu.VMEM((TM, D), bf16), pltpu.VMEM((TM, D), f32)],
        compiler_params=_params(("parallel", "arbitrary"), est),
        name="ffn",
    )(*args)


def _log_sigmoid(x):
    return jnp.minimum(x, 0.0) - jnp.log(1.0 + jnp.exp(-jnp.abs(x)))


def _norm_proj_body(*refs, outs, has_bias, nres):
    x_ref, g_ref, w_ref = refs[:3]
    pos = 3
    b_ref = None
    if has_bias:
        b_ref = refs[pos]
        pos += 1
    x = x_ref[...]
    for p in range(nres):
        x = x + _dot(refs[pos][...], refs[pos + 1][...])
        pos += 2
    o_refs = refs[pos:]
    if nres:
        o_refs[0][...] = x
        o_refs = o_refs[1:]
    h = _rms(x, g_ref[...]).astype(bf16)
    cache = {}
    for o_ref, (off, n, dt, scale, act) in zip(o_refs, outs):
        if (off, n) not in cache:
            cache[(off, n)] = _dot(h, w_ref[:, off:off + n])
        r = cache[(off, n)]
        if act == "logsig":
            r = _log_sigmoid(r + b_ref[...])
        if scale != 1.0:
            r = r * scale
        o_ref[...] = r.astype(dt)


def _norm_proj(x, g, w, outs, bias=None, res=()):
    T, D = x.shape
    N = w.shape[1]
    TM = min(512, T)
    in_specs = [pl.BlockSpec((TM, D), lambda i: (i, 0)),
                pl.BlockSpec((1, D), lambda i: (0, 0)),
                pl.BlockSpec((D, N), lambda i: (0, 0))]
    args = [x, g.reshape(1, D), w]
    if bias is not None:
        in_specs.append(pl.BlockSpec((1, bias.shape[-1]), lambda i: (0, 0)))
        args.append(bias)
    est = 2 * TM * D * 4 + 2 * D * N * 2 + sum(2 * TM * n * 4 for (_, n, _, _, _) in outs) + TM * N * 4
    for a, wr in res:
        K = a.shape[1]
        in_specs += [pl.BlockSpec((TM, K), lambda i: (i, 0)), pl.BlockSpec((K, D), lambda i: (0, 0))]
        args += [a, wr]
        est += 2 * TM * K * 2 + 2 * K * D * 2
    out_specs = [pl.BlockSpec((TM, n), lambda i: (i, 0)) for (_, n, _, _, _) in outs]
    out_shape = [jax.ShapeDtypeStruct((T, n), dt) for (_, n, dt, _, _) in outs]
    if res:
        out_specs.insert(0, pl.BlockSpec((TM, D), lambda i: (i, 0)))
        out_shape.insert(0, jax.ShapeDtypeStruct((T, D), f32))
        est += 2 * TM * D * 4
    return pl.pallas_call(
        functools.partial(_norm_proj_body, outs=tuple(outs), has_bias=bias is not None, nres=len(res)),
        grid=(T // TM,),
        in_specs=in_specs,
        out_specs=out_specs,
        out_shape=out_shape,
        compiler_params=_params(("parallel",), est),
        name="norm_proj",
    )(*args)


def _cumsum_body(x_ref, f_ref, ft_ref, carry, *, nsub):
    @pl.when(pl.program_id(0) == 0)
    def _():
        carry[...] = jnp.zeros_like(carry)

    r = lax.broadcasted_iota(jnp.int32, (V7X_LANES, V7X_LANES), 0)
    c = lax.broadcasted_iota(jnp.int32, (V7X_LANES, V7X_LANES), 1)
    tri = (r >= c).astype(f32)
    run = carry[...]
    ncol = x_ref.shape[1] // V7X_LANES
    for sb in range(nsub):
        rows = slice(sb * V7X_LANES, (sb + 1) * V7X_LANES)
        blk = _dot_hi(tri, x_ref[rows, :]) + run
        f_ref[rows, :] = blk
        for cb in range(ncol):
            cols = slice(cb * V7X_LANES, (cb + 1) * V7X_LANES)
            ft_ref[cols, rows] = blk[:, cols].T
        run = blk[V7X_LANES - 1:V7X_LANES, :]
    carry[...] = run


def _cumsum_rows(x):
    L, N = x.shape
    nsub = 8 if L % (8 * V7X_LANES) == 0 else 1
    TB = nsub * V7X_LANES
    return pl.pallas_call(
        functools.partial(_cumsum_body, nsub=nsub),
        grid=(L // TB,),
        in_specs=[pl.BlockSpec((TB, N), lambda i: (i, 0))],
        out_specs=[pl.BlockSpec((TB, N), lambda i: (i, 0)), pl.BlockSpec((N, TB), lambda i: (0, i))],
        out_shape=[jax.ShapeDtypeStruct((L, N), f32), jax.ShapeDtypeStruct((N, L), f32)],
        scratch_shapes=[pltpu.VMEM((1, N), f32)],
        compiler_params=_params(("arbitrary",), 8 * TB * N * 4),
        name="cumsum_rows",
    )(x)


BAND_TQ = 256
BAND_NB = 3


def _band_body(q_ref, k0_ref, k1_ref, k2_ref, v0_ref, v1_ref, v2_ref, b_ref, o_ref, *, nheads):
    i = pl.program_id(0)
    TQ = BAND_TQ
    lane = lax.broadcasted_iota(jnp.int32, (TQ, V7X_LANES), 1)
    lo = lane < 64
    krefs = (k0_ref, k1_ref, k2_ref)
    vrefs = (v0_ref, v1_ref, v2_ref)
    pens = [jnp.where(i - (BAND_NB - 1) + b < 0, NEG, 0.0).astype(f32) for b in range(BAND_NB)]
    for hp in range(nheads // 2):
        cols = slice(hp * V7X_LANES, (hp + 1) * V7X_LANES)
        qp = q_ref[:, cols]
        kp = [kr[:, cols] for kr in krefs]
        vp = [vr[:, cols] for vr in vrefs]
        res = []
        for half in range(2):
            h = 2 * hp + half
            qm = jnp.where(lo if half == 0 else jnp.logical_not(lo), qp, jnp.zeros_like(qp))
            s = [_dot_nt(qm, kp[b]) + b_ref[h, :, b * TQ:(b + 1) * TQ] + pens[b] for b in range(BAND_NB)]
            m = jnp.max(s[0], axis=-1, keepdims=True)
            for b in range(1, BAND_NB):
                m = jnp.maximum(m, jnp.max(s[b], axis=-1, keepdims=True))
            p = [jnp.exp(sb - m) for sb in s]
            l = p[0].sum(axis=-1, keepdims=True)
            o = _dot(p[0].astype(bf16), vp[0])
            for b in range(1, BAND_NB):
                l = l + p[b].sum(axis=-1, keepdims=True)
                o = o + _dot(p[b].astype(bf16), vp[b])
            res.append(o / l)
        o_ref[:, cols] = jnp.where(lo, res[0], res[1]).astype(bf16)


def _toeplitz_bias(table, nq, nk, off):
    H = table.shape[0]
    m = np.arange(nq + nk - 1)
    idx = np.clip(off + nq - 1 - m, -REL_CLIP, REL_CLIP) + REL_CLIP
    w = jnp.concatenate([table[:, idx].astype(f32), jnp.zeros((H, 1), f32)], axis=-1)
    skew = jnp.tile(w, (1, nq))[:, :nq * (nq + nk - 1)].reshape(H, nq, nq + nk - 1)
    return skew[:, :, nq - 1:]


def _band_bias(table):
    qc = np.arange(BAND_TQ)[:, None] // CHUNK
    kc = np.arange(BAND_NB * BAND_TQ)[None, :] // CHUNK
    valid = (kc >= qc) & (kc <= qc + BAND_CHUNKS)
    bias = _toeplitz_bias(table, BAND_TQ, BAND_NB * BAND_TQ, BAND_CHUNKS * CHUNK)
    return jnp.where(valid[None], bias, NEG)


def _band_prompt(q, k, v, bias):
    L, W = q.shape
    TQ = BAND_TQ
    H = bias.shape[0]
    kspec = [pl.BlockSpec((TQ, W), functools.partial(lambda i, d: (jnp.maximum(i - d, 0), 0), d=d))
             for d in (2, 1, 0)]
    est = 2 * 7 * TQ * W * 2 + 2 * H * TQ * BAND_NB * TQ * 4 + 16 * TQ * TQ * 4
    return pl.pallas_call(
        functools.partial(_band_body, nheads=H),
        grid=(L // TQ,),
        in_specs=[pl.BlockSpec((TQ, W), lambda i: (i, 0))] + kspec + kspec
                 + [pl.BlockSpec((H, TQ, BAND_NB * TQ), lambda i: (0, 0, 0))],
        out_specs=pl.BlockSpec((TQ, W), lambda i: (i, 0)),
        out_shape=jax.ShapeDtypeStruct((L, W), bf16),
        compiler_params=_params(("parallel",), est),
        name="band_prompt",
    )(q, k, k, k, v, v, v, bias)


FOX_T = 1024
FOX_VROWS = 80
FOX_QPIECE = 64
FOX_KPIECE = 67
LOG2E = 1.4426950408889634


def _fox_pack_body(q_ref, k_ref, v_ref, f_ref, e_ref, qa_ref, ka_ref, vt_ref, *, nheads, dh):
    T, W2 = q_ref.shape
    a = (f_ref[...] - f_ref[0:1, :]) * LOG2E
    hi = a.astype(bf16)
    r1 = a - hi.astype(f32)
    mid = r1.astype(bf16)
    lo = (r1 - mid.astype(f32)).astype(bf16)
    aug = _dot(jnp.concatenate([hi, mid, lo], axis=1), e_ref[...])
    lane = lax.broadcasted_iota(jnp.int32, (T, W2), 1) % V7X_LANES
    ones_q = jnp.where((lane >= FOX_KPIECE) & (lane < FOX_KPIECE + 3), 1.0, 0.0)
    ones_k = jnp.where((lane >= FOX_QPIECE) & (lane < FOX_QPIECE + 3), 1.0, 0.0)
    qa_ref[...] = (q_ref[...].astype(f32) + aug[:, :W2] + ones_q).astype(bf16)
    ka_ref[...] = (k_ref[...].astype(f32) + aug[:, W2:] + ones_k).astype(bf16)
    vT = v_ref[...].astype(f32).T
    pad = FOX_VROWS - dh
    ones_row = jnp.where(lax.broadcasted_iota(jnp.int32, (pad, T), 0) == 0, 1.0, 0.0).astype(bf16)
    for h in range(nheads):
        vt_ref[h, 0:dh, :] = vT[h * dh:(h + 1) * dh, :].astype(bf16)
        vt_ref[h, dh:FOX_VROWS, :] = ones_row


def _fox_pack(q, k, v, F, nheads):
    L, W2 = q.shape
    W = v.shape[1]
    dh = W // nheads
    T = FOX_T
    e = np.zeros((3 * V7X_LANES, 2 * W2), np.float32)
    for h in range(nheads):
        for t in range(3):
            e[t * V7X_LANES + h, h * V7X_LANES + FOX_QPIECE + t] = 1.0
            e[t * V7X_LANES + h, W2 + h * V7X_LANES + FOX_KPIECE + t] = -1.0
    est = 2 * (4 * T * W2 * 2 + T * W * 2 + nheads * FOX_VROWS * T * 2) + 6 * T * W2 * 4
    return pl.pallas_call(
        functools.partial(_fox_pack_body, nheads=nheads, dh=dh),
        grid=(L // T,),
        in_specs=[pl.BlockSpec((T, W2), lambda i: (i, 0)),
                  pl.BlockSpec((T, W2), lambda i: (i, 0)),
                  pl.BlockSpec((T, W), lambda i: (i, 0)),
                  pl.BlockSpec((T, V7X_LANES), lambda i: (i, 0)),
                  pl.BlockSpec((3 * V7X_LANES, 2 * W2), lambda i: (0, 0))],
        out_specs=[pl.BlockSpec((T, W2), lambda i: (i, 0)),
                   pl.BlockSpec((T, W2), lambda i: (i, 0)),
                   pl.BlockSpec((nheads, FOX_VROWS, T), lambda i: (0, 0, i))],
        out_shape=[jax.ShapeDtypeStruct((L, W2), bf16), jax.ShapeDtypeStruct((L, W2), bf16),
                   jax.ShapeDtypeStruct((nheads, FOX_VROWS, L), bf16)],
        compiler_params=_params(("parallel",), est),
        name="fox_pack",
    )(q, k, v, F, jnp.asarray(e, bf16))


def _fox_body(qi_ref, kj_ref, qa_ref, ka_ref, vt_ref, d_ref, o_ref, m_scr, acc_scr, *, nheads, dh):
    s_id = pl.program_id(0)
    i = qi_ref[s_id]
    j = kj_ref[s_id]
    T = FOX_T

    @pl.when(j == 0)
    def _():
        m_scr[...] = jnp.full_like(m_scr, NEG)
        acc_scr[...] = jnp.zeros_like(acc_scr)

    def step(masked):
        if masked:
            keep = lax.broadcasted_iota(jnp.int32, (T, T), 0) <= lax.broadcasted_iota(jnp.int32, (T, T), 1)
        def scores(h):
            cols = slice(h * V7X_LANES, (h + 1) * V7X_LANES)
            st = _dot_nt(ka_ref[:, cols], qa_ref[:, cols])
            return jnp.where(keep, st, NEG) if masked else st

        ahead = 2
        pending = [scores(h) for h in range(ahead)]
        for h in range(nheads):
            st = pending.pop(0)
            if h + ahead < nheads:
                pending.append(scores(h + ahead))
            d = d_ref[s_id * nheads + h]
            m_prev = m_scr[h:h + 1, :]
            m_new = jnp.maximum(m_prev, jnp.max(st, axis=0, keepdims=True) + d)
            p = jnp.exp2(st - (m_new - d))
            alpha = jnp.exp2(m_prev - m_new)
            acc_scr[h] = acc_scr[h] * alpha + _dot(vt_ref[h], p.astype(bf16))
            m_scr[h:h + 1, :] = m_new

    @pl.when(j < i)
    def _():
        step(False)

    @pl.when(j == i)
    def _():
        step(True)
        for hp in range(nheads // 2):
            a0 = acc_scr[2 * hp]
            a1 = acc_scr[2 * hp + 1]
            o2 = jnp.concatenate([a0[0:dh, :] / a0[dh:dh + 1, :], a1[0:dh, :] / a1[dh:dh + 1, :]], axis=0)
            o_ref[:, hp * V7X_LANES:(hp + 1) * V7X_LANES] = o2.T.astype(bf16)


def _fox_prompt(qa, ka, vt, F, nheads):
    L, W2 = qa.shape
    T = FOX_T
    n = L // T
    dh = V7X_LANES // 2
    W = nheads * dh
    pairs = [(i, j) for i in range(n) for j in range(i + 1)]
    qi = np.array([p[0] for p in pairs], np.int32)
    kj = np.array([p[1] for p in pairs], np.int32)
    fs = F[::T, :nheads]
    d = ((fs[qi] - fs[kj]) * LOG2E).reshape(-1)
    grid_spec = pltpu.PrefetchScalarGridSpec(
        num_scalar_prefetch=2,
        grid=(len(pairs),),
        in_specs=[pl.BlockSpec((T, W2), lambda s, qi, kj: (qi[s], 0)),
                  pl.BlockSpec((T, W2), lambda s, qi, kj: (kj[s], 0)),
                  pl.BlockSpec((nheads, FOX_VROWS, T), lambda s, qi, kj: (0, 0, kj[s])),
                  pl.BlockSpec(memory_space=pltpu.SMEM)],
        out_specs=pl.BlockSpec((T, W), lambda s, qi, kj: (qi[s], 0)),
        scratch_shapes=[pltpu.VMEM((nheads, T), f32), pltpu.VMEM((nheads, FOX_VROWS, T), f32)],
    )
    est = 2 * (2 * T * W2 * 2 + nheads * FOX_VROWS * T * 2 + T * W * 2) + nheads * FOX_VROWS * T * 4 + 24 * T * T * 4
    return pl.pallas_call(
        functools.partial(_fox_body, nheads=nheads, dh=dh),
        grid_spec=grid_spec,
        out_shape=jax.ShapeDtypeStruct((L, W), bf16),
        compiler_params=_params(("arbitrary",), est),
        name="fox_prompt",
    )(jnp.asarray(qi), jnp.asarray(kj), qa, ka, vt, d)


def _cross_body(*refs, nres, nheads, scale):
    x_ref, g_ref, wq_ref = refs[:3]
    x = x_ref[...]
    for p in range(nres):
        x = x + _dot(refs[3 + 2 * p][...], refs[4 + 2 * p][...])
    k_ref, v_ref, xo_ref, o_ref = refs[3 + 2 * nres:]
    xo_ref[...] = x
    q = (_dot(_rms(x, g_ref[...]).astype(bf16), wq_ref[...]) * scale).astype(bf16)
    dh = q.shape[1] // nheads
    hs = range(nheads)
    s = [_dot_nt(q[:, h * dh:(h + 1) * dh], k_ref[:, h * dh:(h + 1) * dh]) for h in hs]
    m = [jnp.max(sh, axis=-1, keepdims=True) for sh in s]
    p = [jnp.exp(s[h] - m[h]) for h in hs]
    l = [ph.sum(axis=-1, keepdims=True) for ph in p]
    o = [_dot(p[h].astype(bf16), v_ref[:, h * dh:(h + 1) * dh]) / l[h] for h in hs]
    o_ref[...] = jnp.concatenate(o, axis=-1).astype(bf16)


def _cross_prompt(x, g, wq, res, mk, mv, nheads):
    T, D = x.shape
    W = wq.shape[1]
    S = mk.shape[0]
    TM = min(512, T)
    row = lambda width: pl.BlockSpec((TM, width), lambda i: (i, 0))
    whole = lambda r, c: pl.BlockSpec((r, c), lambda i: (0, 0))
    in_specs = [row(D), whole(1, D), whole(D, W)]
    args = [x, g.reshape(1, D), wq]
    est = 4 * TM * D * 4 + 2 * D * W * 2 + 4 * S * W * 2 + 2 * TM * W * 2 + 3 * TM * W * 4 + 8 * TM * S * 4
    for a, wr in res:
        K = a.shape[1]
        in_specs += [row(K), whole(K, D)]
        args += [a, wr]
        est += 2 * TM * K * 2 + 2 * K * D * 2
    in_specs += [whole(S, W), whole(S, W)]
    args += [mk, mv]
    return pl.pallas_call(
        functools.partial(_cross_body, nres=len(res), nheads=nheads, scale=(W // nheads) ** -0.5),
        grid=(T // TM,),
        in_specs=in_specs,
        out_specs=[row(D), row(W)],
        out_shape=[jax.ShapeDtypeStruct((T, D), f32), jax.ShapeDtypeStruct((T, W), bf16)],
        compiler_params=_params(("parallel",), est),
        name="cross_prompt",
    )(*args)


def _sample_attn_body(*refs, nheads, mode):
    q_ref, kc_ref, vc_ref, kn_ref, vn_ref = refs[:5]
    if mode == "rel":
        bc_ref, bn_ref, o_ref = refs[5:]
    else:
        fq_ref, fkc_ref, fkn_ref, o_ref = refs[5:]

    m_q, W = q_ref.shape
    dh = W // nheads
    R = nheads * m_q
    q = q_ref[...].astype(f32)
    qt = jnp.concatenate([q] * nheads, axis=0)
    hrow = lax.broadcasted_iota(jnp.int32, (R, W), 0) // m_q
    hlane = lax.broadcasted_iota(jnp.int32, (R, W), 1) // dh
    qbd = jnp.where(hrow == hlane, qt, 0.0).astype(bf16)

    def rows_of(f_ref):
        n = f_ref.shape[-1]
        return jnp.concatenate([jnp.broadcast_to(f_ref[h:h + 1, :], (m_q, n)) for h in range(nheads)], axis=0)

    sc = _dot_nt(qbd, kc_ref[...].astype(bf16))
    sn = _dot_nt(qbd, kn_ref[...].astype(bf16))
    if mode == "rel":
        sc = sc + bc_ref[...]
        sn = sn + bn_ref[...]
    else:
        sc = sc + (fq_ref[...] - rows_of(fkc_ref))
        sn = sn + (fq_ref[...] - rows_of(fkn_ref))
        qpos = lax.broadcasted_iota(jnp.int32, (R, m_q), 0) % m_q
        kpos = lax.broadcasted_iota(jnp.int32, (R, m_q), 1)
        sn = jnp.where(kpos <= qpos, sn, NEG)
    m = jnp.maximum(jnp.max(sc, axis=-1, keepdims=True), jnp.max(sn, axis=-1, keepdims=True))
    pc = jnp.exp(sc - m)
    pn = jnp.exp(sn - m)
    l = pc.sum(axis=-1, keepdims=True) + pn.sum(axis=-1, keepdims=True)
    o = _dot(pc.astype(bf16), vc_ref[...].astype(bf16)) + _dot(pn.astype(bf16), vn_ref[...].astype(bf16))
    o = o / l
    hl = lax.broadcasted_iota(jnp.int32, (m_q, W), 1) // dh
    out = jnp.zeros((m_q, W), f32)
    for h in range(nheads):
        out = out + jnp.where(hl == h, o[h * m_q:(h + 1) * m_q, :], 0.0)
    o_ref[...] = out.astype(bf16)


def _sample_attn(q, kc, vc, nheads, mode, kn, vn, extra):
    B, m_q, W = q.shape
    P = kc.shape[1]
    R = nheads * m_q
    per_b = lambda *shape: pl.BlockSpec((None,) + shape, lambda b: (b,) + (0,) * len(shape))
    shared = lambda *shape: pl.BlockSpec(shape, lambda b: (0,) * len(shape))
    in_specs = [per_b(m_q, W), per_b(P, W), per_b(P, W), per_b(m_q, W), per_b(m_q, W)]
    if mode == "rel":
        in_specs += [shared(R, P), shared(R, m_q)]
    else:
        in_specs += [per_b(R, 1), per_b(nheads, P), per_b(nheads, m_q)]
    est = 2 * 2 * P * W * 4 + 2 * P * W * 2 + 6 * R * P * 4 + 4 * R * W * 4
    return pl.pallas_call(
        functools.partial(_sample_attn_body, nheads=nheads, mode=mode),
        grid=(B,),
        in_specs=in_specs,
        out_specs=per_b(m_q, W),
        out_shape=jax.ShapeDtypeStruct((B, m_q, W), bf16),
        compiler_params=_params(("parallel",), est, [False, True, True] + [False] * (2 + len(extra))),
        name="sample_attn_" + mode,
    )(q, kc, vc, kn, vn, *extra)


def _cross_sample_body(q_ref, kc_ref, vc_ref, o_ref, *, nheads):
    m_q, W = q_ref.shape
    dh = W // nheads
    npc = dh // kc_ref.shape[1]
    P = kc_ref.shape[0] // (nheads * npc)
    hs = range(nheads)

    def head_rows(ref, h):
        pieces = [ref[pl.ds(h * npc + j, P, stride=nheads * npc), :] for j in range(npc)]
        return jnp.concatenate(pieces, axis=-1).astype(bf16)

    q = q_ref[...]
    sc = [_dot_nt(q[:, h * dh:(h + 1) * dh], head_rows(kc_ref, h)) for h in hs]
    m = [jnp.max(s, axis=-1, keepdims=True) for s in sc]
    pc = [jnp.exp(sc[h] - m[h]) for h in hs]
    l = [p.sum(axis=-1, keepdims=True) for p in pc]
    o = [_dot(pc[h].astype(bf16), head_rows(vc_ref, h)) / l[h] for h in hs]
    o_ref[...] = jnp.concatenate(o, axis=-1).astype(bf16)


def _cross_sample(q, kc_all, vc_all, layer):
    B, m_q, W = q.shape
    nl, _, P, nheads, dh = kc_all.shape
    nrows = P * nheads * (dh // V7X_LANES)
    kc_all = kc_all.reshape(nl, B, nrows, V7X_LANES)
    vc_all = vc_all.reshape(nl, B, nrows, V7X_LANES)
    cache = pl.BlockSpec((None, None, nrows, V7X_LANES), lambda b: (layer, b, 0, 0))
    qo = pl.BlockSpec((None, m_q, W), lambda b: (b, 0, 0))
    est = 2 * 2 * nrows * V7X_LANES * 4 + 4 * P * W * 2 + 8 * nheads * m_q * P * 4
    return pl.pallas_call(
        functools.partial(_cross_sample_body, nheads=nheads),
        grid=(B,),
        in_specs=[qo, cache, cache],
        out_specs=qo,
        out_shape=jax.ShapeDtypeStruct((B, m_q, W), bf16),
        compiler_params=_params(("parallel",), est, [False, True, True]),
        name="cross_sample",
    )(q, kc_all, vc_all)


GDN_HALO = 8


def _softplus(x):
    return jnp.maximum(x, 0.0) + jnp.log(1.0 + jnp.exp(-jnp.abs(x)))


def _split2(a):
    hi = a.astype(bf16)
    return hi, (a - hi.astype(f32)).astype(bf16)


def _split3(a):
    hi = a.astype(bf16)
    r1 = a - hi.astype(f32)
    mid = r1.astype(bf16)
    return hi, mid, (r1 - mid.astype(f32)).astype(bf16)


def _gdn_conv_qkv(xbuf, R, cw_ref, nheads, dk):
    H0 = GDN_HALO
    W = nheads * dk
    cw = cw_ref[...]
    conv = xbuf[H0 - 3:H0 - 3 + R, :] * cw[0:1, :]
    for jj in range(1, CONV):
        conv = conv + xbuf[H0 - 3 + jj:H0 - 3 + jj + R, :] * cw[jj:jj + 1, :]
    tail = xbuf[H0 + R - (CONV - 1):H0 + R, :]
    act = _silu(conv)
    hs = range(nheads)
    qn = [act[:, h * dk:(h + 1) * dk] for h in hs]
    kn = [act[:, W + h * dk:W + (h + 1) * dk] for h in hs]
    qn = [q * lax.rsqrt(jnp.sum(q * q, axis=-1, keepdims=True) + EPS) * (dk ** -0.5) for q in qn]
    kn = [k * lax.rsqrt(jnp.sum(k * k, axis=-1, keepdims=True) + EPS) for k in kn]
    vn = [act[:, 2 * W + h * dk:2 * W + (h + 1) * dk] for h in hs]
    return qn, kn, vn, tail


def _gdn_proj_body(x_ref, g_ref, w_ref, cs_ref, cw_ref, q_ref, k_ref, v_ref, z_ref, ab_ref, cf_ref, xbuf,
                   *, nsteps, nheads, dk):
    i = pl.program_id(0)
    W = nheads * dk
    TM = x_ref.shape[0]
    H0 = GDN_HALO

    @pl.when(i == 0)
    def _():
        xbuf[...] = jnp.zeros_like(xbuf)

    qn, kn, vn, tail = _gdn_conv_qkv(xbuf, TM, cw_ref, nheads, dk)
    q_ref[...] = jnp.concatenate(qn, axis=-1)
    k_ref[...] = jnp.concatenate(kn, axis=-1)
    v_ref[...] = jnp.concatenate(vn, axis=-1)

    xbuf[H0 - (CONV - 1):H0, :] = jnp.where(i == 0, cs_ref[...], tail)
    cf_ref[...] = tail

    h = _rms(x_ref[...], g_ref[...]).astype(bf16)
    xbuf[H0:H0 + TM, :] = _dot(h, w_ref[:, :3 * W])
    z_ref[...] = _dot(h, w_ref[:, 3 * W:4 * W])
    ab_ref[...] = _dot(h, w_ref[:, 4 * W:])


def _gdn_proj(x, g, w, conv_state, conv_w, nheads):
    T, D = x.shape
    N = w.shape[1]
    W = (N - V7X_LANES) // 4
    dk = W // nheads
    TM = min(256, T)
    nsteps = T // TM
    row = lambda width: pl.BlockSpec((TM, width), lambda i: (jnp.minimum(i, nsteps - 1), 0))
    late = lambda width: pl.BlockSpec((TM, width), lambda i: (jnp.maximum(i - 1, 0), 0))
    whole = lambda *shape: pl.BlockSpec(shape, lambda i: (0,) * len(shape))
    est = 2 * TM * D * 4 + 2 * D * N * 2 + 2 * TM * (4 * W + V7X_LANES) * 4 + (GDN_HALO + TM) * 3 * W * 4 + 4 * TM * 3 * W * 4
    q, k, v, z, ab, cf = pl.pallas_call(
        functools.partial(_gdn_proj_body, nsteps=nsteps, nheads=nheads, dk=dk),
        grid=(nsteps + 1,),
        in_specs=[row(D), whole(1, D), whole(D, N),
                  pl.BlockSpec((None, CONV - 1, 3 * W), lambda i: (0, 0, 0)), whole(CONV, 3 * W)],
        out_specs=[late(W), late(W), late(W), row(W), row(V7X_LANES),
                   pl.BlockSpec((None, CONV - 1, 3 * W), lambda i: (0, 0, 0))],
        out_shape=[jax.ShapeDtypeStruct((T, W), f32)] * 4
                  + [jax.ShapeDtypeStruct((T, V7X_LANES), f32), jax.ShapeDtypeStruct((1, CONV - 1, 3 * W), f32)],
        scratch_shapes=[pltpu.VMEM((GDN_HALO + TM, 3 * W), f32)],
        compiler_params=_params(("arbitrary",), est),
        name="gdn_proj",
    )(x, g.reshape(1, D), w, conv_state, conv_w)
    return q, k, v, z, ab, cf


def _gdn_body(*refs, C, NS, nsteps, nheads, dk, pre):
    if pre:
        q_ref, k_ref, v_ref, ab_ref, z_ref, s0_ref, alog_ref, dtb_ref, ng_ref, o_ref, sf_ref, s_scr = refs
    else:
        (x_ref, ab_ref, z_ref, cs_ref, s0_ref, cw_ref, alog_ref, dtb_ref, ng_ref,
         o_ref, sf_ref, cf_ref, xbuf, s_scr) = refs
    c = pl.program_id(1)
    R = NS * C
    G = V7X_LANES // C
    NG = nheads // G
    GW = G * dk
    hs = range(nheads)

    @pl.when(c == 0)
    def _():
        s_scr[...] = s0_ref[...]

    if pre:
        qn = [q_ref[:, h * dk:(h + 1) * dk] for h in hs]
        kn = [k_ref[:, h * dk:(h + 1) * dk] for h in hs]
        vn = [v_ref[:, h * dk:(h + 1) * dk] for h in hs]
    else:
        H0 = GDN_HALO

        @pl.when(c == 0)
        def _():
            xbuf[H0 - (CONV - 1):H0, :] = cs_ref[...]

        xbuf[H0:H0 + R, :] = x_ref[...]
        qn, kn, vn, tail = _gdn_conv_qkv(xbuf, R, cw_ref, nheads, dk)
        xbuf[H0 - (CONV - 1):H0, :] = tail

        @pl.when(c == nsteps - 1)
        def _():
            cf_ref[...] = tail

    ab = ab_ref[...]
    gfull = -jnp.exp(alog_ref[...]) * _softplus(ab + dtb_ref[...])
    bfull = 1.0 / (1.0 + jnp.exp(-ab))
    z = z_ref[...]
    ng = ng_ref[...]

    r_cc = lax.broadcasted_iota(jnp.int32, (C, C), 0)
    c_cc = lax.broadcasted_iota(jnp.int32, (C, C), 1)
    tri_f = (r_cc >= c_cc).astype(f32)
    ri = lax.broadcasted_iota(jnp.int32, (C, V7X_LANES), 0)
    jl = lax.broadcasted_iota(jnp.int32, (C, V7X_LANES), 1) % C
    lblk = lax.broadcasted_iota(jnp.int32, (C, V7X_LANES), 1) // C
    tri_g = ri >= jl
    strict_g = ri > jl
    eye_g = ri == jl
    r128 = lax.broadcasted_iota(jnp.int32, (V7X_LANES, V7X_LANES), 0)
    l128 = lax.broadcasted_iota(jnp.int32, (V7X_LANES, V7X_LANES), 1)
    same_blk = (r128 // C) == (l128 // C)
    wide_blk = (lax.broadcasted_iota(jnp.int32, (V7X_LANES, GW), 0) // C
                == lax.broadcasted_iota(jnp.int32, (V7X_LANES, GW), 1) // dk)

    def bd_sq(p):
        return jnp.where(same_blk, jnp.concatenate([p] * G, axis=0), jnp.zeros((), p.dtype))

    def bd_wide(xs):
        row = jnp.concatenate(xs, axis=1)
        return jnp.where(wide_blk, jnp.concatenate([row] * G, axis=0), 0.0).astype(bf16)

    def gmm3(xg, pg):
        xh, xl = _split2(xg)
        ph, plo = _split2(pg)
        bh = bd_sq(ph)
        return _dot(xh, bh) + (_dot(xh, bd_sq(plo)) + _dot(xl, bh))

    prep = []
    for s in range(NS):
        sl = slice(s * C, (s + 1) * C)
        gc = _dot_hi(tri_f, gfull[sl])
        egc = jnp.exp(gc)
        kdec = jnp.exp(gc[C - 1:C, :] - gc)
        gcT = jnp.concatenate([gc] * G, axis=0).T
        bcols = [bfull[sl, nheads + h:nheads + h + 1] for h in hs]
        ecols = [egc[:, h:h + 1] for h in hs]
        kbs = [kn[h][sl] * bcols[h] for h in hs]
        Ms, Aqks = [], []
        for grp in range(NG):
            heads = range(grp * G, (grp + 1) * G)
            gcol = jnp.broadcast_to(gc[:, grp * G:grp * G + 1], (C, V7X_LANES))
            grow = gcT[grp * G:grp * G + 1, :]
            for g in range(1, G):
                h = grp * G + g
                gcol = jnp.where(lblk == g, gc[:, h:h + 1], gcol)
                grow = jnp.where(lblk[0:1, :] == g, gcT[h:h + 1, :], grow)
            Lm = jnp.where(tri_g, jnp.exp(jnp.where(tri_g, gcol - grow, 0.0)), 0.0)
            kbd = bd_wide([kn[h][sl] for h in heads])
            kb_row = jnp.concatenate([kbs[h] for h in heads], axis=1).astype(bf16)
            q_row = jnp.concatenate([qn[h][sl] for h in heads], axis=1).astype(bf16)
            Ms.append(jnp.where(strict_g, _dot_nt(kb_row, kbd) * Lm, 0.0))
            Aqks.append(jnp.where(tri_g, _dot_nt(q_row, kbd) * Lm, 0.0).astype(bf16))
        Xs = [jnp.where(eye_g, 1.0, 0.0) - M for M in Ms]
        Pws = [gmm3(M, M) for M in Ms]
        e = 2
        while e < C:
            Xs = [X + gmm3(X, Pw) for X, Pw in zip(Xs, Pws)]
            e *= 2
            if e < C:
                Pws = [gmm3(Pw, Pw) for Pw in Pws]
        us, ws = [], []
        for grp in range(NG):
            heads = range(grp * G, (grp + 1) * G)
            Tm = Xs[grp].astype(bf16)
            U = _dot(Tm, bd_wide([vn[h][sl] * bcols[h] for h in heads]))
            Wm = _dot(Tm, bd_wide([kbs[h] * ecols[h] for h in heads]))
            for g, h in enumerate(heads):
                us.append(U[:, g * dk:(g + 1) * dk])
                ws.append(Wm[:, g * dk:(g + 1) * dk].astype(bf16))
        qgs = [(qn[h][sl] * ecols[h]).astype(bf16) for h in hs]
        kgTs = [(kn[h][sl] * kdec[:, h:h + 1]).T.astype(bf16) for h in hs]
        decs = [egc[C - 1:C, h:h + 1] for h in hs]
        prep.append((us, ws, Aqks, qgs, kgTs, decs))

    Ss = [s_scr[h] for h in hs]
    for s in range(NS):
        us, ws, Aqks, qgs, kgTs, decs = prep[s]
        Sbs = [S.astype(bf16) for S in Ss]
        vnews = [us[h] - _dot(ws[h], Sbs[h]) for h in hs]
        vnbs = [v.astype(bf16) for v in vnews]
        intra = [_dot(Aqks[grp], bd_wide([vnews[h] for h in range(grp * G, (grp + 1) * G)])) for grp in range(NG)]
        os_ = [_dot(qgs[h], Sbs[h]) + intra[h // G][:, (h % G) * dk:(h % G + 1) * dk] for h in hs]
        Ss = [Ss[h] * decs[h] + _dot(kgTs[h], vnbs[h]) for h in hs]
        zs = z[s * C:(s + 1) * C, :]
        ys = [(_rms(os_[h], ng) * _silu(zs[:, h * dk:(h + 1) * dk])).astype(bf16) for h in hs]
        o_ref[s * C:(s + 1) * C, :] = jnp.concatenate(ys, axis=-1)
    for h in hs:
        s_scr[h] = Ss[h]

    @pl.when(c == nsteps - 1)
    def _():
        sf_ref[...] = s_scr[...]


def _gdn_core_pre(q, k, v, ab, z, S0, a_log, dt_bias, norm_g, C, NS):
    T = q.shape[0]
    B, H, dk, dv = S0.shape
    W = H * dk
    R = NS * C
    nsteps = T // (B * R)
    pad = lambda vec: jnp.zeros((1, V7X_LANES), f32).at[0, :H].set(vec.astype(f32))
    row = lambda width: pl.BlockSpec((R, width), lambda b, c: (b * nsteps + c, 0))
    lane_row = pl.BlockSpec((1, V7X_LANES), lambda b, c: (0, 0))
    state = pl.BlockSpec((None, H, dk, dv), lambda b, c: (b, 0, 0, 0))
    est = 2 * R * (4 * W + V7X_LANES) * 4 + 3 * H * dk * dv * 4 * 2 + 16 * R * W * 4
    return pl.pallas_call(
        functools.partial(_gdn_body, C=C, NS=NS, nsteps=nsteps, nheads=H, dk=dk, pre=True),
        grid=(B, nsteps),
        in_specs=[row(W), row(W), row(W), row(V7X_LANES), row(W), state, lane_row, lane_row,
                  pl.BlockSpec((1, dv), lambda b, c: (0, 0))],
        out_specs=[row(W), state],
        out_shape=[jax.ShapeDtypeStruct((T, W), bf16), jax.ShapeDtypeStruct((B, H, dk, dv), f32)],
        scratch_shapes=[pltpu.VMEM((H, dk, dv), f32)],
        compiler_params=_params(("parallel", "arbitrary"), est),
        name="gdn_core_pre",
    )(q, k, v, ab, z, S0, pad(a_log), pad(dt_bias), norm_g.reshape(1, dv))


def _gdn_core(qkv, ab, z, conv_state, S0, conv_w, a_log, dt_bias, norm_g, C, NS):
    T = qkv.shape[0]
    B, H, dk, dv = S0.shape
    W = H * dk
    R = NS * C
    nsteps = T // (B * R)
    pad = lambda vec: jnp.zeros((1, V7X_LANES), f32).at[0, :H].set(vec.astype(f32))
    row = lambda width: pl.BlockSpec((R, width), lambda b, c: (b * nsteps + c, 0))
    est = 2 * R * (3 * W + W + V7X_LANES) * 4 + 3 * H * dk * dv * 4 * 2 + 12 * R * 3 * W * 4
    return pl.pallas_call(
        functools.partial(_gdn_body, C=C, NS=NS, nsteps=nsteps, nheads=H, dk=dk, pre=False),
        grid=(B, nsteps),
        in_specs=[row(3 * W), row(V7X_LANES), row(W),
                  pl.BlockSpec((None, CONV - 1, 3 * W), lambda b, c: (b, 0, 0)),
                  pl.BlockSpec((None, H, dk, dv), lambda b, c: (b, 0, 0, 0)),
                  pl.BlockSpec((CONV, 3 * W), lambda b, c: (0, 0)),
                  pl.BlockSpec((1, V7X_LANES), lambda b, c: (0, 0)),
                  pl.BlockSpec((1, V7X_LANES), lambda b, c: (0, 0)),
                  pl.BlockSpec((1, dv), lambda b, c: (0, 0))],
        out_specs=[row(W),
                   pl.BlockSpec((None, H, dk, dv), lambda b, c: (b, 0, 0, 0)),
                   pl.BlockSpec((None, CONV - 1, 3 * W), lambda b, c: (b, 0, 0))],
        out_shape=[jax.ShapeDtypeStruct((T, W), bf16),
                   jax.ShapeDtypeStruct((B, H, dk, dv), f32),
                   jax.ShapeDtypeStruct((B, CONV - 1, 3 * W), f32)],
        scratch_shapes=[pltpu.VMEM((GDN_HALO + R, 3 * W), f32), pltpu.VMEM((H, dk, dv), f32)],
        compiler_params=_params(("parallel", "arbitrary"), est),
        name="gdn_core",
    )(qkv, ab, z, conv_state, S0, conv_w, pad(a_log), pad(dt_bias), norm_g.reshape(1, dv))


def kernel(x_prompt, x_sample, mem_prompt, cache_a_k, cache_a_v, cache_b_k, cache_b_v, cache_b_logf, state_gdn, state_gdn_conv, cache_mem_k, cache_mem_v, norm_g, mem_norm_g, final_norm_g, ffn_w_gate, ffn_w_up, ffn_w_down, xa_w_q, xa_w_k, xa_w_v, xa_w_o, ab_w_in, ab_b_f, ab_rel_bias, ab_w_o, gdn_w_in, gdn_conv_w, gdn_a_log, gdn_dt_bias, gdn_norm_g, gdn_w_o):
    depth = norm_g.shape[0]
    BP, SEQ, D = x_prompt.shape
    BS, MS, _ = x_sample.shape
    assert BP == 1
    HA = HB = ab_b_f.shape[1]
    WA = WB = (ab_w_in.shape[2] - HB) // 6
    DHA = WA // HA
    HG = gdn_a_log.shape[1]
    WG = gdn_w_o.shape[1]
    DKG = WG // HG
    HX = cache_mem_k.shape[3]
    DHX = cache_mem_k.shape[4]
    WX = HX * DHX
    PA = cache_a_k.shape[2]
    PB = cache_b_k.shape[2]
    LANES = V7X_LANES

    xp = x_prompt.reshape(SEQ, D)
    xs = x_sample.reshape(BS * MS, D)

    wgate = ffn_w_gate.astype(bf16)
    wup = ffn_w_up.astype(bf16)
    wdown = ffn_w_down.astype(bf16)
    wxq = xa_w_q.astype(bf16)
    wxo = xa_w_o.astype(bf16)
    wxkv = jnp.concatenate([xa_w_k, xa_w_v], axis=-1).astype(bf16)

    outs = {k: [] for k in ("akp", "avp", "bkp", "bvp", "blp", "sgp", "scp", "mkp", "mvp",
                            "aks", "avs", "bks", "bvs", "bls", "sgs", "scs")}

    for l in range(depth):
        xp = _ffn(xp, norm_g[l, 0], wgate[l, 0], wup[l, 0], wdown[l, 0])
        xs = _ffn(xs, norm_g[l, 0], wgate[l, 0], wup[l, 0], wdown[l, 0])

        if l % 2 == 0:
            e = l // 2
            w_in = ab_w_in[e]
            nq = 3 * WA + 3 * WB
            w_pad = jnp.concatenate([w_in, jnp.zeros((D, LANES - HB), f32)], axis=-1).astype(bf16)
            b_pad = jnp.zeros((1, LANES), f32).at[0, :HB].set(ab_b_f[e].astype(f32))
            spec = [(0, WA, bf16, DHA ** -0.5, None),
                    (WA, WA, f32, 1.0, None), (WA, WA, bf16, 1.0, None),
                    (2 * WA, WA, f32, 1.0, None), (2 * WA, WA, bf16, 1.0, None),
                    (3 * WA, WB, bf16, (WB // HB) ** -0.5, None),
                    (3 * WA + WB, WB, f32, 1.0, None), (3 * WA + WB, WB, bf16, 1.0, None),
                    (3 * WA + 2 * WB, WB, f32, 1.0, None), (3 * WA + 2 * WB, WB, bf16, 1.0, None),
                    (nq, LANES, f32, 1.0, "logsig")]
            wo = ab_w_o[e].astype(bf16)

            qa, ka, kab, va, vab = _norm_proj(xp, norm_g[l, 1], w_pad[:, :3 * WA], spec[:5])
            oa = _band_prompt(qa, kab, vab, _band_bias(ab_rel_bias[e]))
            DHB = WB // HB
            spread = lambda w: jnp.pad(w.reshape(D, HB, DHB), ((0, 0), (0, 0), (0, LANES - DHB))).reshape(D, HB * LANES)
            wq_b = w_in[:, 3 * WA:3 * WA + WB]
            wk_b = w_in[:, 3 * WA + WB:3 * WA + 2 * WB]
            w_b = jnp.concatenate([spread(wq_b), wk_b, spread(wk_b), w_in[:, 3 * WA + 2 * WB:nq],
                                   w_pad[:, nq:]], axis=-1).astype(bf16)
            W2 = HB * LANES
            spec_b = [(0, W2, bf16, DHB ** -0.5 * LOG2E, None),
                      (W2, WB, f32, 1.0, None), (W2 + WB, W2, bf16, 1.0, None),
                      (2 * W2 + WB, WB, f32, 1.0, None), (2 * W2 + WB, WB, bf16, 1.0, None),
                      (2 * W2 + 2 * WB, LANES, f32, 1.0, "logsig")]
            qbs, kb, kbs, vb, vbb, lf = _norm_proj(xp, norm_g[l, 1], w_b, spec_b, bias=b_pad)
            F, _ = _cumsum_rows(lf)
            qaug, kaug, vt = _fox_pack(qbs, kbs, vbb, F, HB)
            ob = _fox_prompt(qaug, kaug, vt, F, HB)
            res_p = [(oa, wo[:WA]), (ob, wo[WA:])]
            keep = min(BAND_CHUNKS * CHUNK, SEQ)
            outs["akp"].append(ka[SEQ - keep:].reshape(1, keep, HA, DHA))
            outs["avp"].append(va[SEQ - keep:].reshape(1, keep, HA, DHA))
            outs["bkp"].append(kb.reshape(1, SEQ, HB, WB // HB))
            outs["bvp"].append(vb.reshape(1, SEQ, HB, WB // HB))
            outs["blp"].append(lf[:, :HB].reshape(1, SEQ, HB))

            qa, ka, _, va, _, qb, kb, _, vb, _, lf = _norm_proj(xs, norm_g[l, 1], w_pad, spec, bias=b_pad)
            bias_s = _toeplitz_bias(ab_rel_bias[e], MS, PA + MS, PA).reshape(HA * MS, PA + MS)
            oa = _sample_attn(qa.reshape(BS, MS, WA), cache_a_k[e].reshape(BS, PA, WA), cache_a_v[e].reshape(BS, PA, WA),
                              HA, "rel", ka.reshape(BS, MS, WA), va.reshape(BS, MS, WA),
                              extra=(bias_s[:, :PA], bias_s[:, PA:]))
            lfn = lf[:, :HB].reshape(BS, MS, HB)
            lcat = jnp.concatenate([cache_b_logf[e].astype(f32), lfn], axis=1)
            LP = -(-(PB + MS) // LANES) * LANES
            lcat = jnp.pad(lcat.transpose(1, 0, 2).reshape(PB + MS, BS * HB), ((0, LP - PB - MS), (0, 0)))
            _, FTs = _cumsum_rows(lcat)
            FTs = FTs.reshape(BS, HB, LP)
            fq = FTs[:, :, PB:PB + MS].reshape(BS, HB * MS, 1)
            ob = _sample_attn(qb.reshape(BS, MS, WB), cache_b_k[e].reshape(BS, PB, WB), cache_b_v[e].reshape(BS, PB, WB),
                              HB, "fox", kb.reshape(BS, MS, WB), vb.reshape(BS, MS, WB),
                              extra=(fq, FTs[:, :, :PB], FTs[:, :, PB:PB + MS]))
            res_s = [(oa.reshape(BS * MS, WA), wo[:WA]), (ob.reshape(BS * MS, WB), wo[WA:])]
            outs["aks"].append(ka.reshape(BS, MS, HA, DHA))
            outs["avs"].append(va.reshape(BS, MS, HA, DHA))
            outs["bks"].append(kb.reshape(BS, MS, HB, WB // HB))
            outs["bvs"].append(vb.reshape(BS, MS, HB, WB // HB))
            outs["bls"].append(lfn)
        else:
            o = l // 2
            w_in = gdn_w_in[o]
            w_pad = jnp.concatenate([w_in, jnp.zeros((D, LANES - 2 * HG), f32)], axis=-1).astype(bf16)
            spec = [(0, 3 * WG, f32, 1.0, None), (3 * WG, WG, f32, 1.0, None), (4 * WG, LANES, f32, 1.0, None)]
            wo = gdn_w_o[o].astype(bf16)

            qg, kg, vg, z, ab, cvp = _gdn_proj(xp, norm_g[l, 1], w_pad, jnp.zeros((1, CONV - 1, 3 * WG), f32),
                                               gdn_conv_w[o], HG)
            y, Sp = _gdn_core_pre(qg, kg, vg, ab, z, jnp.zeros((1, HG, DKG, DKG), f32),
                                  gdn_a_log[o], gdn_dt_bias[o], gdn_norm_g[o], CHUNK, 4)
            res_p = [(y, wo)]
            outs["sgp"].append(Sp)
            outs["scp"].append(cvp)

            qkv, z, ab = _norm_proj(xs, norm_g[l, 1], w_pad, spec)
            y, Ss, cvs = _gdn_core(qkv, ab, z, state_gdn_conv[o].astype(f32), state_gdn[o].astype(f32),
                                   gdn_conv_w[o], gdn_a_log[o], gdn_dt_bias[o], gdn_norm_g[o], MS, 1)
            res_s = [(y, wo)]
            outs["sgs"].append(Ss)
            outs["scs"].append(cvs)

        mk, mkb, mv, mvb = _norm_proj(mem_prompt.reshape(MEM, D), mem_norm_g[l], wxkv[l],
                                      [(0, WX, f32, 1.0, None), (0, WX, bf16, 1.0, None),
                                       (WX, WX, f32, 1.0, None), (WX, WX, bf16, 1.0, None)])
        outs["mkp"].append(mk.reshape(1, MEM, HX, DHX))
        outs["mvp"].append(mv.reshape(1, MEM, HX, DHX))
        qspec = [(0, WX, bf16, DHX ** -0.5, None)]
        xp, ocp = _cross_prompt(xp, norm_g[l, 2], wxq[l], res_p, mkb, mvb, HX)
        xs, q = _norm_proj(xs, norm_g[l, 2], wxq[l], qspec, res=res_s)
        ocs = _cross_sample(q.reshape(BS, MS, WX), cache_mem_k, cache_mem_v, l)

        fin = final_norm_g if l == depth - 1 else None
        xp = _ffn(xp, norm_g[l, 3], wgate[l, 1], wup[l, 1], wdown[l, 1], fin, res=(ocp, wxo[l]))
        xs = _ffn(xs, norm_g[l, 3], wgate[l, 1], wup[l, 1], wdown[l, 1], fin, res=(ocs.reshape(BS * MS, WX), wxo[l]))

    st = lambda k: jnp.stack(outs[k])
    return (xp.reshape(BP, SEQ, D), xs.reshape(BS, MS, D),
            st("akp"), st("avp"), st("bkp"), st("bvp"), st("blp"),
            st("sgp"), st("scp"), st("mkp"), st("mvp"),
            st("aks"), st("avs"), st("bks"), st("bvs"), st("bls"),
            st("sgs"), st("scs"))
```

```python
import functools

import numpy as np
import jax
import jax.numpy as jnp
from jax import lax
from jax.experimental import pallas as pl
from jax.experimental.pallas import tpu as pltpu

f32 = jnp.float32
bf16 = jnp.bfloat16
EPS = 1e-6
NEG = -1e30

V7X_VMEM_BYTES = 64 * 1024 * 1024
V7X_LANES = 128
MIB = 1024 * 1024

CHUNK = 64
BAND_CHUNKS = 8
REL_CLIP = 256
CONV = 4
MEM = 256


def _params(sem, est_bytes, fuse_inputs=None):
    limit = int(min(V7X_VMEM_BYTES - 8 * MIB, max(32 * MIB, est_bytes + 8 * MIB)))
    return pltpu.CompilerParams(dimension_semantics=sem, vmem_limit_bytes=limit, allow_input_fusion=fuse_inputs)


def _rms(x, g):
    ms = jnp.mean(x * x, axis=-1, keepdims=True)
    return x * lax.rsqrt(ms + EPS) * g


def _silu(x):
    h = 0.5 * x
    return h + h * jnp.tanh(h)


def _dot(a, b):
    return jnp.dot(a, b, preferred_element_type=f32)


def _dot_nt(a, b):
    return lax.dot_general(a, b, (((1,), (1,)), ((), ())), preferred_element_type=f32)


def _dot_hi(a, b):
    return jnp.dot(a, b, preferred_element_type=f32, precision=lax.Precision.HIGHEST)


def _ffn_body(*refs, nj, final, has_res):
    x_ref, g_ref, wg_ref, wu_ref, wd_ref = refs[:5]
    pos = 5
    if has_res:
        a_ref, wo_ref = refs[pos:pos + 2]
        pos += 2
    if final:
        gf_ref = refs[pos]
        pos += 1
    o_ref, h_scr, acc = refs[pos:]
    j = pl.program_id(1)

    @pl.when(j == 0)
    def _():
        x = x_ref[...]
        if has_res:
            x = x + _dot(a_ref[...], wo_ref[...])
        o_ref[...] = x
        h_scr[...] = _rms(x, g_ref[...]).astype(bf16)
        acc[...] = jnp.zeros_like(acc)

    h = h_scr[...]
    a = _silu(_dot(h, wg_ref[...])) * _dot(h, wu_ref[...])
    acc[...] += _dot(a.astype(bf16), wd_ref[...])

    @pl.when(j == nj - 1)
    def _():
        y = o_ref[...] + 0.5 * acc[...]
        if final:
            y = _rms(y, gf_ref[...])
        o_ref[...] = y


def _ffn(x, g, wg, wu, wd, final_g=None, res=None):
    T, D = x.shape
    FF = wg.shape[1]
    TM = min(1024, T)
    TF = min(1024, FF)
    nj = FF // TF
    final = final_g is not None
    in_specs = [pl.BlockSpec((TM, D), lambda i, j: (i, 0)),
                pl.BlockSpec((1, D), lambda i, j: (0, 0)),
                pl.BlockSpec((D, TF), lambda i, j: (0, j)),
                pl.BlockSpec((D, TF), lambda i, j: (0, j)),
                pl.BlockSpec((TF, D), lambda i, j: (j, 0))]
    args = [x, g.reshape(1, D), wg, wu, wd]
    est = 2 * (2 * TM * D * 4) + 2 * 3 * D * TF * 2 + TM * D * 6 + 3 * TM * TF * 4
    if res is not None:
        a, wo = res
        K = a.shape[1]
        in_specs += [pl.BlockSpec((TM, K), lambda i, j: (i, 0)), pl.BlockSpec((K, D), lambda i, j: (0, 0))]
        args += [a, wo]
        est += 2 * TM * K * 2 + 2 * K * D * 2
    if final:
        in_specs.append(pl.BlockSpec((1, D), lambda i, j: (0, 0)))
        args.append(final_g.reshape(1, D))
    return pl.pallas_call(
        functools.partial(_ffn_body, nj=nj, final=final, has_res=res is not None),
        grid=(T // TM, nj),
        in_specs=in_specs,
        out_specs=pl.BlockSpec((TM, D), lambda i, j: (i, 0)),
        out_shape=jax.ShapeDtypeStruct((T, D), f32),
        scratch_shapes=[pltpu.VMEM((TM, D), bf16), pltpu.VMEM((TM, D), f32)],
        compiler_params=_params(("parallel", "arbitrary"), est),
        name="ffn",
    )(*args)


def _log_sigmoid(x):
    return jnp.minimum(x, 0.0) - jnp.log(1.0 + jnp.exp(-jnp.abs(x)))


def _norm_proj_body(*refs, outs, has_bias, nres):
    x_ref, g_ref, w_ref = refs[:3]
    pos = 3
    b_ref = None
    if has_bias:
        b_ref = refs[pos]
        pos += 1
    x = x_ref[...]
    for p in range(nres):
        x = x + _dot(refs[pos][...], refs[pos + 1][...])
        pos += 2
    o_refs = refs[pos:]
    if nres:
        o_refs[0][...] = x
        o_refs = o_refs[1:]
    h = _rms(x, g_ref[...]).astype(bf16)
    cache = {}
    for o_ref, (off, n, dt, scale, act) in zip(o_refs, outs):
        if (off, n) not in cache:
            cache[(off, n)] = _dot(h, w_ref[:, off:off + n])
        r = cache[(off, n)]
        if act == "logsig":
            r = _log_sigmoid(r + b_ref[...])
        if scale != 1.0:
            r = r * scale
        o_ref[...] = r.astype(dt)


def _norm_proj(x, g, w, outs, bias=None, res=()):
    T, D = x.shape
    N = w.shape[1]
    TM = min(512, T)
    in_specs = [pl.BlockSpec((TM, D), lambda i: (i, 0)),
                pl.BlockSpec((1, D), lambda i: (0, 0)),
                pl.BlockSpec((D, N), lambda i: (0, 0))]
    args = [x, g.reshape(1, D), w]
    if bias is not None:
        in_specs.append(pl.BlockSpec((1, bias.shape[-1]), lambda i: (0, 0)))
        args.append(bias)
    est = 2 * TM * D * 4 + 2 * D * N * 2 + sum(2 * TM * n * 4 for (_, n, _, _, _) in outs) + TM * N * 4
    for a, wr in res:
        K = a.shape[1]
        in_specs += [pl.BlockSpec((TM, K), lambda i: (i, 0)), pl.BlockSpec((K, D), lambda i: (0, 0))]
        args += [a, wr]
        est += 2 * TM * K * 2 + 2 * K * D * 2
    out_specs = [pl.BlockSpec((TM, n), lambda i: (i, 0)) for (_, n, _, _, _) in outs]
    out_shape = [jax.ShapeDtypeStruct((T, n), dt) for (_, n, dt, _, _) in outs]
    if res:
        out_specs.insert(0, pl.BlockSpec((TM, D), lambda i: (i, 0)))
        out_shape.insert(0, jax.ShapeDtypeStruct((T, D), f32))
        est += 2 * TM * D * 4
    return pl.pallas_call(
        functools.partial(_norm_proj_body, outs=tuple(outs), has_bias=bias is not None, nres=len(res)),
        grid=(T // TM,),
        in_specs=in_specs,
        out_specs=out_specs,
        out_shape=out_shape,
        compiler_params=_params(("parallel",), est),
        name="norm_proj",
    )(*args)


def _cumsum_body(x_ref, f_ref, ft_ref, carry, *, nsub):
    @pl.when(pl.program_id(0) == 0)
    def _():
        carry[...] = jnp.zeros_like(carry)

    r = lax.broadcasted_iota(jnp.int32, (V7X_LANES, V7X_LANES), 0)
    c = lax.broadcasted_iota(jnp.int32, (V7X_LANES, V7X_LANES), 1)
    tri = (r >= c).astype(f32)
    run = carry[...]
    ncol = x_ref.shape[1] // V7X_LANES
    for sb in range(nsub):
        rows = slice(sb * V7X_LANES, (sb + 1) * V7X_LANES)
        blk = _dot_hi(tri, x_ref[rows, :]) + run
        f_ref[rows, :] = blk
        for cb in range(ncol):
            cols = slice(cb * V7X_LANES, (cb + 1) * V7X_LANES)
            ft_ref[cols, rows] = blk[:, cols].T
        run = blk[V7X_LANES - 1:V7X_LANES, :]
    carry[...] = run


def _cumsum_rows(x):
    L, N = x.shape
    nsub = 8 if L % (8 * V7X_LANES) == 0 else 1
    TB = nsub * V7X_LANES
    return pl.pallas_call(
        functools.partial(_cumsum_body, nsub=nsub),
        grid=(L // TB,),
        in_specs=[pl.BlockSpec((TB, N), lambda i: (i, 0))],
        out_specs=[pl.BlockSpec((TB, N), lambda i: (i, 0)), pl.BlockSpec((N, TB), lambda i: (0, i))],
        out_shape=[jax.ShapeDtypeStruct((L, N), f32), jax.ShapeDtypeStruct((N, L), f32)],
        scratch_shapes=[pltpu.VMEM((1, N), f32)],
        compiler_params=_params(("arbitrary",), 8 * TB * N * 4),
        name="cumsum_rows",
    )(x)


BAND_TQ = 256
BAND_NB = 3


def _band_body(q_ref, k0_ref, k1_ref, k2_ref, v0_ref, v1_ref, v2_ref, b_ref, o_ref, *, nheads):
    TQ = BAND_TQ
    lane = lax.broadcasted_iota(jnp.int32, (TQ, V7X_LANES), 1)
    lo = lane < 64
    krefs = (k0_ref, k1_ref, k2_ref)
    vrefs = (v0_ref, v1_ref, v2_ref)
    for hp in range(nheads // 2):
        cols = slice(hp * V7X_LANES, (hp + 1) * V7X_LANES)
        qp = q_ref[:, cols]
        kp = [kr[:, cols] for kr in krefs]
        vp = [vr[:, cols] for vr in vrefs]
        res = []
        for half in range(2):
            h = 2 * hp + half
            qm = jnp.where(lo if half == 0 else jnp.logical_not(lo), qp, jnp.zeros_like(qp))
            s = [_dot_nt(qm, kp[b]) + b_ref[h, :, b * TQ:(b + 1) * TQ] for b in range(BAND_NB)]
            m = jnp.max(s[0], axis=-1, keepdims=True)
            for b in range(1, BAND_NB):
                m = jnp.maximum(m, jnp.max(s[b], axis=-1, keepdims=True))
            p = [jnp.exp2(sb - m) for sb in s]
            l = p[0].sum(axis=-1, keepdims=True)
            o = _dot(p[0].astype(bf16), vp[0])
            for b in range(1, BAND_NB):
                l = l + p[b].sum(axis=-1, keepdims=True)
                o = o + _dot(p[b].astype(bf16), vp[b])
            res.append(o / l)
        o_ref[:, cols] = jnp.where(lo, res[0], res[1]).astype(bf16)


def _toeplitz_bias(table, nq, nk, off):
    H = table.shape[0]
    m = np.arange(nq + nk - 1)
    idx = np.clip(off + nq - 1 - m, -REL_CLIP, REL_CLIP) + REL_CLIP
    w = jnp.concatenate([table[:, idx].astype(f32), jnp.zeros((H, 1), f32)], axis=-1)
    skew = jnp.tile(w, (1, nq))[:, :nq * (nq + nk - 1)].reshape(H, nq, nq + nk - 1)
    return skew[:, :, nq - 1:]


def _band_bias(table):
    qc = np.arange(BAND_TQ)[:, None] // CHUNK
    ki = np.arange(BAND_NB * BAND_TQ)[None, :]
    kc = ki // CHUNK
    in_band = (kc >= qc) & (kc <= qc + BAND_CHUNKS)
    valid = np.stack([in_band & (ki // BAND_TQ >= BAND_NB - 1 - n) for n in range(BAND_NB)])
    bias = _toeplitz_bias(table, BAND_TQ, BAND_NB * BAND_TQ, BAND_CHUNKS * CHUNK) * LOG2E
    return jnp.where(valid[:, None], bias[None], NEG)


def _band_prompt(q, k, v, bias):
    L, W = q.shape
    TQ = BAND_TQ
    H = bias.shape[1]
    kspec = [pl.BlockSpec((TQ, W), functools.partial(lambda i, d: (jnp.maximum(i - d, 0), 0), d=d))
             for d in (2, 1, 0)]
    est = 2 * 7 * TQ * W * 2 + 2 * H * TQ * BAND_NB * TQ * 4 + 16 * TQ * TQ * 4
    return pl.pallas_call(
        functools.partial(_band_body, nheads=H),
        grid=(L // TQ,),
        in_specs=[pl.BlockSpec((TQ, W), lambda i: (i, 0))] + kspec + kspec
                 + [pl.BlockSpec((None, H, TQ, BAND_NB * TQ), lambda i: (jnp.minimum(i, BAND_NB - 1), 0, 0, 0))],
        out_specs=pl.BlockSpec((TQ, W), lambda i: (i, 0)),
        out_shape=jax.ShapeDtypeStruct((L, W), bf16),
        compiler_params=_params(("parallel",), est),
        name="band_prompt",
    )(q, k, k, k, v, v, v, bias)


FOX_T = 1024
FOX_VROWS = 80
FOX_QPIECE = 64
FOX_KPIECE = 67
LOG2E = 1.4426950408889634


def _fox_pack_body(q_ref, k_ref, v_ref, f_ref, e_ref, qa_ref, ka_ref, vt_ref, *, nheads, dh):
    T, W2 = q_ref.shape
    a = (f_ref[...] - f_ref[0:1, :]) * LOG2E
    hi = a.astype(bf16)
    r1 = a - hi.astype(f32)
    mid = r1.astype(bf16)
    lo = (r1 - mid.astype(f32)).astype(bf16)
    aug = _dot(jnp.concatenate([hi, mid, lo], axis=1), e_ref[...])
    lane = lax.broadcasted_iota(jnp.int32, (T, W2), 1) % V7X_LANES
    ones_q = jnp.where((lane >= FOX_KPIECE) & (lane < FOX_KPIECE + 3), 1.0, 0.0)
    ones_k = jnp.where((lane >= FOX_QPIECE) & (lane < FOX_QPIECE + 3), 1.0, 0.0)
    qa_ref[...] = (q_ref[...].astype(f32) + aug[:, :W2] + ones_q).astype(bf16)
    ka_ref[...] = (k_ref[...].astype(f32) + aug[:, W2:] + ones_k).astype(bf16)
    vT = v_ref[...].astype(f32).T
    pad = FOX_VROWS - dh
    ones_row = jnp.where(lax.broadcasted_iota(jnp.int32, (pad, T), 0) == 0, 1.0, 0.0).astype(bf16)
    for h in range(nheads):
        vt_ref[h, 0:dh, :] = vT[h * dh:(h + 1) * dh, :].astype(bf16)
        vt_ref[h, dh:FOX_VROWS, :] = ones_row


def _fox_pack(q, k, v, F, nheads):
    L, W2 = q.shape
    W = v.shape[1]
    dh = W // nheads
    T = FOX_T
    e = np.zeros((3 * V7X_LANES, 2 * W2), np.float32)
    for h in range(nheads):
        for t in range(3):
            e[t * V7X_LANES + h, h * V7X_LANES + FOX_QPIECE + t] = 1.0
            e[t * V7X_LANES + h, W2 + h * V7X_LANES + FOX_KPIECE + t] = -1.0
    est = 2 * (4 * T * W2 * 2 + T * W * 2 + nheads * FOX_VROWS * T * 2) + 6 * T * W2 * 4
    return pl.pallas_call(
        functools.partial(_fox_pack_body, nheads=nheads, dh=dh),
        grid=(L // T,),
        in_specs=[pl.BlockSpec((T, W2), lambda i: (i, 0)),
                  pl.BlockSpec((T, W2), lambda i: (i, 0)),
                  pl.BlockSpec((T, W), lambda i: (i, 0)),
                  pl.BlockSpec((T, V7X_LANES), lambda i: (i, 0)),
                  pl.BlockSpec((3 * V7X_LANES, 2 * W2), lambda i: (0, 0))],
        out_specs=[pl.BlockSpec((T, W2), lambda i: (i, 0)),
                   pl.BlockSpec((T, W2), lambda i: (i, 0)),
                   pl.BlockSpec((nheads, FOX_VROWS, T), lambda i: (0, 0, i))],
        out_shape=[jax.ShapeDtypeStruct((L, W2), bf16), jax.ShapeDtypeStruct((L, W2), bf16),
                   jax.ShapeDtypeStruct((nheads, FOX_VROWS, L), bf16)],
        compiler_params=_params(("parallel",), est),
        name="fox_pack",
    )(q, k, v, F, jnp.asarray(e, bf16))


def _fox_body(qi_ref, kj_ref, qa_ref, ka_ref, vt_ref, d_ref, o_ref, m_scr, acc_scr, *, nheads, dh):
    s_id = pl.program_id(0)
    i = qi_ref[s_id]
    j = kj_ref[s_id]
    T = FOX_T

    @pl.when(j == 0)
    def _():
        m_scr[...] = jnp.full_like(m_scr, NEG)
        acc_scr[...] = jnp.zeros_like(acc_scr)

    def step(masked):
        if masked:
            keep = lax.broadcasted_iota(jnp.int32, (T, T), 0) <= lax.broadcasted_iota(jnp.int32, (T, T), 1)
        def scores(h):
            cols = slice(h * V7X_LANES, (h + 1) * V7X_LANES)
            st = _dot_nt(ka_ref[:, cols], qa_ref[:, cols])
            return jnp.where(keep, st, NEG) if masked else st

        nxt = scores(0)
        for h in range(nheads):
            st = nxt
            if h + 1 < nheads:
                nxt = scores(h + 1)
            d = d_ref[s_id * nheads + h]
            m_prev = m_scr[h:h + 1, :]
            m_new = jnp.maximum(m_prev, jnp.max(st, axis=0, keepdims=True) + d)
            p = jnp.exp2(st - (m_new - d))
            alpha = jnp.exp2(m_prev - m_new)
            acc_scr[h] = acc_scr[h] * alpha + _dot(vt_ref[h], p.astype(bf16))
            m_scr[h:h + 1, :] = m_new

    @pl.when(j < i)
    def _():
        step(False)

    @pl.when(j == i)
    def _():
        step(True)
        for hp in range(nheads // 2):
            a0 = acc_scr[2 * hp]
            a1 = acc_scr[2 * hp + 1]
            o2 = jnp.concatenate([a0[0:dh, :] / a0[dh:dh + 1, :], a1[0:dh, :] / a1[dh:dh + 1, :]], axis=0)
            o_ref[:, hp * V7X_LANES:(hp + 1) * V7X_LANES] = o2.T.astype(bf16)


def _fox_prompt(qa, ka, vt, F, nheads):
    L, W2 = qa.shape
    T = FOX_T
    n = L // T
    dh = V7X_LANES // 2
    W = nheads * dh
    pairs = [(i, j) for i in range(n) for j in range(i + 1)]
    qi = np.array([p[0] for p in pairs], np.int32)
    kj = np.array([p[1] for p in pairs], np.int32)
    fs = F[::T, :nheads]
    d = ((fs[qi] - fs[kj]) * LOG2E).reshape(-1)
    grid_spec = pltpu.PrefetchScalarGridSpec(
        num_scalar_prefetch=2,
        grid=(len(pairs),),
        in_specs=[pl.BlockSpec((T, W2), lambda s, qi, kj: (qi[s], 0)),
                  pl.BlockSpec((T, W2), lambda s, qi, kj: (kj[s], 0)),
                  pl.BlockSpec((nheads, FOX_VROWS, T), lambda s, qi, kj: (0, 0, kj[s])),
                  pl.BlockSpec(memory_space=pltpu.SMEM)],
        out_specs=pl.BlockSpec((T, W), lambda s, qi, kj: (qi[s], 0)),
        scratch_shapes=[pltpu.VMEM((nheads, T), f32), pltpu.VMEM((nheads, FOX_VROWS, T), f32)],
    )
    est = 2 * (2 * T * W2 * 2 + nheads * FOX_VROWS * T * 2 + T * W * 2) + nheads * FOX_VROWS * T * 4 + 24 * T * T * 4
    return pl.pallas_call(
        functools.partial(_fox_body, nheads=nheads, dh=dh),
        grid_spec=grid_spec,
        out_shape=jax.ShapeDtypeStruct((L, W), bf16),
        compiler_params=_params(("arbitrary",), est),
        name="fox_prompt",
    )(jnp.asarray(qi), jnp.asarray(kj), qa, ka, vt, d)


def _cross_body(*refs, nres, nheads, scale):
    x_ref, g_ref, wq_ref = refs[:3]
    x = x_ref[...]
    for p in range(nres):
        x = x + _dot(refs[3 + 2 * p][...], refs[4 + 2 * p][...])
    k_ref, v_ref, xo_ref, o_ref = refs[3 + 2 * nres:]
    xo_ref[...] = x
    q = (_dot(_rms(x, g_ref[...]).astype(bf16), wq_ref[...]) * scale).astype(bf16)
    dh = q.shape[1] // nheads
    hs = range(nheads)
    s = [_dot_nt(q[:, h * dh:(h + 1) * dh], k_ref[:, h * dh:(h + 1) * dh]) for h in hs]
    m = [jnp.max(sh, axis=-1, keepdims=True) for sh in s]
    p = [jnp.exp(s[h] - m[h]) for h in hs]
    l = [ph.sum(axis=-1, keepdims=True) for ph in p]
    o = [_dot(p[h].astype(bf16), v_ref[:, h * dh:(h + 1) * dh]) / l[h] for h in hs]
    o_ref[...] = jnp.concatenate(o, axis=-1).astype(bf16)


def _cross_prompt(x, g, wq, res, mk, mv, nheads):
    T, D = x.shape
    W = wq.shape[1]
    S = mk.shape[0]
    TM = min(512, T)
    row = lambda width: pl.BlockSpec((TM, width), lambda i: (i, 0))
    whole = lambda r, c: pl.BlockSpec((r, c), lambda i: (0, 0))
    in_specs = [row(D), whole(1, D), whole(D, W)]
    args = [x, g.reshape(1, D), wq]
    est = 4 * TM * D * 4 + 2 * D * W * 2 + 4 * S * W * 2 + 2 * TM * W * 2 + 3 * TM * W * 4 + 8 * TM * S * 4
    for a, wr in res:
        K = a.shape[1]
        in_specs += [row(K), whole(K, D)]
        args += [a, wr]
        est += 2 * TM * K * 2 + 2 * K * D * 2
    in_specs += [whole(S, W), whole(S, W)]
    args += [mk, mv]
    return pl.pallas_call(
        functools.partial(_cross_body, nres=len(res), nheads=nheads, scale=(W // nheads) ** -0.5),
        grid=(T // TM,),
        in_specs=in_specs,
        out_specs=[row(D), row(W)],
        out_shape=[jax.ShapeDtypeStruct((T, D), f32), jax.ShapeDtypeStruct((T, W), bf16)],
        compiler_params=_params(("parallel",), est),
        name="cross_prompt",
    )(*args)


def _sample_attn_body(*refs, nheads, mode):
    q_ref, kc_ref, vc_ref, kn_ref, vn_ref = refs[:5]
    if mode == "rel":
        bc_ref, bn_ref, o_ref = refs[5:]
    else:
        fq_ref, fkc_ref, fkn_ref, o_ref = refs[5:]

    m_q, W = q_ref.shape
    dh = W // nheads
    R = nheads * m_q
    q = q_ref[...].astype(f32)
    qt = jnp.concatenate([q] * nheads, axis=0)
    hrow = lax.broadcasted_iota(jnp.int32, (R, W), 0) // m_q
    hlane = lax.broadcasted_iota(jnp.int32, (R, W), 1) // dh
    qbd = jnp.where(hrow == hlane, qt, 0.0).astype(bf16)

    def rows_of(f_ref):
        n = f_ref.shape[-1]
        return jnp.concatenate([jnp.broadcast_to(f_ref[h:h + 1, :], (m_q, n)) for h in range(nheads)], axis=0)

    sc = _dot_nt(qbd, kc_ref[...].astype(bf16))
    sn = _dot_nt(qbd, kn_ref[...].astype(bf16))
    if mode == "rel":
        sc = sc + bc_ref[...]
        sn = sn + bn_ref[...]
    else:
        sc = sc + (fq_ref[...] - rows_of(fkc_ref))
        sn = sn + (fq_ref[...] - rows_of(fkn_ref))
        qpos = lax.broadcasted_iota(jnp.int32, (R, m_q), 0) % m_q
        kpos = lax.broadcasted_iota(jnp.int32, (R, m_q), 1)
        sn = jnp.where(kpos <= qpos, sn, NEG)
    m = jnp.maximum(jnp.max(sc, axis=-1, keepdims=True), jnp.max(sn, axis=-1, keepdims=True))
    pc = jnp.exp(sc - m)
    pn = jnp.exp(sn - m)
    l = pc.sum(axis=-1, keepdims=True) + pn.sum(axis=-1, keepdims=True)
    o = _dot(pc.astype(bf16), vc_ref[...].astype(bf16)) + _dot(pn.astype(bf16), vn_ref[...].astype(bf16))
    o = o / l
    hl = lax.broadcasted_iota(jnp.int32, (m_q, W), 1) // dh
    out = jnp.zeros((m_q, W), f32)
    for h in range(nheads):
        out = out + jnp.where(hl == h, o[h * m_q:(h + 1) * m_q, :], 0.0)
    o_ref[...] = out.astype(bf16)


def _sample_attn(q, kc, vc, nheads, mode, kn, vn, extra):
    B, m_q, W = q.shape
    P = kc.shape[1]
    R = nheads * m_q
    per_b = lambda *shape: pl.BlockSpec((None,) + shape, lambda b: (b,) + (0,) * len(shape))
    shared = lambda *shape: pl.BlockSpec(shape, lambda b: (0,) * len(shape))
    in_specs = [per_b(m_q, W), per_b(P, W), per_b(P, W), per_b(m_q, W), per_b(m_q, W)]
    if mode == "rel":
        in_specs += [shared(R, P), shared(R, m_q)]
    else:
        in_specs += [per_b(R, 1), per_b(nheads, P), per_b(nheads, m_q)]
    est = 2 * 2 * P * W * 4 + 2 * P * W * 2 + 6 * R * P * 4 + 4 * R * W * 4
    return pl.pallas_call(
        functools.partial(_sample_attn_body, nheads=nheads, mode=mode),
        grid=(B,),
        in_specs=in_specs,
        out_specs=per_b(m_q, W),
        out_shape=jax.ShapeDtypeStruct((B, m_q, W), bf16),
        compiler_params=_params(("parallel",), est, [False, True, True] + [False] * (2 + len(extra))),
        name="sample_attn_" + mode,
    )(q, kc, vc, kn, vn, *extra)


def _cross_sample_body(q_ref, kc_ref, vc_ref, o_ref, *, nheads):
    m_q, W = q_ref.shape
    dh = W // nheads
    npc = dh // kc_ref.shape[1]
    P = kc_ref.shape[0] // (nheads * npc)
    hs = range(nheads)

    def head_rows(ref, h):
        pieces = [ref[pl.ds(h * npc + j, P, stride=nheads * npc), :] for j in range(npc)]
        return jnp.concatenate(pieces, axis=-1).astype(bf16)

    q = q_ref[...]
    sc = [_dot_nt(q[:, h * dh:(h + 1) * dh], head_rows(kc_ref, h)) for h in hs]
    m = [jnp.max(s, axis=-1, keepdims=True) for s in sc]
    pc = [jnp.exp(sc[h] - m[h]) for h in hs]
    l = [p.sum(axis=-1, keepdims=True) for p in pc]
    o = [_dot(pc[h].astype(bf16), head_rows(vc_ref, h)) / l[h] for h in hs]
    o_ref[...] = jnp.concatenate(o, axis=-1).astype(bf16)


def _cross_sample(q, kc_all, vc_all, layer):
    B, m_q, W = q.shape
    nl, _, P, nheads, dh = kc_all.shape
    nrows = P * nheads * (dh // V7X_LANES)
    kc_all = kc_all.reshape(nl, B, nrows, V7X_LANES)
    vc_all = vc_all.reshape(nl, B, nrows, V7X_LANES)
    cache = pl.BlockSpec((None, None, nrows, V7X_LANES), lambda b: (layer, b, 0, 0))
    qo = pl.BlockSpec((None, m_q, W), lambda b: (b, 0, 0))
    est = 2 * 2 * nrows * V7X_LANES * 4 + 4 * P * W * 2 + 8 * nheads * m_q * P * 4
    return pl.pallas_call(
        functools.partial(_cross_sample_body, nheads=nheads),
        grid=(B,),
        in_specs=[qo, cache, cache],
        out_specs=qo,
        out_shape=jax.ShapeDtypeStruct((B, m_q, W), bf16),
        compiler_params=_params(("parallel",), est, [False, True, True]),
        name="cross_sample",
    )(q, kc_all, vc_all)


GDN_HALO = 8


def _softplus(x):
    return jnp.maximum(x, 0.0) + jnp.log(1.0 + jnp.exp(-jnp.abs(x)))


def _split2(a):
    hi = a.astype(bf16)
    return hi, (a - hi.astype(f32)).astype(bf16)


def _split3(a):
    hi = a.astype(bf16)
    r1 = a - hi.astype(f32)
    mid = r1.astype(bf16)
    return hi, mid, (r1 - mid.astype(f32)).astype(bf16)


def _gdn_conv_qkv(xbuf, R, cw_ref, nheads, dk):
    H0 = GDN_HALO
    W = nheads * dk
    cw = cw_ref[...]
    conv = xbuf[H0 - 3:H0 - 3 + R, :] * cw[0:1, :]
    for jj in range(1, CONV):
        conv = conv + xbuf[H0 - 3 + jj:H0 - 3 + jj + R, :] * cw[jj:jj + 1, :]
    tail = xbuf[H0 + R - (CONV - 1):H0 + R, :]
    act = _silu(conv)
    hs = range(nheads)
    qn = [act[:, h * dk:(h + 1) * dk] for h in hs]
    kn = [act[:, W + h * dk:W + (h + 1) * dk] for h in hs]
    qn = [q * lax.rsqrt(jnp.sum(q * q, axis=-1, keepdims=True) + EPS) * (dk ** -0.5) for q in qn]
    kn = [k * lax.rsqrt(jnp.sum(k * k, axis=-1, keepdims=True) + EPS) for k in kn]
    vn = [act[:, 2 * W + h * dk:2 * W + (h + 1) * dk] for h in hs]
    return qn, kn, vn, tail


def _gdn_proj_body(x_ref, g_ref, w_ref, cs_ref, cw_ref, q_ref, k_ref, v_ref, z_ref, ab_ref, cf_ref, xbuf,
                   *, nsteps, nheads, dk):
    i = pl.program_id(0)
    W = nheads * dk
    TM = x_ref.shape[0]
    H0 = GDN_HALO

    @pl.when(i == 0)
    def _():
        xbuf[...] = jnp.zeros_like(xbuf)

    qn, kn, vn, tail = _gdn_conv_qkv(xbuf, TM, cw_ref, nheads, dk)
    q_ref[...] = jnp.concatenate(qn, axis=-1)
    k_ref[...] = jnp.concatenate(kn, axis=-1)
    v_ref[...] = jnp.concatenate(vn, axis=-1)

    xbuf[H0 - (CONV - 1):H0, :] = jnp.where(i == 0, cs_ref[...], tail)
    cf_ref[...] = tail

    h = _rms(x_ref[...], g_ref[...]).astype(bf16)
    xbuf[H0:H0 + TM, :] = _dot(h, w_ref[:, :3 * W])
    z_ref[...] = _dot(h, w_ref[:, 3 * W:4 * W])
    ab_ref[...] = _dot(h, w_ref[:, 4 * W:])


def _gdn_proj(x, g, w, conv_state, conv_w, nheads):
    T, D = x.shape
    N = w.shape[1]
    W = (N - V7X_LANES) // 4
    dk = W // nheads
    TM = min(256, T)
    nsteps = T // TM
    row = lambda width: pl.BlockSpec((TM, width), lambda i: (jnp.minimum(i, nsteps - 1), 0))
    late = lambda width: pl.BlockSpec((TM, width), lambda i: (jnp.maximum(i - 1, 0), 0))
    whole = lambda *shape: pl.BlockSpec(shape, lambda i: (0,) * len(shape))
    est = 2 * TM * D * 4 + 2 * D * N * 2 + 2 * TM * (4 * W + V7X_LANES) * 4 + (GDN_HALO + TM) * 3 * W * 4 + 4 * TM * 3 * W * 4
    q, k, v, z, ab, cf = pl.pallas_call(
        functools.partial(_gdn_proj_body, nsteps=nsteps, nheads=nheads, dk=dk),
        grid=(nsteps + 1,),
        in_specs=[row(D), whole(1, D), whole(D, N),
                  pl.BlockSpec((None, CONV - 1, 3 * W), lambda i: (0, 0, 0)), whole(CONV, 3 * W)],
        out_specs=[late(W), late(W), late(W), row(W), row(V7X_LANES),
                   pl.BlockSpec((None, CONV - 1, 3 * W), lambda i: (0, 0, 0))],
        out_shape=[jax.ShapeDtypeStruct((T, W), f32)] * 4
                  + [jax.ShapeDtypeStruct((T, V7X_LANES), f32), jax.ShapeDtypeStruct((1, CONV - 1, 3 * W), f32)],
        scratch_shapes=[pltpu.VMEM((GDN_HALO + TM, 3 * W), f32)],
        compiler_params=_params(("arbitrary",), est),
        name="gdn_proj",
    )(x, g.reshape(1, D), w, conv_state, conv_w)
    return q, k, v, z, ab, cf


def _gdn_body(*refs, C, NS, nsteps, nheads, dk, pre):
    if pre:
        q_ref, k_ref, v_ref, ab_ref, z_ref, s0_ref, alog_ref, dtb_ref, ng_ref, o_ref, sf_ref, s_scr = refs
    else:
        (x_ref, ab_ref, z_ref, cs_ref, s0_ref, cw_ref, alog_ref, dtb_ref, ng_ref,
         o_ref, sf_ref, cf_ref, xbuf, s_scr) = refs
    c = pl.program_id(1)
    R = NS * C
    G = V7X_LANES // C
    NG = nheads // G
    GW = G * dk
    hs = range(nheads)

    @pl.when(c == 0)
    def _():
        s_scr[...] = s0_ref[...]

    if pre:
        qn = [q_ref[:, h * dk:(h + 1) * dk] for h in hs]
        kn = [k_ref[:, h * dk:(h + 1) * dk] for h in hs]
        vn = [v_ref[:, h * dk:(h + 1) * dk] for h in hs]
    else:
        H0 = GDN_HALO

        @pl.when(c == 0)
        def _():
            xbuf[H0 - (CONV - 1):H0, :] = cs_ref[...]

        xbuf[H0:H0 + R, :] = x_ref[...]
        qn, kn, vn, tail = _gdn_conv_qkv(xbuf, R, cw_ref, nheads, dk)
        xbuf[H0 - (CONV - 1):H0, :] = tail

        @pl.when(c == nsteps - 1)
        def _():
            cf_ref[...] = tail

    ab = ab_ref[...]
    gfull = -jnp.exp(alog_ref[...]) * _softplus(ab + dtb_ref[...])
    bfull = 1.0 / (1.0 + jnp.exp(-ab))
    z = z_ref[...]
    ng = ng_ref[...]

    r_cc = lax.broadcasted_iota(jnp.int32, (C, C), 0)
    c_cc = lax.broadcasted_iota(jnp.int32, (C, C), 1)
    tri_f = (r_cc >= c_cc).astype(f32)
    ri = lax.broadcasted_iota(jnp.int32, (C, V7X_LANES), 0)
    jl = lax.broadcasted_iota(jnp.int32, (C, V7X_LANES), 1) % C
    lblk = lax.broadcasted_iota(jnp.int32, (C, V7X_LANES), 1) // C
    tri_g = ri >= jl
    strict_g = ri > jl
    eye_g = ri == jl
    r128 = lax.broadcasted_iota(jnp.int32, (V7X_LANES, V7X_LANES), 0)
    l128 = lax.broadcasted_iota(jnp.int32, (V7X_LANES, V7X_LANES), 1)
    same_blk = (r128 // C) == (l128 // C)
    wide_blk = (lax.broadcasted_iota(jnp.int32, (V7X_LANES, GW), 0) // C
                == lax.broadcasted_iota(jnp.int32, (V7X_LANES, GW), 1) // dk)

    def bd_sq(p):
        return jnp.where(same_blk, jnp.concatenate([p] * G, axis=0), jnp.zeros((), p.dtype))

    def bd_wide(xs):
        row = jnp.concatenate(xs, axis=1)
        return jnp.where(wide_blk, jnp.concatenate([row] * G, axis=0), 0.0).astype(bf16)

    def gmm3(xg, pg):
        xh, xl = _split2(xg)
        ph, plo = _split2(pg)
        bh = bd_sq(ph)
        return _dot(xh, bh) + (_dot(xh, bd_sq(plo)) + _dot(xl, bh))

    sls = [slice(s * C, (s + 1) * C) for s in range(NS)]
    gcs = [_dot_hi(tri_f, gfull[sl]) for sl in sls]
    egcs = [jnp.exp(gc) for gc in gcs]
    kdecs = [jnp.exp(gc[C - 1:C, :] - gc) for gc in gcs]
    gcTs = [jnp.concatenate([gc] * G, axis=0).T for gc in gcs]
    bcols_s = [[bfull[sl, nheads + h:nheads + h + 1] for h in hs] for sl in sls]
    ecols_s = [[egc[:, h:h + 1] for h in hs] for egc in egcs]
    kbs_s = [[kn[h][sls[s]] * bcols_s[s][h] for h in hs] for s in range(NS)]
    Ms, Aqks = [], []
    for s in range(NS):
        sl, gc, gcT, kbs = sls[s], gcs[s], gcTs[s], kbs_s[s]
        for grp in range(NG):
            heads = range(grp * G, (grp + 1) * G)
            gcol = jnp.broadcast_to(gc[:, grp * G:grp * G + 1], (C, V7X_LANES))
            grow = gcT[grp * G:grp * G + 1, :]
            for g in range(1, G):
                h = grp * G + g
                gcol = jnp.where(lblk == g, gc[:, h:h + 1], gcol)
                grow = jnp.where(lblk[0:1, :] == g, gcT[h:h + 1, :], grow)
            Lm = jnp.where(tri_g, jnp.exp(jnp.where(tri_g, gcol - grow, 0.0)), 0.0)
            kbd = bd_wide([kn[h][sl] for h in heads])
            kb_row = jnp.concatenate([kbs[h] for h in heads], axis=1).astype(bf16)
            q_row = jnp.concatenate([qn[h][sl] for h in heads], axis=1).astype(bf16)
            Ms.append(jnp.where(strict_g, _dot_nt(kb_row, kbd) * Lm, 0.0))
            Aqks.append(jnp.where(tri_g, _dot_nt(q_row, kbd) * Lm, 0.0).astype(bf16))
    Xs = [jnp.where(eye_g, 1.0, 0.0) - M for M in Ms]
    Pws = [gmm3(M, M) for M in Ms]
    e = 2
    while e < C:
        Xs = [X + gmm3(X, Pw) for X, Pw in zip(Xs, Pws)]
        e *= 2
        if e < C:
            Pws = [gmm3(Pw, Pw) for Pw in Pws]
    prep = []
    for s in range(NS):
        sl, bcols, ecols, kbs = sls[s], bcols_s[s], ecols_s[s], kbs_s[s]
        us, ws = [], []
        for grp in range(NG):
            heads = range(grp * G, (grp + 1) * G)
            Tm = Xs[s * NG + grp].astype(bf16)
            U = _dot(Tm, bd_wide([vn[h][sl] * bcols[h] for h in heads]))
            Wm = _dot(Tm, bd_wide([kbs[h] * ecols[h] for h in heads]))
            for g, h in enumerate(heads):
                us.append(U[:, g * dk:(g + 1) * dk])
                ws.append(Wm[:, g * dk:(g + 1) * dk].astype(bf16))
        qgs = [(qn[h][sl] * ecols[h]).astype(bf16) for h in hs]
        kgTs = [(kn[h][sl] * kdecs[s][:, h:h + 1]).T.astype(bf16) for h in hs]
        decs = [egcs[s][C - 1:C, h:h + 1] for h in hs]
        prep.append((us, ws, Aqks[s * NG:(s + 1) * NG], qgs, kgTs, decs))

    Ss = [s_scr[h] for h in hs]
    for s in range(NS):
        us, ws, Aqks, qgs, kgTs, decs = prep[s]
        Sbs = [S.astype(bf16) for S in Ss]
        vnews = [us[h] - _dot(ws[h], Sbs[h]) for h in hs]
        vnbs = [v.astype(bf16) for v in vnews]
        intra = [_dot(Aqks[grp], bd_wide([vnews[h] for h in range(grp * G, (grp + 1) * G)])) for grp in range(NG)]
        os_ = [_dot(qgs[h], Sbs[h]) + intra[h // G][:, (h % G) * dk:(h % G + 1) * dk] for h in hs]
        Ss = [Ss[h] * decs[h] + _dot(kgTs[h], vnbs[h]) for h in hs]
        zs = z[s * C:(s + 1) * C, :]
        ys = [(_rms(os_[h], ng) * _silu(zs[:, h * dk:(h + 1) * dk])).astype(bf16) for h in hs]
        o_ref[s * C:(s + 1) * C, :] = jnp.concatenate(ys, axis=-1)
    for h in hs:
        s_scr[h] = Ss[h]

    @pl.when(c == nsteps - 1)
    def _():
        sf_ref[...] = s_scr[...]


def _gdn_core_pre(q, k, v, ab, z, S0, a_log, dt_bias, norm_g, C, NS):
    T = q.shape[0]
    B, H, dk, dv = S0.shape
    W = H * dk
    R = NS * C
    nsteps = T // (B * R)
    pad = lambda vec: jnp.zeros((1, V7X_LANES), f32).at[0, :H].set(vec.astype(f32))
    row = lambda width: pl.BlockSpec((R, width), lambda b, c: (b * nsteps + c, 0))
    lane_row = pl.BlockSpec((1, V7X_LANES), lambda b, c: (0, 0))
    state = pl.BlockSpec((None, H, dk, dv), lambda b, c: (b, 0, 0, 0))
    est = 2 * R * (4 * W + V7X_LANES) * 4 + 3 * H * dk * dv * 4 * 2 + 16 * R * W * 4
    return pl.pallas_call(
        functools.partial(_gdn_body, C=C, NS=NS, nsteps=nsteps, nheads=H, dk=dk, pre=True),
        grid=(B, nsteps),
        in_specs=[row(W), row(W), row(W), row(V7X_LANES), row(W), state, lane_row, lane_row,
                  pl.BlockSpec((1, dv), lambda b, c: (0, 0))],
        out_specs=[row(W), state],
        out_shape=[jax.ShapeDtypeStruct((T, W), bf16), jax.ShapeDtypeStruct((B, H, dk, dv), f32)],
        scratch_shapes=[pltpu.VMEM((H, dk, dv), f32)],
        compiler_params=_params(("parallel", "arbitrary"), est),
        name="gdn_core_pre",
    )(q, k, v, ab, z, S0, pad(a_log), pad(dt_bias), norm_g.reshape(1, dv))


def _gdn_core(qkv, ab, z, conv_state, S0, conv_w, a_log, dt_bias, norm_g, C, NS):
    T = qkv.shape[0]
    B, H, dk, dv = S0.shape
    W = H * dk
    R = NS * C
    nsteps = T // (B * R)
    pad = lambda vec: jnp.zeros((1, V7X_LANES), f32).at[0, :H].set(vec.astype(f32))
    row = lambda width: pl.BlockSpec((R, width), lambda b, c: (b * nsteps + c, 0))
    est = 2 * R * (3 * W + W + V7X_LANES) * 4 + 3 * H * dk * dv * 4 * 2 + 12 * R * 3 * W * 4
    return pl.pallas_call(
        functools.partial(_gdn_body, C=C, NS=NS, nsteps=nsteps, nheads=H, dk=dk, pre=False),
        grid=(B, nsteps),
        in_specs=[row(3 * W), row(V7X_LANES), row(W),
                  pl.BlockSpec((None, CONV - 1, 3 * W), lambda b, c: (b, 0, 0)),
                  pl.BlockSpec((None, H, dk, dv), lambda b, c: (b, 0, 0, 0)),
                  pl.BlockSpec((CONV, 3 * W), lambda b, c: (0, 0)),
                  pl.BlockSpec((1, V7X_LANES), lambda b, c: (0, 0)),
                  pl.BlockSpec((1, V7X_LANES), lambda b, c: (0, 0)),
                  pl.BlockSpec((1, dv), lambda b, c: (0, 0))],
        out_specs=[row(W),
                   pl.BlockSpec((None, H, dk, dv), lambda b, c: (b, 0, 0, 0)),
                   pl.BlockSpec((None, CONV - 1, 3 * W), lambda b, c: (b, 0, 0))],
        out_shape=[jax.ShapeDtypeStruct((T, W), bf16),
                   jax.ShapeDtypeStruct((B, H, dk, dv), f32),
                   jax.ShapeDtypeStruct((B, CONV - 1, 3 * W), f32)],
        scratch_shapes=[pltpu.VMEM((GDN_HALO + R, 3 * W), f32), pltpu.VMEM((H, dk, dv), f32)],
        compiler_params=_params(("parallel", "arbitrary"), est),
        name="gdn_core",
    )(qkv, ab, z, conv_state, S0, conv_w, pad(a_log), pad(dt_bias), norm_g.reshape(1, dv))


def kernel(x_prompt, x_sample, mem_prompt, cache_a_k, cache_a_v, cache_b_k, cache_b_v, cache_b_logf, state_gdn, state_gdn_conv, cache_mem_k, cache_mem_v, norm_g, mem_norm_g, final_norm_g, ffn_w_gate, ffn_w_up, ffn_w_down, xa_w_q, xa_w_k, xa_w_v, xa_w_o, ab_w_in, ab_b_f, ab_rel_bias, ab_w_o, gdn_w_in, gdn_conv_w, gdn_a_log, gdn_dt_bias, gdn_norm_g, gdn_w_o):
    depth = norm_g.shape[0]
    BP, SEQ, D = x_prompt.shape
    BS, MS, _ = x_sample.shape
    assert BP == 1
    HA = HB = ab_b_f.shape[1]
    WA = WB = (ab_w_in.shape[2] - HB) // 6
    DHA = WA // HA
    HG = gdn_a_log.shape[1]
    WG = gdn_w_o.shape[1]
    DKG = WG // HG
    HX = cache_mem_k.shape[3]
    DHX = cache_mem_k.shape[4]
    WX = HX * DHX
    PA = cache_a_k.shape[2]
    PB = cache_b_k.shape[2]
    LANES = V7X_LANES

    xp = x_prompt.reshape(SEQ, D)
    xs = x_sample.reshape(BS * MS, D)

    wgate = ffn_w_gate.astype(bf16)
    wup = ffn_w_up.astype(bf16)
    wdown = ffn_w_down.astype(bf16)
    wxq = xa_w_q.astype(bf16)
    wxo = xa_w_o.astype(bf16)
    wxkv = jnp.concatenate([xa_w_k, xa_w_v], axis=-1).astype(bf16)

    outs = {k: [] for k in ("akp", "avp", "bkp", "bvp", "blp", "sgp", "scp", "mkp", "mvp",
                            "aks", "avs", "bks", "bvs", "bls", "sgs", "scs")}

    for l in range(depth):
        xp = _ffn(xp, norm_g[l, 0], wgate[l, 0], wup[l, 0], wdown[l, 0])
        xs = _ffn(xs, norm_g[l, 0], wgate[l, 0], wup[l, 0], wdown[l, 0])

        if l % 2 == 0:
            e = l // 2
            w_in = ab_w_in[e]
            nq = 3 * WA + 3 * WB
            w_pad = jnp.concatenate([w_in, jnp.zeros((D, LANES - HB), f32)], axis=-1).astype(bf16)
            b_pad = jnp.zeros((1, LANES), f32).at[0, :HB].set(ab_b_f[e].astype(f32))
            spec = [(0, WA, bf16, DHA ** -0.5, None),
                    (WA, WA, f32, 1.0, None), (WA, WA, bf16, 1.0, None),
                    (2 * WA, WA, f32, 1.0, None), (2 * WA, WA, bf16, 1.0, None),
                    (3 * WA, WB, bf16, (WB // HB) ** -0.5, None),
                    (3 * WA + WB, WB, f32, 1.0, None), (3 * WA + WB, WB, bf16, 1.0, None),
                    (3 * WA + 2 * WB, WB, f32, 1.0, None), (3 * WA + 2 * WB, WB, bf16, 1.0, None),
                    (nq, LANES, f32, 1.0, "logsig")]
            wo = ab_w_o[e].astype(bf16)

            spec_a = [(0, WA, bf16, DHA ** -0.5 * LOG2E, None)] + spec[1:5]
            qa, ka, kab, va, vab = _norm_proj(xp, norm_g[l, 1], w_pad[:, :3 * WA], spec_a)
            oa = _band_prompt(qa, kab, vab, _band_bias(ab_rel_bias[e]))
            DHB = WB // HB
            spread = lambda w: jnp.pad(w.reshape(D, HB, DHB), ((0, 0), (0, 0), (0, LANES - DHB))).reshape(D, HB * LANES)
            wq_b = w_in[:, 3 * WA:3 * WA + WB]
            wk_b = w_in[:, 3 * WA + WB:3 * WA + 2 * WB]
            w_b = jnp.concatenate([spread(wq_b), wk_b, spread(wk_b), w_in[:, 3 * WA + 2 * WB:nq],
                                   w_pad[:, nq:]], axis=-1).astype(bf16)
            W2 = HB * LANES
            spec_b = [(0, W2, bf16, DHB ** -0.5 * LOG2E, None),
                      (W2, WB, f32, 1.0, None), (W2 + WB, W2, bf16, 1.0, None),
                      (2 * W2 + WB, WB, f32, 1.0, None), (2 * W2 + WB, WB, bf16, 1.0, None),
                      (2 * W2 + 2 * WB, LANES, f32, 1.0, "logsig")]
            qbs, kb, kbs, vb, vbb, lf = _norm_proj(xp, norm_g[l, 1], w_b, spec_b, bias=b_pad)
            F, _ = _cumsum_rows(lf)
            qaug, kaug, vt = _fox_pack(qbs, kbs, vbb, F, HB)
            ob = _fox_prompt(qaug, kaug, vt, F, HB)
            res_p = [(oa, wo[:WA]), (ob, wo[WA:])]
            keep = min(BAND_CHUNKS * CHUNK, SEQ)
            outs["akp"].append(ka[SEQ - keep:].reshape(1, keep, HA, DHA))
            outs["avp"].append(va[SEQ - keep:].reshape(1, keep, HA, DHA))
            outs["bkp"].append(kb.reshape(1, SEQ, HB, WB // HB))
            outs["bvp"].append(vb.reshape(1, SEQ, HB, WB // HB))
            outs["blp"].append(lf[:, :HB].reshape(1, SEQ, HB))

            qa, ka, _, va, _, qb, kb, _, vb, _, lf = _norm_proj(xs, norm_g[l, 1], w_pad, spec, bias=b_pad)
            bias_s = _toeplitz_bias(ab_rel_bias[e], MS, PA + MS, PA).reshape(HA * MS, PA + MS)
            oa = _sample_attn(qa.reshape(BS, MS, WA), cache_a_k[e].reshape(BS, PA, WA), cache_a_v[e].reshape(BS, PA, WA),
                              HA, "rel", ka.reshape(BS, MS, WA), va.reshape(BS, MS, WA),
                              extra=(bias_s[:, :PA], bias_s[:, PA:]))
            lfn = lf[:, :HB].reshape(BS, MS, HB)
            lcat = jnp.concatenate([cache_b_logf[e].astype(f32), lfn], axis=1)
            LP = -(-(PB + MS) // LANES) * LANES
            lcat = jnp.pad(lcat.transpose(1, 0, 2).reshape(PB + MS, BS * HB), ((0, LP - PB - MS), (0, 0)))
            _, FTs = _cumsum_rows(lcat)
            FTs = FTs.reshape(BS, HB, LP)
            fq = FTs[:, :, PB:PB + MS].reshape(BS, HB * MS, 1)
            ob = _sample_attn(qb.reshape(BS, MS, WB), cache_b_k[e].reshape(BS, PB, WB), cache_b_v[e].reshape(BS, PB, WB),
                              HB, "fox", kb.reshape(BS, MS, WB), vb.reshape(BS, MS, WB),
                              extra=(fq, FTs[:, :, :PB], FTs[:, :, PB:PB + MS]))
            res_s = [(oa.reshape(BS * MS, WA), wo[:WA]), (ob.reshape(BS * MS, WB), wo[WA:])]
            outs["aks"].append(ka.reshape(BS, MS, HA, DHA))
            outs["avs"].append(va.reshape(BS, MS, HA, DHA))
            outs["bks"].append(kb.reshape(BS, MS, HB, WB // HB))
            outs["bvs"].append(vb.reshape(BS, MS, HB, WB // HB))
            outs["bls"].append(lfn)
        else:
            o = l // 2
            w_in = gdn_w_in[o]
            w_pad = jnp.concatenate([w_in, jnp.zeros((D, LANES - 2 * HG), f32)], axis=-1).astype(bf16)
            spec = [(0, 3 * WG, f32, 1.0, None), (3 * WG, WG, f32, 1.0, None), (4 * WG, LANES, f32, 1.0, None)]
            wo = gdn_w_o[o].astype(bf16)

            qg, kg, vg, z, ab, cvp = _gdn_proj(xp, norm_g[l, 1], w_pad, jnp.zeros((1, CONV - 1, 3 * WG), f32),
                                               gdn_conv_w[o], HG)
            y, Sp = _gdn_core_pre(qg, kg, vg, ab, z, jnp.zeros((1, HG, DKG, DKG), f32),
                                  gdn_a_log[o], gdn_dt_bias[o], gdn_norm_g[o], CHUNK, 4)
            res_p = [(y, wo)]
            outs["sgp"].append(Sp)
            outs["scp"].append(cvp)

            qkv, z, ab = _norm_proj(xs, norm_g[l, 1], w_pad, spec)
            y, Ss, cvs = _gdn_core(qkv, ab, z, state_gdn_conv[o].astype(f32), state_gdn[o].astype(f32),
                                   gdn_conv_w[o], gdn_a_log[o], gdn_dt_bias[o], gdn_norm_g[o], MS, 1)
            res_s = [(y, wo)]
            outs["sgs"].append(Ss)
            outs["scs"].append(cvs)

        mk, mkb, mv, mvb = _norm_proj(mem_prompt.reshape(MEM, D), mem_norm_g[l], wxkv[l],
                                      [(0, WX, f32, 1.0, None), (0, WX, bf16, 1.0, None),
                                       (WX, WX, f32, 1.0, None), (WX, WX, bf16, 1.0, None)])
        outs["mkp"].append(mk.reshape(1, MEM, HX, DHX))
        outs["mvp"].append(mv.reshape(1, MEM, HX, DHX))
        qspec = [(0, WX, bf16, DHX ** -0.5, None)]
        xp, ocp = _cross_prompt(xp, norm_g[l, 2], wxq[l], res_p, mkb, mvb, HX)
        xs, q = _norm_proj(xs, norm_g[l, 2], wxq[l], qspec, res=res_s)
        ocs = _cross_sample(q.reshape(BS, MS, WX), cache_mem_k, cache_mem_v, l)

        fin = final_norm_g if l == depth - 1 else None
        xp = _ffn(xp, norm_g[l, 3], wgate[l, 1], wup[l, 1], wdown[l, 1], fin, res=(ocp, wxo[l]))
        xs = _ffn(xs, norm_g[l, 3], wgate[l, 1], wup[l, 1], wdown[l, 1], fin, res=(ocs.reshape(BS * MS, WX), wxo[l]))

    st = lambda k: jnp.stack(outs[k])
    return (xp.reshape(BP, SEQ, D), xs.reshape(BS, MS, D),
            st("akp"), st("avp"), st("bkp"), st("bvp"), st("blp"),
            st("sgp"), st("scp"), st("mkp"), st("mvp"),
            st("aks"), st("avs"), st("bks"), st("bvs"), st("bls"),
            st("sgs"), st("scs"))
```

```python
import functools

import numpy as np
import jax
import jax.numpy as jnp
from jax import lax
from jax.experimental import pallas as pl
from jax.experimental.pallas import tpu as pltpu

f32 = jnp.float32
bf16 = jnp.bfloat16
EPS = 1e-6
NEG = -1e30

V7X_VMEM_BYTES = 64 * 1024 * 1024
V7X_LANES = 128
MIB = 1024 * 1024

CHUNK = 64
BAND_CHUNKS = 8
REL_CLIP = 256
CONV = 4
MEM = 256


def _params(sem, est_bytes):
    limit = int(min(V7X_VMEM_BYTES - 8 * MIB, max(32 * MIB, est_bytes + 8 * MIB)))
    return pltpu.CompilerParams(dimension_semantics=sem, vmem_limit_bytes=limit)


def _rms(x, g):
    ms = jnp.mean(x * x, axis=-1, keepdims=True)
    return x * lax.rsqrt(ms + EPS) * g


def _silu(x):
    h = 0.5 * x
    return h + h * jnp.tanh(h)


def _dot(a, b):
    return jnp.dot(a, b, preferred_element_type=f32)


def _dot_nt(a, b):
    return lax.dot_general(a, b, (((1,), (1,)), ((), ())), preferred_element_type=f32)


def _dot_hi(a, b):
    return jnp.dot(a, b, preferred_element_type=f32, precision=lax.Precision.HIGHEST)


def _split2(a):
    hi = a.astype(bf16)
    return hi, (a - hi.astype(f32)).astype(bf16)


def _split3(a):
    hi, rest = a.astype(bf16), a
    rest = rest - hi.astype(f32)
    mid = rest.astype(bf16)
    return hi, mid, (rest - mid.astype(f32)).astype(bf16)


def _ffn_body(*refs, nj, final, has_res):
    x_ref, g_ref, wg_ref, wu_ref, wd_ref = refs[:5]
    pos = 5
    if has_res:
        a_ref, wo_ref = refs[pos:pos + 2]
        pos += 2
    if final:
        gf_ref = refs[pos]
        pos += 1
    o_ref, h_scr, acc = refs[pos:]
    j = pl.program_id(1)

    @pl.when(j == 0)
    def _():
        x = x_ref[...]
        if has_res:
            x = x + _dot(a_ref[...], wo_ref[...])
        o_ref[...] = x
        h_scr[...] = _rms(x, g_ref[...]).astype(bf16)
        acc[...] = jnp.zeros_like(acc)

    h = h_scr[...]
    a = _silu(_dot(h, wg_ref[...])) * _dot(h, wu_ref[...])
    acc[...] += _dot(a.astype(bf16), wd_ref[...])

    @pl.when(j == nj - 1)
    def _():
        y = o_ref[...] + 0.5 * acc[...]
        if final:
            y = _rms(y, gf_ref[...])
        o_ref[...] = y


def _ffn(x, g, wg, wu, wd, final_g=None, res=None):
    T, D = x.shape
    FF = wg.shape[1]
    TM = min(1024, T)
    TF = min(1024, FF)
    nj = FF // TF
    final = final_g is not None
    in_specs = [pl.BlockSpec((TM, D), lambda i, j: (i, 0)),
                pl.BlockSpec((1, D), lambda i, j: (0, 0)),
                pl.BlockSpec((D, TF), lambda i, j: (0, j)),
                pl.BlockSpec((D, TF), lambda i, j: (0, j)),
                pl.BlockSpec((TF, D), lambda i, j: (j, 0))]
    args = [x, g.reshape(1, D), wg, wu, wd]
    est = 2 * (2 * TM * D * 4) + 2 * 3 * D * TF * 2 + TM * D * 6 + 3 * TM * TF * 4
    if res is not None:
        a, wo = res
        K = a.shape[1]
        in_specs += [pl.BlockSpec((TM, K), lambda i, j: (i, 0)), pl.BlockSpec((K, D), lambda i, j: (0, 0))]
        args += [a, wo]
        est += 2 * TM * K * 2 + 2 * K * D * 2
    if final:
        in_specs.append(pl.BlockSpec((1, D), lambda i, j: (0, 0)))
        args.append(final_g.reshape(1, D))
    return pl.pallas_call(
        functools.partial(_ffn_body, nj=nj, final=final, has_res=res is not None),
        grid=(T // TM, nj),
        in_specs=in_specs,
        out_specs=pl.BlockSpec((TM, D), lambda i, j: (i, 0)),
        out_shape=jax.ShapeDtypeStruct((T, D), f32),
        scratch_shapes=[pltpu.VMEM((TM, D), bf16), pltpu.VMEM((TM, D), f32)],
        compiler_params=_params(("parallel", "arbitrary"), est),
        name="ffn",
    )(*args)


def _log_sigmoid(x):
    return jnp.minimum(x, 0.0) - jnp.log(1.0 + jnp.exp(-jnp.abs(x)))


def _norm_proj_body(*refs, outs, has_bias, nres):
    x_ref, g_ref, w_ref = refs[:3]
    pos = 3
    b_ref = None
    if has_bias:
        b_ref = refs[pos]
        pos += 1
    x = x_ref[...]
    for p in range(nres):
        x = x + _dot(refs[pos][...], refs[pos + 1][...])
        pos += 2
    o_refs = refs[pos:]
    if nres:
        o_refs[0][...] = x
        o_refs = o_refs[1:]
    h = _rms(x, g_ref[...]).astype(bf16)
    cache = {}
    for o_ref, (off, n, dt, scale, act) in zip(o_refs, outs):
        if (off, n) not in cache:
            cache[(off, n)] = _dot(h, w_ref[:, off:off + n])
        r = cache[(off, n)]
        if act == "logsig":
            r = _log_sigmoid(r + b_ref[...])
        if scale != 1.0:
            r = r * scale
        o_ref[...] = r.astype(dt)


def _norm_proj(x, g, w, outs, bias=None, res=()):
    T, D = x.shape
    N = w.shape[1]
    TM = min(512, T)
    in_specs = [pl.BlockSpec((TM, D), lambda i: (i, 0)),
                pl.BlockSpec((1, D), lambda i: (0, 0)),
                pl.BlockSpec((D, N), lambda i: (0, 0))]
    args = [x, g.reshape(1, D), w]
    if bias is not None:
        in_specs.append(pl.BlockSpec((1, bias.shape[-1]), lambda i: (0, 0)))
        args.append(bias)
    est = 2 * TM * D * 4 + 2 * D * N * 2 + sum(2 * TM * n * 4 for (_, n, _, _, _) in outs) + TM * N * 4
    for a, wr in res:
        K = a.shape[1]
        in_specs += [pl.BlockSpec((TM, K), lambda i: (i, 0)), pl.BlockSpec((K, D), lambda i: (0, 0))]
        args += [a, wr]
        est += 2 * TM * K * 2 + 2 * K * D * 2
    out_specs = [pl.BlockSpec((TM, n), lambda i: (i, 0)) for (_, n, _, _, _) in outs]
    out_shape = [jax.ShapeDtypeStruct((T, n), dt) for (_, n, dt, _, _) in outs]
    if res:
        out_specs.insert(0, pl.BlockSpec((TM, D), lambda i: (i, 0)))
        out_shape.insert(0, jax.ShapeDtypeStruct((T, D), f32))
        est += 2 * TM * D * 4
    return pl.pallas_call(
        functools.partial(_norm_proj_body, outs=tuple(outs), has_bias=bias is not None, nres=len(res)),
        grid=(T // TM,),
        in_specs=in_specs,
        out_specs=out_specs,
        out_shape=out_shape,
        compiler_params=_params(("parallel",), est),
        name="norm_proj",
    )(*args)


def _cumsum_body(x_ref, f_ref, ft_ref, carry, *, nsub):
    @pl.when(pl.program_id(0) == 0)
    def _():
        carry[...] = jnp.zeros_like(carry)

    r = lax.broadcasted_iota(jnp.int32, (V7X_LANES, V7X_LANES), 0)
    c = lax.broadcasted_iota(jnp.int32, (V7X_LANES, V7X_LANES), 1)
    tri = (r >= c).astype(f32)
    run = carry[...]
    ncol = x_ref.shape[1] // V7X_LANES
    for sb in range(nsub):
        rows = slice(sb * V7X_LANES, (sb + 1) * V7X_LANES)
        blk = _dot_hi(tri, x_ref[rows, :]) + run
        f_ref[rows, :] = blk
        for cb in range(ncol):
            cols = slice(cb * V7X_LANES, (cb + 1) * V7X_LANES)
            ft_ref[cols, rows] = blk[:, cols].T
        run = blk[V7X_LANES - 1:V7X_LANES, :]
    carry[...] = run


def _cumsum_rows(x):
    L, N = x.shape
    nsub = 8 if L % (8 * V7X_LANES) == 0 else 1
    TB = nsub * V7X_LANES
    return pl.pallas_call(
        functools.partial(_cumsum_body, nsub=nsub),
        grid=(L // TB,),
        in_specs=[pl.BlockSpec((TB, N), lambda i: (i, 0))],
        out_specs=[pl.BlockSpec((TB, N), lambda i: (i, 0)), pl.BlockSpec((N, TB), lambda i: (0, i))],
        out_shape=[jax.ShapeDtypeStruct((L, N), f32), jax.ShapeDtypeStruct((N, L), f32)],
        scratch_shapes=[pltpu.VMEM((1, N), f32)],
        compiler_params=_params(("arbitrary",), 8 * TB * N * 4),
        name="cumsum_rows",
    )(x)


BAND_TQ = 256
BAND_NB = 3


def _band_body(q_ref, k0_ref, k1_ref, k2_ref, v0_ref, v1_ref, v2_ref, b_ref, o_ref, *, nheads):
    TQ = BAND_TQ
    lane = lax.broadcasted_iota(jnp.int32, (TQ, V7X_LANES), 1)
    lo = lane < 64
    krefs = (k0_ref, k1_ref, k2_ref)
    vrefs = (v0_ref, v1_ref, v2_ref)
    for hp in range(nheads // 2):
        cols = slice(hp * V7X_LANES, (hp + 1) * V7X_LANES)
        qp = q_ref[:, cols]
        kp = [kr[:, cols] for kr in krefs]
        vp = [vr[:, cols] for vr in vrefs]
        res = []
        for half in range(2):
            h = 2 * hp + half
            qm = jnp.where(lo if half == 0 else jnp.logical_not(lo), qp, jnp.zeros_like(qp))
            s = [_dot_nt(qm, kp[b]) + b_ref[h, :, b * TQ:(b + 1) * TQ] for b in range(BAND_NB)]
            m = jnp.max(s[0], axis=-1, keepdims=True)
            for b in range(1, BAND_NB):
                m = jnp.maximum(m, jnp.max(s[b], axis=-1, keepdims=True))
            p = [jnp.exp2(sb - m) for sb in s]
            l = p[0].sum(axis=-1, keepdims=True)
            o = _dot(p[0].astype(bf16), vp[0])
            for b in range(1, BAND_NB):
                l = l + p[b].sum(axis=-1, keepdims=True)
                o = o + _dot(p[b].astype(bf16), vp[b])
            res.append(o / l)
        o_ref[:, cols] = jnp.where(lo, res[0], res[1]).astype(bf16)


def _toeplitz_bias(table, nq, nk, off):
    H = table.shape[0]
    m = np.arange(nq + nk - 1)
    idx = np.clip(off + nq - 1 - m, -REL_CLIP, REL_CLIP) + REL_CLIP
    w = jnp.concatenate([table[:, idx].astype(f32), jnp.zeros((H, 1), f32)], axis=-1)
    skew = jnp.tile(w, (1, nq))[:, :nq * (nq + nk - 1)].reshape(H, nq, nq + nk - 1)
    return skew[:, :, nq - 1:]


def _band_bias(table):
    qc = np.arange(BAND_TQ)[:, None] // CHUNK
    ki = np.arange(BAND_NB * BAND_TQ)[None, :]
    kc = ki // CHUNK
    in_band = (kc >= qc) & (kc <= qc + BAND_CHUNKS)
    valid = np.stack([in_band & (ki // BAND_TQ >= BAND_NB - 1 - n) for n in range(BAND_NB)])
    bias = _toeplitz_bias(table, BAND_TQ, BAND_NB * BAND_TQ, BAND_CHUNKS * CHUNK) * LOG2E
    return jnp.where(valid[:, None], bias[None], NEG)


def _band_prompt(q, k, v, bias):
    L, W = q.shape
    TQ = BAND_TQ
    H = bias.shape[1]
    kspec = [pl.BlockSpec((TQ, W), functools.partial(lambda i, d: (jnp.maximum(i - d, 0), 0), d=d))
             for d in (2, 1, 0)]
    est = 2 * 7 * TQ * W * 2 + 2 * H * TQ * BAND_NB * TQ * 4 + 16 * TQ * TQ * 4
    return pl.pallas_call(
        functools.partial(_band_body, nheads=H),
        grid=(L // TQ,),
        in_specs=[pl.BlockSpec((TQ, W), lambda i: (i, 0))] + kspec + kspec
                 + [pl.BlockSpec((None, H, TQ, BAND_NB * TQ), lambda i: (jnp.minimum(i, BAND_NB - 1), 0, 0, 0))],
        out_specs=pl.BlockSpec((TQ, W), lambda i: (i, 0)),
        out_shape=jax.ShapeDtypeStruct((L, W), bf16),
        compiler_params=_params(("parallel",), est),
        name="band_prompt",
    )(q, k, k, k, v, v, v, bias)


FOX_T = 1024
FOX_VROWS = 80
FOX_QPIECE = 64
FOX_KPIECE = 67
LOG2E = 1.4426950408889634


def _fox_pack_body(q_ref, k_ref, v_ref, f_ref, e_ref, qa_ref, ka_ref, vt_ref, *, nheads, dh):
    T, W2 = q_ref.shape
    pieces = _split3((f_ref[...] - f_ref[0:1, :]) * LOG2E)
    aug = _dot(jnp.concatenate(pieces, axis=1), e_ref[...])
    lane = lax.broadcasted_iota(jnp.int32, (T, W2), 1) % V7X_LANES
    ones_q = jnp.where((lane >= FOX_KPIECE) & (lane < FOX_KPIECE + 3), 1.0, 0.0)
    ones_k = jnp.where((lane >= FOX_QPIECE) & (lane < FOX_QPIECE + 3), 1.0, 0.0)
    qa_ref[...] = (q_ref[...].astype(f32) + aug[:, :W2] + ones_q).astype(bf16)
    ka_ref[...] = (k_ref[...].astype(f32) + aug[:, W2:] + ones_k).astype(bf16)
    vT = v_ref[...].astype(f32).T
    pad = FOX_VROWS - dh
    ones_row = jnp.where(lax.broadcasted_iota(jnp.int32, (pad, T), 0) == 0, 1.0, 0.0).astype(bf16)
    for h in range(nheads):
        vt_ref[h, 0:dh, :] = vT[h * dh:(h + 1) * dh, :].astype(bf16)
        vt_ref[h, dh:FOX_VROWS, :] = ones_row


def _fox_pack(q, k, v, F, nheads):
    L, W2 = q.shape
    W = v.shape[1]
    dh = W // nheads
    T = FOX_T
    e = np.zeros((3 * V7X_LANES, 2 * W2), np.float32)
    for h in range(nheads):
        for t in range(3):
            e[t * V7X_LANES + h, h * V7X_LANES + FOX_QPIECE + t] = 1.0
            e[t * V7X_LANES + h, W2 + h * V7X_LANES + FOX_KPIECE + t] = -1.0
    est = 2 * (4 * T * W2 * 2 + T * W * 2 + nheads * FOX_VROWS * T * 2) + 6 * T * W2 * 4
    return pl.pallas_call(
        functools.partial(_fox_pack_body, nheads=nheads, dh=dh),
        grid=(L // T,),
        in_specs=[pl.BlockSpec((T, W2), lambda i: (i, 0)),
                  pl.BlockSpec((T, W2), lambda i: (i, 0)),
                  pl.BlockSpec((T, W), lambda i: (i, 0)),
                  pl.BlockSpec((T, V7X_LANES), lambda i: (i, 0)),
                  pl.BlockSpec((3 * V7X_LANES, 2 * W2), lambda i: (0, 0))],
        out_specs=[pl.BlockSpec((T, W2), lambda i: (i, 0)),
                   pl.BlockSpec((T, W2), lambda i: (i, 0)),
                   pl.BlockSpec((nheads, FOX_VROWS, T), lambda i: (0, 0, i))],
        out_shape=[jax.ShapeDtypeStruct((L, W2), bf16), jax.ShapeDtypeStruct((L, W2), bf16),
                   jax.ShapeDtypeStruct((nheads, FOX_VROWS, L), bf16)],
        compiler_params=_params(("parallel",), est),
        name="fox_pack",
    )(q, k, v, F, jnp.asarray(e, bf16))


def _fox_body(qi_ref, kj_ref, qa_ref, ka_ref, vt_ref, d_ref, o_ref, m_scr, acc_scr, *, nheads, dh):
    s_id = pl.program_id(0)
    i = qi_ref[s_id]
    j = kj_ref[s_id]
    T = FOX_T

    @pl.when(j == 0)
    def _():
        m_scr[...] = jnp.full_like(m_scr, NEG)
        acc_scr[...] = jnp.zeros_like(acc_scr)

    def step(masked):
        if masked:
            keep = lax.broadcasted_iota(jnp.int32, (T, T), 0) <= lax.broadcasted_iota(jnp.int32, (T, T), 1)
        def scores(h):
            cols = slice(h * V7X_LANES, (h + 1) * V7X_LANES)
            st = _dot_nt(ka_ref[:, cols], qa_ref[:, cols])
            return jnp.where(keep, st, NEG) if masked else st

        nxt = scores(0)
        for h in range(nheads):
            st = nxt
            if h + 1 < nheads:
                nxt = scores(h + 1)
            d = d_ref[s_id * nheads + h]
            m_prev = m_scr[h:h + 1, :]
            m_new = jnp.maximum(m_prev, jnp.max(st, axis=0, keepdims=True) + d)
            p = jnp.exp2(st - (m_new - d))
            alpha = jnp.exp2(m_prev - m_new)
            acc_scr[h] = acc_scr[h] * alpha + _dot(vt_ref[h], p.astype(bf16))
            m_scr[h:h + 1, :] = m_new

    @pl.when(j < i)
    def _():
        step(False)

    @pl.when(j == i)
    def _():
        step(True)
        for hp in range(nheads // 2):
            a0 = acc_scr[2 * hp]
            a1 = acc_scr[2 * hp + 1]
            o2 = jnp.concatenate([a0[0:dh, :] / a0[dh:dh + 1, :], a1[0:dh, :] / a1[dh:dh + 1, :]], axis=0)
            o_ref[:, hp * V7X_LANES:(hp + 1) * V7X_LANES] = o2.T.astype(bf16)


def _fox_prompt(qa, ka, vt, F, nheads):
    L, W2 = qa.shape
    T = FOX_T
    n = L // T
    dh = V7X_LANES // 2
    W = nheads * dh
    pairs = [(i, j) for i in range(n) for j in range(i + 1)]
    qi = np.array([p[0] for p in pairs], np.int32)
    kj = np.array([p[1] for p in pairs], np.int32)
    fs = F[::T, :nheads]
    d = ((fs[qi] - fs[kj]) * LOG2E).reshape(-1)
    grid_spec = pltpu.PrefetchScalarGridSpec(
        num_scalar_prefetch=2,
        grid=(len(pairs),),
        in_specs=[pl.BlockSpec((T, W2), lambda s, qi, kj: (qi[s], 0)),
                  pl.BlockSpec((T, W2), lambda s, qi, kj: (kj[s], 0)),
                  pl.BlockSpec((nheads, FOX_VROWS, T), lambda s, qi, kj: (0, 0, kj[s])),
                  pl.BlockSpec(memory_space=pltpu.SMEM)],
        out_specs=pl.BlockSpec((T, W), lambda s, qi, kj: (qi[s], 0)),
        scratch_shapes=[pltpu.VMEM((nheads, T), f32), pltpu.VMEM((nheads, FOX_VROWS, T), f32)],
    )
    est = 2 * (2 * T * W2 * 2 + nheads * FOX_VROWS * T * 2 + T * W * 2) + nheads * FOX_VROWS * T * 4 + 24 * T * T * 4
    return pl.pallas_call(
        functools.partial(_fox_body, nheads=nheads, dh=dh),
        grid_spec=grid_spec,
        out_shape=jax.ShapeDtypeStruct((L, W), bf16),
        compiler_params=_params(("arbitrary",), est),
        name="fox_prompt",
    )(jnp.asarray(qi), jnp.asarray(kj), qa, ka, vt, d)


def _cross_body(*refs, nres, nheads, scale):
    x_ref, g_ref, wq_ref = refs[:3]
    x = x_ref[...]
    for p in range(nres):
        x = x + _dot(refs[3 + 2 * p][...], refs[4 + 2 * p][...])
    k_ref, v_ref, xo_ref, o_ref = refs[3 + 2 * nres:]
    xo_ref[...] = x
    q = (_dot(_rms(x, g_ref[...]).astype(bf16), wq_ref[...]) * scale).astype(bf16)
    dh = q.shape[1] // nheads
    hs = range(nheads)
    s = [_dot_nt(q[:, h * dh:(h + 1) * dh], k_ref[:, h * dh:(h + 1) * dh]) for h in hs]
    m = [jnp.max(sh, axis=-1, keepdims=True) for sh in s]
    p = [jnp.exp(s[h] - m[h]) for h in hs]
    l = [ph.sum(axis=-1, keepdims=True) for ph in p]
    o = [_dot(p[h].astype(bf16), v_ref[:, h * dh:(h + 1) * dh]) / l[h] for h in hs]
    o_ref[...] = jnp.concatenate(o, axis=-1).astype(bf16)


def _cross_prompt(x, g, wq, res, mk, mv, nheads):
    T, D = x.shape
    W = wq.shape[1]
    S = mk.shape[0]
    TM = min(512, T)
    row = lambda width: pl.BlockSpec((TM, width), lambda i: (i, 0))
    whole = lambda r, c: pl.BlockSpec((r, c), lambda i: (0, 0))
    in_specs = [row(D), whole(1, D), whole(D, W)]
    args = [x, g.reshape(1, D), wq]
    est = 4 * TM * D * 4 + 2 * D * W * 2 + 4 * S * W * 2 + 2 * TM * W * 2 + 3 * TM * W * 4 + 8 * TM * S * 4
    for a, wr in res:
        K = a.shape[1]
        in_specs += [row(K), whole(K, D)]
        args += [a, wr]
        est += 2 * TM * K * 2 + 2 * K * D * 2
    in_specs += [whole(S, W), whole(S, W)]
    args += [mk, mv]
    return pl.pallas_call(
        functools.partial(_cross_body, nres=len(res), nheads=nheads, scale=(W // nheads) ** -0.5),
        grid=(T // TM,),
        in_specs=in_specs,
        out_specs=[row(D), row(W)],
        out_shape=[jax.ShapeDtypeStruct((T, D), f32), jax.ShapeDtypeStruct((T, W), bf16)],
        compiler_params=_params(("parallel",), est),
        name="cross_prompt",
    )(*args)


def _sample_attn_body(*refs, nheads, mode):
    q_ref, kc_ref, vc_ref, kn_ref, vn_ref = refs[:5]
    if mode == "rel":
        bc_ref, bn_ref, o_ref = refs[5:]
    else:
        fq_ref, fkc_ref, fkn_ref, o_ref = refs[5:]

    m_q, W = q_ref.shape
    dh = W // nheads
    R = nheads * m_q
    q = q_ref[...].astype(f32)
    qt = jnp.concatenate([q] * nheads, axis=0)
    hrow = lax.broadcasted_iota(jnp.int32, (R, W), 0) // m_q
    hlane = lax.broadcasted_iota(jnp.int32, (R, W), 1) // dh
    qbd = jnp.where(hrow == hlane, qt, 0.0).astype(bf16)

    def rows_of(f_ref):
        n = f_ref.shape[-1]
        return jnp.concatenate([jnp.broadcast_to(f_ref[h:h + 1, :], (m_q, n)) for h in range(nheads)], axis=0)

    sc = _dot_nt(qbd, kc_ref[...].astype(bf16))
    sn = _dot_nt(qbd, kn_ref[...].astype(bf16))
    if mode == "rel":
        sc = sc + bc_ref[...]
        sn = sn + bn_ref[...]
    else:
        sc = sc + (fq_ref[...] - rows_of(fkc_ref))
        sn = sn + (fq_ref[...] - rows_of(fkn_ref))
        qpos = lax.broadcasted_iota(jnp.int32, (R, m_q), 0) % m_q
        kpos = lax.broadcasted_iota(jnp.int32, (R, m_q), 1)
        sn = jnp.where(kpos <= qpos, sn, NEG)
    m = jnp.maximum(jnp.max(sc, axis=-1, keepdims=True), jnp.max(sn, axis=-1, keepdims=True))
    pc = jnp.exp(sc - m)
    pn = jnp.exp(sn - m)
    l = pc.sum(axis=-1, keepdims=True) + pn.sum(axis=-1, keepdims=True)
    o = _dot(pc.astype(bf16), vc_ref[...].astype(bf16)) + _dot(pn.astype(bf16), vn_ref[...].astype(bf16))
    o = o / l
    hl = lax.broadcasted_iota(jnp.int32, (m_q, W), 1) // dh
    out = jnp.zeros((m_q, W), f32)
    for h in range(nheads):
        out = out + jnp.where(hl == h, o[h * m_q:(h + 1) * m_q, :], 0.0)
    o_ref[...] = out.astype(bf16)


def _sample_attn(q, kc, vc, nheads, mode, kn, vn, extra):
    B, m_q, W = q.shape
    P = kc.shape[1]
    R = nheads * m_q
    per_b = lambda *shape: pl.BlockSpec((None,) + shape, lambda b: (b,) + (0,) * len(shape))
    shared = lambda *shape: pl.BlockSpec(shape, lambda b: (0,) * len(shape))
    in_specs = [per_b(m_q, W), per_b(P, W), per_b(P, W), per_b(m_q, W), per_b(m_q, W)]
    if mode == "rel":
        in_specs += [shared(R, P), shared(R, m_q)]
    else:
        in_specs += [per_b(R, 1), per_b(nheads, P), per_b(nheads, m_q)]
    est = 2 * 2 * P * W * 4 + 2 * P * W * 2 + 6 * R * P * 4 + 4 * R * W * 4
    return pl.pallas_call(
        functools.partial(_sample_attn_body, nheads=nheads, mode=mode),
        grid=(B,),
        in_specs=in_specs,
        out_specs=per_b(m_q, W),
        out_shape=jax.ShapeDtypeStruct((B, m_q, W), bf16),
        compiler_params=_params(("parallel",), est),
        name="sample_attn_" + mode,
    )(q, kc, vc, kn, vn, *extra)


def _cross_sample_body(q_ref, kc_ref, vc_ref, o_ref, *, nheads):
    m_q, W = q_ref.shape
    dh = W // nheads
    npc = dh // kc_ref.shape[1]
    P = kc_ref.shape[0] // (nheads * npc)
    hs = range(nheads)

    def head_rows(ref, h):
        pieces = [ref[pl.ds(h * npc + j, P, stride=nheads * npc), :] for j in range(npc)]
        return jnp.concatenate(pieces, axis=-1).astype(bf16)

    q = q_ref[...]
    sc = [_dot_nt(q[:, h * dh:(h + 1) * dh], head_rows(kc_ref, h)) for h in hs]
    m = [jnp.max(s, axis=-1, keepdims=True) for s in sc]
    pc = [jnp.exp(sc[h] - m[h]) for h in hs]
    l = [p.sum(axis=-1, keepdims=True) for p in pc]
    o = [_dot(pc[h].astype(bf16), head_rows(vc_ref, h)) / l[h] for h in hs]
    o_ref[...] = jnp.concatenate(o, axis=-1).astype(bf16)


def _cross_sample(q, kc_all, vc_all, layer):
    B, m_q, W = q.shape
    nl, _, P, nheads, dh = kc_all.shape
    nrows = P * nheads * (dh // V7X_LANES)
    kc_all = kc_all.reshape(nl, B, nrows, V7X_LANES)
    vc_all = vc_all.reshape(nl, B, nrows, V7X_LANES)
    cache = pl.BlockSpec((None, None, nrows, V7X_LANES), lambda b: (layer, b, 0, 0))
    qo = pl.BlockSpec((None, m_q, W), lambda b: (b, 0, 0))
    est = 2 * 2 * nrows * V7X_LANES * 4 + 4 * P * W * 2 + 8 * nheads * m_q * P * 4
    return pl.pallas_call(
        functools.partial(_cross_sample_body, nheads=nheads),
        grid=(B,),
        in_specs=[qo, cache, cache],
        out_specs=qo,
        out_shape=jax.ShapeDtypeStruct((B, m_q, W), bf16),
        compiler_params=_params(("parallel",), est),
        name="cross_sample",
    )(q, kc_all, vc_all)


GDN_HALO = 8


def _softplus(x):
    return jnp.maximum(x, 0.0) + jnp.log(1.0 + jnp.exp(-jnp.abs(x)))


def _gdn_conv_qkv(xbuf, R, cw_ref, nheads, dk):
    H0 = GDN_HALO
    tail = xbuf[H0 + R - (CONV - 1):H0 + R, :]

    def act_cols(c0):
        cols = slice(c0, c0 + dk)
        conv = xbuf[H0 - 3:H0 - 3 + R, cols] * cw_ref[0:1, cols]
        for jj in range(1, CONV):
            conv = conv + xbuf[H0 - 3 + jj:H0 - 3 + jj + R, cols] * cw_ref[jj:jj + 1, cols]
        return _silu(conv)

    def l2n(a):
        return a * lax.rsqrt(jnp.sum(a * a, axis=-1, keepdims=True) + EPS)

    W = nheads * dk
    qn = [l2n(act_cols(h * dk)) * (dk ** -0.5) for h in range(nheads)]
    kn = [l2n(act_cols(W + h * dk)) for h in range(nheads)]
    vn = [act_cols(2 * W + h * dk) for h in range(nheads)]
    return qn, kn, vn, tail


def _gdn_proj_body(x_ref, g_ref, w_ref, cs_ref, cw_ref, q_ref, k_ref, v_ref, z_ref, ab_ref, cf_ref, xbuf,
                   *, nsteps, nheads, dk):
    i = pl.program_id(0)
    W = nheads * dk
    TM = x_ref.shape[0]
    H0 = GDN_HALO

    @pl.when(i == 0)
    def _():
        xbuf[...] = jnp.zeros_like(xbuf)

    qn, kn, vn, tail = _gdn_conv_qkv(xbuf, TM, cw_ref, nheads, dk)
    q_ref[...] = jnp.concatenate(qn, axis=-1)
    k_ref[...] = jnp.concatenate(kn, axis=-1)
    v_ref[...] = jnp.concatenate(vn, axis=-1)

    xbuf[H0 - (CONV - 1):H0, :] = jnp.where(i == 0, cs_ref[...], tail)
    cf_ref[...] = tail

    h = _rms(x_ref[...], g_ref[...]).astype(bf16)
    xbuf[H0:H0 + TM, :] = _dot(h, w_ref[:, :3 * W])
    z_ref[...] = _dot(h, w_ref[:, 3 * W:4 * W])
    ab_ref[...] = _dot(h, w_ref[:, 4 * W:])


def _gdn_proj(x, g, w, conv_state, conv_w, nheads):
    T, D = x.shape
    N = w.shape[1]
    W = (N - V7X_LANES) // 4
    dk = W // nheads
    TM = min(512, T)
    nsteps = T // TM
    row = lambda width: pl.BlockSpec((TM, width), lambda i: (jnp.minimum(i, nsteps - 1), 0))
    late = lambda width: pl.BlockSpec((TM, width), lambda i: (jnp.maximum(i - 1, 0), 0))
    whole = lambda *shape: pl.BlockSpec(shape, lambda i: (0,) * len(shape))
    est = 2 * TM * D * 4 + 2 * D * N * 2 + 2 * TM * (4 * W + V7X_LANES) * 4 + (GDN_HALO + TM) * 3 * W * 4 + 4 * TM * 3 * W * 4
    q, k, v, z, ab, cf = pl.pallas_call(
        functools.partial(_gdn_proj_body, nsteps=nsteps, nheads=nheads, dk=dk),
        grid=(nsteps + 1,),
        in_specs=[row(D), whole(1, D), whole(D, N),
                  pl.BlockSpec((None, CONV - 1, 3 * W), lambda i: (0, 0, 0)), whole(CONV, 3 * W)],
        out_specs=[late(W), late(W), late(W), row(W), row(V7X_LANES),
                   pl.BlockSpec((None, CONV - 1, 3 * W), lambda i: (0, 0, 0))],
        out_shape=[jax.ShapeDtypeStruct((T, W), f32)] * 4
                  + [jax.ShapeDtypeStruct((T, V7X_LANES), f32), jax.ShapeDtypeStruct((1, CONV - 1, 3 * W), f32)],
        scratch_shapes=[pltpu.VMEM((GDN_HALO + TM, 3 * W), f32)],
        compiler_params=_params(("arbitrary",), est),
        name="gdn_proj",
    )(x, g.reshape(1, D), w, conv_state, conv_w)
    return q, k, v, z, ab, cf


def _gdn_body(*refs, C, NS, nsteps, nheads, dk, pre):
    if pre:
        q_ref, k_ref, v_ref, ab_ref, z_ref, s0_ref, alog_ref, dtb_ref, ng_ref, o_ref, sf_ref, s_scr = refs
    else:
        (x_ref, ab_ref, z_ref, cs_ref, s0_ref, cw_ref, alog_ref, dtb_ref, ng_ref,
         o_ref, sf_ref, cf_ref, xbuf, s_scr) = refs
    c = pl.program_id(1)
    R = NS * C
    G = V7X_LANES // C
    NG = nheads // G
    GW = G * dk
    hs = range(nheads)

    @pl.when(c == 0)
    def _():
        s_scr[...] = s0_ref[...]

    if pre:
        qn = [q_ref[:, h * dk:(h + 1) * dk] for h in hs]
        kn = [k_ref[:, h * dk:(h + 1) * dk] for h in hs]
        vn = [v_ref[:, h * dk:(h + 1) * dk] for h in hs]
    else:
        H0 = GDN_HALO

        @pl.when(c == 0)
        def _():
            xbuf[H0 - (CONV - 1):H0, :] = cs_ref[...]

        xbuf[H0:H0 + R, :] = x_ref[...]
        qn, kn, vn, tail = _gdn_conv_qkv(xbuf, R, cw_ref, nheads, dk)
        xbuf[H0 - (CONV - 1):H0, :] = tail

        @pl.when(c == nsteps - 1)
        def _():
            cf_ref[...] = tail

    ab = ab_ref[...]
    gfull = -jnp.exp(alog_ref[...]) * _softplus(ab + dtb_ref[...])
    bfull = 1.0 / (1.0 + jnp.exp(-ab))
    z = z_ref[...]
    ng = ng_ref[...]

    r_cc = lax.broadcasted_iota(jnp.int32, (C, C), 0)
    c_cc = lax.broadcasted_iota(jnp.int32, (C, C), 1)
    tri_f = (r_cc >= c_cc).astype(f32)
    ri = lax.broadcasted_iota(jnp.int32, (C, V7X_LANES), 0)
    jl = lax.broadcasted_iota(jnp.int32, (C, V7X_LANES), 1) % C
    lblk = lax.broadcasted_iota(jnp.int32, (C, V7X_LANES), 1) // C
    tri_g = ri >= jl
    strict_g = ri > jl
    eye_g = ri == jl
    r128 = lax.broadcasted_iota(jnp.int32, (V7X_LANES, V7X_LANES), 0)
    l128 = lax.broadcasted_iota(jnp.int32, (V7X_LANES, V7X_LANES), 1)
    same_blk = (r128 // C) == (l128 // C)
    wide_blk = (lax.broadcasted_iota(jnp.int32, (V7X_LANES, GW), 0) // C
                == lax.broadcasted_iota(jnp.int32, (V7X_LANES, GW), 1) // dk)

    def bd_sq(p):
        return jnp.where(same_blk, jnp.concatenate([p] * G, axis=0), jnp.zeros((), p.dtype))

    def bd_wide(xs):
        row = jnp.concatenate(xs, axis=1)
        return jnp.where(wide_blk, jnp.concatenate([row] * G, axis=0), 0.0).astype(bf16)

    def gmm3(xg, pg):
        xh, xl = _split2(xg)
        ph, plo = _split2(pg)
        bh = bd_sq(ph)
        return _dot(xh, bh) + (_dot(xh, bd_sq(plo)) + _dot(xl, bh))

    sls = [slice(s * C, (s + 1) * C) for s in range(NS)]
    gcs = [_dot_hi(tri_f, gfull[sl]) for sl in sls]
    egcs = [jnp.exp(gc) for gc in gcs]
    kdecs = [jnp.exp(gc[C - 1:C, :] - gc) for gc in gcs]
    gcTs = [jnp.concatenate([gc] * G, axis=0).T for gc in gcs]
    bcols_s = [[bfull[sl, nheads + h:nheads + h + 1] for h in hs] for sl in sls]
    ecols_s = [[egc[:, h:h + 1] for h in hs] for egc in egcs]
    kbs_s = [[kn[h][sls[s]] * bcols_s[s][h] for h in hs] for s in range(NS)]
    Ms, Aqks = [], []
    for s in range(NS):
        sl, gc, gcT, kbs = sls[s], gcs[s], gcTs[s], kbs_s[s]
        for grp in range(NG):
            heads = range(grp * G, (grp + 1) * G)
            gcol = jnp.broadcast_to(gc[:, grp * G:grp * G + 1], (C, V7X_LANES))
            grow = gcT[grp * G:grp * G + 1, :]
            for g in range(1, G):
                h = grp * G + g
                gcol = jnp.where(lblk == g, gc[:, h:h + 1], gcol)
                grow = jnp.where(lblk[0:1, :] == g, gcT[h:h + 1, :], grow)
            Lm = jnp.where(tri_g, jnp.exp(jnp.where(tri_g, gcol - grow, 0.0)), 0.0)
            kbd = bd_wide([kn[h][sl] for h in heads])
            kb_row = jnp.concatenate([kbs[h] for h in heads], axis=1).astype(bf16)
            q_row = jnp.concatenate([qn[h][sl] for h in heads], axis=1).astype(bf16)
            Ms.append(jnp.where(strict_g, _dot_nt(kb_row, kbd) * Lm, 0.0))
            Aqks.append(jnp.where(tri_g, _dot_nt(q_row, kbd) * Lm, 0.0).astype(bf16))
    Xs = [jnp.where(eye_g, 1.0, 0.0) - M for M in Ms]
    Pws = [gmm3(M, M) for M in Ms]
    e = 2
    while e < C:
        Xs = [X + gmm3(X, Pw) for X, Pw in zip(Xs, Pws)]
        e *= 2
        if e < C:
            Pws = [gmm3(Pw, Pw) for Pw in Pws]
    prep = []
    for s in range(NS):
        sl, bcols, ecols, kbs = sls[s], bcols_s[s], ecols_s[s], kbs_s[s]
        us, ws = [], []
        for grp in range(NG):
            heads = range(grp * G, (grp + 1) * G)
            Tm = Xs[s * NG + grp].astype(bf16)
            U = _dot(Tm, bd_wide([vn[h][sl] * bcols[h] for h in heads]))
            Wm = _dot(Tm, bd_wide([kbs[h] * ecols[h] for h in heads]))
            for g, h in enumerate(heads):
                us.append(U[:, g * dk:(g + 1) * dk])
                ws.append(Wm[:, g * dk:(g + 1) * dk].astype(bf16))
        qgs = [(qn[h][sl] * ecols[h]).astype(bf16) for h in hs]
        kgTs = [(kn[h][sl] * kdecs[s][:, h:h + 1]).T.astype(bf16) for h in hs]
        decs = [egcs[s][C - 1:C, h:h + 1] for h in hs]
        prep.append((us, ws, Aqks[s * NG:(s + 1) * NG], qgs, kgTs, decs))

    Ss = [s_scr[h] for h in hs]
    for s in range(NS):
        us, ws, Aqks, qgs, kgTs, decs = prep[s]
        Sbs = [S.astype(bf16) for S in Ss]
        vnews = [us[h] - _dot(ws[h], Sbs[h]) for h in hs]
        vnbs = [v.astype(bf16) for v in vnews]
        intra = [_dot(Aqks[grp], bd_wide([vnews[h] for h in range(grp * G, (grp + 1) * G)])) for grp in range(NG)]
        os_ = [_dot(qgs[h], Sbs[h]) + intra[h // G][:, (h % G) * dk:(h % G + 1) * dk] for h in hs]
        Ss = [Ss[h] * decs[h] + _dot(kgTs[h], vnbs[h]) for h in hs]
        zs = z[s * C:(s + 1) * C, :]
        ys = [(_rms(os_[h], ng) * _silu(zs[:, h * dk:(h + 1) * dk])).astype(bf16) for h in hs]
        o_ref[s * C:(s + 1) * C, :] = jnp.concatenate(ys, axis=-1)
    for h in hs:
        s_scr[h] = Ss[h]

    @pl.when(c == nsteps - 1)
    def _():
        sf_ref[...] = s_scr[...]


def _gdn_core_pre(q, k, v, ab, z, S0, a_log, dt_bias, norm_g, C, NS):
    T = q.shape[0]
    B, H, dk, dv = S0.shape
    W = H * dk
    R = NS * C
    nsteps = T // (B * R)
    pad = lambda vec: jnp.zeros((1, V7X_LANES), f32).at[0, :H].set(vec.astype(f32))
    row = lambda width: pl.BlockSpec((R, width), lambda b, c: (b * nsteps + c, 0))
    lane_row = pl.BlockSpec((1, V7X_LANES), lambda b, c: (0, 0))
    state = pl.BlockSpec((None, H, dk, dv), lambda b, c: (b, 0, 0, 0))
    est = 2 * R * (4 * W + V7X_LANES) * 4 + 3 * H * dk * dv * 4 * 2 + 16 * R * W * 4
    return pl.pallas_call(
        functools.partial(_gdn_body, C=C, NS=NS, nsteps=nsteps, nheads=H, dk=dk, pre=True),
        grid=(B, nsteps),
        in_specs=[row(W), row(W), row(W), row(V7X_LANES), row(W), state, lane_row, lane_row,
                  pl.BlockSpec((1, dv), lambda b, c: (0, 0))],
        out_specs=[row(W), state],
        out_shape=[jax.ShapeDtypeStruct((T, W), bf16), jax.ShapeDtypeStruct((B, H, dk, dv), f32)],
        scratch_shapes=[pltpu.VMEM((H, dk, dv), f32)],
        compiler_params=_params(("parallel", "arbitrary"), est),
        name="gdn_core_pre",
    )(q, k, v, ab, z, S0, pad(a_log), pad(dt_bias), norm_g.reshape(1, dv))


def _gdn_core(qkv, ab, z, conv_state, S0, conv_w, a_log, dt_bias, norm_g, C, NS):
    T = qkv.shape[0]
    B, H, dk, dv = S0.shape
    W = H * dk
    R = NS * C
    nsteps = T // (B * R)
    pad = lambda vec: jnp.zeros((1, V7X_LANES), f32).at[0, :H].set(vec.astype(f32))
    row = lambda width: pl.BlockSpec((R, width), lambda b, c: (b * nsteps + c, 0))
    est = 2 * R * (3 * W + W + V7X_LANES) * 4 + 3 * H * dk * dv * 4 * 2 + 12 * R * 3 * W * 4
    return pl.pallas_call(
        functools.partial(_gdn_body, C=C, NS=NS, nsteps=nsteps, nheads=H, dk=dk, pre=False),
        grid=(B, nsteps),
        in_specs=[row(3 * W), row(V7X_LANES), row(W),
                  pl.BlockSpec((None, CONV - 1, 3 * W), lambda b, c: (b, 0, 0)),
                  pl.BlockSpec((None, H, dk, dv), lambda b, c: (b, 0, 0, 0)),
                  pl.BlockSpec((CONV, 3 * W), lambda b, c: (0, 0)),
                  pl.BlockSpec((1, V7X_LANES), lambda b, c: (0, 0)),
                  pl.BlockSpec((1, V7X_LANES), lambda b, c: (0, 0)),
                  pl.BlockSpec((1, dv), lambda b, c: (0, 0))],
        out_specs=[row(W),
                   pl.BlockSpec((None, H, dk, dv), lambda b, c: (b, 0, 0, 0)),
                   pl.BlockSpec((None, CONV - 1, 3 * W), lambda b, c: (b, 0, 0))],
        out_shape=[jax.ShapeDtypeStruct((T, W), bf16),
                   jax.ShapeDtypeStruct((B, H, dk, dv), f32),
                   jax.ShapeDtypeStruct((B, CONV - 1, 3 * W), f32)],
        scratch_shapes=[pltpu.VMEM((GDN_HALO + R, 3 * W), f32), pltpu.VMEM((H, dk, dv), f32)],
        compiler_params=_params(("parallel", "arbitrary"), est),
        name="gdn_core",
    )(qkv, ab, z, conv_state, S0, conv_w, pad(a_log), pad(dt_bias), norm_g.reshape(1, dv))


def kernel(x_prompt, x_sample, mem_prompt, cache_a_k, cache_a_v, cache_b_k, cache_b_v, cache_b_logf, state_gdn, state_gdn_conv, cache_mem_k, cache_mem_v, norm_g, mem_norm_g, final_norm_g, ffn_w_gate, ffn_w_up, ffn_w_down, xa_w_q, xa_w_k, xa_w_v, xa_w_o, ab_w_in, ab_b_f, ab_rel_bias, ab_w_o, gdn_w_in, gdn_conv_w, gdn_a_log, gdn_dt_bias, gdn_norm_g, gdn_w_o):
    depth = norm_g.shape[0]
    BP, SEQ, D = x_prompt.shape
    BS, MS, _ = x_sample.shape
    assert BP == 1
    HA = HB = ab_b_f.shape[1]
    WA = WB = (ab_w_in.shape[2] - HB) // 6
    DHA = WA // HA
    HG = gdn_a_log.shape[1]
    WG = gdn_w_o.shape[1]
    DKG = WG // HG
    HX = cache_mem_k.shape[3]
    DHX = cache_mem_k.shape[4]
    WX = HX * DHX
    PA = cache_a_k.shape[2]
    PB = cache_b_k.shape[2]
    LANES = V7X_LANES

    xp = x_prompt.reshape(SEQ, D)
    xs = x_sample.reshape(BS * MS, D)

    wgate = ffn_w_gate.astype(bf16)
    wup = ffn_w_up.astype(bf16)
    wdown = ffn_w_down.astype(bf16)
    wxq = xa_w_q.astype(bf16)
    wxo = xa_w_o.astype(bf16)
    wxkv = jnp.concatenate([xa_w_k, xa_w_v], axis=-1).astype(bf16)

    outs = {k: [] for k in ("akp", "avp", "bkp", "bvp", "blp", "sgp", "scp", "mkp", "mvp",
                            "aks", "avs", "bks", "bvs", "bls", "sgs", "scs")}

    for l in range(depth):
        xp = _ffn(xp, norm_g[l, 0], wgate[l, 0], wup[l, 0], wdown[l, 0])
        xs = _ffn(xs, norm_g[l, 0], wgate[l, 0], wup[l, 0], wdown[l, 0])

        if l % 2 == 0:
            e = l // 2
            w_in = ab_w_in[e]
            nq = 3 * WA + 3 * WB
            w_pad = jnp.concatenate([w_in, jnp.zeros((D, LANES - HB), f32)], axis=-1).astype(bf16)
            b_pad = jnp.zeros((1, LANES), f32).at[0, :HB].set(ab_b_f[e].astype(f32))
            spec = [(0, WA, bf16, DHA ** -0.5, None),
                    (WA, WA, f32, 1.0, None), (WA, WA, bf16, 1.0, None),
                    (2 * WA, WA, f32, 1.0, None), (2 * WA, WA, bf16, 1.0, None),
                    (3 * WA, WB, bf16, (WB // HB) ** -0.5, None),
                    (3 * WA + WB, WB, f32, 1.0, None), (3 * WA + WB, WB, bf16, 1.0, None),
                    (3 * WA + 2 * WB, WB, f32, 1.0, None), (3 * WA + 2 * WB, WB, bf16, 1.0, None),
                    (nq, LANES, f32, 1.0, "logsig")]
            wo = ab_w_o[e].astype(bf16)

            spec_a = [(0, WA, bf16, DHA ** -0.5 * LOG2E, None)] + spec[1:5]
            qa, ka, kab, va, vab = _norm_proj(xp, norm_g[l, 1], w_pad[:, :3 * WA], spec_a)
            oa = _band_prompt(qa, kab, vab, _band_bias(ab_rel_bias[e]))
            DHB = WB // HB
            spread = lambda w: jnp.pad(w.reshape(D, HB, DHB), ((0, 0), (0, 0), (0, LANES - DHB))).reshape(D, HB * LANES)
            wq_b = w_in[:, 3 * WA:3 * WA + WB]
            wk_b = w_in[:, 3 * WA + WB:3 * WA + 2 * WB]
            w_b = jnp.concatenate([spread(wq_b), wk_b, spread(wk_b), w_in[:, 3 * WA + 2 * WB:nq],
                                   w_pad[:, nq:]], axis=-1).astype(bf16)
            W2 = HB * LANES
            spec_b = [(0, W2, bf16, DHB ** -0.5 * LOG2E, None),
                      (W2, WB, f32, 1.0, None), (W2 + WB, W2, bf16, 1.0, None),
                      (2 * W2 + WB, WB, f32, 1.0, None), (2 * W2 + WB, WB, bf16, 1.0, None),
                      (2 * W2 + 2 * WB, LANES, f32, 1.0, "logsig")]
            qbs, kb, kbs, vb, vbb, lf = _norm_proj(xp, norm_g[l, 1], w_b, spec_b, bias=b_pad)
            F, _ = _cumsum_rows(lf)
            qaug, kaug, vt = _fox_pack(qbs, kbs, vbb, F, HB)
            ob = _fox_prompt(qaug, kaug, vt, F, HB)
            res_p = [(oa, wo[:WA]), (ob, wo[WA:])]
            keep = min(BAND_CHUNKS * CHUNK, SEQ)
            outs["akp"].append(ka[SEQ - keep:].reshape(1, keep, HA, DHA))
            outs["avp"].append(va[SEQ - keep:].reshape(1, keep, HA, DHA))
            outs["bkp"].append(kb.reshape(1, SEQ, HB, WB // HB))
            outs["bvp"].append(vb.reshape(1, SEQ, HB, WB // HB))
            outs["blp"].append(lf[:, :HB].reshape(1, SEQ, HB))

            qa, ka, _, va, _, qb, kb, _, vb, _, lf = _norm_proj(xs, norm_g[l, 1], w_pad, spec, bias=b_pad)
            bias_s = _toeplitz_bias(ab_rel_bias[e], MS, PA + MS, PA).reshape(HA * MS, PA + MS)
            oa = _sample_attn(qa.reshape(BS, MS, WA), cache_a_k[e].reshape(BS, PA, WA), cache_a_v[e].reshape(BS, PA, WA),
                              HA, "rel", ka.reshape(BS, MS, WA), va.reshape(BS, MS, WA),
                              extra=(bias_s[:, :PA], bias_s[:, PA:]))
            lfn = lf[:, :HB].reshape(BS, MS, HB)
            lcat = jnp.concatenate([cache_b_logf[e].astype(f32), lfn], axis=1)
            LP = -(-(PB + MS) // LANES) * LANES
            lcat = jnp.pad(lcat.transpose(1, 0, 2).reshape(PB + MS, BS * HB), ((0, LP - PB - MS), (0, 0)))
            _, FTs = _cumsum_rows(lcat)
            FTs = FTs.reshape(BS, HB, LP)
            fq = FTs[:, :, PB:PB + MS].reshape(BS, HB * MS, 1)
            ob = _sample_attn(qb.reshape(BS, MS, WB), cache_b_k[e].reshape(BS, PB, WB), cache_b_v[e].reshape(BS, PB, WB),
                              HB, "fox", kb.reshape(BS, MS, WB), vb.reshape(BS, MS, WB),
                              extra=(fq, FTs[:, :, :PB], FTs[:, :, PB:PB + MS]))
            res_s = [(oa.reshape(BS * MS, WA), wo[:WA]), (ob.reshape(BS * MS, WB), wo[WA:])]
            outs["aks"].append(ka.reshape(BS, MS, HA, DHA))
            outs["avs"].append(va.reshape(BS, MS, HA, DHA))
            outs["bks"].append(kb.reshape(BS, MS, HB, WB // HB))
            outs["bvs"].append(vb.reshape(BS, MS, HB, WB // HB))
            outs["bls"].append(lfn)
        else:
            o = l // 2
            w_in = gdn_w_in[o]
            w_pad = jnp.concatenate([w_in, jnp.zeros((D, LANES - 2 * HG), f32)], axis=-1).astype(bf16)
            spec = [(0, 3 * WG, f32, 1.0, None), (3 * WG, WG, f32, 1.0, None), (4 * WG, LANES, f32, 1.0, None)]
            wo = gdn_w_o[o].astype(bf16)

            qg, kg, vg, z, ab, cvp = _gdn_proj(xp, norm_g[l, 1], w_pad, jnp.zeros((1, CONV - 1, 3 * WG), f32),
                                               gdn_conv_w[o], HG)
            y, Sp = _gdn_core_pre(qg, kg, vg, ab, z, jnp.zeros((1, HG, DKG, DKG), f32),
                                  gdn_a_log[o], gdn_dt_bias[o], gdn_norm_g[o], CHUNK, 4)
            res_p = [(y, wo)]
            outs["sgp"].append(Sp)
            outs["scp"].append(cvp)

            qkv, z, ab = _norm_proj(xs, norm_g[l, 1], w_pad, spec)
            y, Ss, cvs = _gdn_core(qkv, ab, z, state_gdn_conv[o].astype(f32), state_gdn[o].astype(f32),
                                   gdn_conv_w[o], gdn_a_log[o], gdn_dt_bias[o], gdn_norm_g[o], MS, 1)
            res_s = [(y, wo)]
            outs["sgs"].append(Ss)
            outs["scs"].append(cvs)

        mk, mkb, mv, mvb = _norm_proj(mem_prompt.reshape(MEM, D), mem_norm_g[l], wxkv[l],
                                      [(0, WX, f32, 1.0, None), (0, WX, bf16, 1.0, None),
                                       (WX, WX, f32, 1.0, None), (WX, WX, bf16, 1.0, None)])
        outs["mkp"].append(mk.reshape(1, MEM, HX, DHX))
        outs["mvp"].append(mv.reshape(1, MEM, HX, DHX))
        qspec = [(0, WX, bf16, DHX ** -0.5, None)]
        xp, ocp = _cross_prompt(xp, norm_g[l, 2], wxq[l], res_p, mkb, mvb, HX)
        xs, q = _norm_proj(xs, norm_g[l, 2], wxq[l], qspec, res=res_s)
        ocs = _cross_sample(q.reshape(BS, MS, WX), cache_mem_k, cache_mem_v, l)

        fin = final_norm_g if l == depth - 1 else None
        xp = _ffn(xp, norm_g[l, 3], wgate[l, 1], wup[l, 1], wdown[l, 1], fin, res=(ocp, wxo[l]))
        xs = _ffn(xs, norm_g[l, 3], wgate[l, 1], wup[l, 1], wdown[l, 1], fin, res=(ocs.reshape(BS * MS, WX), wxo[l]))

    st = lambda k: jnp.stack(outs[k])
    return (xp.reshape(BP, SEQ, D), xs.reshape(BS, MS, D),
            st("akp"), st("avp"), st("bkp"), st("bvp"), st("blp"),
            st("sgp"), st("scp"), st("mkp"), st("mvp"),
            st("aks"), st("avs"), st("bks"), st("bvs"), st("bls"),
            st("sgs"), st("scs"))
```

```python
import functools

import numpy as np
import jax
import jax.numpy as jnp
from jax import lax
from jax.experimental import pallas as pl
from jax.experimental.pallas import tpu as pltpu

f32 = jnp.float32
bf16 = jnp.bfloat16
EPS = 1e-6
NEG = -1e30

V7X_VMEM_BYTES = 64 * 1024 * 1024
V7X_LANES = 128
MIB = 1024 * 1024

CHUNK = 64
BAND_CHUNKS = 8
REL_CLIP = 256
CONV = 4
MEM = 256


def _params(sem, est_bytes):
    limit = int(min(V7X_VMEM_BYTES - 8 * MIB, max(32 * MIB, est_bytes + 8 * MIB)))
    return pltpu.CompilerParams(dimension_semantics=sem, vmem_limit_bytes=limit)


def _rms(x, g):
    ms = jnp.mean(x * x, axis=-1, keepdims=True)
    return x * lax.rsqrt(ms + EPS) * g


def _silu(x):
    h = 0.5 * x
    return h + h * jnp.tanh(h)


def _dot(a, b):
    return jnp.dot(a, b, preferred_element_type=f32)


def _dot_nt(a, b):
    return lax.dot_general(a, b, (((1,), (1,)), ((), ())), preferred_element_type=f32)


def _dot_hi(a, b):
    return jnp.dot(a, b, preferred_element_type=f32, precision=lax.Precision.HIGHEST)


def _split2(a):
    hi = a.astype(bf16)
    return hi, (a - hi.astype(f32)).astype(bf16)


def _split3(a):
    hi, rest = a.astype(bf16), a
    rest = rest - hi.astype(f32)
    mid = rest.astype(bf16)
    return hi, mid, (rest - mid.astype(f32)).astype(bf16)


def _ffn_body(*refs, nj, final, has_res):
    x_ref, g_ref, wg_ref, wu_ref, wd_ref = refs[:5]
    pos = 5
    if has_res:
        a_ref, wo_ref = refs[pos:pos + 2]
        pos += 2
    if final:
        gf_ref = refs[pos]
        pos += 1
    o_ref, h_scr, acc = refs[pos:]
    j = pl.program_id(1)

    @pl.when(j == 0)
    def _():
        x = x_ref[...]
        if has_res:
            x = x + _dot(a_ref[...], wo_ref[...])
        o_ref[...] = x
        h_scr[...] = _rms(x, g_ref[...]).astype(bf16)
        acc[...] = jnp.zeros_like(acc)

    h = h_scr[...]
    a = _silu(_dot(h, wg_ref[...])) * _dot(h, wu_ref[...])
    acc[...] += _dot(a.astype(bf16), wd_ref[...])

    @pl.when(j == nj - 1)
    def _():
        y = o_ref[...] + 0.5 * acc[...]
        if final:
            y = _rms(y, gf_ref[...])
        o_ref[...] = y


def _ffn(x, g, wg, wu, wd, final_g=None, res=None):
    T, D = x.shape
    FF = wg.shape[1]
    TM = min(1024, T)
    TF = min(1024, FF)
    nj = FF // TF
    final = final_g is not None
    in_specs = [pl.BlockSpec((TM, D), lambda i, j: (i, 0)),
                pl.BlockSpec((1, D), lambda i, j: (0, 0)),
                pl.BlockSpec((D, TF), lambda i, j: (0, j)),
                pl.BlockSpec((D, TF), lambda i, j: (0, j)),
                pl.BlockSpec((TF, D), lambda i, j: (j, 0))]
    args = [x, g.reshape(1, D), wg, wu, wd]
    est = 2 * (2 * TM * D * 4) + 2 * 3 * D * TF * 2 + TM * D * 6 + 3 * TM * TF * 4
    if res is not None:
        a, wo = res
        K = a.shape[1]
        in_specs += [pl.BlockSpec((TM, K), lambda i, j: (i, 0)), pl.BlockSpec((K, D), lambda i, j: (0, 0))]
        args += [a, wo]
        est += 2 * TM * K * 2 + 2 * K * D * 2
    if final:
        in_specs.append(pl.BlockSpec((1, D), lambda i, j: (0, 0)))
        args.append(final_g.reshape(1, D))
    return pl.pallas_call(
        functools.partial(_ffn_body, nj=nj, final=final, has_res=res is not None),
        grid=(T // TM, nj),
        in_specs=in_specs,
        out_specs=pl.BlockSpec((TM, D), lambda i, j: (i, 0)),
        out_shape=jax.ShapeDtypeStruct((T, D), f32),
        scratch_shapes=[pltpu.VMEM((TM, D), bf16), pltpu.VMEM((TM, D), f32)],
        compiler_params=_params(("parallel", "arbitrary"), est),
        name="ffn",
    )(*args)


def _log_sigmoid(x):
    return jnp.minimum(x, 0.0) - jnp.log(1.0 + jnp.exp(-jnp.abs(x)))


def _norm_proj_body(*refs, outs, has_bias, nres):
    x_ref, g_ref, w_ref = refs[:3]
    pos = 3
    b_ref = None
    if has_bias:
        b_ref = refs[pos]
        pos += 1
    x = x_ref[...]
    for p in range(nres):
        x = x + _dot(refs[pos][...], refs[pos + 1][...])
        pos += 2
    o_refs = refs[pos:]
    if nres:
        o_refs[0][...] = x
        o_refs = o_refs[1:]
    h = _rms(x, g_ref[...]).astype(bf16)
    cache = {}
    for o_ref, (off, n, dt, scale, act) in zip(o_refs, outs):
        if (off, n) not in cache:
            cache[(off, n)] = _dot(h, w_ref[:, off:off + n])
        r = cache[(off, n)]
        if act == "logsig":
            r = _log_sigmoid(r + b_ref[...])
        if scale != 1.0:
            r = r * scale
        o_ref[...] = r.astype(dt)


def _norm_proj(x, g, w, outs, bias=None, res=()):
    T, D = x.shape
    N = w.shape[1]
    TM = min(512, T)
    in_specs = [pl.BlockSpec((TM, D), lambda i: (i, 0)),
                pl.BlockSpec((1, D), lambda i: (0, 0)),
                pl.BlockSpec((D, N), lambda i: (0, 0))]
    args = [x, g.reshape(1, D), w]
    if bias is not None:
        in_specs.append(pl.BlockSpec((1, bias.shape[-1]), lambda i: (0, 0)))
        args.append(bias)
    est = 2 * TM * D * 4 + 2 * D * N * 2 + sum(2 * TM * n * 4 for (_, n, _, _, _) in outs) + TM * N * 4
    for a, wr in res:
        K = a.shape[1]
        in_specs += [pl.BlockSpec((TM, K), lambda i: (i, 0)), pl.BlockSpec((K, D), lambda i: (0, 0))]
        args += [a, wr]
        est += 2 * TM * K * 2 + 2 * K * D * 2
    out_specs = [pl.BlockSpec((TM, n), lambda i: (i, 0)) for (_, n, _, _, _) in outs]
    out_shape = [jax.ShapeDtypeStruct((T, n), dt) for (_, n, dt, _, _) in outs]
    if res:
        out_specs.insert(0, pl.BlockSpec((TM, D), lambda i: (i, 0)))
        out_shape.insert(0, jax.ShapeDtypeStruct((T, D), f32))
        est += 2 * TM * D * 4
    return pl.pallas_call(
        functools.partial(_norm_proj_body, outs=tuple(outs), has_bias=bias is not None, nres=len(res)),
        grid=(T // TM,),
        in_specs=in_specs,
        out_specs=out_specs,
        out_shape=out_shape,
        compiler_params=_params(("parallel",), est),
        name="norm_proj",
    )(*args)


def _cumsum_body(x_ref, f_ref, ft_ref, carry, *, nsub):
    @pl.when(pl.program_id(0) == 0)
    def _():
        carry[...] = jnp.zeros_like(carry)

    r = lax.broadcasted_iota(jnp.int32, (V7X_LANES, V7X_LANES), 0)
    c = lax.broadcasted_iota(jnp.int32, (V7X_LANES, V7X_LANES), 1)
    tri = (r >= c).astype(f32)
    run = carry[...]
    ncol = x_ref.shape[1] // V7X_LANES
    for sb in range(nsub):
        rows = slice(sb * V7X_LANES, (sb + 1) * V7X_LANES)
        blk = _dot_hi(tri, x_ref[rows, :]) + run
        f_ref[rows, :] = blk
        for cb in range(ncol):
            cols = slice(cb * V7X_LANES, (cb + 1) * V7X_LANES)
            ft_ref[cols, rows] = blk[:, cols].T
        run = blk[V7X_LANES - 1:V7X_LANES, :]
    carry[...] = run


def _cumsum_rows(x):
    L, N = x.shape
    nsub = 8 if L % (8 * V7X_LANES) == 0 else 1
    TB = nsub * V7X_LANES
    return pl.pallas_call(
        functools.partial(_cumsum_body, nsub=nsub),
        grid=(L // TB,),
        in_specs=[pl.BlockSpec((TB, N), lambda i: (i, 0))],
        out_specs=[pl.BlockSpec((TB, N), lambda i: (i, 0)), pl.BlockSpec((N, TB), lambda i: (0, i))],
        out_shape=[jax.ShapeDtypeStruct((L, N), f32), jax.ShapeDtypeStruct((N, L), f32)],
        scratch_shapes=[pltpu.VMEM((1, N), f32)],
        compiler_params=_params(("arbitrary",), 8 * TB * N * 4),
        name="cumsum_rows",
    )(x)


BAND_TQ = 256
BAND_NB = 3


def _band_body(q_ref, k0_ref, k1_ref, k2_ref, v0_ref, v1_ref, v2_ref, b_ref, o_ref, *, nheads):
    TQ = BAND_TQ
    lane = lax.broadcasted_iota(jnp.int32, (TQ, V7X_LANES), 1)
    lo = lane < 64
    krefs = (k0_ref, k1_ref, k2_ref)
    vrefs = (v0_ref, v1_ref, v2_ref)
    for hp in range(nheads // 2):
        cols = slice(hp * V7X_LANES, (hp + 1) * V7X_LANES)
        qp = q_ref[:, cols]
        kp = [kr[:, cols] for kr in krefs]
        vp = [vr[:, cols] for vr in vrefs]
        res = []
        for half in range(2):
            h = 2 * hp + half
            qm = jnp.where(lo if half == 0 else jnp.logical_not(lo), qp, jnp.zeros_like(qp))
            s = [_dot_nt(qm, kp[b]) + b_ref[h, :, b * TQ:(b + 1) * TQ] for b in range(BAND_NB)]
            m = jnp.max(s[0], axis=-1, keepdims=True)
            for b in range(1, BAND_NB):
                m = jnp.maximum(m, jnp.max(s[b], axis=-1, keepdims=True))
            p = [jnp.exp2(sb - m) for sb in s]
            l = p[0].sum(axis=-1, keepdims=True)
            o = _dot(p[0].astype(bf16), vp[0])
            for b in range(1, BAND_NB):
                l = l + p[b].sum(axis=-1, keepdims=True)
                o = o + _dot(p[b].astype(bf16), vp[b])
            res.append(o / l)
        o_ref[:, cols] = jnp.where(lo, res[0], res[1]).astype(bf16)


def _toeplitz_bias(table, nq, nk, off):
    H = table.shape[0]
    m = np.arange(nq + nk - 1)
    idx = np.clip(off + nq - 1 - m, -REL_CLIP, REL_CLIP) + REL_CLIP
    w = jnp.concatenate([table[:, idx].astype(f32), jnp.zeros((H, 1), f32)], axis=-1)
    skew = jnp.tile(w, (1, nq))[:, :nq * (nq + nk - 1)].reshape(H, nq, nq + nk - 1)
    return skew[:, :, nq - 1:]


def _band_bias(table):
    qc = np.arange(BAND_TQ)[:, None] // CHUNK
    ki = np.arange(BAND_NB * BAND_TQ)[None, :]
    kc = ki // CHUNK
    in_band = (kc >= qc) & (kc <= qc + BAND_CHUNKS)
    valid = np.stack([in_band & (ki // BAND_TQ >= BAND_NB - 1 - n) for n in range(BAND_NB)])
    bias = _toeplitz_bias(table, BAND_TQ, BAND_NB * BAND_TQ, BAND_CHUNKS * CHUNK) * LOG2E
    return jnp.where(valid[:, None], bias[None], NEG)


def _band_prompt(q, k, v, bias):
    L, W = q.shape
    TQ = BAND_TQ
    H = bias.shape[1]
    kspec = [pl.BlockSpec((TQ, W), functools.partial(lambda i, d: (jnp.maximum(i - d, 0), 0), d=d))
             for d in (2, 1, 0)]
    est = 2 * 7 * TQ * W * 2 + 2 * H * TQ * BAND_NB * TQ * 4 + 16 * TQ * TQ * 4
    return pl.pallas_call(
        functools.partial(_band_body, nheads=H),
        grid=(L // TQ,),
        in_specs=[pl.BlockSpec((TQ, W), lambda i: (i, 0))] + kspec + kspec
                 + [pl.BlockSpec((None, H, TQ, BAND_NB * TQ), lambda i: (jnp.minimum(i, BAND_NB - 1), 0, 0, 0))],
        out_specs=pl.BlockSpec((TQ, W), lambda i: (i, 0)),
        out_shape=jax.ShapeDtypeStruct((L, W), bf16),
        compiler_params=_params(("parallel",), est),
        name="band_prompt",
    )(q, k, k, k, v, v, v, bias)


FOX_T = 1024
FOX_VROWS = 80
FOX_QPIECE = 64
FOX_KPIECE = 67
LOG2E = 1.4426950408889634


def _fox_pack_body(q_ref, k_ref, v_ref, f_ref, e_ref, qa_ref, ka_ref, vt_ref, *, nheads, dh):
    T, W2 = q_ref.shape
    pieces = _split3((f_ref[...] - f_ref[0:1, :]) * LOG2E)
    aug = _dot(jnp.concatenate(pieces, axis=1), e_ref[...])
    lane = lax.broadcasted_iota(jnp.int32, (T, W2), 1) % V7X_LANES
    ones_q = jnp.where((lane >= FOX_KPIECE) & (lane < FOX_KPIECE + 3), 1.0, 0.0)
    ones_k = jnp.where((lane >= FOX_QPIECE) & (lane < FOX_QPIECE + 3), 1.0, 0.0)
    qa_ref[...] = (q_ref[...].astype(f32) + aug[:, :W2] + ones_q).astype(bf16)
    ka_ref[...] = (k_ref[...].astype(f32) + aug[:, W2:] + ones_k).astype(bf16)
    vT = v_ref[...].astype(f32).T
    pad = FOX_VROWS - dh
    ones_row = jnp.where(lax.broadcasted_iota(jnp.int32, (pad, T), 0) == 0, 1.0, 0.0).astype(bf16)
    for h in range(nheads):
        vt_ref[h, 0:dh, :] = vT[h * dh:(h + 1) * dh, :].astype(bf16)
        vt_ref[h, dh:FOX_VROWS, :] = ones_row


def _fox_pack(q, k, v, F, nheads):
    L, W2 = q.shape
    W = v.shape[1]
    dh = W // nheads
    T = FOX_T
    e = np.zeros((3 * V7X_LANES, 2 * W2), np.float32)
    for h in range(nheads):
        for t in range(3):
            e[t * V7X_LANES + h, h * V7X_LANES + FOX_QPIECE + t] = 1.0
            e[t * V7X_LANES + h, W2 + h * V7X_LANES + FOX_KPIECE + t] = -1.0
    est = 2 * (4 * T * W2 * 2 + T * W * 2 + nheads * FOX_VROWS * T * 2) + 6 * T * W2 * 4
    return pl.pallas_call(
        functools.partial(_fox_pack_body, nheads=nheads, dh=dh),
        grid=(L // T,),
        in_specs=[pl.BlockSpec((T, W2), lambda i: (i, 0)),
                  pl.BlockSpec((T, W2), lambda i: (i, 0)),
                  pl.BlockSpec((T, W), lambda i: (i, 0)),
                  pl.BlockSpec((T, V7X_LANES), lambda i: (i, 0)),
                  pl.BlockSpec((3 * V7X_LANES, 2 * W2), lambda i: (0, 0))],
        out_specs=[pl.BlockSpec((T, W2), lambda i: (i, 0)),
                   pl.BlockSpec((T, W2), lambda i: (i, 0)),
                   pl.BlockSpec((nheads, FOX_VROWS, T), lambda i: (0, 0, i))],
        out_shape=[jax.ShapeDtypeStruct((L, W2), bf16), jax.ShapeDtypeStruct((L, W2), bf16),
                   jax.ShapeDtypeStruct((nheads, FOX_VROWS, L), bf16)],
        compiler_params=_params(("parallel",), est),
        name="fox_pack",
    )(q, k, v, F, jnp.asarray(e, bf16))


def _fox_body(qi_ref, kj_ref, qa_ref, ka_ref, vt_ref, d_ref, o_ref, m_scr, acc_scr, *, nheads, dh):
    s_id = pl.program_id(0)
    i = qi_ref[s_id]
    j = kj_ref[s_id]
    T = FOX_T

    @pl.when(j == 0)
    def _():
        m_scr[...] = jnp.full_like(m_scr, NEG)
        acc_scr[...] = jnp.zeros_like(acc_scr)

    def step(masked):
        if masked:
            keep = lax.broadcasted_iota(jnp.int32, (T, T), 0) <= lax.broadcasted_iota(jnp.int32, (T, T), 1)
        def scores(h):
            cols = slice(h * V7X_LANES, (h + 1) * V7X_LANES)
            st = _dot_nt(ka_ref[:, cols], qa_ref[:, cols])
            return jnp.where(keep, st, NEG) if masked else st

        nxt = scores(0)
        for h in range(nheads):
            st = nxt
            if h + 1 < nheads:
                nxt = scores(h + 1)
            d = d_ref[s_id * nheads + h]
            m_prev = m_scr[h:h + 1, :]
            m_new = jnp.maximum(m_prev, jnp.max(st, axis=0, keepdims=True) + d)
            p = jnp.exp2(st - (m_new - d))
            alpha = jnp.exp2(m_prev - m_new)
            acc_scr[h] = acc_scr[h] * alpha + _dot(vt_ref[h], p.astype(bf16))
            m_scr[h:h + 1, :] = m_new

    @pl.when(j < i)
    def _():
        step(False)

    @pl.when(j == i)
    def _():
        step(True)
        for hp in range(nheads // 2):
            a0 = acc_scr[2 * hp]
            a1 = acc_scr[2 * hp + 1]
            o2 = jnp.concatenate([a0[0:dh, :] / a0[dh:dh + 1, :], a1[0:dh, :] / a1[dh:dh + 1, :]], axis=0)
            o_ref[:, hp * V7X_LANES:(hp + 1) * V7X_LANES] = o2.T.astype(bf16)


def _fox_prompt(qa, ka, vt, F, nheads):
    L, W2 = qa.shape
    T = FOX_T
    n = L // T
    dh = V7X_LANES // 2
    W = nheads * dh
    pairs = [(i, j) for i in range(n) for j in range(i + 1)]
    qi = np.array([p[0] for p in pairs], np.int32)
    kj = np.array([p[1] for p in pairs], np.int32)
    fs = F[::T, :nheads]
    d = ((fs[qi] - fs[kj]) * LOG2E).reshape(-1)
    grid_spec = pltpu.PrefetchScalarGridSpec(
        num_scalar_prefetch=2,
        grid=(len(pairs),),
        in_specs=[pl.BlockSpec((T, W2), lambda s, qi, kj: (qi[s], 0)),
                  pl.BlockSpec((T, W2), lambda s, qi, kj: (kj[s], 0)),
                  pl.BlockSpec((nheads, FOX_VROWS, T), lambda s, qi, kj: (0, 0, kj[s])),
                  pl.BlockSpec(memory_space=pltpu.SMEM)],
        out_specs=pl.BlockSpec((T, W), lambda s, qi, kj: (qi[s], 0)),
        scratch_shapes=[pltpu.VMEM((nheads, T), f32), pltpu.VMEM((nheads, FOX_VROWS, T), f32)],
    )
    est = 2 * (2 * T * W2 * 2 + nheads * FOX_VROWS * T * 2 + T * W * 2) + nheads * FOX_VROWS * T * 4 + 24 * T * T * 4
    return pl.pallas_call(
        functools.partial(_fox_body, nheads=nheads, dh=dh),
        grid_spec=grid_spec,
        out_shape=jax.ShapeDtypeStruct((L, W), bf16),
        compiler_params=_params(("arbitrary",), est),
        name="fox_prompt",
    )(jnp.asarray(qi), jnp.asarray(kj), qa, ka, vt, d)


def _cross_body(*refs, nres, nheads, scale):
    x_ref, g_ref, wq_ref = refs[:3]
    x = x_ref[...]
    for p in range(nres):
        x = x + _dot(refs[3 + 2 * p][...], refs[4 + 2 * p][...])
    k_ref, v_ref, xo_ref, o_ref = refs[3 + 2 * nres:]
    xo_ref[...] = x
    q = (_dot(_rms(x, g_ref[...]).astype(bf16), wq_ref[...]) * scale).astype(bf16)
    dh = q.shape[1] // nheads
    hs = range(nheads)
    s = [_dot_nt(q[:, h * dh:(h + 1) * dh], k_ref[:, h * dh:(h + 1) * dh]) for h in hs]
    m = [jnp.max(sh, axis=-1, keepdims=True) for sh in s]
    p = [jnp.exp(s[h] - m[h]) for h in hs]
    l = [ph.sum(axis=-1, keepdims=True) for ph in p]
    o = [_dot(p[h].astype(bf16), v_ref[:, h * dh:(h + 1) * dh]) / l[h] for h in hs]
    o_ref[...] = jnp.concatenate(o, axis=-1).astype(bf16)


def _cross_prompt(x, g, wq, res, mk, mv, nheads):
    T, D = x.shape
    W = wq.shape[1]
    S = mk.shape[0]
    TM = min(512, T)
    row = lambda width: pl.BlockSpec((TM, width), lambda i: (i, 0))
    whole = lambda r, c: pl.BlockSpec((r, c), lambda i: (0, 0))
    in_specs = [row(D), whole(1, D), whole(D, W)]
    args = [x, g.reshape(1, D), wq]
    est = 4 * TM * D * 4 + 2 * D * W * 2 + 4 * S * W * 2 + 2 * TM * W * 2 + 3 * TM * W * 4 + 8 * TM * S * 4
    for a, wr in res:
        K = a.shape[1]
        in_specs += [row(K), whole(K, D)]
        args += [a, wr]
        est += 2 * TM * K * 2 + 2 * K * D * 2
    in_specs += [whole(S, W), whole(S, W)]
    args += [mk, mv]
    return pl.pallas_call(
        functools.partial(_cross_body, nres=len(res), nheads=nheads, scale=(W // nheads) ** -0.5),
        grid=(T // TM,),
        in_specs=in_specs,
        out_specs=[row(D), row(W)],
        out_shape=[jax.ShapeDtypeStruct((T, D), f32), jax.ShapeDtypeStruct((T, W), bf16)],
        compiler_params=_params(("parallel",), est),
        name="cross_prompt",
    )(*args)


def _sample_attn_body(*refs, nheads, mode):
    q_ref, kc_ref, vc_ref, kn_ref, vn_ref = refs[:5]
    if mode == "rel":
        bc_ref, bn_ref, o_ref = refs[5:]
    else:
        fq_ref, fkc_ref, fkn_ref, o_ref = refs[5:]

    m_q, W = q_ref.shape
    dh = W // nheads
    R = nheads * m_q
    q = q_ref[...].astype(f32)
    qt = jnp.concatenate([q] * nheads, axis=0)
    hrow = lax.broadcasted_iota(jnp.int32, (R, W), 0) // m_q
    hlane = lax.broadcasted_iota(jnp.int32, (R, W), 1) // dh
    qbd = jnp.where(hrow == hlane, qt, 0.0).astype(bf16)

    def rows_of(f_ref):
        n = f_ref.shape[-1]
        return jnp.concatenate([jnp.broadcast_to(f_ref[h:h + 1, :], (m_q, n)) for h in range(nheads)], axis=0)

    sc = _dot_nt(qbd, kc_ref[...].astype(bf16))
    sn = _dot_nt(qbd, kn_ref[...].astype(bf16))
    if mode == "rel":
        sc = sc + bc_ref[...]
        sn = sn + bn_ref[...]
    else:
        sc = sc + (fq_ref[...] - rows_of(fkc_ref))
        sn = sn + (fq_ref[...] - rows_of(fkn_ref))
        qpos = lax.broadcasted_iota(jnp.int32, (R, m_q), 0) % m_q
        kpos = lax.broadcasted_iota(jnp.int32, (R, m_q), 1)
        sn = jnp.where(kpos <= qpos, sn, NEG)
    m = jnp.maximum(jnp.max(sc, axis=-1, keepdims=True), jnp.max(sn, axis=-1, keepdims=True))
    pc = jnp.exp(sc - m)
    pn = jnp.exp(sn - m)
    l = pc.sum(axis=-1, keepdims=True) + pn.sum(axis=-1, keepdims=True)
    o = _dot(pc.astype(bf16), vc_ref[...].astype(bf16)) + _dot(pn.astype(bf16), vn_ref[...].astype(bf16))
    o = o / l
    hl = lax.broadcasted_iota(jnp.int32, (m_q, W), 1) // dh
    out = jnp.zeros((m_q, W), f32)
    for h in range(nheads):
        out = out + jnp.where(hl == h, o[h * m_q:(h + 1) * m_q, :], 0.0)
    o_ref[...] = out.astype(bf16)


def _sample_attn(q, kc, vc, nheads, mode, kn, vn, extra):
    B, m_q, W = q.shape
    P = kc.shape[1]
    R = nheads * m_q
    per_b = lambda *shape: pl.BlockSpec((None,) + shape, lambda b: (b,) + (0,) * len(shape))
    shared = lambda *shape: pl.BlockSpec(shape, lambda b: (0,) * len(shape))
    in_specs = [per_b(m_q, W), per_b(P, W), per_b(P, W), per_b(m_q, W), per_b(m_q, W)]
    if mode == "rel":
        in_specs += [shared(R, P), shared(R, m_q)]
    else:
        in_specs += [per_b(R, 1), per_b(nheads, P), per_b(nheads, m_q)]
    est = 2 * 2 * P * W * 4 + 2 * P * W * 2 + 6 * R * P * 4 + 4 * R * W * 4
    return pl.pallas_call(
        functools.partial(_sample_attn_body, nheads=nheads, mode=mode),
        grid=(B,),
        in_specs=in_specs,
        out_specs=per_b(m_q, W),
        out_shape=jax.ShapeDtypeStruct((B, m_q, W), bf16),
        compiler_params=_params(("parallel",), est),
        name="sample_attn_" + mode,
    )(q, kc, vc, kn, vn, *extra)


def _cross_sample_body(q_ref, kc_ref, vc_ref, o_ref, *, nheads):
    m_q, W = q_ref.shape
    dh = W // nheads
    npc = dh // kc_ref.shape[1]
    P = kc_ref.shape[0] // (nheads * npc)
    hs = range(nheads)

    def head_rows(ref, h):
        pieces = [ref[pl.ds(h * npc + j, P, stride=nheads * npc), :] for j in range(npc)]
        return jnp.concatenate(pieces, axis=-1).astype(bf16)

    q = q_ref[...]
    sc = [_dot_nt(q[:, h * dh:(h + 1) * dh], head_rows(kc_ref, h)) for h in hs]
    m = [jnp.max(s, axis=-1, keepdims=True) for s in sc]
    pc = [jnp.exp(sc[h] - m[h]) for h in hs]
    l = [p.sum(axis=-1, keepdims=True) for p in pc]
    o = [_dot(pc[h].astype(bf16), head_rows(vc_ref, h)) / l[h] for h in hs]
    o_ref[...] = jnp.concatenate(o, axis=-1).astype(bf16)


def _cross_sample(q, kc_all, vc_all, layer):
    B, m_q, W = q.shape
    nl, _, P, nheads, dh = kc_all.shape
    nrows = P * nheads * (dh // V7X_LANES)
    kc_all = kc_all.reshape(nl, B, nrows, V7X_LANES)
    vc_all = vc_all.reshape(nl, B, nrows, V7X_LANES)
    cache = pl.BlockSpec((None, None, nrows, V7X_LANES), lambda b: (layer, b, 0, 0))
    qo = pl.BlockSpec((None, m_q, W), lambda b: (b, 0, 0))
    est = 2 * 2 * nrows * V7X_LANES * 4 + 4 * P * W * 2 + 8 * nheads * m_q * P * 4
    return pl.pallas_call(
        functools.partial(_cross_sample_body, nheads=nheads),
        grid=(B,),
        in_specs=[qo, cache, cache],
        out_specs=qo,
        out_shape=jax.ShapeDtypeStruct((B, m_q, W), bf16),
        compiler_params=_params(("parallel",), est),
        name="cross_sample",
    )(q, kc_all, vc_all)


GDN_HALO = 8


def _softplus(x):
    return jnp.maximum(x, 0.0) + jnp.log(1.0 + jnp.exp(-jnp.abs(x)))


def _gdn_conv_qkv(xbuf, R, cw_ref, nheads, dk):
    H0 = GDN_HALO
    tail = xbuf[H0 + R - (CONV - 1):H0 + R, :]

    def act_cols(c0):
        cols = slice(c0, c0 + dk)
        conv = xbuf[H0 - 3:H0 - 3 + R, cols] * cw_ref[0:1, cols]
        for jj in range(1, CONV):
            conv = conv + xbuf[H0 - 3 + jj:H0 - 3 + jj + R, cols] * cw_ref[jj:jj + 1, cols]
        return _silu(conv)

    def l2n(a):
        return a * lax.rsqrt(jnp.sum(a * a, axis=-1, keepdims=True) + EPS)

    W = nheads * dk
    qn = [l2n(act_cols(h * dk)) * (dk ** -0.5) for h in range(nheads)]
    kn = [l2n(act_cols(W + h * dk)) for h in range(nheads)]
    vn = [act_cols(2 * W + h * dk) for h in range(nheads)]
    return qn, kn, vn, tail


def _gdn_proj_body(x_ref, g_ref, w_ref, cs_ref, cw_ref, q_ref, k_ref, v_ref, z_ref, ab_ref, cf_ref, xbuf,
                   *, nsteps, nheads, dk):
    i = pl.program_id(0)
    W = nheads * dk
    TM = x_ref.shape[0]
    H0 = GDN_HALO

    @pl.when(i == 0)
    def _():
        xbuf[...] = jnp.zeros_like(xbuf)

    qn, kn, vn, tail = _gdn_conv_qkv(xbuf, TM, cw_ref, nheads, dk)
    q_ref[...] = jnp.concatenate(qn, axis=-1)
    k_ref[...] = jnp.concatenate(kn, axis=-1)
    v_ref[...] = jnp.concatenate(vn, axis=-1)

    xbuf[H0 - (CONV - 1):H0, :] = jnp.where(i == 0, cs_ref[...], tail)
    cf_ref[...] = tail

    h = _rms(x_ref[...], g_ref[...]).astype(bf16)
    xbuf[H0:H0 + TM, :] = _dot(h, w_ref[:, :3 * W])
    z_ref[...] = _dot(h, w_ref[:, 3 * W:4 * W])
    ab_ref[...] = _dot(h, w_ref[:, 4 * W:])


def _gdn_proj(x, g, w, conv_state, conv_w, nheads):
    T, D = x.shape
    N = w.shape[1]
    W = (N - V7X_LANES) // 4
    dk = W // nheads
    TM = min(512, T)
    nsteps = T // TM
    row = lambda width: pl.BlockSpec((TM, width), lambda i: (jnp.minimum(i, nsteps - 1), 0))
    late = lambda width: pl.BlockSpec((TM, width), lambda i: (jnp.maximum(i - 1, 0), 0))
    whole = lambda *shape: pl.BlockSpec(shape, lambda i: (0,) * len(shape))
    est = 2 * TM * D * 4 + 2 * D * N * 2 + 2 * TM * (4 * W + V7X_LANES) * 4 + (GDN_HALO + TM) * 3 * W * 4 + 4 * TM * 3 * W * 4
    q, k, v, z, ab, cf = pl.pallas_call(
        functools.partial(_gdn_proj_body, nsteps=nsteps, nheads=nheads, dk=dk),
        grid=(nsteps + 1,),
        in_specs=[row(D), whole(1, D), whole(D, N),
                  pl.BlockSpec((None, CONV - 1, 3 * W), lambda i: (0, 0, 0)), whole(CONV, 3 * W)],
        out_specs=[late(W), late(W), late(W), row(W), row(V7X_LANES),
                   pl.BlockSpec((None, CONV - 1, 3 * W), lambda i: (0, 0, 0))],
        out_shape=[jax.ShapeDtypeStruct((T, W), f32)] * 4
                  + [jax.ShapeDtypeStruct((T, V7X_LANES), f32), jax.ShapeDtypeStruct((1, CONV - 1, 3 * W), f32)],
        scratch_shapes=[pltpu.VMEM((GDN_HALO + TM, 3 * W), f32)],
        compiler_params=_params(("arbitrary",), est),
        name="gdn_proj",
    )(x, g.reshape(1, D), w, conv_state, conv_w)
    return q, k, v, z, ab, cf


def _gdn_body(*refs, C, NS, nsteps, nheads, dk, pre):
    if pre:
        q_ref, k_ref, v_ref, ab_ref, z_ref, s0_ref, alog_ref, dtb_ref, ng_ref, o_ref, sf_ref, s_scr = refs
    else:
        (x_ref, ab_ref, z_ref, cs_ref, s0_ref, cw_ref, alog_ref, dtb_ref, ng_ref,
         o_ref, sf_ref, cf_ref, xbuf, s_scr) = refs
    c = pl.program_id(1)
    R = NS * C
    G = V7X_LANES // C
    NG = nheads // G
    GW = G * dk
    hs = range(nheads)

    @pl.when(c == 0)
    def _():
        s_scr[...] = s0_ref[...]

    if pre:
        qn = [q_ref[:, h * dk:(h + 1) * dk] for h in hs]
        kn = [k_ref[:, h * dk:(h + 1) * dk] for h in hs]
        vn = [v_ref[:, h * dk:(h + 1) * dk] for h in hs]
    else:
        H0 = GDN_HALO

        @pl.when(c == 0)
        def _():
            xbuf[H0 - (CONV - 1):H0, :] = cs_ref[...]

        xbuf[H0:H0 + R, :] = x_ref[...]
        qn, kn, vn, tail = _gdn_conv_qkv(xbuf, R, cw_ref, nheads, dk)
        xbuf[H0 - (CONV - 1):H0, :] = tail

        @pl.when(c == nsteps - 1)
        def _():
            cf_ref[...] = tail

    ab = ab_ref[...]
    gfull = -jnp.exp(alog_ref[...]) * _softplus(ab + dtb_ref[...])
    bfull = 1.0 / (1.0 + jnp.exp(-ab))
    z = z_ref[...]
    ng = ng_ref[...]

    r_cc = lax.broadcasted_iota(jnp.int32, (C, C), 0)
    c_cc = lax.broadcasted_iota(jnp.int32, (C, C), 1)
    tri_f = (r_cc >= c_cc).astype(f32)
    ri = lax.broadcasted_iota(jnp.int32, (C, V7X_LANES), 0)
    jl = lax.broadcasted_iota(jnp.int32, (C, V7X_LANES), 1) % C
    lblk = lax.broadcasted_iota(jnp.int32, (C, V7X_LANES), 1) // C
    tri_g = ri >= jl
    strict_g = ri > jl
    eye_g = ri == jl
    r128 = lax.broadcasted_iota(jnp.int32, (V7X_LANES, V7X_LANES), 0)
    l128 = lax.broadcasted_iota(jnp.int32, (V7X_LANES, V7X_LANES), 1)
    same_blk = (r128 // C) == (l128 // C)
    wide_blk = (lax.broadcasted_iota(jnp.int32, (V7X_LANES, GW), 0) // C
                == lax.broadcasted_iota(jnp.int32, (V7X_LANES, GW), 1) // dk)

    def bd_sq(p):
        return jnp.where(same_blk, jnp.concatenate([p] * G, axis=0), jnp.zeros((), p.dtype))

    def bd_wide(xs):
        row = jnp.concatenate(xs, axis=1)
        return jnp.where(wide_blk, jnp.concatenate([row] * G, axis=0), 0.0).astype(bf16)

    def gmm3(xs, pg):
        n = len(xs)
        parts = [_split2(x) for x in xs]
        ph, plo = _split2(pg)
        top = _dot(jnp.concatenate([p for hl in parts for p in hl], axis=0), bd_sq(ph))
        bot = _dot(jnp.concatenate([hl[0] for hl in parts], axis=0), bd_sq(plo)) if n > 1 else _dot(parts[0][0], bd_sq(plo))
        return [top[2 * k * C:(2 * k + 1) * C] + (bot[k * C:(k + 1) * C] + top[(2 * k + 1) * C:(2 * k + 2) * C])
                for k in range(n)]

    sls = [slice(s * C, (s + 1) * C) for s in range(NS)]
    gcs = [_dot_hi(tri_f, gfull[sl]) for sl in sls]
    egcs = [jnp.exp(gc) for gc in gcs]
    kdecs = [jnp.exp(gc[C - 1:C, :] - gc) for gc in gcs]
    gcTs = [jnp.concatenate([gc] * G, axis=0).T for gc in gcs]
    bcols_s = [[bfull[sl, nheads + h:nheads + h + 1] for h in hs] for sl in sls]
    ecols_s = [[egc[:, h:h + 1] for h in hs] for egc in egcs]
    kbs_s = [[kn[h][sls[s]] * bcols_s[s][h] for h in hs] for s in range(NS)]
    Ms, Aqks = [], []
    for s in range(NS):
        sl, gc, gcT, kbs = sls[s], gcs[s], gcTs[s], kbs_s[s]
        for grp in range(NG):
            heads = range(grp * G, (grp + 1) * G)
            gcol = jnp.broadcast_to(gc[:, grp * G:grp * G + 1], (C, V7X_LANES))
            grow = gcT[grp * G:grp * G + 1, :]
            for g in range(1, G):
                h = grp * G + g
                gcol = jnp.where(lblk == g, gc[:, h:h + 1], gcol)
                grow = jnp.where(lblk[0:1, :] == g, gcT[h:h + 1, :], grow)
            Lm = jnp.where(tri_g, jnp.exp(jnp.where(tri_g, gcol - grow, 0.0)), 0.0)
            kbd = bd_wide([kn[h][sl] for h in heads])
            kb_row = jnp.concatenate([kbs[h] for h in heads], axis=1)
            q_row = jnp.concatenate([qn[h][sl] for h in heads], axis=1)
            kq = _dot_nt(jnp.concatenate([kb_row, q_row], axis=0).astype(bf16), kbd)
            Ms.append(jnp.where(strict_g, kq[:C] * Lm, 0.0))
            Aqks.append(jnp.where(tri_g, kq[C:] * Lm, 0.0).astype(bf16))
    Xs = [jnp.where(eye_g, 1.0, 0.0) - M for M in Ms]
    Pws = [gmm3([M], M)[0] for M in Ms]
    e = 2
    while e < C:
        e *= 2
        if e < C:
            nxt = [gmm3([X, Pw], Pw) for X, Pw in zip(Xs, Pws)]
            Xs = [X + r[0] for X, r in zip(Xs, nxt)]
            Pws = [r[1] for r in nxt]
        else:
            Xs = [X + gmm3([X], Pw)[0] for X, Pw in zip(Xs, Pws)]
    prep = []
    for s in range(NS):
        sl, bcols, ecols, kbs = sls[s], bcols_s[s], ecols_s[s], kbs_s[s]
        us, ws = [], []
        for grp in range(NG):
            heads = range(grp * G, (grp + 1) * G)
            Tm = Xs[s * NG + grp].astype(bf16)
            U = _dot(Tm, bd_wide([vn[h][sl] * bcols[h] for h in heads]))
            Wm = _dot(Tm, bd_wide([kbs[h] * ecols[h] for h in heads]))
            for g, h in enumerate(heads):
                us.append(U[:, g * dk:(g + 1) * dk])
                ws.append(Wm[:, g * dk:(g + 1) * dk].astype(bf16))
        qgs = [(qn[h][sl] * ecols[h]).astype(bf16) for h in hs]
        kgTs = [(kn[h][sl] * kdecs[s][:, h:h + 1]).T.astype(bf16) for h in hs]
        decs = [egcs[s][C - 1:C, h:h + 1] for h in hs]
        wqs = [jnp.concatenate([ws[h], qgs[h]], axis=0) for h in hs]
        prep.append((us, wqs, Aqks[s * NG:(s + 1) * NG], kgTs, decs))

    Ss = [s_scr[h] for h in hs]
    for s in range(NS):
        us, wqs, Aqks, kgTs, decs = prep[s]
        wqS = [_dot(wqs[h], Ss[h].astype(bf16)) for h in hs]
        vnews = [us[h] - wqS[h][:C] for h in hs]
        vnbs = [v.astype(bf16) for v in vnews]
        intra = [_dot(Aqks[grp], bd_wide([vnews[h] for h in range(grp * G, (grp + 1) * G)])) for grp in range(NG)]
        os_ = [wqS[h][C:] + intra[h // G][:, (h % G) * dk:(h % G + 1) * dk] for h in hs]
        Ss = [Ss[h] * decs[h] + _dot(kgTs[h], vnbs[h]) for h in hs]
        zs = z[s * C:(s + 1) * C, :]
        ys = [(_rms(os_[h], ng) * _silu(zs[:, h * dk:(h + 1) * dk])).astype(bf16) for h in hs]
        o_ref[s * C:(s + 1) * C, :] = jnp.concatenate(ys, axis=-1)
    for h in hs:
        s_scr[h] = Ss[h]

    @pl.when(c == nsteps - 1)
    def _():
        sf_ref[...] = s_scr[...]


def _gdn_core_pre(q, k, v, ab, z, S0, a_log, dt_bias, norm_g, C, NS):
    T = q.shape[0]
    B, H, dk, dv = S0.shape
    W = H * dk
    R = NS * C
    nsteps = T // (B * R)
    pad = lambda vec: jnp.zeros((1, V7X_LANES), f32).at[0, :H].set(vec.astype(f32))
    row = lambda width: pl.BlockSpec((R, width), lambda b, c: (b * nsteps + c, 0))
    lane_row = pl.BlockSpec((1, V7X_LANES), lambda b, c: (0, 0))
    state = pl.BlockSpec((None, H, dk, dv), lambda b, c: (b, 0, 0, 0))
    est = 2 * R * (4 * W + V7X_LANES) * 4 + 3 * H * dk * dv * 4 * 2 + 16 * R * W * 4
    return pl.pallas_call(
        functools.partial(_gdn_body, C=C, NS=NS, nsteps=nsteps, nheads=H, dk=dk, pre=True),
        grid=(B, nsteps),
        in_specs=[row(W), row(W), row(W), row(V7X_LANES), row(W), state, lane_row, lane_row,
                  pl.BlockSpec((1, dv), lambda b, c: (0, 0))],
        out_specs=[row(W), state],
        out_shape=[jax.ShapeDtypeStruct((T, W), bf16), jax.ShapeDtypeStruct((B, H, dk, dv), f32)],
        scratch_shapes=[pltpu.VMEM((H, dk, dv), f32)],
        compiler_params=_params(("parallel", "arbitrary"), est),
        name="gdn_core_pre",
    )(q, k, v, ab, z, S0, pad(a_log), pad(dt_bias), norm_g.reshape(1, dv))


def _gdn_core(qkv, ab, z, conv_state, S0, conv_w, a_log, dt_bias, norm_g, C, NS):
    T = qkv.shape[0]
    B, H, dk, dv = S0.shape
    W = H * dk
    R = NS * C
    nsteps = T // (B * R)
    pad = lambda vec: jnp.zeros((1, V7X_LANES), f32).at[0, :H].set(vec.astype(f32))
    row = lambda width: pl.BlockSpec((R, width), lambda b, c: (b * nsteps + c, 0))
    est = 2 * R * (3 * W + W + V7X_LANES) * 4 + 3 * H * dk * dv * 4 * 2 + 12 * R * 3 * W * 4
    return pl.pallas_call(
        functools.partial(_gdn_body, C=C, NS=NS, nsteps=nsteps, nheads=H, dk=dk, pre=False),
        grid=(B, nsteps),
        in_specs=[row(3 * W), row(V7X_LANES), row(W),
                  pl.BlockSpec((None, CONV - 1, 3 * W), lambda b, c: (b, 0, 0)),
                  pl.BlockSpec((None, H, dk, dv), lambda b, c: (b, 0, 0, 0)),
                  pl.BlockSpec((CONV, 3 * W), lambda b, c: (0, 0)),
                  pl.BlockSpec((1, V7X_LANES), lambda b, c: (0, 0)),
                  pl.BlockSpec((1, V7X_LANES), lambda b, c: (0, 0)),
                  pl.BlockSpec((1, dv), lambda b, c: (0, 0))],
        out_specs=[row(W),
                   pl.BlockSpec((None, H, dk, dv), lambda b, c: (b, 0, 0, 0)),
                   pl.BlockSpec((None, CONV - 1, 3 * W), lambda b, c: (b, 0, 0))],
        out_shape=[jax.ShapeDtypeStruct((T, W), bf16),
                   jax.ShapeDtypeStruct((B, H, dk, dv), f32),
                   jax.ShapeDtypeStruct((B, CONV - 1, 3 * W), f32)],
        scratch_shapes=[pltpu.VMEM((GDN_HALO + R, 3 * W), f32), pltpu.VMEM((H, dk, dv), f32)],
        compiler_params=_params(("parallel", "arbitrary"), est),
        name="gdn_core",
    )(qkv, ab, z, conv_state, S0, conv_w, pad(a_log), pad(dt_bias), norm_g.reshape(1, dv))


def kernel(x_prompt, x_sample, mem_prompt, cache_a_k, cache_a_v, cache_b_k, cache_b_v, cache_b_logf, state_gdn, state_gdn_conv, cache_mem_k, cache_mem_v, norm_g, mem_norm_g, final_norm_g, ffn_w_gate, ffn_w_up, ffn_w_down, xa_w_q, xa_w_k, xa_w_v, xa_w_o, ab_w_in, ab_b_f, ab_rel_bias, ab_w_o, gdn_w_in, gdn_conv_w, gdn_a_log, gdn_dt_bias, gdn_norm_g, gdn_w_o):
    depth = norm_g.shape[0]
    BP, SEQ, D = x_prompt.shape
    BS, MS, _ = x_sample.shape
    assert BP == 1
    HA = HB = ab_b_f.shape[1]
    WA = WB = (ab_w_in.shape[2] - HB) // 6
    DHA = WA // HA
    HG = gdn_a_log.shape[1]
    WG = gdn_w_o.shape[1]
    DKG = WG // HG
    HX = cache_mem_k.shape[3]
    DHX = cache_mem_k.shape[4]
    WX = HX * DHX
    PA = cache_a_k.shape[2]
    PB = cache_b_k.shape[2]
    LANES = V7X_LANES

    xp = x_prompt.reshape(SEQ, D)
    xs = x_sample.reshape(BS * MS, D)

    wgate = ffn_w_gate.astype(bf16)
    wup = ffn_w_up.astype(bf16)
    wdown = ffn_w_down.astype(bf16)
    wxq = xa_w_q.astype(bf16)
    wxo = xa_w_o.astype(bf16)
    wxkv = jnp.concatenate([xa_w_k, xa_w_v], axis=-1).astype(bf16)

    outs = {k: [] for k in ("akp", "avp", "bkp", "bvp", "blp", "sgp", "scp", "mkp", "mvp",
                            "aks", "avs", "bks", "bvs", "bls", "sgs", "scs")}

    for l in range(depth):
        xp = _ffn(xp, norm_g[l, 0], wgate[l, 0], wup[l, 0], wdown[l, 0])
        xs = _ffn(xs, norm_g[l, 0], wgate[l, 0], wup[l, 0], wdown[l, 0])

        if l % 2 == 0:
            e = l // 2
            w_in = ab_w_in[e]
            nq = 3 * WA + 3 * WB
            w_pad = jnp.concatenate([w_in, jnp.zeros((D, LANES - HB), f32)], axis=-1).astype(bf16)
            b_pad = jnp.zeros((1, LANES), f32).at[0, :HB].set(ab_b_f[e].astype(f32))
            spec = [(0, WA, bf16, DHA ** -0.5, None),
                    (WA, WA, f32, 1.0, None), (WA, WA, bf16, 1.0, None),
                    (2 * WA, WA, f32, 1.0, None), (2 * WA, WA, bf16, 1.0, None),
                    (3 * WA, WB, bf16, (WB // HB) ** -0.5, None),
                    (3 * WA + WB, WB, f32, 1.0, None), (3 * WA + WB, WB, bf16, 1.0, None),
                    (3 * WA + 2 * WB, WB, f32, 1.0, None), (3 * WA + 2 * WB, WB, bf16, 1.0, None),
                    (nq, LANES, f32, 1.0, "logsig")]
            wo = ab_w_o[e].astype(bf16)

            spec_a = [(0, WA, bf16, DHA ** -0.5 * LOG2E, None)] + spec[1:5]
            qa, ka, kab, va, vab = _norm_proj(xp, norm_g[l, 1], w_pad[:, :3 * WA], spec_a)
            oa = _band_prompt(qa, kab, vab, _band_bias(ab_rel_bias[e]))
            DHB = WB // HB
            spread = lambda w: jnp.pad(w.reshape(D, HB, DHB), ((0, 0), (0, 0), (0, LANES - DHB))).reshape(D, HB * LANES)
            wq_b = w_in[:, 3 * WA:3 * WA + WB]
            wk_b = w_in[:, 3 * WA + WB:3 * WA + 2 * WB]
            w_b = jnp.concatenate([spread(wq_b), wk_b, spread(wk_b), w_in[:, 3 * WA + 2 * WB:nq],
                                   w_pad[:, nq:]], axis=-1).astype(bf16)
            W2 = HB * LANES
            spec_b = [(0, W2, bf16, DHB ** -0.5 * LOG2E, None),
                      (W2, WB, f32, 1.0, None), (W2 + WB, W2, bf16, 1.0, None),
                      (2 * W2 + WB, WB, f32, 1.0, None), (2 * W2 + WB, WB, bf16, 1.0, None),
                      (2 * W2 + 2 * WB, LANES, f32, 1.0, "logsig")]
            qbs, kb, kbs, vb, vbb, lf = _norm_proj(xp, norm_g[l, 1], w_b, spec_b, bias=b_pad)
            F, _ = _cumsum_rows(lf)
            qaug, kaug, vt = _fox_pack(qbs, kbs, vbb, F, HB)
            ob = _fox_prompt(qaug, kaug, vt, F, HB)
            res_p = [(oa, wo[:WA]), (ob, wo[WA:])]
            keep = min(BAND_CHUNKS * CHUNK, SEQ)
            outs["akp"].append(ka[SEQ - keep:].reshape(1, keep, HA, DHA))
            outs["avp"].append(va[SEQ - keep:].reshape(1, keep, HA, DHA))
            outs["bkp"].append(kb.reshape(1, SEQ, HB, WB // HB))
            outs["bvp"].append(vb.reshape(1, SEQ, HB, WB // HB))
            outs["blp"].append(lf[:, :HB].reshape(1, SEQ, HB))

            qa, ka, _, va, _, qb, kb, _, vb, _, lf = _norm_proj(xs, norm_g[l, 1], w_pad, spec, bias=b_pad)
            bias_s = _toeplitz_bias(ab_rel_bias[e], MS, PA + MS, PA).reshape(HA * MS, PA + MS)
            oa = _sample_attn(qa.reshape(BS, MS, WA), cache_a_k[e].reshape(BS, PA, WA), cache_a_v[e].reshape(BS, PA, WA),
                              HA, "rel", ka.reshape(BS, MS, WA), va.reshape(BS, MS, WA),
                              extra=(bias_s[:, :PA], bias_s[:, PA:]))
            lfn = lf[:, :HB].reshape(BS, MS, HB)
            lcat = jnp.concatenate([cache_b_logf[e].astype(f32), lfn], axis=1)
            LP = -(-(PB + MS) // LANES) * LANES
            lcat = jnp.pad(lcat.transpose(1, 0, 2).reshape(PB + MS, BS * HB), ((0, LP - PB - MS), (0, 0)))
            _, FTs = _cumsum_rows(lcat)
            FTs = FTs.reshape(BS, HB, LP)
            fq = FTs[:, :, PB:PB + MS].reshape(BS, HB * MS, 1)
            ob = _sample_attn(qb.reshape(BS, MS, WB), cache_b_k[e].reshape(BS, PB, WB), cache_b_v[e].reshape(BS, PB, WB),
                              HB, "fox", kb.reshape(BS, MS, WB), vb.reshape(BS, MS, WB),
                              extra=(fq, FTs[:, :, :PB], FTs[:, :, PB:PB + MS]))
            res_s = [(oa.reshape(BS * MS, WA), wo[:WA]), (ob.reshape(BS * MS, WB), wo[WA:])]
            outs["aks"].append(ka.reshape(BS, MS, HA, DHA))
            outs["avs"].append(va.reshape(BS, MS, HA, DHA))
            outs["bks"].append(kb.reshape(BS, MS, HB, WB // HB))
            outs["bvs"].append(vb.reshape(BS, MS, HB, WB // HB))
            outs["bls"].append(lfn)
        else:
            o = l // 2
            w_in = gdn_w_in[o]
            w_pad = jnp.concatenate([w_in, jnp.zeros((D, LANES - 2 * HG), f32)], axis=-1).astype(bf16)
            spec = [(0, 3 * WG, f32, 1.0, None), (3 * WG, WG, f32, 1.0, None), (4 * WG, LANES, f32, 1.0, None)]
            wo = gdn_w_o[o].astype(bf16)

            qg, kg, vg, z, ab, cvp = _gdn_proj(xp, norm_g[l, 1], w_pad, jnp.zeros((1, CONV - 1, 3 * WG), f32),
                                               gdn_conv_w[o], HG)
            y, Sp = _gdn_core_pre(qg, kg, vg, ab, z, jnp.zeros((1, HG, DKG, DKG), f32),
                                  gdn_a_log[o], gdn_dt_bias[o], gdn_norm_g[o], CHUNK, 8)
            res_p = [(y, wo)]
            outs["sgp"].append(Sp)
            outs["scp"].append(cvp)

            qkv, z, ab = _norm_proj(xs, norm_g[l, 1], w_pad, spec)
            y, Ss, cvs = _gdn_core(qkv, ab, z, state_gdn_conv[o].astype(f32), state_gdn[o].astype(f32),
                                   gdn_conv_w[o], gdn_a_log[o], gdn_dt_bias[o], gdn_norm_g[o], MS, 1)
            res_s = [(y, wo)]
            outs["sgs"].append(Ss)
            outs["scs"].append(cvs)

        mk, mkb, mv, mvb = _norm_proj(mem_prompt.reshape(MEM, D), mem_norm_g[l], wxkv[l],
                                      [(0, WX, f32, 1.0, None), (0, WX, bf16, 1.0, None),
                                       (WX, WX, f32, 1.0, None), (WX, WX, bf16, 1.0, None)])
        outs["mkp"].append(mk.reshape(1, MEM, HX, DHX))
        outs["mvp"].append(mv.reshape(1, MEM, HX, DHX))
        qspec = [(0, WX, bf16, DHX ** -0.5, None)]
        xp, ocp = _cross_prompt(xp, norm_g[l, 2], wxq[l], res_p, mkb, mvb, HX)
        xs, q = _norm_proj(xs, norm_g[l, 2], wxq[l], qspec, res=res_s)
        ocs = _cross_sample(q.reshape(BS, MS, WX), cache_mem_k, cache_mem_v, l)

        fin = final_norm_g if l == depth - 1 else None
        xp = _ffn(xp, norm_g[l, 3], wgate[l, 1], wup[l, 1], wdown[l, 1], fin, res=(ocp, wxo[l]))
        xs = _ffn(xs, norm_g[l, 3], wgate[l, 1], wup[l, 1], wdown[l, 1], fin, res=(ocs.reshape(BS * MS, WX), wxo[l]))

    st = lambda k: jnp.stack(outs[k])
    return (xp.reshape(BP, SEQ, D), xs.reshape(BS, MS, D),
            st("akp"), st("avp"), st("bkp"), st("bvp"), st("blp"),
            st("sgp"), st("scp"), st("mkp"), st("mvp"),
            st("aks"), st("avs"), st("bks"), st("bvs"), st("bls"),
            st("sgs"), st("scs"))
```

```python
import functools

import numpy as np
import jax
import jax.numpy as jnp
from jax import lax
from jax.experimental import pallas as pl
from jax.experimental.pallas import tpu as pltpu

f32 = jnp.float32
bf16 = jnp.bfloat16
EPS = 1e-6
NEG = -1e30

V7X_VMEM_BYTES = 64 * 1024 * 1024
V7X_LANES = 128
MIB = 1024 * 1024

CHUNK = 64
BAND_CHUNKS = 8
REL_CLIP = 256
CONV = 4
MEM = 256


def _params(sem, est_bytes):
    limit = int(min(V7X_VMEM_BYTES - 8 * MIB, max(32 * MIB, est_bytes + 8 * MIB)))
    return pltpu.CompilerParams(dimension_semantics=sem, vmem_limit_bytes=limit)


def _rms(x, g):
    ms = jnp.mean(x * x, axis=-1, keepdims=True)
    return x * lax.rsqrt(ms + EPS) * g


def _silu(x):
    h = 0.5 * x
    return h + h * jnp.tanh(h)


def _dot(a, b):
    return jnp.dot(a, b, preferred_element_type=f32)


def _dot_nt(a, b):
    return lax.dot_general(a, b, (((1,), (1,)), ((), ())), preferred_element_type=f32)


def _dot_hi(a, b):
    return jnp.dot(a, b, preferred_element_type=f32, precision=lax.Precision.HIGHEST)


def _split2(a):
    hi = a.astype(bf16)
    return hi, (a - hi.astype(f32)).astype(bf16)


def _split3(a):
    hi, rest = a.astype(bf16), a
    rest = rest - hi.astype(f32)
    mid = rest.astype(bf16)
    return hi, mid, (rest - mid.astype(f32)).astype(bf16)


def _ffn_body(*refs, nj, final, has_res):
    x_ref, g_ref, wg_ref, wu_ref, wd_ref = refs[:5]
    pos = 5
    if has_res:
        a_ref, wo_ref = refs[pos:pos + 2]
        pos += 2
    if final:
        gf_ref = refs[pos]
        pos += 1
    o_ref, h_scr, acc = refs[pos:]
    j = pl.program_id(1)

    @pl.when(j == 0)
    def _():
        x = x_ref[...]
        if has_res:
            x = x + _dot(a_ref[...], wo_ref[...])
        o_ref[...] = x
        h_scr[...] = _rms(x, g_ref[...]).astype(bf16)
        acc[...] = jnp.zeros_like(acc)

    h = h_scr[...]
    a = _silu(_dot(h, wg_ref[...])) * _dot(h, wu_ref[...])
    acc[...] += _dot(a.astype(bf16), wd_ref[...])

    @pl.when(j == nj - 1)
    def _():
        y = o_ref[...] + 0.5 * acc[...]
        if final:
            y = _rms(y, gf_ref[...])
        o_ref[...] = y


def _ffn(x, g, wg, wu, wd, final_g=None, res=None):
    T, D = x.shape
    FF = wg.shape[1]
    TM = min(1024, T)
    TF = min(1024, FF)
    nj = FF // TF
    final = final_g is not None
    in_specs = [pl.BlockSpec((TM, D), lambda i, j: (i, 0)),
                pl.BlockSpec((1, D), lambda i, j: (0, 0)),
                pl.BlockSpec((D, TF), lambda i, j: (0, j)),
                pl.BlockSpec((D, TF), lambda i, j: (0, j)),
                pl.BlockSpec((TF, D), lambda i, j: (j, 0))]
    args = [x, g.reshape(1, D), wg, wu, wd]
    est = 2 * (2 * TM * D * 4) + 2 * 3 * D * TF * 2 + TM * D * 6 + 3 * TM * TF * 4
    if res is not None:
        a, wo = res
        K = a.shape[1]
        in_specs += [pl.BlockSpec((TM, K), lambda i, j: (i, 0)), pl.BlockSpec((K, D), lambda i, j: (0, 0))]
        args += [a, wo]
        est += 2 * TM * K * 2 + 2 * K * D * 2
    if final:
        in_specs.append(pl.BlockSpec((1, D), lambda i, j: (0, 0)))
        args.append(final_g.reshape(1, D))
    return pl.pallas_call(
        functools.partial(_ffn_body, nj=nj, final=final, has_res=res is not None),
        grid=(T // TM, nj),
        in_specs=in_specs,
        out_specs=pl.BlockSpec((TM, D), lambda i, j: (i, 0)),
        out_shape=jax.ShapeDtypeStruct((T, D), f32),
        scratch_shapes=[pltpu.VMEM((TM, D), bf16), pltpu.VMEM((TM, D), f32)],
        compiler_params=_params(("parallel", "arbitrary"), est),
        name="ffn",
    )(*args)


def _log_sigmoid(x):
    return jnp.minimum(x, 0.0) - jnp.log(1.0 + jnp.exp(-jnp.abs(x)))


def _norm_proj_body(*refs, outs, has_bias, nres):
    x_ref, g_ref, w_ref = refs[:3]
    pos = 3
    b_ref = None
    if has_bias:
        b_ref = refs[pos]
        pos += 1
    x = x_ref[...]
    for p in range(nres):
        x = x + _dot(refs[pos][...], refs[pos + 1][...])
        pos += 2
    o_refs = refs[pos:]
    if nres:
        o_refs[0][...] = x
        o_refs = o_refs[1:]
    h = _rms(x, g_ref[...]).astype(bf16)
    cache = {}
    for o_ref, (off, n, dt, scale, act) in zip(o_refs, outs):
        if (off, n) not in cache:
            cache[(off, n)] = _dot(h, w_ref[:, off:off + n])
        r = cache[(off, n)]
        if act == "logsig":
            r = _log_sigmoid(r + b_ref[...])
        if scale != 1.0:
            r = r * scale
        o_ref[...] = r.astype(dt)


def _norm_proj(x, g, w, outs, bias=None, res=()):
    T, D = x.shape
    N = w.shape[1]
    TM = min(512, T)
    in_specs = [pl.BlockSpec((TM, D), lambda i: (i, 0)),
                pl.BlockSpec((1, D), lambda i: (0, 0)),
                pl.BlockSpec((D, N), lambda i: (0, 0))]
    args = [x, g.reshape(1, D), w]
    if bias is not None:
        in_specs.append(pl.BlockSpec((1, bias.shape[-1]), lambda i: (0, 0)))
        args.append(bias)
    est = 2 * TM * D * 4 + 2 * D * N * 2 + sum(2 * TM * n * 4 for (_, n, _, _, _) in outs) + TM * N * 4
    for a, wr in res:
        K = a.shape[1]
        in_specs += [pl.BlockSpec((TM, K), lambda i: (i, 0)), pl.BlockSpec((K, D), lambda i: (0, 0))]
        args += [a, wr]
        est += 2 * TM * K * 2 + 2 * K * D * 2
    out_specs = [pl.BlockSpec((TM, n), lambda i: (i, 0)) for (_, n, _, _, _) in outs]
    out_shape = [jax.ShapeDtypeStruct((T, n), dt) for (_, n, dt, _, _) in outs]
    if res:
        out_specs.insert(0, pl.BlockSpec((TM, D), lambda i: (i, 0)))
        out_shape.insert(0, jax.ShapeDtypeStruct((T, D), f32))
        est += 2 * TM * D * 4
    return pl.pallas_call(
        functools.partial(_norm_proj_body, outs=tuple(outs), has_bias=bias is not None, nres=len(res)),
        grid=(T // TM,),
        in_specs=in_specs,
        out_specs=out_specs,
        out_shape=out_shape,
        compiler_params=_params(("parallel",), est),
        name="norm_proj",
    )(*args)


def _cumsum_body(x_ref, f_ref, ft_ref, carry, *, nsub):
    @pl.when(pl.program_id(0) == 0)
    def _():
        carry[...] = jnp.zeros_like(carry)

    r = lax.broadcasted_iota(jnp.int32, (V7X_LANES, V7X_LANES), 0)
    c = lax.broadcasted_iota(jnp.int32, (V7X_LANES, V7X_LANES), 1)
    tri = (r >= c).astype(f32)
    run = carry[...]
    ncol = x_ref.shape[1] // V7X_LANES
    for sb in range(nsub):
        rows = slice(sb * V7X_LANES, (sb + 1) * V7X_LANES)
        blk = _dot_hi(tri, x_ref[rows, :]) + run
        f_ref[rows, :] = blk
        for cb in range(ncol):
            cols = slice(cb * V7X_LANES, (cb + 1) * V7X_LANES)
            ft_ref[cols, rows] = blk[:, cols].T
        run = blk[V7X_LANES - 1:V7X_LANES, :]
    carry[...] = run


def _cumsum_rows(x):
    L, N = x.shape
    nsub = 8 if L % (8 * V7X_LANES) == 0 else 1
    TB = nsub * V7X_LANES
    return pl.pallas_call(
        functools.partial(_cumsum_body, nsub=nsub),
        grid=(L // TB,),
        in_specs=[pl.BlockSpec((TB, N), lambda i: (i, 0))],
        out_specs=[pl.BlockSpec((TB, N), lambda i: (i, 0)), pl.BlockSpec((N, TB), lambda i: (0, i))],
        out_shape=[jax.ShapeDtypeStruct((L, N), f32), jax.ShapeDtypeStruct((N, L), f32)],
        scratch_shapes=[pltpu.VMEM((1, N), f32)],
        compiler_params=_params(("arbitrary",), 8 * TB * N * 4),
        name="cumsum_rows",
    )(x)


BAND_TQ = 256
BAND_NB = 3


def _band_body(q_ref, k0_ref, k1_ref, k2_ref, v0_ref, v1_ref, v2_ref, b_ref, o_ref, *, nheads):
    TQ = BAND_TQ
    lane = lax.broadcasted_iota(jnp.int32, (TQ, V7X_LANES), 1)
    lo = lane < 64
    krefs = (k0_ref, k1_ref, k2_ref)
    vrefs = (v0_ref, v1_ref, v2_ref)
    for hp in range(nheads // 2):
        cols = slice(hp * V7X_LANES, (hp + 1) * V7X_LANES)
        qp = q_ref[:, cols]
        zq = jnp.zeros_like(qp)
        q2 = jnp.concatenate([jnp.where(lo, qp, zq), jnp.where(lo, zq, qp)], axis=0)
        s = []
        for b in range(BAND_NB):
            blk = slice(b * TQ, (b + 1) * TQ)
            bias2 = jnp.concatenate([b_ref[2 * hp, :, blk], b_ref[2 * hp + 1, :, blk]], axis=0)
            s.append(_dot_nt(q2, krefs[b][:, cols]) + bias2)
        smax = s[0]
        for b in range(1, BAND_NB):
            smax = jnp.maximum(smax, s[b])
        m = jnp.max(smax, axis=-1, keepdims=True)
        p = [jnp.exp2(sb - m) for sb in s]
        psum = p[0]
        for b in range(1, BAND_NB):
            psum = psum + p[b]
        l = psum.sum(axis=-1, keepdims=True)
        pcat, vcat = [], []
        for b in range(BAND_NB):
            vp = vrefs[b][:, cols]
            zv = jnp.zeros_like(vp)
            vcat += [jnp.where(lo, vp, zv), jnp.where(lo, zv, vp)]
            pb = p[b].astype(bf16)
            pcat += [pb[:TQ], pb[TQ:]]
        o = _dot(jnp.concatenate(pcat, axis=1), jnp.concatenate(vcat, axis=0))
        o_ref[:, cols] = (o / jnp.where(lo, l[:TQ], l[TQ:])).astype(bf16)


def _toeplitz_bias(table, nq, nk, off):
    H = table.shape[0]
    m = np.arange(nq + nk - 1)
    idx = np.clip(off + nq - 1 - m, -REL_CLIP, REL_CLIP) + REL_CLIP
    w = jnp.concatenate([table[:, idx].astype(f32), jnp.zeros((H, 1), f32)], axis=-1)
    skew = jnp.tile(w, (1, nq))[:, :nq * (nq + nk - 1)].reshape(H, nq, nq + nk - 1)
    return skew[:, :, nq - 1:]


def _band_bias(table):
    qc = np.arange(BAND_TQ)[:, None] // CHUNK
    ki = np.arange(BAND_NB * BAND_TQ)[None, :]
    kc = ki // CHUNK
    in_band = (kc >= qc) & (kc <= qc + BAND_CHUNKS)
    valid = np.stack([in_band & (ki // BAND_TQ >= BAND_NB - 1 - n) for n in range(BAND_NB)])
    bias = _toeplitz_bias(table, BAND_TQ, BAND_NB * BAND_TQ, BAND_CHUNKS * CHUNK) * LOG2E
    return jnp.where(valid[:, None], bias[None], NEG)


def _band_prompt(q, k, v, bias):
    L, W = q.shape
    TQ = BAND_TQ
    H = bias.shape[1]
    kspec = [pl.BlockSpec((TQ, W), functools.partial(lambda i, d: (jnp.maximum(i - d, 0), 0), d=d))
             for d in (2, 1, 0)]
    est = 2 * 7 * TQ * W * 2 + 2 * H * TQ * BAND_NB * TQ * 4 + 16 * TQ * TQ * 4
    return pl.pallas_call(
        functools.partial(_band_body, nheads=H),
        grid=(L // TQ,),
        in_specs=[pl.BlockSpec((TQ, W), lambda i: (i, 0))] + kspec + kspec
                 + [pl.BlockSpec((None, H, TQ, BAND_NB * TQ), lambda i: (jnp.minimum(i, BAND_NB - 1), 0, 0, 0))],
        out_specs=pl.BlockSpec((TQ, W), lambda i: (i, 0)),
        out_shape=jax.ShapeDtypeStruct((L, W), bf16),
        compiler_params=_params(("parallel",), est),
        name="band_prompt",
    )(q, k, k, k, v, v, v, bias)


FOX_T = 1024
FOX_VROWS = 80
FOX_QPIECE = 64
FOX_KPIECE = 67
LOG2E = 1.4426950408889634


def _fox_pack_body(q_ref, k_ref, v_ref, f_ref, e_ref, qa_ref, ka_ref, vt_ref, *, nheads, dh):
    T, W2 = q_ref.shape
    pieces = _split3((f_ref[...] - f_ref[0:1, :]) * LOG2E)
    aug = _dot(jnp.concatenate(pieces, axis=1), e_ref[...])
    lane = lax.broadcasted_iota(jnp.int32, (T, W2), 1) % V7X_LANES
    ones_q = jnp.where((lane >= FOX_KPIECE) & (lane < FOX_KPIECE + 3), 1.0, 0.0)
    ones_k = jnp.where((lane >= FOX_QPIECE) & (lane < FOX_QPIECE + 3), 1.0, 0.0)
    qa_ref[...] = (q_ref[...].astype(f32) + aug[:, :W2] + ones_q).astype(bf16)
    ka_ref[...] = (k_ref[...].astype(f32) + aug[:, W2:] + ones_k).astype(bf16)
    vT = v_ref[...].astype(f32).T
    pad = FOX_VROWS - dh
    ones_row = jnp.where(lax.broadcasted_iota(jnp.int32, (pad, T), 0) == 0, 1.0, 0.0).astype(bf16)
    for h in range(nheads):
        vt_ref[h, 0:dh, :] = vT[h * dh:(h + 1) * dh, :].astype(bf16)
        vt_ref[h, dh:FOX_VROWS, :] = ones_row


def _fox_pack(q, k, v, F, nheads):
    L, W2 = q.shape
    W = v.shape[1]
    dh = W // nheads
    T = FOX_T
    e = np.zeros((3 * V7X_LANES, 2 * W2), np.float32)
    for h in range(nheads):
        for t in range(3):
            e[t * V7X_LANES + h, h * V7X_LANES + FOX_QPIECE + t] = 1.0
            e[t * V7X_LANES + h, W2 + h * V7X_LANES + FOX_KPIECE + t] = -1.0
    est = 2 * (4 * T * W2 * 2 + T * W * 2 + nheads * FOX_VROWS * T * 2) + 6 * T * W2 * 4
    return pl.pallas_call(
        functools.partial(_fox_pack_body, nheads=nheads, dh=dh),
        grid=(L // T,),
        in_specs=[pl.BlockSpec((T, W2), lambda i: (i, 0)),
                  pl.BlockSpec((T, W2), lambda i: (i, 0)),
                  pl.BlockSpec((T, W), lambda i: (i, 0)),
                  pl.BlockSpec((T, V7X_LANES), lambda i: (i, 0)),
                  pl.BlockSpec((3 * V7X_LANES, 2 * W2), lambda i: (0, 0))],
        out_specs=[pl.BlockSpec((T, W2), lambda i: (i, 0)),
                   pl.BlockSpec((T, W2), lambda i: (i, 0)),
                   pl.BlockSpec((nheads, FOX_VROWS, T), lambda i: (0, 0, i))],
        out_shape=[jax.ShapeDtypeStruct((L, W2), bf16), jax.ShapeDtypeStruct((L, W2), bf16),
                   jax.ShapeDtypeStruct((nheads, FOX_VROWS, L), bf16)],
        compiler_params=_params(("parallel",), est),
        name="fox_pack",
    )(q, k, v, F, jnp.asarray(e, bf16))


def _fox_body(qi_ref, kj_ref, qa_ref, ka_ref, vt_ref, d_ref, o_ref, m_scr, acc_scr, *, nheads, dh):
    s_id = pl.program_id(0)
    i = qi_ref[s_id]
    j = kj_ref[s_id]
    T = FOX_T

    @pl.when(j == 0)
    def _():
        m_scr[...] = jnp.full_like(m_scr, NEG)
        acc_scr[...] = jnp.zeros_like(acc_scr)

    def step(masked):
        if masked:
            keep = lax.broadcasted_iota(jnp.int32, (T, T), 0) <= lax.broadcasted_iota(jnp.int32, (T, T), 1)
        def scores(h):
            cols = slice(h * V7X_LANES, (h + 1) * V7X_LANES)
            st = _dot_nt(ka_ref[:, cols], qa_ref[:, cols])
            return jnp.where(keep, st, NEG) if masked else st

        nxt = scores(0)
        for h in range(nheads):
            st = nxt
            if h + 1 < nheads:
                nxt = scores(h + 1)
            d = d_ref[s_id * nheads + h]
            m_prev = m_scr[h:h + 1, :]
            m_new = jnp.maximum(m_prev, jnp.max(st, axis=0, keepdims=True) + d)
            p = jnp.exp2(st - (m_new - d))
            alpha = jnp.exp2(m_prev - m_new)
            acc_scr[h] = acc_scr[h] * alpha + _dot(vt_ref[h], p.astype(bf16))
            m_scr[h:h + 1, :] = m_new

    @pl.when(j < i)
    def _():
        step(False)

    @pl.when(j == i)
    def _():
        step(True)
        for hp in range(nheads // 2):
            a0 = acc_scr[2 * hp]
            a1 = acc_scr[2 * hp + 1]
            o2 = jnp.concatenate([a0[0:dh, :] / a0[dh:dh + 1, :], a1[0:dh, :] / a1[dh:dh + 1, :]], axis=0)
            o_ref[:, hp * V7X_LANES:(hp + 1) * V7X_LANES] = o2.T.astype(bf16)


def _fox_prompt(qa, ka, vt, F, nheads):
    L, W2 = qa.shape
    T = FOX_T
    n = L // T
    dh = V7X_LANES // 2
    W = nheads * dh
    pairs = [(i, j) for i in range(n) for j in range(i + 1)]
    qi = np.array([p[0] for p in pairs], np.int32)
    kj = np.array([p[1] for p in pairs], np.int32)
    fs = F[::T, :nheads]
    d = ((fs[qi] - fs[kj]) * LOG2E).reshape(-1)
    grid_spec = pltpu.PrefetchScalarGridSpec(
        num_scalar_prefetch=2,
        grid=(len(pairs),),
        in_specs=[pl.BlockSpec((T, W2), lambda s, qi, kj: (qi[s], 0)),
                  pl.BlockSpec((T, W2), lambda s, qi, kj: (kj[s], 0)),
                  pl.BlockSpec((nheads, FOX_VROWS, T), lambda s, qi, kj: (0, 0, kj[s])),
                  pl.BlockSpec(memory_space=pltpu.SMEM)],
        out_specs=pl.BlockSpec((T, W), lambda s, qi, kj: (qi[s], 0)),
        scratch_shapes=[pltpu.VMEM((nheads, T), f32), pltpu.VMEM((nheads, FOX_VROWS, T), f32)],
    )
    est = 2 * (2 * T * W2 * 2 + nheads * FOX_VROWS * T * 2 + T * W * 2) + nheads * FOX_VROWS * T * 4 + 24 * T * T * 4
    return pl.pallas_call(
        functools.partial(_fox_body, nheads=nheads, dh=dh),
        grid_spec=grid_spec,
        out_shape=jax.ShapeDtypeStruct((L, W), bf16),
        compiler_params=_params(("arbitrary",), est),
        name="fox_prompt",
    )(jnp.asarray(qi), jnp.asarray(kj), qa, ka, vt, d)


def _cross_body(*refs, nres, nheads, scale):
    x_ref, g_ref, wq_ref = refs[:3]
    x = x_ref[...]
    for p in range(nres):
        x = x + _dot(refs[3 + 2 * p][...], refs[4 + 2 * p][...])
    k_ref, v_ref, xo_ref, o_ref = refs[3 + 2 * nres:]
    xo_ref[...] = x
    q = (_dot(_rms(x, g_ref[...]).astype(bf16), wq_ref[...]) * scale).astype(bf16)
    dh = q.shape[1] // nheads
    hs = range(nheads)
    s = [_dot_nt(q[:, h * dh:(h + 1) * dh], k_ref[:, h * dh:(h + 1) * dh]) for h in hs]
    m = [jnp.max(sh, axis=-1, keepdims=True) for sh in s]
    p = [jnp.exp(s[h] - m[h]) for h in hs]
    l = [ph.sum(axis=-1, keepdims=True) for ph in p]
    o = [_dot(p[h].astype(bf16), v_ref[:, h * dh:(h + 1) * dh]) / l[h] for h in hs]
    o_ref[...] = jnp.concatenate(o, axis=-1).astype(bf16)


def _cross_prompt(x, g, wq, res, mk, mv, nheads):
    T, D = x.shape
    W = wq.shape[1]
    S = mk.shape[0]
    TM = min(512, T)
    row = lambda width: pl.BlockSpec((TM, width), lambda i: (i, 0))
    whole = lambda r, c: pl.BlockSpec((r, c), lambda i: (0, 0))
    in_specs = [row(D), whole(1, D), whole(D, W)]
    args = [x, g.reshape(1, D), wq]
    est = 4 * TM * D * 4 + 2 * D * W * 2 + 4 * S * W * 2 + 2 * TM * W * 2 + 3 * TM * W * 4 + 8 * TM * S * 4
    for a, wr in res:
        K = a.shape[1]
        in_specs += [row(K), whole(K, D)]
        args += [a, wr]
        est += 2 * TM * K * 2 + 2 * K * D * 2
    in_specs += [whole(S, W), whole(S, W)]
    args += [mk, mv]
    return pl.pallas_call(
        functools.partial(_cross_body, nres=len(res), nheads=nheads, scale=(W // nheads) ** -0.5),
        grid=(T // TM,),
        in_specs=in_specs,
        out_specs=[row(D), row(W)],
        out_shape=[jax.ShapeDtypeStruct((T, D), f32), jax.ShapeDtypeStruct((T, W), bf16)],
        compiler_params=_params(("parallel",), est),
        name="cross_prompt",
    )(*args)


def _sample_attn_body(*refs, nheads, mode):
    q_ref, kc_ref, vc_ref, kn_ref, vn_ref = refs[:5]
    if mode == "rel":
        bc_ref, bn_ref, o_ref = refs[5:]
    else:
        fq_ref, fkc_ref, fkn_ref, o_ref = refs[5:]

    m_q, W = q_ref.shape
    dh = W // nheads
    R = nheads * m_q
    q = q_ref[...].astype(f32)
    qt = jnp.concatenate([q] * nheads, axis=0)
    hrow = lax.broadcasted_iota(jnp.int32, (R, W), 0) // m_q
    hlane = lax.broadcasted_iota(jnp.int32, (R, W), 1) // dh
    qbd = jnp.where(hrow == hlane, qt, 0.0).astype(bf16)

    def rows_of(f_ref):
        n = f_ref.shape[-1]
        return jnp.concatenate([jnp.broadcast_to(f_ref[h:h + 1, :], (m_q, n)) for h in range(nheads)], axis=0)

    sc = _dot_nt(qbd, kc_ref[...].astype(bf16))
    sn = _dot_nt(qbd, kn_ref[...].astype(bf16))
    if mode == "rel":
        sc = sc + bc_ref[...]
        sn = sn + bn_ref[...]
    else:
        sc = sc + (fq_ref[...] - rows_of(fkc_ref))
        sn = sn + (fq_ref[...] - rows_of(fkn_ref))
        qpos = lax.broadcasted_iota(jnp.int32, (R, m_q), 0) % m_q
        kpos = lax.broadcasted_iota(jnp.int32, (R, m_q), 1)
        sn = jnp.where(kpos <= qpos, sn, NEG)
    m = jnp.maximum(jnp.max(sc, axis=-1, keepdims=True), jnp.max(sn, axis=-1, keepdims=True))
    pc = jnp.exp(sc - m)
    pn = jnp.exp(sn - m)
    l = pc.sum(axis=-1, keepdims=True) + pn.sum(axis=-1, keepdims=True)
    o = _dot(pc.astype(bf16), vc_ref[...].astype(bf16)) + _dot(pn.astype(bf16), vn_ref[...].astype(bf16))
    o = o / l
    hl = lax.broadcasted_iota(jnp.int32, (m_q, W), 1) // dh
    out = jnp.zeros((m_q, W), f32)
    for h in range(nheads):
        out = out + jnp.where(hl == h, o[h * m_q:(h + 1) * m_q, :], 0.0)
    o_ref[...] = out.astype(bf16)


def _sample_attn(q, kc, vc, nheads, mode, kn, vn, extra):
    B, m_q, W = q.shape
    P = kc.shape[1]
    R = nheads * m_q
    per_b = lambda *shape: pl.BlockSpec((None,) + shape, lambda b: (b,) + (0,) * len(shape))
    shared = lambda *shape: pl.BlockSpec(shape, lambda b: (0,) * len(shape))
    in_specs = [per_b(m_q, W), per_b(P, W), per_b(P, W), per_b(m_q, W), per_b(m_q, W)]
    if mode == "rel":
        in_specs += [shared(R, P), shared(R, m_q)]
    else:
        in_specs += [per_b(R, 1), per_b(nheads, P), per_b(nheads, m_q)]
    est = 2 * 2 * P * W * 4 + 2 * P * W * 2 + 6 * R * P * 4 + 4 * R * W * 4
    return pl.pallas_call(
        functools.partial(_sample_attn_body, nheads=nheads, mode=mode),
        grid=(B,),
        in_specs=in_specs,
        out_specs=per_b(m_q, W),
        out_shape=jax.ShapeDtypeStruct((B, m_q, W), bf16),
        compiler_params=_params(("parallel",), est),
        name="sample_attn_" + mode,
    )(q, kc, vc, kn, vn, *extra)


def _cross_sample_body(q_ref, kc_ref, vc_ref, o_ref, *, nheads):
    m_q, W = q_ref.shape
    dh = W // nheads
    npc = dh // kc_ref.shape[1]
    P = kc_ref.shape[0] // (nheads * npc)
    hs = range(nheads)

    def head_rows(ref, h):
        pieces = [ref[pl.ds(h * npc + j, P, stride=nheads * npc), :] for j in range(npc)]
        return jnp.concatenate(pieces, axis=-1).astype(bf16)

    q = q_ref[...]
    sc = [_dot_nt(q[:, h * dh:(h + 1) * dh], head_rows(kc_ref, h)) for h in hs]
    m = [jnp.max(s, axis=-1, keepdims=True) for s in sc]
    pc = [jnp.exp(sc[h] - m[h]) for h in hs]
    l = [p.sum(axis=-1, keepdims=True) for p in pc]
    o = [_dot(pc[h].astype(bf16), head_rows(vc_ref, h)) / l[h] for h in hs]
    o_ref[...] = jnp.concatenate(o, axis=-1).astype(bf16)


def _cross_sample(q, kc_all, vc_all, layer):
    B, m_q, W = q.shape
    nl, _, P, nheads, dh = kc_all.shape
    nrows = P * nheads * (dh // V7X_LANES)
    kc_all = kc_all.reshape(nl, B, nrows, V7X_LANES)
    vc_all = vc_all.reshape(nl, B, nrows, V7X_LANES)
    cache = pl.BlockSpec((None, None, nrows, V7X_LANES), lambda b: (layer, b, 0, 0))
    qo = pl.BlockSpec((None, m_q, W), lambda b: (b, 0, 0))
    est = 2 * 2 * nrows * V7X_LANES * 4 + 4 * P * W * 2 + 8 * nheads * m_q * P * 4
    return pl.pallas_call(
        functools.partial(_cross_sample_body, nheads=nheads),
        grid=(B,),
        in_specs=[qo, cache, cache],
        out_specs=qo,
        out_shape=jax.ShapeDtypeStruct((B, m_q, W), bf16),
        compiler_params=_params(("parallel",), est),
        name="cross_sample",
    )(q, kc_all, vc_all)


GDN_HALO = 8


def _softplus(x):
    return jnp.maximum(x, 0.0) + jnp.log(1.0 + jnp.exp(-jnp.abs(x)))


def _gdn_conv_qkv(xbuf, R, cw_ref, nheads, dk):
    H0 = GDN_HALO
    tail = xbuf[H0 + R - (CONV - 1):H0 + R, :]

    def act_cols(c0):
        cols = slice(c0, c0 + dk)
        conv = xbuf[H0 - 3:H0 - 3 + R, cols] * cw_ref[0:1, cols]
        for jj in range(1, CONV):
            conv = conv + xbuf[H0 - 3 + jj:H0 - 3 + jj + R, cols] * cw_ref[jj:jj + 1, cols]
        return _silu(conv)

    def l2n(a):
        return a * lax.rsqrt(jnp.sum(a * a, axis=-1, keepdims=True) + EPS)

    W = nheads * dk
    qn = [l2n(act_cols(h * dk)) * (dk ** -0.5) for h in range(nheads)]
    kn = [l2n(act_cols(W + h * dk)) for h in range(nheads)]
    vn = [act_cols(2 * W + h * dk) for h in range(nheads)]
    return qn, kn, vn, tail


def _gdn_proj_body(x_ref, g_ref, w_ref, cs_ref, cw_ref, q_ref, k_ref, v_ref, z_ref, ab_ref, cf_ref, xbuf,
                   *, nsteps, nheads, dk):
    i = pl.program_id(0)
    W = nheads * dk
    TM = x_ref.shape[0]
    H0 = GDN_HALO

    @pl.when(i == 0)
    def _():
        xbuf[...] = jnp.zeros_like(xbuf)

    qn, kn, vn, tail = _gdn_conv_qkv(xbuf, TM, cw_ref, nheads, dk)
    q_ref[...] = jnp.concatenate(qn, axis=-1)
    k_ref[...] = jnp.concatenate(kn, axis=-1)
    v_ref[...] = jnp.concatenate(vn, axis=-1)

    xbuf[H0 - (CONV - 1):H0, :] = jnp.where(i == 0, cs_ref[...], tail)
    cf_ref[...] = tail

    h = _rms(x_ref[...], g_ref[...]).astype(bf16)
    xbuf[H0:H0 + TM, :] = _dot(h, w_ref[:, :3 * W])
    z_ref[...] = _dot(h, w_ref[:, 3 * W:4 * W])
    ab_ref[...] = _dot(h, w_ref[:, 4 * W:])


def _gdn_proj(x, g, w, conv_state, conv_w, nheads):
    T, D = x.shape
    N = w.shape[1]
    W = (N - V7X_LANES) // 4
    dk = W // nheads
    TM = min(512, T)
    nsteps = T // TM
    row = lambda width: pl.BlockSpec((TM, width), lambda i: (jnp.minimum(i, nsteps - 1), 0))
    late = lambda width: pl.BlockSpec((TM, width), lambda i: (jnp.maximum(i - 1, 0), 0))
    whole = lambda *shape: pl.BlockSpec(shape, lambda i: (0,) * len(shape))
    est = 2 * TM * D * 4 + 2 * D * N * 2 + 2 * TM * (4 * W + V7X_LANES) * 4 + (GDN_HALO + TM) * 3 * W * 4 + 4 * TM * 3 * W * 4
    q, k, v, z, ab, cf = pl.pallas_call(
        functools.partial(_gdn_proj_body, nsteps=nsteps, nheads=nheads, dk=dk),
        grid=(nsteps + 1,),
        in_specs=[row(D), whole(1, D), whole(D, N),
                  pl.BlockSpec((None, CONV - 1, 3 * W), lambda i: (0, 0, 0)), whole(CONV, 3 * W)],
        out_specs=[late(W), late(W), late(W), row(W), row(V7X_LANES),
                   pl.BlockSpec((None, CONV - 1, 3 * W), lambda i: (0, 0, 0))],
        out_shape=[jax.ShapeDtypeStruct((T, W), f32)] * 4
                  + [jax.ShapeDtypeStruct((T, V7X_LANES), f32), jax.ShapeDtypeStruct((1, CONV - 1, 3 * W), f32)],
        scratch_shapes=[pltpu.VMEM((GDN_HALO + TM, 3 * W), f32)],
        compiler_params=_params(("arbitrary",), est),
        name="gdn_proj",
    )(x, g.reshape(1, D), w, conv_state, conv_w)
    return q, k, v, z, ab, cf


def _gdn_body(*refs, C, NS, nsteps, nheads, dk, pre):
    if pre:
        q_ref, k_ref, v_ref, ab_ref, z_ref, s0_ref, alog_ref, dtb_ref, ng_ref, o_ref, sf_ref, s_scr = refs
    else:
        (x_ref, ab_ref, z_ref, cs_ref, s0_ref, cw_ref, alog_ref, dtb_ref, ng_ref,
         o_ref, sf_ref, cf_ref, xbuf, s_scr) = refs
    c = pl.program_id(1)
    R = NS * C
    G = V7X_LANES // C
    NG = nheads // G
    GW = G * dk
    hs = range(nheads)

    @pl.when(c == 0)
    def _():
        s_scr[...] = s0_ref[...]

    if pre:
        qn = [q_ref[:, h * dk:(h + 1) * dk] for h in hs]
        kn = [k_ref[:, h * dk:(h + 1) * dk] for h in hs]
        vn = [v_ref[:, h * dk:(h + 1) * dk] for h in hs]
    else:
        H0 = GDN_HALO

        @pl.when(c == 0)
        def _():
            xbuf[H0 - (CONV - 1):H0, :] = cs_ref[...]

        xbuf[H0:H0 + R, :] = x_ref[...]
        qn, kn, vn, tail = _gdn_conv_qkv(xbuf, R, cw_ref, nheads, dk)
        xbuf[H0 - (CONV - 1):H0, :] = tail

        @pl.when(c == nsteps - 1)
        def _():
            cf_ref[...] = tail

    ab = ab_ref[...]
    gfull = -jnp.exp(alog_ref[...]) * _softplus(ab + dtb_ref[...])
    bfull = 1.0 / (1.0 + jnp.exp(-ab))
    z = z_ref[...]
    ng = ng_ref[...]

    r_cc = lax.broadcasted_iota(jnp.int32, (C, C), 0)
    c_cc = lax.broadcasted_iota(jnp.int32, (C, C), 1)
    tri_f = (r_cc >= c_cc).astype(f32)
    ri = lax.broadcasted_iota(jnp.int32, (C, V7X_LANES), 0)
    jl = lax.broadcasted_iota(jnp.int32, (C, V7X_LANES), 1) % C
    lblk = lax.broadcasted_iota(jnp.int32, (C, V7X_LANES), 1) // C
    tri_g = ri >= jl
    strict_g = ri > jl
    eye_g = ri == jl
    r128 = lax.broadcasted_iota(jnp.int32, (V7X_LANES, V7X_LANES), 0)
    l128 = lax.broadcasted_iota(jnp.int32, (V7X_LANES, V7X_LANES), 1)
    same_blk = (r128 // C) == (l128 // C)
    wide_blk = (lax.broadcasted_iota(jnp.int32, (V7X_LANES, GW), 0) // C
                == lax.broadcasted_iota(jnp.int32, (V7X_LANES, GW), 1) // dk)

    def bd_sq(p):
        return jnp.where(same_blk, jnp.concatenate([p] * G, axis=0), jnp.zeros((), p.dtype))

    def bd_wide(xs):
        row = jnp.concatenate(xs, axis=1)
        return jnp.where(wide_blk, jnp.concatenate([row] * G, axis=0), 0.0).astype(bf16)

    def gmm3(xs, pg):
        n = len(xs)
        parts = [_split2(x) for x in xs]
        ph, plo = _split2(pg)
        top = _dot(jnp.concatenate([p for hl in parts for p in hl], axis=0), bd_sq(ph))
        bot = _dot(jnp.concatenate([hl[0] for hl in parts], axis=0), bd_sq(plo)) if n > 1 else _dot(parts[0][0], bd_sq(plo))
        return [top[2 * k * C:(2 * k + 1) * C] + (bot[k * C:(k + 1) * C] + top[(2 * k + 1) * C:(2 * k + 2) * C])
                for k in range(n)]

    sls = [slice(s * C, (s + 1) * C) for s in range(NS)]
    gcs = [_dot_hi(tri_f, gfull[sl]) for sl in sls]
    egcs = [jnp.exp(gc) for gc in gcs]
    kdecs = [jnp.exp(gc[C - 1:C, :] - gc) for gc in gcs]
    gcTs = [jnp.concatenate([gc] * G, axis=0).T for gc in gcs]
    bcols_s = [[bfull[sl, nheads + h:nheads + h + 1] for h in hs] for sl in sls]
    ecols_s = [[egc[:, h:h + 1] for h in hs] for egc in egcs]
    kbs_s = [[kn[h][sls[s]] * bcols_s[s][h] for h in hs] for s in range(NS)]
    Ms, Aqks = [], []
    for s in range(NS):
        sl, gc, gcT, kbs = sls[s], gcs[s], gcTs[s], kbs_s[s]
        for grp in range(NG):
            heads = range(grp * G, (grp + 1) * G)
            gcol = jnp.broadcast_to(gc[:, grp * G:grp * G + 1], (C, V7X_LANES))
            grow = gcT[grp * G:grp * G + 1, :]
            for g in range(1, G):
                h = grp * G + g
                gcol = jnp.where(lblk == g, gc[:, h:h + 1], gcol)
                grow = jnp.where(lblk[0:1, :] == g, gcT[h:h + 1, :], grow)
            Lm = jnp.where(tri_g, jnp.exp(jnp.where(tri_g, gcol - grow, 0.0)), 0.0)
            kbd = bd_wide([kn[h][sl] for h in heads])
            kb_row = jnp.concatenate([kbs[h] for h in heads], axis=1)
            q_row = jnp.concatenate([qn[h][sl] for h in heads], axis=1)
            kq = _dot_nt(jnp.concatenate([kb_row, q_row], axis=0).astype(bf16), kbd)
            Ms.append(jnp.where(strict_g, kq[:C] * Lm, 0.0))
            Aqks.append(jnp.where(tri_g, kq[C:] * Lm, 0.0).astype(bf16))
    Xs = [jnp.where(eye_g, 1.0, 0.0) - M for M in Ms]
    Pws = [gmm3([M], M)[0] for M in Ms]
    e = 2
    while e < C:
        e *= 2
        if e < C:
            nxt = [gmm3([X, Pw], Pw) for X, Pw in zip(Xs, Pws)]
            Xs = [X + r[0] for X, r in zip(Xs, nxt)]
            Pws = [r[1] for r in nxt]
        else:
            Xs = [X + gmm3([X], Pw)[0] for X, Pw in zip(Xs, Pws)]
    prep = []
    for s in range(NS):
        sl, bcols, ecols, kbs = sls[s], bcols_s[s], ecols_s[s], kbs_s[s]
        us, ws = [], []
        for grp in range(NG):
            heads = range(grp * G, (grp + 1) * G)
            Tm = Xs[s * NG + grp].astype(bf16)
            U = _dot(Tm, bd_wide([vn[h][sl] * bcols[h] for h in heads]))
            Wm = _dot(Tm, bd_wide([kbs[h] * ecols[h] for h in heads]))
            for g, h in enumerate(heads):
                us.append(U[:, g * dk:(g + 1) * dk])
                ws.append(Wm[:, g * dk:(g + 1) * dk].astype(bf16))
        qgs = [(qn[h][sl] * ecols[h]).astype(bf16) for h in hs]
        kgTs = [(kn[h][sl] * kdecs[s][:, h:h + 1]).T.astype(bf16) for h in hs]
        decs = [egcs[s][C - 1:C, h:h + 1] for h in hs]
        wqs = [jnp.concatenate([ws[h], qgs[h]], axis=0) for h in hs]
        prep.append((us, wqs, Aqks[s * NG:(s + 1) * NG], kgTs, decs))

    Ss = [s_scr[h] for h in hs]
    for s in range(NS):
        us, wqs, Aqks, kgTs, decs = prep[s]
        wqS = [_dot(wqs[h], Ss[h].astype(bf16)) for h in hs]
        vnews = [us[h] - wqS[h][:C] for h in hs]
        vnbs = [v.astype(bf16) for v in vnews]
        intra = [_dot(Aqks[grp], bd_wide([vnews[h] for h in range(grp * G, (grp + 1) * G)])) for grp in range(NG)]
        os_ = [wqS[h][C:] + intra[h // G][:, (h % G) * dk:(h % G + 1) * dk] for h in hs]
        Ss = [Ss[h] * decs[h] + _dot(kgTs[h], vnbs[h]) for h in hs]
        zs = z[s * C:(s + 1) * C, :]
        ys = [(_rms(os_[h], ng) * _silu(zs[:, h * dk:(h + 1) * dk])).astype(bf16) for h in hs]
        o_ref[s * C:(s + 1) * C, :] = jnp.concatenate(ys, axis=-1)
    for h in hs:
        s_scr[h] = Ss[h]

    @pl.when(c == nsteps - 1)
    def _():
        sf_ref[...] = s_scr[...]


def _gdn_core_pre(q, k, v, ab, z, S0, a_log, dt_bias, norm_g, C, NS):
    T = q.shape[0]
    B, H, dk, dv = S0.shape
    W = H * dk
    R = NS * C
    nsteps = T // (B * R)
    pad = lambda vec: jnp.zeros((1, V7X_LANES), f32).at[0, :H].set(vec.astype(f32))
    row = lambda width: pl.BlockSpec((R, width), lambda b, c: (b * nsteps + c, 0))
    lane_row = pl.BlockSpec((1, V7X_LANES), lambda b, c: (0, 0))
    state = pl.BlockSpec((None, H, dk, dv), lambda b, c: (b, 0, 0, 0))
    est = 2 * R * (4 * W + V7X_LANES) * 4 + 3 * H * dk * dv * 4 * 2 + 16 * R * W * 4
    return pl.pallas_call(
        functools.partial(_gdn_body, C=C, NS=NS, nsteps=nsteps, nheads=H, dk=dk, pre=True),
        grid=(B, nsteps),
        in_specs=[row(W), row(W), row(W), row(V7X_LANES), row(W), state, lane_row, lane_row,
                  pl.BlockSpec((1, dv), lambda b, c: (0, 0))],
        out_specs=[row(W), state],
        out_shape=[jax.ShapeDtypeStruct((T, W), bf16), jax.ShapeDtypeStruct((B, H, dk, dv), f32)],
        scratch_shapes=[pltpu.VMEM((H, dk, dv), f32)],
        compiler_params=_params(("parallel", "arbitrary"), est),
        name="gdn_core_pre",
    )(q, k, v, ab, z, S0, pad(a_log), pad(dt_bias), norm_g.reshape(1, dv))


def _gdn_core(qkv, ab, z, conv_state, S0, conv_w, a_log, dt_bias, norm_g, C, NS):
    T = qkv.shape[0]
    B, H, dk, dv = S0.shape
    W = H * dk
    R = NS * C
    nsteps = T // (B * R)
    pad = lambda vec: jnp.zeros((1, V7X_LANES), f32).at[0, :H].set(vec.astype(f32))
    row = lambda width: pl.BlockSpec((R, width), lambda b, c: (b * nsteps + c, 0))
    est = 2 * R * (3 * W + W + V7X_LANES) * 4 + 3 * H * dk * dv * 4 * 2 + 12 * R * 3 * W * 4
    return pl.pallas_call(
        functools.partial(_gdn_body, C=C, NS=NS, nsteps=nsteps, nheads=H, dk=dk, pre=False),
        grid=(B, nsteps),
        in_specs=[row(3 * W), row(V7X_LANES), row(W),
                  pl.BlockSpec((None, CONV - 1, 3 * W), lambda b, c: (b, 0, 0)),
                  pl.BlockSpec((None, H, dk, dv), lambda b, c: (b, 0, 0, 0)),
                  pl.BlockSpec((CONV, 3 * W), lambda b, c: (0, 0)),
                  pl.BlockSpec((1, V7X_LANES), lambda b, c: (0, 0)),
                  pl.BlockSpec((1, V7X_LANES), lambda b, c: (0, 0)),
                  pl.BlockSpec((1, dv), lambda b, c: (0, 0))],
        out_specs=[row(W),
                   pl.BlockSpec((None, H, dk, dv), lambda b, c: (b, 0, 0, 0)),
                   pl.BlockSpec((None, CONV - 1, 3 * W), lambda b, c: (b, 0, 0))],
        out_shape=[jax.ShapeDtypeStruct((T, W), bf16),
                   jax.ShapeDtypeStruct((B, H, dk, dv), f32),
                   jax.ShapeDtypeStruct((B, CONV - 1, 3 * W), f32)],
        scratch_shapes=[pltpu.VMEM((GDN_HALO + R, 3 * W), f32), pltpu.VMEM((H, dk, dv), f32)],
        compiler_params=_params(("parallel", "arbitrary"), est),
        name="gdn_core",
    )(qkv, ab, z, conv_state, S0, conv_w, pad(a_log), pad(dt_bias), norm_g.reshape(1, dv))


def kernel(x_prompt, x_sample, mem_prompt, cache_a_k, cache_a_v, cache_b_k, cache_b_v, cache_b_logf, state_gdn, state_gdn_conv, cache_mem_k, cache_mem_v, norm_g, mem_norm_g, final_norm_g, ffn_w_gate, ffn_w_up, ffn_w_down, xa_w_q, xa_w_k, xa_w_v, xa_w_o, ab_w_in, ab_b_f, ab_rel_bias, ab_w_o, gdn_w_in, gdn_conv_w, gdn_a_log, gdn_dt_bias, gdn_norm_g, gdn_w_o):
    depth = norm_g.shape[0]
    BP, SEQ, D = x_prompt.shape
    BS, MS, _ = x_sample.shape
    assert BP == 1
    HA = HB = ab_b_f.shape[1]
    WA = WB = (ab_w_in.shape[2] - HB) // 6
    DHA = WA // HA
    HG = gdn_a_log.shape[1]
    WG = gdn_w_o.shape[1]
    DKG = WG // HG
    HX = cache_mem_k.shape[3]
    DHX = cache_mem_k.shape[4]
    WX = HX * DHX
    PA = cache_a_k.shape[2]
    PB = cache_b_k.shape[2]
    LANES = V7X_LANES

    xp = x_prompt.reshape(SEQ, D)
    xs = x_sample.reshape(BS * MS, D)

    wgate = ffn_w_gate.astype(bf16)
    wup = ffn_w_up.astype(bf16)
    wdown = ffn_w_down.astype(bf16)
    wxq = xa_w_q.astype(bf16)
    wxo = xa_w_o.astype(bf16)
    wxkv = jnp.concatenate([xa_w_k, xa_w_v], axis=-1).astype(bf16)

    outs = {k: [] for k in ("akp", "avp", "bkp", "bvp", "blp", "sgp", "scp", "mkp", "mvp",
                            "aks", "avs", "bks", "bvs", "bls", "sgs", "scs")}

    for l in range(depth):
        xp = _ffn(xp, norm_g[l, 0], wgate[l, 0], wup[l, 0], wdown[l, 0])
        xs = _ffn(xs, norm_g[l, 0], wgate[l, 0], wup[l, 0], wdown[l, 0])

        if l % 2 == 0:
            e = l // 2
            w_in = ab_w_in[e]
            nq = 3 * WA + 3 * WB
            w_pad = jnp.concatenate([w_in, jnp.zeros((D, LANES - HB), f32)], axis=-1).astype(bf16)
            b_pad = jnp.zeros((1, LANES), f32).at[0, :HB].set(ab_b_f[e].astype(f32))
            spec = [(0, WA, bf16, DHA ** -0.5, None),
                    (WA, WA, f32, 1.0, None), (WA, WA, bf16, 1.0, None),
                    (2 * WA, WA, f32, 1.0, None), (2 * WA, WA, bf16, 1.0, None),
                    (3 * WA, WB, bf16, (WB // HB) ** -0.5, None),
                    (3 * WA + WB, WB, f32, 1.0, None), (3 * WA + WB, WB, bf16, 1.0, None),
                    (3 * WA + 2 * WB, WB, f32, 1.0, None), (3 * WA + 2 * WB, WB, bf16, 1.0, None),
                    (nq, LANES, f32, 1.0, "logsig")]
            wo = ab_w_o[e].astype(bf16)

            spec_a = [(0, WA, bf16, DHA ** -0.5 * LOG2E, None)] + spec[1:5]
            qa, ka, kab, va, vab = _norm_proj(xp, norm_g[l, 1], w_pad[:, :3 * WA], spec_a)
            oa = _band_prompt(qa, kab, vab, _band_bias(ab_rel_bias[e]))
            DHB = WB // HB
            spread = lambda w: jnp.pad(w.reshape(D, HB, DHB), ((0, 0), (0, 0), (0, LANES - DHB))).reshape(D, HB * LANES)
            wq_b = w_in[:, 3 * WA:3 * WA + WB]
            wk_b = w_in[:, 3 * WA + WB:3 * WA + 2 * WB]
            w_b = jnp.concatenate([spread(wq_b), wk_b, spread(wk_b), w_in[:, 3 * WA + 2 * WB:nq],
                                   w_pad[:, nq:]], axis=-1).astype(bf16)
            W2 = HB * LANES
            spec_b = [(0, W2, bf16, DHB ** -0.5 * LOG2E, None),
                      (W2, WB, f32, 1.0, None), (W2 + WB, W2, bf16, 1.0, None),
                      (2 * W2 + WB, WB, f32, 1.0, None), (2 * W2 + WB, WB, bf16, 1.0, None),
                      (2 * W2 + 2 * WB, LANES, f32, 1.0, "logsig")]
            qbs, kb, kbs, vb, vbb, lf = _norm_proj(xp, norm_g[l, 1], w_b, spec_b, bias=b_pad)
            F, _ = _cumsum_rows(lf)
            qaug, kaug, vt = _fox_pack(qbs, kbs, vbb, F, HB)
            ob = _fox_prompt(qaug, kaug, vt, F, HB)
            res_p = [(oa, wo[:WA]), (ob, wo[WA:])]
            keep = min(BAND_CHUNKS * CHUNK, SEQ)
            outs["akp"].append(ka[SEQ - keep:].reshape(1, keep, HA, DHA))
            outs["avp"].append(va[SEQ - keep:].reshape(1, keep, HA, DHA))
            outs["bkp"].append(kb.reshape(1, SEQ, HB, WB // HB))
            outs["bvp"].append(vb.reshape(1, SEQ, HB, WB // HB))
            outs["blp"].append(lf[:, :HB].reshape(1, SEQ, HB))

            qa, ka, _, va, _, qb, kb, _, vb, _, lf = _norm_proj(xs, norm_g[l, 1], w_pad, spec, bias=b_pad)
            bias_s = _toeplitz_bias(ab_rel_bias[e], MS, PA + MS, PA).reshape(HA * MS, PA + MS)
            oa = _sample_attn(qa.reshape(BS, MS, WA), cache_a_k[e].reshape(BS, PA, WA), cache_a_v[e].reshape(BS, PA, WA),
                              HA, "rel", ka.reshape(BS, MS, WA), va.reshape(BS, MS, WA),
                              extra=(bias_s[:, :PA], bias_s[:, PA:]))
            lfn = lf[:, :HB].reshape(BS, MS, HB)
            lcat = jnp.concatenate([cache_b_logf[e].astype(f32), lfn], axis=1)
            LP = -(-(PB + MS) // LANES) * LANES
            lcat = jnp.pad(lcat.transpose(1, 0, 2).reshape(PB + MS, BS * HB), ((0, LP - PB - MS), (0, 0)))
            _, FTs = _cumsum_rows(lcat)
            FTs = FTs.reshape(BS, HB, LP)
            fq = FTs[:, :, PB:PB + MS].reshape(BS, HB * MS, 1)
            ob = _sample_attn(qb.reshape(BS, MS, WB), cache_b_k[e].reshape(BS, PB, WB), cache_b_v[e].reshape(BS, PB, WB),
                              HB, "fox", kb.reshape(BS, MS, WB), vb.reshape(BS, MS, WB),
                              extra=(fq, FTs[:, :, :PB], FTs[:, :, PB:PB + MS]))
            res_s = [(oa.reshape(BS * MS, WA), wo[:WA]), (ob.reshape(BS * MS, WB), wo[WA:])]
            outs["aks"].append(ka.reshape(BS, MS, HA, DHA))
            outs["avs"].append(va.reshape(BS, MS, HA, DHA))
            outs["bks"].append(kb.reshape(BS, MS, HB, WB // HB))
            outs["bvs"].append(vb.reshape(BS, MS, HB, WB // HB))
            outs["bls"].append(lfn)
        else:
            o = l // 2
            w_in = gdn_w_in[o]
            w_pad = jnp.concatenate([w_in, jnp.zeros((D, LANES - 2 * HG), f32)], axis=-1).astype(bf16)
            spec = [(0, 3 * WG, f32, 1.0, None), (3 * WG, WG, f32, 1.0, None), (4 * WG, LANES, f32, 1.0, None)]
            wo = gdn_w_o[o].astype(bf16)

            qg, kg, vg, z, ab, cvp = _gdn_proj(xp, norm_g[l, 1], w_pad, jnp.zeros((1, CONV - 1, 3 * WG), f32),
                                               gdn_conv_w[o], HG)
            y, Sp = _gdn_core_pre(qg, kg, vg, ab, z, jnp.zeros((1, HG, DKG, DKG), f32),
                                  gdn_a_log[o], gdn_dt_bias[o], gdn_norm_g[o], CHUNK, 8)
            res_p = [(y, wo)]
            outs["sgp"].append(Sp)
            outs["scp"].append(cvp)

            qkv, z, ab = _norm_proj(xs, norm_g[l, 1], w_pad, spec)
            y, Ss, cvs = _gdn_core(qkv, ab, z, state_gdn_conv[o].astype(f32), state_gdn[o].astype(f32),
                                   gdn_conv_w[o], gdn_a_log[o], gdn_dt_bias[o], gdn_norm_g[o], MS, 1)
            res_s = [(y, wo)]
            outs["sgs"].append(Ss)
            outs["scs"].append(cvs)

        mk, mkb, mv, mvb = _norm_proj(mem_prompt.reshape(MEM, D), mem_norm_g[l], wxkv[l],
                                      [(0, WX, f32, 1.0, None), (0, WX, bf16, 1.0, None),
                                       (WX, WX, f32, 1.0, None), (WX, WX, bf16, 1.0, None)])
        outs["mkp"].append(mk.reshape(1, MEM, HX, DHX))
        outs["mvp"].append(mv.reshape(1, MEM, HX, DHX))
        qspec = [(0, WX, bf16, DHX ** -0.5, None)]
        xp, ocp = _cross_prompt(xp, norm_g[l, 2], wxq[l], res_p, mkb, mvb, HX)
        xs, q = _norm_proj(xs, norm_g[l, 2], wxq[l], qspec, res=res_s)
        ocs = _cross_sample(q.reshape(BS, MS, WX), cache_mem_k, cache_mem_v, l)

        fin = final_norm_g if l == depth - 1 else None
        xp = _ffn(xp, norm_g[l, 3], wgate[l, 1], wup[l, 1], wdown[l, 1], fin, res=(ocp, wxo[l]))
        xs = _ffn(xs, norm_g[l, 3], wgate[l, 1], wup[l, 1], wdown[l, 1], fin, res=(ocs.reshape(BS * MS, WX), wxo[l]))

    st = lambda k: jnp.stack(outs[k])
    return (xp.reshape(BP, SEQ, D), xs.reshape(BS, MS, D),
            st("akp"), st("avp"), st("bkp"), st("bvp"), st("blp"),
            st("sgp"), st("scp"), st("mkp"), st("mvp"),
            st("aks"), st("avs"), st("bks"), st("bvs"), st("bls"),
            st("sgs"), st("scs"))
```

```python
import functools

import numpy as np
import jax
import jax.numpy as jnp
from jax import lax
from jax.experimental import pallas as pl
from jax.experimental.pallas import tpu as pltpu

f32 = jnp.float32
bf16 = jnp.bfloat16
EPS = 1e-6
NEG = -1e30

V7X_VMEM_BYTES = 64 * 1024 * 1024
V7X_LANES = 128
MIB = 1024 * 1024

CHUNK = 64
BAND_CHUNKS = 8
REL_CLIP = 256
CONV = 4
MEM = 256


def _params(sem, est_bytes):
    limit = int(min(V7X_VMEM_BYTES - 8 * MIB, max(32 * MIB, est_bytes + 8 * MIB)))
    return pltpu.CompilerParams(dimension_semantics=sem, vmem_limit_bytes=limit)


def _rms(x, g):
    ms = jnp.mean(x * x, axis=-1, keepdims=True)
    return x * lax.rsqrt(ms + EPS) * g


def _silu(x):
    h = 0.5 * x
    return h + h * jnp.tanh(h)


def _dot(a, b):
    return jnp.dot(a, b, preferred_element_type=f32)


def _dot_nt(a, b):
    return lax.dot_general(a, b, (((1,), (1,)), ((), ())), preferred_element_type=f32)


def _dot_hi(a, b):
    return jnp.dot(a, b, preferred_element_type=f32, precision=lax.Precision.HIGHEST)


def _split2(a):
    hi = a.astype(bf16)
    return hi, (a - hi.astype(f32)).astype(bf16)


def _split3(a):
    hi, rest = a.astype(bf16), a
    rest = rest - hi.astype(f32)
    mid = rest.astype(bf16)
    return hi, mid, (rest - mid.astype(f32)).astype(bf16)


def _ffn_body(*refs, nj, final, has_res):
    x_ref, g_ref, wg_ref, wu_ref, wd_ref = refs[:5]
    pos = 5
    if has_res:
        a_ref, wo_ref = refs[pos:pos + 2]
        pos += 2
    if final:
        gf_ref = refs[pos]
        pos += 1
    o_ref, h_scr, acc = refs[pos:]
    j = pl.program_id(1)

    @pl.when(j == 0)
    def _():
        x = x_ref[...]
        if has_res:
            x = x + _dot(a_ref[...], wo_ref[...])
        o_ref[...] = x
        h_scr[...] = _rms(x, g_ref[...]).astype(bf16)
        acc[...] = jnp.zeros_like(acc)

    h = h_scr[...]
    a = _silu(_dot(h, wg_ref[...])) * _dot(h, wu_ref[...])
    acc[...] += _dot(a.astype(bf16), wd_ref[...])

    @pl.when(j == nj - 1)
    def _():
        y = o_ref[...] + 0.5 * acc[...]
        if final:
            y = _rms(y, gf_ref[...])
        o_ref[...] = y


def _ffn(x, g, wg, wu, wd, final_g=None, res=None):
    T, D = x.shape
    FF = wg.shape[1]
    TM = min(1024, T)
    TF = min(1024, FF)
    nj = FF // TF
    final = final_g is not None
    in_specs = [pl.BlockSpec((TM, D), lambda i, j: (i, 0)),
                pl.BlockSpec((1, D), lambda i, j: (0, 0)),
                pl.BlockSpec((D, TF), lambda i, j: (0, j)),
                pl.BlockSpec((D, TF), lambda i, j: (0, j)),
                pl.BlockSpec((TF, D), lambda i, j: (j, 0))]
    args = [x, g.reshape(1, D), wg, wu, wd]
    est = 2 * (2 * TM * D * 4) + 2 * 3 * D * TF * 2 + TM * D * 6 + 3 * TM * TF * 4
    if res is not None:
        a, wo = res
        K = a.shape[1]
        in_specs += [pl.BlockSpec((TM, K), lambda i, j: (i, 0)), pl.BlockSpec((K, D), lambda i, j: (0, 0))]
        args += [a, wo]
        est += 2 * TM * K * 2 + 2 * K * D * 2
    if final:
        in_specs.append(pl.BlockSpec((1, D), lambda i, j: (0, 0)))
        args.append(final_g.reshape(1, D))
    return pl.pallas_call(
        functools.partial(_ffn_body, nj=nj, final=final, has_res=res is not None),
        grid=(T // TM, nj),
        in_specs=in_specs,
        out_specs=pl.BlockSpec((TM, D), lambda i, j: (i, 0)),
        out_shape=jax.ShapeDtypeStruct((T, D), f32),
        scratch_shapes=[pltpu.VMEM((TM, D), bf16), pltpu.VMEM((TM, D), f32)],
        compiler_params=_params(("parallel", "arbitrary"), est),
        name="ffn",
    )(*args)


def _log_sigmoid(x):
    return jnp.minimum(x, 0.0) - jnp.log(1.0 + jnp.exp(-jnp.abs(x)))


def _norm_proj_body(*refs, outs, has_bias, nres):
    x_ref, g_ref, w_ref = refs[:3]
    pos = 3
    b_ref = None
    if has_bias:
        b_ref = refs[pos]
        pos += 1
    x = x_ref[...]
    for p in range(nres):
        x = x + _dot(refs[pos][...], refs[pos + 1][...])
        pos += 2
    o_refs = refs[pos:]
    if nres:
        o_refs[0][...] = x
        o_refs = o_refs[1:]
    h = _rms(x, g_ref[...]).astype(bf16)
    cache = {}
    for o_ref, (off, n, dt, scale, act) in zip(o_refs, outs):
        if (off, n) not in cache:
            cache[(off, n)] = _dot(h, w_ref[:, off:off + n])
        r = cache[(off, n)]
        if act == "logsig":
            r = _log_sigmoid(r + b_ref[...])
        if scale != 1.0:
            r = r * scale
        o_ref[...] = r.astype(dt)


def _norm_proj(x, g, w, outs, bias=None, res=()):
    T, D = x.shape
    N = w.shape[1]
    TM = min(512, T)
    in_specs = [pl.BlockSpec((TM, D), lambda i: (i, 0)),
                pl.BlockSpec((1, D), lambda i: (0, 0)),
                pl.BlockSpec((D, N), lambda i: (0, 0))]
    args = [x, g.reshape(1, D), w]
    if bias is not None:
        in_specs.append(pl.BlockSpec((1, bias.shape[-1]), lambda i: (0, 0)))
        args.append(bias)
    est = 2 * TM * D * 4 + 2 * D * N * 2 + sum(2 * TM * n * 4 for (_, n, _, _, _) in outs) + TM * N * 4
    for a, wr in res:
        K = a.shape[1]
        in_specs += [pl.BlockSpec((TM, K), lambda i: (i, 0)), pl.BlockSpec((K, D), lambda i: (0, 0))]
        args += [a, wr]
        est += 2 * TM * K * 2 + 2 * K * D * 2
    out_specs = [pl.BlockSpec((TM, n), lambda i: (i, 0)) for (_, n, _, _, _) in outs]
    out_shape = [jax.ShapeDtypeStruct((T, n), dt) for (_, n, dt, _, _) in outs]
    if res:
        out_specs.insert(0, pl.BlockSpec((TM, D), lambda i: (i, 0)))
        out_shape.insert(0, jax.ShapeDtypeStruct((T, D), f32))
        est += 2 * TM * D * 4
    return pl.pallas_call(
        functools.partial(_norm_proj_body, outs=tuple(outs), has_bias=bias is not None, nres=len(res)),
        grid=(T // TM,),
        in_specs=in_specs,
        out_specs=out_specs,
        out_shape=out_shape,
        compiler_params=_params(("parallel",), est),
        name="norm_proj",
    )(*args)


def _cumsum_body(x_ref, f_ref, ft_ref, carry, *, nsub):
    @pl.when(pl.program_id(0) == 0)
    def _():
        carry[...] = jnp.zeros_like(carry)

    r = lax.broadcasted_iota(jnp.int32, (V7X_LANES, V7X_LANES), 0)
    c = lax.broadcasted_iota(jnp.int32, (V7X_LANES, V7X_LANES), 1)
    tri = (r >= c).astype(f32)
    run = carry[...]
    ncol = x_ref.shape[1] // V7X_LANES
    for sb in range(nsub):
        rows = slice(sb * V7X_LANES, (sb + 1) * V7X_LANES)
        blk = _dot_hi(tri, x_ref[rows, :]) + run
        f_ref[rows, :] = blk
        for cb in range(ncol):
            cols = slice(cb * V7X_LANES, (cb + 1) * V7X_LANES)
            ft_ref[cols, rows] = blk[:, cols].T
        run = blk[V7X_LANES - 1:V7X_LANES, :]
    carry[...] = run


def _cumsum_rows(x):
    L, N = x.shape
    nsub = 8 if L % (8 * V7X_LANES) == 0 else 1
    TB = nsub * V7X_LANES
    return pl.pallas_call(
        functools.partial(_cumsum_body, nsub=nsub),
        grid=(L // TB,),
        in_specs=[pl.BlockSpec((TB, N), lambda i: (i, 0))],
        out_specs=[pl.BlockSpec((TB, N), lambda i: (i, 0)), pl.BlockSpec((N, TB), lambda i: (0, i))],
        out_shape=[jax.ShapeDtypeStruct((L, N), f32), jax.ShapeDtypeStruct((N, L), f32)],
        scratch_shapes=[pltpu.VMEM((1, N), f32)],
        compiler_params=_params(("arbitrary",), 8 * TB * N * 4),
        name="cumsum_rows",
    )(x)


BAND_TQ = 256
BAND_NB = 3


def _band_body(q_ref, k0_ref, k1_ref, k2_ref, v0_ref, v1_ref, v2_ref, b_ref, o_ref, *, nheads):
    TQ = BAND_TQ
    lane = lax.broadcasted_iota(jnp.int32, (TQ, V7X_LANES), 1)
    lo = lane < 64
    krefs = (k0_ref, k1_ref, k2_ref)
    vrefs = (v0_ref, v1_ref, v2_ref)
    for hp in range(nheads // 2):
        cols = slice(hp * V7X_LANES, (hp + 1) * V7X_LANES)
        qp = q_ref[:, cols]
        zq = jnp.zeros_like(qp)
        q2 = jnp.concatenate([jnp.where(lo, qp, zq), jnp.where(lo, zq, qp)], axis=0)
        s = []
        for b in range(BAND_NB):
            blk = slice(b * TQ, (b + 1) * TQ)
            bias2 = jnp.concatenate([b_ref[2 * hp, :, blk], b_ref[2 * hp + 1, :, blk]], axis=0)
            s.append(_dot_nt(q2, krefs[b][:, cols]) + bias2)
        smax = s[0]
        for b in range(1, BAND_NB):
            smax = jnp.maximum(smax, s[b])
        m = jnp.max(smax, axis=-1, keepdims=True)
        p = [jnp.exp2(sb - m) for sb in s]
        psum = p[0]
        for b in range(1, BAND_NB):
            psum = psum + p[b]
        l = psum.sum(axis=-1, keepdims=True)
        pcat, vcat = [], []
        for b in range(BAND_NB):
            vp = vrefs[b][:, cols]
            zv = jnp.zeros_like(vp)
            vcat += [jnp.where(lo, vp, zv), jnp.where(lo, zv, vp)]
            pb = p[b].astype(bf16)
            pcat += [pb[:TQ], pb[TQ:]]
        o = _dot(jnp.concatenate(pcat, axis=1), jnp.concatenate(vcat, axis=0))
        o_ref[:, cols] = (o / jnp.where(lo, l[:TQ], l[TQ:])).astype(bf16)


def _toeplitz_bias(table, nq, nk, off):
    H = table.shape[0]
    m = np.arange(nq + nk - 1)
    idx = np.clip(off + nq - 1 - m, -REL_CLIP, REL_CLIP) + REL_CLIP
    w = jnp.concatenate([table[:, idx].astype(f32), jnp.zeros((H, 1), f32)], axis=-1)
    skew = jnp.tile(w, (1, nq))[:, :nq * (nq + nk - 1)].reshape(H, nq, nq + nk - 1)
    return skew[:, :, nq - 1:]


def _band_bias(table):
    qc = np.arange(BAND_TQ)[:, None] // CHUNK
    ki = np.arange(BAND_NB * BAND_TQ)[None, :]
    kc = ki // CHUNK
    in_band = (kc >= qc) & (kc <= qc + BAND_CHUNKS)
    valid = np.stack([in_band & (ki // BAND_TQ >= BAND_NB - 1 - n) for n in range(BAND_NB)])
    bias = _toeplitz_bias(table, BAND_TQ, BAND_NB * BAND_TQ, BAND_CHUNKS * CHUNK) * LOG2E
    return jnp.where(valid[:, None], bias[None], NEG)


def _band_prompt(q, k, v, bias):
    L, W = q.shape
    TQ = BAND_TQ
    H = bias.shape[1]
    kspec = [pl.BlockSpec((TQ, W), functools.partial(lambda i, d: (jnp.maximum(i - d, 0), 0), d=d))
             for d in (2, 1, 0)]
    est = 2 * 7 * TQ * W * 2 + 2 * H * TQ * BAND_NB * TQ * 4 + 16 * TQ * TQ * 4
    return pl.pallas_call(
        functools.partial(_band_body, nheads=H),
        grid=(L // TQ,),
        in_specs=[pl.BlockSpec((TQ, W), lambda i: (i, 0))] + kspec + kspec
                 + [pl.BlockSpec((None, H, TQ, BAND_NB * TQ), lambda i: (jnp.minimum(i, BAND_NB - 1), 0, 0, 0))],
        out_specs=pl.BlockSpec((TQ, W), lambda i: (i, 0)),
        out_shape=jax.ShapeDtypeStruct((L, W), bf16),
        compiler_params=_params(("parallel",), est),
        name="band_prompt",
    )(q, k, k, k, v, v, v, bias)


FOX_T = 1024
FOX_VROWS = 80
FOX_QPIECE = 64
FOX_KPIECE = 67
LOG2E = 1.4426950408889634


def _fox_pack_body(q_ref, k_ref, v_ref, f_ref, e_ref, qa_ref, ka_ref, vt_ref, *, nheads, dh):
    T, W2 = q_ref.shape
    pieces = _split3((f_ref[...] - f_ref[0:1, :]) * LOG2E)
    aug = _dot(jnp.concatenate(pieces, axis=1), e_ref[...])
    lane = lax.broadcasted_iota(jnp.int32, (T, W2), 1) % V7X_LANES
    ones_q = jnp.where((lane >= FOX_KPIECE) & (lane < FOX_KPIECE + 3), 1.0, 0.0)
    ones_k = jnp.where((lane >= FOX_QPIECE) & (lane < FOX_QPIECE + 3), 1.0, 0.0)
    qa_ref[...] = (q_ref[...].astype(f32) + aug[:, :W2] + ones_q).astype(bf16)
    ka_ref[...] = (k_ref[...].astype(f32) + aug[:, W2:] + ones_k).astype(bf16)
    vT = v_ref[...].astype(f32).T
    pad = FOX_VROWS - dh
    ones_row = jnp.where(lax.broadcasted_iota(jnp.int32, (pad, T), 0) == 0, 1.0, 0.0).astype(bf16)
    for h in range(nheads):
        vt_ref[h, 0:dh, :] = vT[h * dh:(h + 1) * dh, :].astype(bf16)
        vt_ref[h, dh:FOX_VROWS, :] = ones_row


def _fox_pack(q, k, v, F, nheads):
    L, W2 = q.shape
    W = v.shape[1]
    dh = W // nheads
    T = FOX_T
    e = np.zeros((3 * V7X_LANES, 2 * W2), np.float32)
    for h in range(nheads):
        for t in range(3):
            e[t * V7X_LANES + h, h * V7X_LANES + FOX_QPIECE + t] = 1.0
            e[t * V7X_LANES + h, W2 + h * V7X_LANES + FOX_KPIECE + t] = -1.0
    est = 2 * (4 * T * W2 * 2 + T * W * 2 + nheads * FOX_VROWS * T * 2) + 6 * T * W2 * 4
    return pl.pallas_call(
        functools.partial(_fox_pack_body, nheads=nheads, dh=dh),
        grid=(L // T,),
        in_specs=[pl.BlockSpec((T, W2), lambda i: (i, 0)),
                  pl.BlockSpec((T, W2), lambda i: (i, 0)),
                  pl.BlockSpec((T, W), lambda i: (i, 0)),
                  pl.BlockSpec((T, V7X_LANES), lambda i: (i, 0)),
                  pl.BlockSpec((3 * V7X_LANES, 2 * W2), lambda i: (0, 0))],
        out_specs=[pl.BlockSpec((T, W2), lambda i: (i, 0)),
                   pl.BlockSpec((T, W2), lambda i: (i, 0)),
                   pl.BlockSpec((nheads, FOX_VROWS, T), lambda i: (0, 0, i))],
        out_shape=[jax.ShapeDtypeStruct((L, W2), bf16), jax.ShapeDtypeStruct((L, W2), bf16),
                   jax.ShapeDtypeStruct((nheads, FOX_VROWS, L), bf16)],
        compiler_params=_params(("parallel",), est),
        name="fox_pack",
    )(q, k, v, F, jnp.asarray(e, bf16))


def _fox_body(qi_ref, kj_ref, qa_ref, ka_ref, vt_ref, d_ref, o_ref, m_scr, acc_scr, *, nheads, dh):
    s_id = pl.program_id(0)
    i = qi_ref[s_id]
    j = kj_ref[s_id]
    T = FOX_T

    @pl.when(j == 0)
    def _():
        m_scr[...] = jnp.full_like(m_scr, NEG)
        acc_scr[...] = jnp.zeros_like(acc_scr)

    def step(masked):
        if masked:
            keep = lax.broadcasted_iota(jnp.int32, (T, T), 0) <= lax.broadcasted_iota(jnp.int32, (T, T), 1)
        def scores(h):
            cols = slice(h * V7X_LANES, (h + 1) * V7X_LANES)
            st = _dot_nt(ka_ref[:, cols], qa_ref[:, cols])
            return jnp.where(keep, st, NEG) if masked else st

        nxt = scores(0)
        for h in range(nheads):
            st = nxt
            if h + 1 < nheads:
                nxt = scores(h + 1)
            d = d_ref[s_id * nheads + h]
            m_prev = m_scr[h:h + 1, :]
            m_new = jnp.maximum(m_prev, jnp.max(st, axis=0, keepdims=True) + d)
            p = jnp.exp2(st - (m_new - d))
            alpha = jnp.exp2(m_prev - m_new)
            acc_scr[h] = acc_scr[h] * alpha + _dot(vt_ref[h], p.astype(bf16))
            m_scr[h:h + 1, :] = m_new

    @pl.when(j < i)
    def _():
        step(False)

    @pl.when(j == i)
    def _():
        step(True)
        for hp in range(nheads // 2):
            a0 = acc_scr[2 * hp]
            a1 = acc_scr[2 * hp + 1]
            o2 = jnp.concatenate([a0[0:dh, :] / a0[dh:dh + 1, :], a1[0:dh, :] / a1[dh:dh + 1, :]], axis=0)
            o_ref[:, hp * V7X_LANES:(hp + 1) * V7X_LANES] = o2.T.astype(bf16)


def _fox_prompt(qa, ka, vt, F, nheads):
    L, W2 = qa.shape
    T = FOX_T
    n = L // T
    dh = V7X_LANES // 2
    W = nheads * dh
    pairs = [(i, j) for i in range(n) for j in range(i + 1)]
    qi = np.array([p[0] for p in pairs], np.int32)
    kj = np.array([p[1] for p in pairs], np.int32)
    fs = F[::T, :nheads]
    d = ((fs[qi] - fs[kj]) * LOG2E).reshape(-1)
    grid_spec = pltpu.PrefetchScalarGridSpec(
        num_scalar_prefetch=2,
        grid=(len(pairs),),
        in_specs=[pl.BlockSpec((T, W2), lambda s, qi, kj: (qi[s], 0)),
                  pl.BlockSpec((T, W2), lambda s, qi, kj: (kj[s], 0)),
                  pl.BlockSpec((nheads, FOX_VROWS, T), lambda s, qi, kj: (0, 0, kj[s])),
                  pl.BlockSpec(memory_space=pltpu.SMEM)],
        out_specs=pl.BlockSpec((T, W), lambda s, qi, kj: (qi[s], 0)),
        scratch_shapes=[pltpu.VMEM((nheads, T), f32), pltpu.VMEM((nheads, FOX_VROWS, T), f32)],
    )
    est = 2 * (2 * T * W2 * 2 + nheads * FOX_VROWS * T * 2 + T * W * 2) + nheads * FOX_VROWS * T * 4 + 24 * T * T * 4
    return pl.pallas_call(
        functools.partial(_fox_body, nheads=nheads, dh=dh),
        grid_spec=grid_spec,
        out_shape=jax.ShapeDtypeStruct((L, W), bf16),
        compiler_params=_params(("arbitrary",), est),
        name="fox_prompt",
    )(jnp.asarray(qi), jnp.asarray(kj), qa, ka, vt, d)


def _cross_body(*refs, nres, nheads, scale):
    x_ref, g_ref, wq_ref = refs[:3]
    x = x_ref[...]
    for p in range(nres):
        x = x + _dot(refs[3 + 2 * p][...], refs[4 + 2 * p][...])
    k_ref, v_ref, xo_ref, o_ref = refs[3 + 2 * nres:]
    xo_ref[...] = x
    q = (_dot(_rms(x, g_ref[...]).astype(bf16), wq_ref[...]) * scale).astype(bf16)
    dh = q.shape[1] // nheads
    hs = range(nheads)
    s = [_dot_nt(q[:, h * dh:(h + 1) * dh], k_ref[:, h * dh:(h + 1) * dh]) for h in hs]
    m = [jnp.max(sh, axis=-1, keepdims=True) for sh in s]
    p = [jnp.exp(s[h] - m[h]) for h in hs]
    l = [ph.sum(axis=-1, keepdims=True) for ph in p]
    o = [_dot(p[h].astype(bf16), v_ref[:, h * dh:(h + 1) * dh]) / l[h] for h in hs]
    o_ref[...] = jnp.concatenate(o, axis=-1).astype(bf16)


def _cross_prompt(x, g, wq, res, mk, mv, nheads):
    T, D = x.shape
    W = wq.shape[1]
    S = mk.shape[0]
    TM = min(512, T)
    row = lambda width: pl.BlockSpec((TM, width), lambda i: (i, 0))
    whole = lambda r, c: pl.BlockSpec((r, c), lambda i: (0, 0))
    in_specs = [row(D), whole(1, D), whole(D, W)]
    args = [x, g.reshape(1, D), wq]
    est = 4 * TM * D * 4 + 2 * D * W * 2 + 4 * S * W * 2 + 2 * TM * W * 2 + 3 * TM * W * 4 + 8 * TM * S * 4
    for a, wr in res:
        K = a.shape[1]
        in_specs += [row(K), whole(K, D)]
        args += [a, wr]
        est += 2 * TM * K * 2 + 2 * K * D * 2
    in_specs += [whole(S, W), whole(S, W)]
    args += [mk, mv]
    return pl.pallas_call(
        functools.partial(_cross_body, nres=len(res), nheads=nheads, scale=(W // nheads) ** -0.5),
        grid=(T // TM,),
        in_specs=in_specs,
        out_specs=[row(D), row(W)],
        out_shape=[jax.ShapeDtypeStruct((T, D), f32), jax.ShapeDtypeStruct((T, W), bf16)],
        compiler_params=_params(("parallel",), est),
        name="cross_prompt",
    )(*args)


def _sample_attn_body(*refs, nheads, mode):
    q_ref, kc_ref, vc_ref, kn_ref, vn_ref = refs[:5]
    if mode == "rel":
        bc_ref, bn_ref, o_ref = refs[5:]
    else:
        fq_ref, fkc_ref, fkn_ref, o_ref = refs[5:]

    m_q, W = q_ref.shape
    dh = W // nheads
    R = nheads * m_q
    q = q_ref[...].astype(f32)
    qt = jnp.concatenate([q] * nheads, axis=0)
    hrow = lax.broadcasted_iota(jnp.int32, (R, W), 0) // m_q
    hlane = lax.broadcasted_iota(jnp.int32, (R, W), 1) // dh
    qbd = jnp.where(hrow == hlane, qt, 0.0).astype(bf16)

    def rows_of(f_ref):
        n = f_ref.shape[-1]
        return jnp.concatenate([jnp.broadcast_to(f_ref[h:h + 1, :], (m_q, n)) for h in range(nheads)], axis=0)

    sc = _dot_nt(qbd, kc_ref[...].astype(bf16))
    sn = _dot_nt(qbd, kn_ref[...].astype(bf16))
    if mode == "rel":
        sc = sc + bc_ref[...]
        sn = sn + bn_ref[...]
    else:
        sc = sc + (fq_ref[...] - rows_of(fkc_ref))
        sn = sn + (fq_ref[...] - rows_of(fkn_ref))
        qpos = lax.broadcasted_iota(jnp.int32, (R, m_q), 0) % m_q
        kpos = lax.broadcasted_iota(jnp.int32, (R, m_q), 1)
        sn = jnp.where(kpos <= qpos, sn, NEG)
    m = jnp.maximum(jnp.max(sc, axis=-1, keepdims=True), jnp.max(sn, axis=-1, keepdims=True))
    pc = jnp.exp(sc - m)
    pn = jnp.exp(sn - m)
    l = pc.sum(axis=-1, keepdims=True) + pn.sum(axis=-1, keepdims=True)
    o = _dot(pc.astype(bf16), vc_ref[...].astype(bf16)) + _dot(pn.astype(bf16), vn_ref[...].astype(bf16))
    o = o / l
    hl = lax.broadcasted_iota(jnp.int32, (m_q, W), 1) // dh
    out = jnp.zeros((m_q, W), f32)
    for h in range(nheads):
        out = out + jnp.where(hl == h, o[h * m_q:(h + 1) * m_q, :], 0.0)
    o_ref[...] = out.astype(bf16)


def _sample_attn(q, kc, vc, nheads, mode, kn, vn, extra):
    B, m_q, W = q.shape
    P = kc.shape[1]
    R = nheads * m_q
    per_b = lambda *shape: pl.BlockSpec((None,) + shape, lambda b: (b,) + (0,) * len(shape))
    shared = lambda *shape: pl.BlockSpec(shape, lambda b: (0,) * len(shape))
    in_specs = [per_b(m_q, W), per_b(P, W), per_b(P, W), per_b(m_q, W), per_b(m_q, W)]
    if mode == "rel":
        in_specs += [shared(R, P), shared(R, m_q)]
    else:
        in_specs += [per_b(R, 1), per_b(nheads, P), per_b(nheads, m_q)]
    est = 2 * 2 * P * W * 4 + 2 * P * W * 2 + 6 * R * P * 4 + 4 * R * W * 4
    return pl.pallas_call(
        functools.partial(_sample_attn_body, nheads=nheads, mode=mode),
        grid=(B,),
        in_specs=in_specs,
        out_specs=per_b(m_q, W),
        out_shape=jax.ShapeDtypeStruct((B, m_q, W), bf16),
        compiler_params=_params(("parallel",), est),
        name="sample_attn_" + mode,
    )(q, kc, vc, kn, vn, *extra)


def _cross_sample_body(q_ref, kc_ref, vc_ref, o_ref, *, nheads):
    m_q, W = q_ref.shape
    dh = W // nheads
    npc = dh // kc_ref.shape[1]
    P = kc_ref.shape[0] // (nheads * npc)
    hs = range(nheads)

    def head_rows(ref, h):
        pieces = [ref[pl.ds(h * npc + j, P, stride=nheads * npc), :] for j in range(npc)]
        return jnp.concatenate(pieces, axis=-1).astype(bf16)

    q = q_ref[...]
    sc = [_dot_nt(q[:, h * dh:(h + 1) * dh], head_rows(kc_ref, h)) for h in hs]
    m = [jnp.max(s, axis=-1, keepdims=True) for s in sc]
    pc = [jnp.exp(sc[h] - m[h]) for h in hs]
    l = [p.sum(axis=-1, keepdims=True) for p in pc]
    o = [_dot(pc[h].astype(bf16), head_rows(vc_ref, h)) / l[h] for h in hs]
    o_ref[...] = jnp.concatenate(o, axis=-1).astype(bf16)


def _cross_sample(q, kc_all, vc_all, layer):
    B, m_q, W = q.shape
    nl, _, P, nheads, dh = kc_all.shape
    nrows = P * nheads * (dh // V7X_LANES)
    kc_all = kc_all.reshape(nl, B, nrows, V7X_LANES)
    vc_all = vc_all.reshape(nl, B, nrows, V7X_LANES)
    cache = pl.BlockSpec((None, None, nrows, V7X_LANES), lambda b: (layer, b, 0, 0))
    qo = pl.BlockSpec((None, m_q, W), lambda b: (b, 0, 0))
    est = 2 * 2 * nrows * V7X_LANES * 4 + 4 * P * W * 2 + 8 * nheads * m_q * P * 4
    return pl.pallas_call(
        functools.partial(_cross_sample_body, nheads=nheads),
        grid=(B,),
        in_specs=[qo, cache, cache],
        out_specs=qo,
        out_shape=jax.ShapeDtypeStruct((B, m_q, W), bf16),
        compiler_params=_params(("parallel",), est),
        name="cross_sample",
    )(q, kc_all, vc_all)


GDN_HALO = 8


def _softplus(x):
    return jnp.maximum(x, 0.0) + jnp.log(1.0 + jnp.exp(-jnp.abs(x)))


def _gdn_conv_qkv(xbuf, R, cw_ref, nheads, dk):
    H0 = GDN_HALO
    tail = xbuf[H0 + R - (CONV - 1):H0 + R, :]

    def act_cols(c0):
        cols = slice(c0, c0 + dk)
        full = xbuf[0:H0 + R, cols]
        conv = pltpu.roll(full, CONV - 1, 0)[H0:, :] * cw_ref[0:1, cols]
        for jj in range(1, CONV):
            sh = CONV - 1 - jj
            conv = conv + (pltpu.roll(full, sh, 0) if sh else full)[H0:, :] * cw_ref[jj:jj + 1, cols]
        return _silu(conv)

    def l2n(a):
        return a * lax.rsqrt(jnp.sum(a * a, axis=-1, keepdims=True) + EPS)

    W = nheads * dk
    qn = [l2n(act_cols(h * dk)) * (dk ** -0.5) for h in range(nheads)]
    kn = [l2n(act_cols(W + h * dk)) for h in range(nheads)]
    vn = [act_cols(2 * W + h * dk) for h in range(nheads)]
    return qn, kn, vn, tail


def _gdn_proj_body(x_ref, g_ref, w_ref, cs_ref, cw_ref, q_ref, k_ref, v_ref, z_ref, ab_ref, cf_ref, xbuf,
                   *, nsteps, nheads, dk):
    i = pl.program_id(0)
    W = nheads * dk
    TM = x_ref.shape[0]
    H0 = GDN_HALO

    @pl.when(i == 0)
    def _():
        xbuf[...] = jnp.zeros_like(xbuf)

    qn, kn, vn, tail = _gdn_conv_qkv(xbuf, TM, cw_ref, nheads, dk)
    q_ref[...] = jnp.concatenate(qn, axis=-1)
    k_ref[...] = jnp.concatenate(kn, axis=-1)
    v_ref[...] = jnp.concatenate(vn, axis=-1)

    xbuf[H0 - (CONV - 1):H0, :] = jnp.where(i == 0, cs_ref[...], tail)
    cf_ref[...] = tail

    h = _rms(x_ref[...], g_ref[...]).astype(bf16)
    xbuf[H0:H0 + TM, :] = _dot(h, w_ref[:, :3 * W])
    z_ref[...] = _dot(h, w_ref[:, 3 * W:4 * W])
    ab_ref[...] = _dot(h, w_ref[:, 4 * W:])


def _gdn_proj(x, g, w, conv_state, conv_w, nheads):
    T, D = x.shape
    N = w.shape[1]
    W = (N - V7X_LANES) // 4
    dk = W // nheads
    TM = min(512, T)
    nsteps = T // TM
    row = lambda width: pl.BlockSpec((TM, width), lambda i: (jnp.minimum(i, nsteps - 1), 0))
    late = lambda width: pl.BlockSpec((TM, width), lambda i: (jnp.maximum(i - 1, 0), 0))
    whole = lambda *shape: pl.BlockSpec(shape, lambda i: (0,) * len(shape))
    est = 2 * TM * D * 4 + 2 * D * N * 2 + 2 * TM * (4 * W + V7X_LANES) * 4 + (GDN_HALO + TM) * 3 * W * 4 + 4 * TM * 3 * W * 4
    q, k, v, z, ab, cf = pl.pallas_call(
        functools.partial(_gdn_proj_body, nsteps=nsteps, nheads=nheads, dk=dk),
        grid=(nsteps + 1,),
        in_specs=[row(D), whole(1, D), whole(D, N),
                  pl.BlockSpec((None, CONV - 1, 3 * W), lambda i: (0, 0, 0)), whole(CONV, 3 * W)],
        out_specs=[late(W), late(W), late(W), row(W), row(V7X_LANES),
                   pl.BlockSpec((None, CONV - 1, 3 * W), lambda i: (0, 0, 0))],
        out_shape=[jax.ShapeDtypeStruct((T, W), f32)] * 4
                  + [jax.ShapeDtypeStruct((T, V7X_LANES), f32), jax.ShapeDtypeStruct((1, CONV - 1, 3 * W), f32)],
        scratch_shapes=[pltpu.VMEM((GDN_HALO + TM, 3 * W), f32)],
        compiler_params=_params(("arbitrary",), est),
        name="gdn_proj",
    )(x, g.reshape(1, D), w, conv_state, conv_w)
    return q, k, v, z, ab, cf


def _gdn_body(*refs, C, NS, nsteps, nheads, dk, pre):
    if pre:
        q_ref, k_ref, v_ref, ab_ref, z_ref, s0_ref, alog_ref, dtb_ref, ng_ref, o_ref, sf_ref, s_scr = refs
    else:
        (x_ref, ab_ref, z_ref, cs_ref, s0_ref, cw_ref, alog_ref, dtb_ref, ng_ref,
         o_ref, sf_ref, cf_ref, xbuf, s_scr) = refs
    c = pl.program_id(1)
    R = NS * C
    G = V7X_LANES // C
    NG = nheads // G
    GW = G * dk
    hs = range(nheads)

    @pl.when(c == 0)
    def _():
        s_scr[...] = s0_ref[...]

    if pre:
        qn = [q_ref[:, h * dk:(h + 1) * dk] for h in hs]
        kn = [k_ref[:, h * dk:(h + 1) * dk] for h in hs]
        vn = [v_ref[:, h * dk:(h + 1) * dk] for h in hs]
    else:
        H0 = GDN_HALO

        @pl.when(c == 0)
        def _():
            xbuf[H0 - (CONV - 1):H0, :] = cs_ref[...]

        xbuf[H0:H0 + R, :] = x_ref[...]
        qn, kn, vn, tail = _gdn_conv_qkv(xbuf, R, cw_ref, nheads, dk)
        xbuf[H0 - (CONV - 1):H0, :] = tail

        @pl.when(c == nsteps - 1)
        def _():
            cf_ref[...] = tail

    ab = ab_ref[...]
    gfull = -jnp.exp(alog_ref[...]) * _softplus(ab + dtb_ref[...])
    bfull = 1.0 / (1.0 + jnp.exp(-ab))
    z = z_ref[...]
    ng = ng_ref[...]

    r_cc = lax.broadcasted_iota(jnp.int32, (C, C), 0)
    c_cc = lax.broadcasted_iota(jnp.int32, (C, C), 1)
    tri_f = (r_cc >= c_cc).astype(f32)
    ri = lax.broadcasted_iota(jnp.int32, (C, V7X_LANES), 0)
    jl = lax.broadcasted_iota(jnp.int32, (C, V7X_LANES), 1) % C
    lblk = lax.broadcasted_iota(jnp.int32, (C, V7X_LANES), 1) // C
    tri_g = ri >= jl
    strict_g = ri > jl
    eye_g = ri == jl
    r128 = lax.broadcasted_iota(jnp.int32, (V7X_LANES, V7X_LANES), 0)
    l128 = lax.broadcasted_iota(jnp.int32, (V7X_LANES, V7X_LANES), 1)
    same_blk = (r128 // C) == (l128 // C)
    wide_blk = (lax.broadcasted_iota(jnp.int32, (V7X_LANES, GW), 0) // C
                == lax.broadcasted_iota(jnp.int32, (V7X_LANES, GW), 1) // dk)

    def bd_sq(p):
        return jnp.where(same_blk, jnp.concatenate([p] * G, axis=0), jnp.zeros((), p.dtype))

    def bd_wide(xs):
        row = jnp.concatenate(xs, axis=1)
        return jnp.where(wide_blk, jnp.concatenate([row] * G, axis=0), 0.0).astype(bf16)

    def gmm3(xs, pg):
        n = len(xs)
        parts = [_split2(x) for x in xs]
        ph, plo = _split2(pg)
        top = _dot(jnp.concatenate([p for hl in parts for p in hl], axis=0), bd_sq(ph))
        bot = _dot(jnp.concatenate([hl[0] for hl in parts], axis=0), bd_sq(plo)) if n > 1 else _dot(parts[0][0], bd_sq(plo))
        return [top[2 * k * C:(2 * k + 1) * C] + (bot[k * C:(k + 1) * C] + top[(2 * k + 1) * C:(2 * k + 2) * C])
                for k in range(n)]

    sls = [slice(s * C, (s + 1) * C) for s in range(NS)]
    gcs = [_dot_hi(tri_f, gfull[sl]) for sl in sls]
    egcs = [jnp.exp(gc) for gc in gcs]
    kdecs = [jnp.exp(gc[C - 1:C, :] - gc) for gc in gcs]
    gcTs = [jnp.concatenate([gc] * G, axis=0).T for gc in gcs]
    bcols_s = [[bfull[sl, nheads + h:nheads + h + 1] for h in hs] for sl in sls]
    ecols_s = [[egc[:, h:h + 1] for h in hs] for egc in egcs]
    kbs_s = [[kn[h][sls[s]] * bcols_s[s][h] for h in hs] for s in range(NS)]
    Ms, Aqks = [], []
    for s in range(NS):
        sl, gc, gcT, kbs = sls[s], gcs[s], gcTs[s], kbs_s[s]
        for grp in range(NG):
            heads = range(grp * G, (grp + 1) * G)
            gcol = jnp.broadcast_to(gc[:, grp * G:grp * G + 1], (C, V7X_LANES))
            grow = gcT[grp * G:grp * G + 1, :]
            for g in range(1, G):
                h = grp * G + g
                gcol = jnp.where(lblk == g, gc[:, h:h + 1], gcol)
                grow = jnp.where(lblk[0:1, :] == g, gcT[h:h + 1, :], grow)
            Lm = jnp.where(tri_g, jnp.exp(jnp.where(tri_g, gcol - grow, 0.0)), 0.0)
            kbd = bd_wide([kn[h][sl] for h in heads])
            kb_row = jnp.concatenate([kbs[h] for h in heads], axis=1)
            q_row = jnp.concatenate([qn[h][sl] for h in heads], axis=1)
            kq = _dot_nt(jnp.concatenate([kb_row, q_row], axis=0).astype(bf16), kbd)
            Ms.append(jnp.where(strict_g, kq[:C] * Lm, 0.0))
            Aqks.append(jnp.where(tri_g, kq[C:] * Lm, 0.0).astype(bf16))
    Xs = [jnp.where(eye_g, 1.0, 0.0) - M for M in Ms]
    Pws = [gmm3([M], M)[0] for M in Ms]
    e = 2
    while e < C:
        e *= 2
        if e < C:
            nxt = [gmm3([X, Pw], Pw) for X, Pw in zip(Xs, Pws)]
            Xs = [X + r[0] for X, r in zip(Xs, nxt)]
            Pws = [r[1] for r in nxt]
        else:
            Xs = [X + gmm3([X], Pw)[0] for X, Pw in zip(Xs, Pws)]
    prep = []
    for s in range(NS):
        sl, bcols, ecols, kbs = sls[s], bcols_s[s], ecols_s[s], kbs_s[s]
        us, ws = [], []
        for grp in range(NG):
            heads = range(grp * G, (grp + 1) * G)
            Tm = Xs[s * NG + grp].astype(bf16)
            U = _dot(Tm, bd_wide([vn[h][sl] * bcols[h] for h in heads]))
            Wm = _dot(Tm, bd_wide([kbs[h] * ecols[h] for h in heads]))
            for g, h in enumerate(heads):
                us.append(U[:, g * dk:(g + 1) * dk])
                ws.append(Wm[:, g * dk:(g + 1) * dk].astype(bf16))
        qgs = [(qn[h][sl] * ecols[h]).astype(bf16) for h in hs]
        kgTs = [(kn[h][sl] * kdecs[s][:, h:h + 1]).T.astype(bf16) for h in hs]
        decs = [egcs[s][C - 1:C, h:h + 1] for h in hs]
        wqs = [jnp.concatenate([ws[h], qgs[h]], axis=0) for h in hs]
        prep.append((us, wqs, Aqks[s * NG:(s + 1) * NG], kgTs, decs))

    Ss = [s_scr[h] for h in hs]
    for s in range(NS):
        us, wqs, Aqks, kgTs, decs = prep[s]
        wqS = [_dot(wqs[h], Ss[h].astype(bf16)) for h in hs]
        vnews = [us[h] - wqS[h][:C] for h in hs]
        vnbs = [v.astype(bf16) for v in vnews]
        intra = [_dot(Aqks[grp], bd_wide([vnews[h] for h in range(grp * G, (grp + 1) * G)])) for grp in range(NG)]
        os_ = [wqS[h][C:] + intra[h // G][:, (h % G) * dk:(h % G + 1) * dk] for h in hs]
        Ss = [Ss[h] * decs[h] + _dot(kgTs[h], vnbs[h]) for h in hs]
        zs = z[s * C:(s + 1) * C, :]
        ys = [(_rms(os_[h], ng) * _silu(zs[:, h * dk:(h + 1) * dk])).astype(bf16) for h in hs]
        o_ref[s * C:(s + 1) * C, :] = jnp.concatenate(ys, axis=-1)
    for h in hs:
        s_scr[h] = Ss[h]

    @pl.when(c == nsteps - 1)
    def _():
        sf_ref[...] = s_scr[...]


def _gdn_core_pre(q, k, v, ab, z, S0, a_log, dt_bias, norm_g, C, NS):
    T = q.shape[0]
    B, H, dk, dv = S0.shape
    W = H * dk
    R = NS * C
    nsteps = T // (B * R)
    pad = lambda vec: jnp.zeros((1, V7X_LANES), f32).at[0, :H].set(vec.astype(f32))
    row = lambda width: pl.BlockSpec((R, width), lambda b, c: (b * nsteps + c, 0))
    lane_row = pl.BlockSpec((1, V7X_LANES), lambda b, c: (0, 0))
    state = pl.BlockSpec((None, H, dk, dv), lambda b, c: (b, 0, 0, 0))
    est = 2 * R * (4 * W + V7X_LANES) * 4 + 3 * H * dk * dv * 4 * 2 + 16 * R * W * 4
    return pl.pallas_call(
        functools.partial(_gdn_body, C=C, NS=NS, nsteps=nsteps, nheads=H, dk=dk, pre=True),
        grid=(B, nsteps),
        in_specs=[row(W), row(W), row(W), row(V7X_LANES), row(W), state, lane_row, lane_row,
                  pl.BlockSpec((1, dv), lambda b, c: (0, 0))],
        out_specs=[row(W), state],
        out_shape=[jax.ShapeDtypeStruct((T, W), bf16), jax.ShapeDtypeStruct((B, H, dk, dv), f32)],
        scratch_shapes=[pltpu.VMEM((H, dk, dv), f32)],
        compiler_params=_params(("parallel", "arbitrary"), est),
        name="gdn_core_pre",
    )(q, k, v, ab, z, S0, pad(a_log), pad(dt_bias), norm_g.reshape(1, dv))


def _gdn_core(qkv, ab, z, conv_state, S0, conv_w, a_log, dt_bias, norm_g, C, NS):
    T = qkv.shape[0]
    B, H, dk, dv = S0.shape
    W = H * dk
    R = NS * C
    nsteps = T // (B * R)
    pad = lambda vec: jnp.zeros((1, V7X_LANES), f32).at[0, :H].set(vec.astype(f32))
    row = lambda width: pl.BlockSpec((R, width), lambda b, c: (b * nsteps + c, 0))
    est = 2 * R * (3 * W + W + V7X_LANES) * 4 + 3 * H * dk * dv * 4 * 2 + 12 * R * 3 * W * 4
    return pl.pallas_call(
        functools.partial(_gdn_body, C=C, NS=NS, nsteps=nsteps, nheads=H, dk=dk, pre=False),
        grid=(B, nsteps),
        in_specs=[row(3 * W), row(V7X_LANES), row(W),
                  pl.BlockSpec((None, CONV - 1, 3 * W), lambda b, c: (b, 0, 0)),
                  pl.BlockSpec((None, H, dk, dv), lambda b, c: (b, 0, 0, 0)),
                  pl.BlockSpec((CONV, 3 * W), lambda b, c: (0, 0)),
                  pl.BlockSpec((1, V7X_LANES), lambda b, c: (0, 0)),
                  pl.BlockSpec((1, V7X_LANES), lambda b, c: (0, 0)),
                  pl.BlockSpec((1, dv), lambda b, c: (0, 0))],
        out_specs=[row(W),
                   pl.BlockSpec((None, H, dk, dv), lambda b, c: (b, 0, 0, 0)),
                   pl.BlockSpec((None, CONV - 1, 3 * W), lambda b, c: (b, 0, 0))],
        out_shape=[jax.ShapeDtypeStruct((T, W), bf16),
                   jax.ShapeDtypeStruct((B, H, dk, dv), f32),
                   jax.ShapeDtypeStruct((B, CONV - 1, 3 * W), f32)],
        scratch_shapes=[pltpu.VMEM((GDN_HALO + R, 3 * W), f32), pltpu.VMEM((H, dk, dv), f32)],
        compiler_params=_params(("parallel", "arbitrary"), est),
        name="gdn_core",
    )(qkv, ab, z, conv_state, S0, conv_w, pad(a_log), pad(dt_bias), norm_g.reshape(1, dv))


def kernel(x_prompt, x_sample, mem_prompt, cache_a_k, cache_a_v, cache_b_k, cache_b_v, cache_b_logf, state_gdn, state_gdn_conv, cache_mem_k, cache_mem_v, norm_g, mem_norm_g, final_norm_g, ffn_w_gate, ffn_w_up, ffn_w_down, xa_w_q, xa_w_k, xa_w_v, xa_w_o, ab_w_in, ab_b_f, ab_rel_bias, ab_w_o, gdn_w_in, gdn_conv_w, gdn_a_log, gdn_dt_bias, gdn_norm_g, gdn_w_o):
    depth = norm_g.shape[0]
    BP, SEQ, D = x_prompt.shape
    BS, MS, _ = x_sample.shape
    assert BP == 1
    HA = HB = ab_b_f.shape[1]
    WA = WB = (ab_w_in.shape[2] - HB) // 6
    DHA = WA // HA
    HG = gdn_a_log.shape[1]
    WG = gdn_w_o.shape[1]
    DKG = WG // HG
    HX = cache_mem_k.shape[3]
    DHX = cache_mem_k.shape[4]
    WX = HX * DHX
    PA = cache_a_k.shape[2]
    PB = cache_b_k.shape[2]
    LANES = V7X_LANES

    xp = x_prompt.reshape(SEQ, D)
    xs = x_sample.reshape(BS * MS, D)

    wgate = ffn_w_gate.astype(bf16)
    wup = ffn_w_up.astype(bf16)
    wdown = ffn_w_down.astype(bf16)
    wxq = xa_w_q.astype(bf16)
    wxo = xa_w_o.astype(bf16)
    wxkv = jnp.concatenate([xa_w_k, xa_w_v], axis=-1).astype(bf16)

    outs = {k: [] for k in ("akp", "avp", "bkp", "bvp", "blp", "sgp", "scp", "mkp", "mvp",
                            "aks", "avs", "bks", "bvs", "bls", "sgs", "scs")}

    for l in range(depth):
        xp = _ffn(xp, norm_g[l, 0], wgate[l, 0], wup[l, 0], wdown[l, 0])
        xs = _ffn(xs, norm_g[l, 0], wgate[l, 0], wup[l, 0], wdown[l, 0])

        if l % 2 == 0:
            e = l // 2
            w_in = ab_w_in[e]
            nq = 3 * WA + 3 * WB
            w_pad = jnp.concatenate([w_in, jnp.zeros((D, LANES - HB), f32)], axis=-1).astype(bf16)
            b_pad = jnp.zeros((1, LANES), f32).at[0, :HB].set(ab_b_f[e].astype(f32))
            spec = [(0, WA, bf16, DHA ** -0.5, None),
                    (WA, WA, f32, 1.0, None), (WA, WA, bf16, 1.0, None),
                    (2 * WA, WA, f32, 1.0, None), (2 * WA, WA, bf16, 1.0, None),
                    (3 * WA, WB, bf16, (WB // HB) ** -0.5, None),
                    (3 * WA + WB, WB, f32, 1.0, None), (3 * WA + WB, WB, bf16, 1.0, None),
                    (3 * WA + 2 * WB, WB, f32, 1.0, None), (3 * WA + 2 * WB, WB, bf16, 1.0, None),
                    (nq, LANES, f32, 1.0, "logsig")]
            wo = ab_w_o[e].astype(bf16)

            spec_a = [(0, WA, bf16, DHA ** -0.5 * LOG2E, None)] + spec[1:5]
            qa, ka, kab, va, vab = _norm_proj(xp, norm_g[l, 1], w_pad[:, :3 * WA], spec_a)
            oa = _band_prompt(qa, kab, vab, _band_bias(ab_rel_bias[e]))
            DHB = WB // HB
            spread = lambda w: jnp.pad(w.reshape(D, HB, DHB), ((0, 0), (0, 0), (0, LANES - DHB))).reshape(D, HB * LANES)
            wq_b = w_in[:, 3 * WA:3 * WA + WB]
            wk_b = w_in[:, 3 * WA + WB:3 * WA + 2 * WB]
            w_b = jnp.concatenate([spread(wq_b), wk_b, spread(wk_b), w_in[:, 3 * WA + 2 * WB:nq],
                                   w_pad[:, nq:]], axis=-1).astype(bf16)
            W2 = HB * LANES
            spec_b = [(0, W2, bf16, DHB ** -0.5 * LOG2E, None),
                      (W2, WB, f32, 1.0, None), (W2 + WB, W2, bf16, 1.0, None),
                      (2 * W2 + WB, WB, f32, 1.0, None), (2 * W2 + WB, WB, bf16, 1.0, None),
                      (2 * W2 + 2 * WB, LANES, f32, 1.0, "logsig")]
            qbs, kb, kbs, vb, vbb, lf = _norm_proj(xp, norm_g[l, 1], w_b, spec_b, bias=b_pad)
            F, _ = _cumsum_rows(lf)
            qaug, kaug, vt = _fox_pack(qbs, kbs, vbb, F, HB)
            ob = _fox_prompt(qaug, kaug, vt, F, HB)
            res_p = [(oa, wo[:WA]), (ob, wo[WA:])]
            keep = min(BAND_CHUNKS * CHUNK, SEQ)
            outs["akp"].append(ka[SEQ - keep:].reshape(1, keep, HA, DHA))
            outs["avp"].append(va[SEQ - keep:].reshape(1, keep, HA, DHA))
            outs["bkp"].append(kb.reshape(1, SEQ, HB, WB // HB))
            outs["bvp"].append(vb.reshape(1, SEQ, HB, WB // HB))
            outs["blp"].append(lf[:, :HB].reshape(1, SEQ, HB))

            qa, ka, _, va, _, qb, kb, _, vb, _, lf = _norm_proj(xs, norm_g[l, 1], w_pad, spec, bias=b_pad)
            bias_s = _toeplitz_bias(ab_rel_bias[e], MS, PA + MS, PA).reshape(HA * MS, PA + MS)
            oa = _sample_attn(qa.reshape(BS, MS, WA), cache_a_k[e].reshape(BS, PA, WA), cache_a_v[e].reshape(BS, PA, WA),
                              HA, "rel", ka.reshape(BS, MS, WA), va.reshape(BS, MS, WA),
                              extra=(bias_s[:, :PA], bias_s[:, PA:]))
            lfn = lf[:, :HB].reshape(BS, MS, HB)
            lcat = jnp.concatenate([cache_b_logf[e].astype(f32), lfn], axis=1)
            LP = -(-(PB + MS) // LANES) * LANES
            lcat = jnp.pad(lcat.transpose(1, 0, 2).reshape(PB + MS, BS * HB), ((0, LP - PB - MS), (0, 0)))
            _, FTs = _cumsum_rows(lcat)
            FTs = FTs.reshape(BS, HB, LP)
            fq = FTs[:, :, PB:PB + MS].reshape(BS, HB * MS, 1)
            ob = _sample_attn(qb.reshape(BS, MS, WB), cache_b_k[e].reshape(BS, PB, WB), cache_b_v[e].reshape(BS, PB, WB),
                              HB, "fox", kb.reshape(BS, MS, WB), vb.reshape(BS, MS, WB),
                              extra=(fq, FTs[:, :, :PB], FTs[:, :, PB:PB + MS]))
            res_s = [(oa.reshape(BS * MS, WA), wo[:WA]), (ob.reshape(BS * MS, WB), wo[WA:])]
            outs["aks"].append(ka.reshape(BS, MS, HA, DHA))
            outs["avs"].append(va.reshape(BS, MS, HA, DHA))
            outs["bks"].append(kb.reshape(BS, MS, HB, WB // HB))
            outs["bvs"].append(vb.reshape(BS, MS, HB, WB // HB))
            outs["bls"].append(lfn)
        else:
            o = l // 2
            w_in = gdn_w_in[o]
            w_pad = jnp.concatenate([w_in, jnp.zeros((D, LANES - 2 * HG), f32)], axis=-1).astype(bf16)
            spec = [(0, 3 * WG, f32, 1.0, None), (3 * WG, WG, f32, 1.0, None), (4 * WG, LANES, f32, 1.0, None)]
            wo = gdn_w_o[o].astype(bf16)

            qg, kg, vg, z, ab, cvp = _gdn_proj(xp, norm_g[l, 1], w_pad, jnp.zeros((1, CONV - 1, 3 * WG), f32),
                                               gdn_conv_w[o], HG)
            y, Sp = _gdn_core_pre(qg, kg, vg, ab, z, jnp.zeros((1, HG, DKG, DKG), f32),
                                  gdn_a_log[o], gdn_dt_bias[o], gdn_norm_g[o], CHUNK, 8)
            res_p = [(y, wo)]
            outs["sgp"].append(Sp)
            outs["scp"].append(cvp)

            qkv, z, ab = _norm_proj(xs, norm_g[l, 1], w_pad, spec)
            y, Ss, cvs = _gdn_core(qkv, ab, z, state_gdn_conv[o].astype(f32), state_gdn[o].astype(f32),
                                   gdn_conv_w[o], gdn_a_log[o], gdn_dt_bias[o], gdn_norm_g[o], MS, 1)
            res_s = [(y, wo)]
            outs["sgs"].append(Ss)
            outs["scs"].append(cvs)

        mk, mkb, mv, mvb = _norm_proj(mem_prompt.reshape(MEM, D), mem_norm_g[l], wxkv[l],
                                      [(0, WX, f32, 1.0, None), (0, WX, bf16, 1.0, None),
                                       (WX, WX, f32, 1.0, None), (WX, WX, bf16, 1.0, None)])
        outs["mkp"].append(mk.reshape(1, MEM, HX, DHX))
        outs["mvp"].append(mv.reshape(1, MEM, HX, DHX))
        qspec = [(0, WX, bf16, DHX ** -0.5, None)]
        xp, ocp = _cross_prompt(xp, norm_g[l, 2], wxq[l], res_p, mkb, mvb, HX)
        xs, q = _norm_proj(xs, norm_g[l, 2], wxq[l], qspec, res=res_s)
        ocs = _cross_sample(q.reshape(BS, MS, WX), cache_mem_k, cache_mem_v, l)

        fin = final_norm_g if l == depth - 1 else None
        xp = _ffn(xp, norm_g[l, 3], wgate[l, 1], wup[l, 1], wdown[l, 1], fin, res=(ocp, wxo[l]))
        xs = _ffn(xs, norm_g[l, 3], wgate[l, 1], wup[l, 1], wdown[l, 1], fin, res=(ocs.reshape(BS * MS, WX), wxo[l]))

    st = lambda k: jnp.stack(outs[k])
    return (xp.reshape(BP, SEQ, D), xs.reshape(BS, MS, D),
            st("akp"), st("avp"), st("bkp"), st("bvp"), st("blp"),
            st("sgp"), st("scp"), st("mkp"), st("mvp"),
            st("aks"), st("avs"), st("bks"), st("bvs"), st("bls"),
            st("sgs"), st("scs"))
```

```python
import functools

import numpy as np
import jax
import jax.numpy as jnp
from jax import lax
from jax.experimental import pallas as pl
from jax.experimental.pallas import tpu as pltpu

f32 = jnp.float32
bf16 = jnp.bfloat16
EPS = 1e-6
NEG = -1e30

V7X_VMEM_BYTES = 64 * 1024 * 1024
V7X_LANES = 128
MIB = 1024 * 1024

CHUNK = 64
BAND_CHUNKS = 8
REL_CLIP = 256
CONV = 4
MEM = 256


def _params(sem, est_bytes):
    limit = int(min(V7X_VMEM_BYTES - 8 * MIB, max(32 * MIB, est_bytes + 8 * MIB)))
    return pltpu.CompilerParams(dimension_semantics=sem, vmem_limit_bytes=limit)


def _rms(x, g):
    ms = jnp.mean(x * x, axis=-1, keepdims=True)
    return x * lax.rsqrt(ms + EPS) * g


def _silu(x):
    h = 0.5 * x
    return h + h * jnp.tanh(h)


def _dot(a, b):
    return jnp.dot(a, b, preferred_element_type=f32)


def _dot_nt(a, b):
    return lax.dot_general(a, b, (((1,), (1,)), ((), ())), preferred_element_type=f32)


def _dot_hi(a, b):
    return jnp.dot(a, b, preferred_element_type=f32, precision=lax.Precision.HIGHEST)


def _split2(a):
    hi = a.astype(bf16)
    return hi, (a - hi.astype(f32)).astype(bf16)


def _split3(a):
    hi, rest = a.astype(bf16), a
    rest = rest - hi.astype(f32)
    mid = rest.astype(bf16)
    return hi, mid, (rest - mid.astype(f32)).astype(bf16)


def _ffn_body(*refs, nj, final, has_res):
    x_ref, g_ref, wg_ref, wu_ref, wd_ref = refs[:5]
    pos = 5
    if has_res:
        a_ref, wo_ref = refs[pos:pos + 2]
        pos += 2
    if final:
        gf_ref = refs[pos]
        pos += 1
    o_ref, h_scr, acc = refs[pos:]
    j = pl.program_id(1)

    @pl.when(j == 0)
    def _():
        x = x_ref[...]
        if has_res:
            x = x + _dot(a_ref[...], wo_ref[...])
        o_ref[...] = x
        h_scr[...] = _rms(x, g_ref[...]).astype(bf16)
        acc[...] = jnp.zeros_like(acc)

    h = h_scr[...]
    a = _silu(_dot(h, wg_ref[...])) * _dot(h, wu_ref[...])
    acc[...] += _dot(a.astype(bf16), wd_ref[...])

    @pl.when(j == nj - 1)
    def _():
        y = o_ref[...] + 0.5 * acc[...]
        if final:
            y = _rms(y, gf_ref[...])
        o_ref[...] = y


def _ffn(x, g, wg, wu, wd, final_g=None, res=None):
    T, D = x.shape
    FF = wg.shape[1]
    TM = min(1024, T)
    TF = min(1024, FF)
    nj = FF // TF
    final = final_g is not None
    in_specs = [pl.BlockSpec((TM, D), lambda i, j: (i, 0)),
                pl.BlockSpec((1, D), lambda i, j: (0, 0)),
                pl.BlockSpec((D, TF), lambda i, j: (0, j)),
                pl.BlockSpec((D, TF), lambda i, j: (0, j)),
                pl.BlockSpec((TF, D), lambda i, j: (j, 0))]
    args = [x, g.reshape(1, D), wg, wu, wd]
    est = 2 * (2 * TM * D * 4) + 2 * 3 * D * TF * 2 + TM * D * 6 + 3 * TM * TF * 4
    if res is not None:
        a, wo = res
        K = a.shape[1]
        in_specs += [pl.BlockSpec((TM, K), lambda i, j: (i, 0)), pl.BlockSpec((K, D), lambda i, j: (0, 0))]
        args += [a, wo]
        est += 2 * TM * K * 2 + 2 * K * D * 2
    if final:
        in_specs.append(pl.BlockSpec((1, D), lambda i, j: (0, 0)))
        args.append(final_g.reshape(1, D))
    return pl.pallas_call(
        functools.partial(_ffn_body, nj=nj, final=final, has_res=res is not None),
        grid=(T // TM, nj),
        in_specs=in_specs,
        out_specs=pl.BlockSpec((TM, D), lambda i, j: (i, 0)),
        out_shape=jax.ShapeDtypeStruct((T, D), f32),
        scratch_shapes=[pltpu.VMEM((TM, D), bf16), pltpu.VMEM((TM, D), f32)],
        compiler_params=_params(("parallel", "arbitrary"), est),
        name="ffn",
    )(*args)


def _log_sigmoid(x):
    return jnp.minimum(x, 0.0) - jnp.log(1.0 + jnp.exp(-jnp.abs(x)))


def _norm_proj_body(*refs, outs, has_bias, nres):
    x_ref, g_ref, w_ref = refs[:3]
    pos = 3
    b_ref = None
    if has_bias:
        b_ref = refs[pos]
        pos += 1
    x = x_ref[...]
    for p in range(nres):
        x = x + _dot(refs[pos][...], refs[pos + 1][...])
        pos += 2
    o_refs = refs[pos:]
    if nres:
        o_refs[0][...] = x
        o_refs = o_refs[1:]
    h = _rms(x, g_ref[...]).astype(bf16)
    cache = {}
    for o_ref, (off, n, dt, scale, act) in zip(o_refs, outs):
        if (off, n) not in cache:
            cache[(off, n)] = _dot(h, w_ref[:, off:off + n])
        r = cache[(off, n)]
        if act == "logsig":
            r = _log_sigmoid(r + b_ref[...])
        if scale != 1.0:
            r = r * scale
        o_ref[...] = r.astype(dt)


def _norm_proj(x, g, w, outs, bias=None, res=()):
    T, D = x.shape
    N = w.shape[1]
    TM = min(512, T)
    in_specs = [pl.BlockSpec((TM, D), lambda i: (i, 0)),
                pl.BlockSpec((1, D), lambda i: (0, 0)),
                pl.BlockSpec((D, N), lambda i: (0, 0))]
    args = [x, g.reshape(1, D), w]
    if bias is not None:
        in_specs.append(pl.BlockSpec((1, bias.shape[-1]), lambda i: (0, 0)))
        args.append(bias)
    est = 2 * TM * D * 4 + 2 * D * N * 2 + sum(2 * TM * n * 4 for (_, n, _, _, _) in outs) + TM * N * 4
    for a, wr in res:
        K = a.shape[1]
        in_specs += [pl.BlockSpec((TM, K), lambda i: (i, 0)), pl.BlockSpec((K, D), lambda i: (0, 0))]
        args += [a, wr]
        est += 2 * TM * K * 2 + 2 * K * D * 2
    out_specs = [pl.BlockSpec((TM, n), lambda i: (i, 0)) for (_, n, _, _, _) in outs]
    out_shape = [jax.ShapeDtypeStruct((T, n), dt) for (_, n, dt, _, _) in outs]
    if res:
        out_specs.insert(0, pl.BlockSpec((TM, D), lambda i: (i, 0)))
        out_shape.insert(0, jax.ShapeDtypeStruct((T, D), f32))
        est += 2 * TM * D * 4
    return pl.pallas_call(
        functools.partial(_norm_proj_body, outs=tuple(outs), has_bias=bias is not None, nres=len(res)),
        grid=(T // TM,),
        in_specs=in_specs,
        out_specs=out_specs,
        out_shape=out_shape,
        compiler_params=_params(("parallel",), est),
        name="norm_proj",
    )(*args)


def _cumsum_body(x_ref, f_ref, ft_ref, carry, *, nsub):
    @pl.when(pl.program_id(0) == 0)
    def _():
        carry[...] = jnp.zeros_like(carry)

    r = lax.broadcasted_iota(jnp.int32, (V7X_LANES, V7X_LANES), 0)
    c = lax.broadcasted_iota(jnp.int32, (V7X_LANES, V7X_LANES), 1)
    tri = (r >= c).astype(f32)
    run = carry[...]
    ncol = x_ref.shape[1] // V7X_LANES
    for sb in range(nsub):
        rows = slice(sb * V7X_LANES, (sb + 1) * V7X_LANES)
        blk = _dot_hi(tri, x_ref[rows, :]) + run
        f_ref[rows, :] = blk
        for cb in range(ncol):
            cols = slice(cb * V7X_LANES, (cb + 1) * V7X_LANES)
            ft_ref[cols, rows] = blk[:, cols].T
        run = blk[V7X_LANES - 1:V7X_LANES, :]
    carry[...] = run


def _cumsum_rows(x):
    L, N = x.shape
    nsub = 8 if L % (8 * V7X_LANES) == 0 else 1
    TB = nsub * V7X_LANES
    return pl.pallas_call(
        functools.partial(_cumsum_body, nsub=nsub),
        grid=(L // TB,),
        in_specs=[pl.BlockSpec((TB, N), lambda i: (i, 0))],
        out_specs=[pl.BlockSpec((TB, N), lambda i: (i, 0)), pl.BlockSpec((N, TB), lambda i: (0, i))],
        out_shape=[jax.ShapeDtypeStruct((L, N), f32), jax.ShapeDtypeStruct((N, L), f32)],
        scratch_shapes=[pltpu.VMEM((1, N), f32)],
        compiler_params=_params(("arbitrary",), 8 * TB * N * 4),
        name="cumsum_rows",
    )(x)


BAND_TQ = 256
BAND_NB = 3


def _band_body(q_ref, k0_ref, k1_ref, k2_ref, v0_ref, v1_ref, v2_ref, b_ref, o_ref, *, nheads):
    TQ = BAND_TQ
    lane = lax.broadcasted_iota(jnp.int32, (TQ, V7X_LANES), 1)
    lo = lane < 64
    krefs = (k0_ref, k1_ref, k2_ref)
    vrefs = (v0_ref, v1_ref, v2_ref)
    for hp in range(nheads // 2):
        cols = slice(hp * V7X_LANES, (hp + 1) * V7X_LANES)
        qp = q_ref[:, cols]
        zq = jnp.zeros_like(qp)
        q2 = jnp.concatenate([jnp.where(lo, qp, zq), jnp.where(lo, zq, qp)], axis=0)
        s = []
        for b in range(BAND_NB):
            blk = slice(b * TQ, (b + 1) * TQ)
            bias2 = jnp.concatenate([b_ref[2 * hp, :, blk], b_ref[2 * hp + 1, :, blk]], axis=0)
            s.append(_dot_nt(q2, krefs[b][:, cols]) + bias2)
        smax = s[0]
        for b in range(1, BAND_NB):
            smax = jnp.maximum(smax, s[b])
        m = jnp.max(smax, axis=-1, keepdims=True)
        p = [jnp.exp2(sb - m) for sb in s]
        psum = p[0]
        for b in range(1, BAND_NB):
            psum = psum + p[b]
        l = psum.sum(axis=-1, keepdims=True)
        pcat, vcat = [], []
        for b in range(BAND_NB):
            vp = vrefs[b][:, cols]
            zv = jnp.zeros_like(vp)
            vcat += [jnp.where(lo, vp, zv), jnp.where(lo, zv, vp)]
            pb = p[b].astype(bf16)
            pcat += [pb[:TQ], pb[TQ:]]
        o = _dot(jnp.concatenate(pcat, axis=1), jnp.concatenate(vcat, axis=0))
        o_ref[:, cols] = (o / jnp.where(lo, l[:TQ], l[TQ:])).astype(bf16)


def _toeplitz_bias(table, nq, nk, off):
    H = table.shape[0]
    m = np.arange(nq + nk - 1)
    idx = np.clip(off + nq - 1 - m, -REL_CLIP, REL_CLIP) + REL_CLIP
    w = jnp.concatenate([table[:, idx].astype(f32), jnp.zeros((H, 1), f32)], axis=-1)
    skew = jnp.tile(w, (1, nq))[:, :nq * (nq + nk - 1)].reshape(H, nq, nq + nk - 1)
    return skew[:, :, nq - 1:]


def _band_bias(table):
    qc = np.arange(BAND_TQ)[:, None] // CHUNK
    ki = np.arange(BAND_NB * BAND_TQ)[None, :]
    kc = ki // CHUNK
    in_band = (kc >= qc) & (kc <= qc + BAND_CHUNKS)
    valid = np.stack([in_band & (ki // BAND_TQ >= BAND_NB - 1 - n) for n in range(BAND_NB)])
    bias = _toeplitz_bias(table, BAND_TQ, BAND_NB * BAND_TQ, BAND_CHUNKS * CHUNK) * LOG2E
    return jnp.where(valid[:, None], bias[None], NEG)


def _band_prompt(q, k, v, bias):
    L, W = q.shape
    TQ = BAND_TQ
    H = bias.shape[1]
    kspec = [pl.BlockSpec((TQ, W), functools.partial(lambda i, d: (jnp.maximum(i - d, 0), 0), d=d))
             for d in (2, 1, 0)]
    est = 2 * 7 * TQ * W * 2 + 2 * H * TQ * BAND_NB * TQ * 4 + 16 * TQ * TQ * 4
    return pl.pallas_call(
        functools.partial(_band_body, nheads=H),
        grid=(L // TQ,),
        in_specs=[pl.BlockSpec((TQ, W), lambda i: (i, 0))] + kspec + kspec
                 + [pl.BlockSpec((None, H, TQ, BAND_NB * TQ), lambda i: (jnp.minimum(i, BAND_NB - 1), 0, 0, 0))],
        out_specs=pl.BlockSpec((TQ, W), lambda i: (i, 0)),
        out_shape=jax.ShapeDtypeStruct((L, W), bf16),
        compiler_params=_params(("parallel",), est),
        name="band_prompt",
    )(q, k, k, k, v, v, v, bias)


FOX_T = 1024
FOX_VROWS = 80
FOX_QPIECE = 64
FOX_KPIECE = 67
LOG2E = 1.4426950408889634


def _fox_pack_body(q_ref, k_ref, v_ref, f_ref, e_ref, qa_ref, ka_ref, vt_ref, *, nheads, dh):
    T, W2 = q_ref.shape
    pieces = _split3((f_ref[...] - f_ref[0:1, :]) * LOG2E)
    aug = _dot(jnp.concatenate(pieces, axis=1), e_ref[...])
    lane = lax.broadcasted_iota(jnp.int32, (T, W2), 1) % V7X_LANES
    ones_q = jnp.where((lane >= FOX_KPIECE) & (lane < FOX_KPIECE + 3), 1.0, 0.0)
    ones_k = jnp.where((lane >= FOX_QPIECE) & (lane < FOX_QPIECE + 3), 1.0, 0.0)
    qa_ref[...] = (q_ref[...].astype(f32) + aug[:, :W2] + ones_q).astype(bf16)
    ka_ref[...] = (k_ref[...].astype(f32) + aug[:, W2:] + ones_k).astype(bf16)
    vT = v_ref[...].astype(f32).T
    pad = FOX_VROWS - dh
    ones_row = jnp.where(lax.broadcasted_iota(jnp.int32, (pad, T), 0) == 0, 1.0, 0.0).astype(bf16)
    for h in range(nheads):
        vt_ref[h, 0:dh, :] = vT[h * dh:(h + 1) * dh, :].astype(bf16)
        vt_ref[h, dh:FOX_VROWS, :] = ones_row


def _fox_pack(q, k, v, F, nheads):
    L, W2 = q.shape
    W = v.shape[1]
    dh = W // nheads
    T = FOX_T
    e = np.zeros((3 * V7X_LANES, 2 * W2), np.float32)
    for h in range(nheads):
        for t in range(3):
            e[t * V7X_LANES + h, h * V7X_LANES + FOX_QPIECE + t] = 1.0
            e[t * V7X_LANES + h, W2 + h * V7X_LANES + FOX_KPIECE + t] = -1.0
    est = 2 * (4 * T * W2 * 2 + T * W * 2 + nheads * FOX_VROWS * T * 2) + 6 * T * W2 * 4
    return pl.pallas_call(
        functools.partial(_fox_pack_body, nheads=nheads, dh=dh),
        grid=(L // T,),
        in_specs=[pl.BlockSpec((T, W2), lambda i: (i, 0)),
                  pl.BlockSpec((T, W2), lambda i: (i, 0)),
                  pl.BlockSpec((T, W), lambda i: (i, 0)),
                  pl.BlockSpec((T, V7X_LANES), lambda i: (i, 0)),
                  pl.BlockSpec((3 * V7X_LANES, 2 * W2), lambda i: (0, 0))],
        out_specs=[pl.BlockSpec((T, W2), lambda i: (i, 0)),
                   pl.BlockSpec((T, W2), lambda i: (i, 0)),
                   pl.BlockSpec((nheads, FOX_VROWS, T), lambda i: (0, 0, i))],
        out_shape=[jax.ShapeDtypeStruct((L, W2), bf16), jax.ShapeDtypeStruct((L, W2), bf16),
                   jax.ShapeDtypeStruct((nheads, FOX_VROWS, L), bf16)],
        compiler_params=_params(("parallel",), est),
        name="fox_pack",
    )(q, k, v, F, jnp.asarray(e, bf16))


def _fox_body(qi_ref, kj_ref, qa_ref, ka_ref, vt_ref, d_ref, o_ref, m_scr, acc_scr, *, nheads, dh):
    s_id = pl.program_id(0)
    i = qi_ref[s_id]
    j = kj_ref[s_id]
    T = FOX_T

    @pl.when(j == 0)
    def _():
        m_scr[...] = jnp.full_like(m_scr, NEG)
        acc_scr[...] = jnp.zeros_like(acc_scr)

    def step(masked):
        if masked:
            keep = lax.broadcasted_iota(jnp.int32, (T, T), 0) <= lax.broadcasted_iota(jnp.int32, (T, T), 1)
        def scores(h):
            cols = slice(h * V7X_LANES, (h + 1) * V7X_LANES)
            st = _dot_nt(ka_ref[:, cols], qa_ref[:, cols])
            return jnp.where(keep, st, NEG) if masked else st

        nxt = scores(0)
        for h in range(nheads):
            st = nxt
            if h + 1 < nheads:
                nxt = scores(h + 1)
            d = d_ref[s_id * nheads + h]
            m_prev = m_scr[h:h + 1, :]
            m_new = jnp.maximum(m_prev, jnp.max(st, axis=0, keepdims=True) + d)
            p = jnp.exp2(st - (m_new - d))
            alpha = jnp.exp2(m_prev - m_new)
            acc_scr[h] = acc_scr[h] * alpha + _dot(vt_ref[h], p.astype(bf16))
            m_scr[h:h + 1, :] = m_new

    @pl.when(j < i)
    def _():
        step(False)

    @pl.when(j == i)
    def _():
        step(True)
        for hp in range(nheads // 2):
            a0 = acc_scr[2 * hp]
            a1 = acc_scr[2 * hp + 1]
            o2 = jnp.concatenate([a0[0:dh, :] / a0[dh:dh + 1, :], a1[0:dh, :] / a1[dh:dh + 1, :]], axis=0)
            o_ref[:, hp * V7X_LANES:(hp + 1) * V7X_LANES] = o2.T.astype(bf16)


def _fox_prompt(qa, ka, vt, F, nheads):
    L, W2 = qa.shape
    T = FOX_T
    n = L // T
    dh = V7X_LANES // 2
    W = nheads * dh
    pairs = [(i, j) for i in range(n) for j in range(i + 1)]
    qi = np.array([p[0] for p in pairs], np.int32)
    kj = np.array([p[1] for p in pairs], np.int32)
    fs = F[::T, :nheads]
    d = ((fs[qi] - fs[kj]) * LOG2E).reshape(-1)
    grid_spec = pltpu.PrefetchScalarGridSpec(
        num_scalar_prefetch=2,
        grid=(len(pairs),),
        in_specs=[pl.BlockSpec((T, W2), lambda s, qi, kj: (qi[s], 0)),
                  pl.BlockSpec((T, W2), lambda s, qi, kj: (kj[s], 0)),
                  pl.BlockSpec((nheads, FOX_VROWS, T), lambda s, qi, kj: (0, 0, kj[s])),
                  pl.BlockSpec(memory_space=pltpu.SMEM)],
        out_specs=pl.BlockSpec((T, W), lambda s, qi, kj: (qi[s], 0)),
        scratch_shapes=[pltpu.VMEM((nheads, T), f32), pltpu.VMEM((nheads, FOX_VROWS, T), f32)],
    )
    est = 2 * (2 * T * W2 * 2 + nheads * FOX_VROWS * T * 2 + T * W * 2) + nheads * FOX_VROWS * T * 4 + 24 * T * T * 4
    return pl.pallas_call(
        functools.partial(_fox_body, nheads=nheads, dh=dh),
        grid_spec=grid_spec,
        out_shape=jax.ShapeDtypeStruct((L, W), bf16),
        compiler_params=_params(("arbitrary",), est),
        name="fox_prompt",
    )(jnp.asarray(qi), jnp.asarray(kj), qa, ka, vt, d)


def _cross_body(*refs, nres, nheads, scale):
    x_ref, g_ref, wq_ref = refs[:3]
    x = x_ref[...]
    for p in range(nres):
        x = x + _dot(refs[3 + 2 * p][...], refs[4 + 2 * p][...])
    k_ref, v_ref, xo_ref, o_ref = refs[3 + 2 * nres:]
    xo_ref[...] = x
    q = (_dot(_rms(x, g_ref[...]).astype(bf16), wq_ref[...]) * scale).astype(bf16)
    dh = q.shape[1] // nheads
    hs = range(nheads)
    s = [_dot_nt(q[:, h * dh:(h + 1) * dh], k_ref[:, h * dh:(h + 1) * dh]) for h in hs]
    m = [jnp.max(sh, axis=-1, keepdims=True) for sh in s]
    p = [jnp.exp(s[h] - m[h]) for h in hs]
    l = [ph.sum(axis=-1, keepdims=True) for ph in p]
    o = [_dot(p[h].astype(bf16), v_ref[:, h * dh:(h + 1) * dh]) / l[h] for h in hs]
    o_ref[...] = jnp.concatenate(o, axis=-1).astype(bf16)


def _cross_prompt(x, g, wq, res, mk, mv, nheads):
    T, D = x.shape
    W = wq.shape[1]
    S = mk.shape[0]
    TM = min(512, T)
    row = lambda width: pl.BlockSpec((TM, width), lambda i: (i, 0))
    whole = lambda r, c: pl.BlockSpec((r, c), lambda i: (0, 0))
    in_specs = [row(D), whole(1, D), whole(D, W)]
    args = [x, g.reshape(1, D), wq]
    est = 4 * TM * D * 4 + 2 * D * W * 2 + 4 * S * W * 2 + 2 * TM * W * 2 + 3 * TM * W * 4 + 8 * TM * S * 4
    for a, wr in res:
        K = a.shape[1]
        in_specs += [row(K), whole(K, D)]
        args += [a, wr]
        est += 2 * TM * K * 2 + 2 * K * D * 2
    in_specs += [whole(S, W), whole(S, W)]
    args += [mk, mv]
    return pl.pallas_call(
        functools.partial(_cross_body, nres=len(res), nheads=nheads, scale=(W // nheads) ** -0.5),
        grid=(T // TM,),
        in_specs=in_specs,
        out_specs=[row(D), row(W)],
        out_shape=[jax.ShapeDtypeStruct((T, D), f32), jax.ShapeDtypeStruct((T, W), bf16)],
        compiler_params=_params(("parallel",), est),
        name="cross_prompt",
    )(*args)


SAMPLE_INNER = 2


def _sample_attn_body(*refs, nheads, mode):
    shared = (5, 6) if mode == "rel" else ()
    for g in range(SAMPLE_INNER):
        _sample_attn_one(*[r if n in shared else r.at[g] for n, r in enumerate(refs)], nheads=nheads, mode=mode)


def _sample_attn_one(*refs, nheads, mode):
    q_ref, kc_ref, vc_ref, kn_ref, vn_ref = refs[:5]
    if mode == "rel":
        bc_ref, bn_ref, o_ref = refs[5:]
    else:
        fq_ref, fkc_ref, fkn_ref, o_ref = refs[5:]

    m_q, W = q_ref.shape
    dh = W // nheads
    R = nheads * m_q
    q = q_ref[...].astype(f32)
    qt = jnp.concatenate([q] * nheads, axis=0)
    hrow = lax.broadcasted_iota(jnp.int32, (R, W), 0) // m_q
    hlane = lax.broadcasted_iota(jnp.int32, (R, W), 1) // dh
    qbd = jnp.where(hrow == hlane, qt, 0.0).astype(bf16)

    def rows_of(f_ref):
        n = f_ref.shape[-1]
        return jnp.concatenate([jnp.broadcast_to(f_ref[h:h + 1, :], (m_q, n)) for h in range(nheads)], axis=0)

    sc = _dot_nt(qbd, kc_ref[...].astype(bf16))
    sn = _dot_nt(qbd, kn_ref[...].astype(bf16))
    if mode == "rel":
        sc = sc + bc_ref[...]
        sn = sn + bn_ref[...]
    else:
        sc = sc + (fq_ref[...] - rows_of(fkc_ref))
        sn = sn + (fq_ref[...] - rows_of(fkn_ref))
        qpos = lax.broadcasted_iota(jnp.int32, (R, m_q), 0) % m_q
        kpos = lax.broadcasted_iota(jnp.int32, (R, m_q), 1)
        sn = jnp.where(kpos <= qpos, sn, NEG)
    m = jnp.maximum(jnp.max(sc, axis=-1, keepdims=True), jnp.max(sn, axis=-1, keepdims=True))
    pc = jnp.exp(sc - m)
    pn = jnp.exp(sn - m)
    l = pc.sum(axis=-1, keepdims=True) + pn.sum(axis=-1, keepdims=True)
    o = _dot(pc.astype(bf16), vc_ref[...].astype(bf16)) + _dot(pn.astype(bf16), vn_ref[...].astype(bf16))
    o = o / l
    hl = lax.broadcasted_iota(jnp.int32, (m_q, W), 1) // dh
    out = jnp.zeros((m_q, W), f32)
    for h in range(nheads):
        out = out + jnp.where(hl == h, o[h * m_q:(h + 1) * m_q, :], 0.0)
    o_ref[...] = out.astype(bf16)


def _sample_attn(q, kc, vc, nheads, mode, kn, vn, extra):
    B, m_q, W = q.shape
    P = kc.shape[1]
    R = nheads * m_q
    G = SAMPLE_INNER
    per_b = lambda *shape: pl.BlockSpec((G,) + shape, lambda b: (b,) + (0,) * len(shape))
    shared = lambda *shape: pl.BlockSpec(shape, lambda b: (0,) * len(shape))
    in_specs = [per_b(m_q, W), per_b(P, W), per_b(P, W), per_b(m_q, W), per_b(m_q, W)]
    if mode == "rel":
        in_specs += [shared(R, P), shared(R, m_q)]
    else:
        in_specs += [per_b(R, 1), per_b(nheads, P), per_b(nheads, m_q)]
    est = G * (2 * 2 * P * W * 4 + 2 * P * W * 2 + 6 * R * P * 4 + 4 * R * W * 4)
    return pl.pallas_call(
        functools.partial(_sample_attn_body, nheads=nheads, mode=mode),
        grid=(B // G,),
        in_specs=in_specs,
        out_specs=per_b(m_q, W),
        out_shape=jax.ShapeDtypeStruct((B, m_q, W), bf16),
        compiler_params=_params(("parallel",), est),
        name="sample_attn_" + mode,
    )(q, kc, vc, kn, vn, *extra)


def _cross_sample_body(q_ref, kc_ref, vc_ref, o_ref, *, nheads):
    m_q, W = q_ref.shape
    dh = W // nheads
    npc = dh // kc_ref.shape[1]
    P = kc_ref.shape[0] // (nheads * npc)
    hs = range(nheads)

    def head_rows(ref, h):
        pieces = [ref[pl.ds(h * npc + j, P, stride=nheads * npc), :] for j in range(npc)]
        return jnp.concatenate(pieces, axis=-1).astype(bf16)

    q = q_ref[...]
    sc = [_dot_nt(q[:, h * dh:(h + 1) * dh], head_rows(kc_ref, h)) for h in hs]
    m = [jnp.max(s, axis=-1, keepdims=True) for s in sc]
    pc = [jnp.exp(sc[h] - m[h]) for h in hs]
    l = [p.sum(axis=-1, keepdims=True) for p in pc]
    o = [_dot(pc[h].astype(bf16), head_rows(vc_ref, h)) / l[h] for h in hs]
    o_ref[...] = jnp.concatenate(o, axis=-1).astype(bf16)


def _cross_sample(q, kc_all, vc_all, layer):
    B, m_q, W = q.shape
    nl, _, P, nheads, dh = kc_all.shape
    nrows = P * nheads * (dh // V7X_LANES)
    kc_all = kc_all.reshape(nl, B, nrows, V7X_LANES)
    vc_all = vc_all.reshape(nl, B, nrows, V7X_LANES)
    cache = pl.BlockSpec((None, None, nrows, V7X_LANES), lambda b: (layer, b, 0, 0))
    qo = pl.BlockSpec((None, m_q, W), lambda b: (b, 0, 0))
    est = 2 * 2 * nrows * V7X_LANES * 4 + 4 * P * W * 2 + 8 * nheads * m_q * P * 4
    return pl.pallas_call(
        functools.partial(_cross_sample_body, nheads=nheads),
        grid=(B,),
        in_specs=[qo, cache, cache],
        out_specs=qo,
        out_shape=jax.ShapeDtypeStruct((B, m_q, W), bf16),
        compiler_params=_params(("parallel",), est),
        name="cross_sample",
    )(q, kc_all, vc_all)


GDN_HALO = 8


def _softplus(x):
    return jnp.maximum(x, 0.0) + jnp.log(1.0 + jnp.exp(-jnp.abs(x)))


def _gdn_conv_qkv(xbuf, R, cw_ref, nheads, dk):
    H0 = GDN_HALO
    tail = xbuf[H0 + R - (CONV - 1):H0 + R, :]

    def act_cols(c0):
        cols = slice(c0, c0 + dk)
        full = xbuf[0:H0 + R, cols]
        conv = pltpu.roll(full, CONV - 1, 0)[H0:, :] * cw_ref[0:1, cols]
        for jj in range(1, CONV):
            sh = CONV - 1 - jj
            conv = conv + (pltpu.roll(full, sh, 0) if sh else full)[H0:, :] * cw_ref[jj:jj + 1, cols]
        return _silu(conv)

    def l2n(a):
        return a * lax.rsqrt(jnp.sum(a * a, axis=-1, keepdims=True) + EPS)

    W = nheads * dk
    qn = [l2n(act_cols(h * dk)) * (dk ** -0.5) for h in range(nheads)]
    kn = [l2n(act_cols(W + h * dk)) for h in range(nheads)]
    vn = [act_cols(2 * W + h * dk) for h in range(nheads)]
    return qn, kn, vn, tail


def _gdn_proj_body(x_ref, g_ref, w_ref, cs_ref, cw_ref, q_ref, k_ref, v_ref, z_ref, ab_ref, cf_ref, xbuf,
                   *, nsteps, nheads, dk):
    i = pl.program_id(0)
    W = nheads * dk
    TM = x_ref.shape[0]
    H0 = GDN_HALO

    @pl.when(i == 0)
    def _():
        xbuf[...] = jnp.zeros_like(xbuf)

    qn, kn, vn, tail = _gdn_conv_qkv(xbuf, TM, cw_ref, nheads, dk)
    q_ref[...] = jnp.concatenate(qn, axis=-1)
    k_ref[...] = jnp.concatenate(kn, axis=-1)
    v_ref[...] = jnp.concatenate(vn, axis=-1)

    xbuf[H0 - (CONV - 1):H0, :] = jnp.where(i == 0, cs_ref[...], tail)
    cf_ref[...] = tail

    h = _rms(x_ref[...], g_ref[...]).astype(bf16)
    xbuf[H0:H0 + TM, :] = _dot(h, w_ref[:, :3 * W])
    z_ref[...] = _dot(h, w_ref[:, 3 * W:4 * W])
    ab_ref[...] = _dot(h, w_ref[:, 4 * W:])


def _gdn_proj(x, g, w, conv_state, conv_w, nheads):
    T, D = x.shape
    N = w.shape[1]
    W = (N - V7X_LANES) // 4
    dk = W // nheads
    TM = min(512, T)
    nsteps = T // TM
    row = lambda width: pl.BlockSpec((TM, width), lambda i: (jnp.minimum(i, nsteps - 1), 0))
    late = lambda width: pl.BlockSpec((TM, width), lambda i: (jnp.maximum(i - 1, 0), 0))
    whole = lambda *shape: pl.BlockSpec(shape, lambda i: (0,) * len(shape))
    est = 2 * TM * D * 4 + 2 * D * N * 2 + 2 * TM * (4 * W + V7X_LANES) * 4 + (GDN_HALO + TM) * 3 * W * 4 + 4 * TM * 3 * W * 4
    q, k, v, z, ab, cf = pl.pallas_call(
        functools.partial(_gdn_proj_body, nsteps=nsteps, nheads=nheads, dk=dk),
        grid=(nsteps + 1,),
        in_specs=[row(D), whole(1, D), whole(D, N),
                  pl.BlockSpec((None, CONV - 1, 3 * W), lambda i: (0, 0, 0)), whole(CONV, 3 * W)],
        out_specs=[late(W), late(W), late(W), row(W), row(V7X_LANES),
                   pl.BlockSpec((None, CONV - 1, 3 * W), lambda i: (0, 0, 0))],
        out_shape=[jax.ShapeDtypeStruct((T, W), f32)] * 4
                  + [jax.ShapeDtypeStruct((T, V7X_LANES), f32), jax.ShapeDtypeStruct((1, CONV - 1, 3 * W), f32)],
        scratch_shapes=[pltpu.VMEM((GDN_HALO + TM, 3 * W), f32)],
        compiler_params=_params(("arbitrary",), est),
        name="gdn_proj",
    )(x, g.reshape(1, D), w, conv_state, conv_w)
    return q, k, v, z, ab, cf


def _gdn_body(*refs, C, NS, nsteps, nheads, dk, pre):
    if pre:
        q_ref, k_ref, v_ref, ab_ref, z_ref, s0_ref, alog_ref, dtb_ref, ng_ref, o_ref, sf_ref, s_scr = refs
    else:
        (x_ref, ab_ref, z_ref, cs_ref, s0_ref, cw_ref, alog_ref, dtb_ref, ng_ref,
         o_ref, sf_ref, cf_ref, xbuf, s_scr) = refs
    c = pl.program_id(1)
    R = NS * C
    G = V7X_LANES // C
    NG = nheads // G
    GW = G * dk
    hs = range(nheads)

    @pl.when(c == 0)
    def _():
        s_scr[...] = s0_ref[...]

    if pre:
        qn = [q_ref[:, h * dk:(h + 1) * dk] for h in hs]
        kn = [k_ref[:, h * dk:(h + 1) * dk] for h in hs]
        vn = [v_ref[:, h * dk:(h + 1) * dk] for h in hs]
    else:
        H0 = GDN_HALO

        @pl.when(c == 0)
        def _():
            xbuf[H0 - (CONV - 1):H0, :] = cs_ref[...]

        xbuf[H0:H0 + R, :] = x_ref[...]
        qn, kn, vn, tail = _gdn_conv_qkv(xbuf, R, cw_ref, nheads, dk)
        xbuf[H0 - (CONV - 1):H0, :] = tail

        @pl.when(c == nsteps - 1)
        def _():
            cf_ref[...] = tail

    ab = ab_ref[...]
    gfull = -jnp.exp(alog_ref[...]) * _softplus(ab + dtb_ref[...])
    bfull = 1.0 / (1.0 + jnp.exp(-ab))
    z = z_ref[...]
    ng = ng_ref[...]

    r_cc = lax.broadcasted_iota(jnp.int32, (C, C), 0)
    c_cc = lax.broadcasted_iota(jnp.int32, (C, C), 1)
    tri_f = (r_cc >= c_cc).astype(f32)
    ri = lax.broadcasted_iota(jnp.int32, (C, V7X_LANES), 0)
    jl = lax.broadcasted_iota(jnp.int32, (C, V7X_LANES), 1) % C
    lblk = lax.broadcasted_iota(jnp.int32, (C, V7X_LANES), 1) // C
    tri_g = ri >= jl
    strict_g = ri > jl
    eye_g = ri == jl
    r128 = lax.broadcasted_iota(jnp.int32, (V7X_LANES, V7X_LANES), 0)
    l128 = lax.broadcasted_iota(jnp.int32, (V7X_LANES, V7X_LANES), 1)
    same_blk = (r128 // C) == (l128 // C)
    wide_blk = (lax.broadcasted_iota(jnp.int32, (V7X_LANES, GW), 0) // C
                == lax.broadcasted_iota(jnp.int32, (V7X_LANES, GW), 1) // dk)

    def bd_sq(p):
        return jnp.where(same_blk, jnp.concatenate([p] * G, axis=0), jnp.zeros((), p.dtype))

    def bd_wide(xs):
        row = jnp.concatenate(xs, axis=1)
        return jnp.where(wide_blk, jnp.concatenate([row] * G, axis=0), 0.0).astype(bf16)

    def gmm3(xs, pg):
        n = len(xs)
        parts = [_split2(x) for x in xs]
        ph, plo = _split2(pg)
        top = _dot(jnp.concatenate([p for hl in parts for p in hl], axis=0), bd_sq(ph))
        bot = _dot(jnp.concatenate([hl[0] for hl in parts], axis=0), bd_sq(plo)) if n > 1 else _dot(parts[0][0], bd_sq(plo))
        return [top[2 * k * C:(2 * k + 1) * C] + (bot[k * C:(k + 1) * C] + top[(2 * k + 1) * C:(2 * k + 2) * C])
                for k in range(n)]

    sls = [slice(s * C, (s + 1) * C) for s in range(NS)]
    gcs = [_dot_hi(tri_f, gfull[sl]) for sl in sls]
    egcs = [jnp.exp(gc) for gc in gcs]
    kdecs = [jnp.exp(gc[C - 1:C, :] - gc) for gc in gcs]
    gcTs = [jnp.concatenate([gc] * G, axis=0).T for gc in gcs]
    bcols_s = [[bfull[sl, nheads + h:nheads + h + 1] for h in hs] for sl in sls]
    ecols_s = [[egc[:, h:h + 1] for h in hs] for egc in egcs]
    kbs_s = [[kn[h][sls[s]] * bcols_s[s][h] for h in hs] for s in range(NS)]
    Ms, Aqks = [], []
    for s in range(NS):
        sl, gc, gcT, kbs = sls[s], gcs[s], gcTs[s], kbs_s[s]
        for grp in range(NG):
            heads = range(grp * G, (grp + 1) * G)
            gcol = jnp.broadcast_to(gc[:, grp * G:grp * G + 1], (C, V7X_LANES))
            grow = gcT[grp * G:grp * G + 1, :]
            for g in range(1, G):
                h = grp * G + g
                gcol = jnp.where(lblk == g, gc[:, h:h + 1], gcol)
                grow = jnp.where(lblk[0:1, :] == g, gcT[h:h + 1, :], grow)
            Lm = jnp.where(tri_g, jnp.exp(jnp.where(tri_g, gcol - grow, 0.0)), 0.0)
            kbd = bd_wide([kn[h][sl] for h in heads])
            kb_row = jnp.concatenate([kbs[h] for h in heads], axis=1)
            q_row = jnp.concatenate([qn[h][sl] for h in heads], axis=1)
            kq = _dot_nt(jnp.concatenate([kb_row, q_row], axis=0).astype(bf16), kbd)
            Ms.append(jnp.where(strict_g, kq[:C] * Lm, 0.0))
            Aqks.append(jnp.where(tri_g, kq[C:] * Lm, 0.0).astype(bf16))
    Xs = [jnp.where(eye_g, 1.0, 0.0) - M for M in Ms]
    Pws = [gmm3([M], M)[0] for M in Ms]
    e = 2
    while e < C:
        e *= 2
        if e < C:
            nxt = [gmm3([X, Pw], Pw) for X, Pw in zip(Xs, Pws)]
            Xs = [X + r[0] for X, r in zip(Xs, nxt)]
            Pws = [r[1] for r in nxt]
        else:
            Xs = [X + gmm3([X], Pw)[0] for X, Pw in zip(Xs, Pws)]
    prep = []
    for s in range(NS):
        sl, bcols, ecols, kbs = sls[s], bcols_s[s], ecols_s[s], kbs_s[s]
        us, ws = [], []
        for grp in range(NG):
            heads = range(grp * G, (grp + 1) * G)
            Tm = Xs[s * NG + grp].astype(bf16)
            U = _dot(Tm, bd_wide([vn[h][sl] * bcols[h] for h in heads]))
            Wm = _dot(Tm, bd_wide([kbs[h] * ecols[h] for h in heads]))
            for g, h in enumerate(heads):
                us.append(U[:, g * dk:(g + 1) * dk])
                ws.append(Wm[:, g * dk:(g + 1) * dk].astype(bf16))
        qgs = [(qn[h][sl] * ecols[h]).astype(bf16) for h in hs]
        kgTs = [(kn[h][sl] * kdecs[s][:, h:h + 1]).T.astype(bf16) for h in hs]
        decs = [egcs[s][C - 1:C, h:h + 1] for h in hs]
        wqs = [jnp.concatenate([ws[h], qgs[h]], axis=0) for h in hs]
        prep.append((us, wqs, Aqks[s * NG:(s + 1) * NG], kgTs, decs))

    Ss = [s_scr[h] for h in hs]
    for s in range(NS):
        us, wqs, Aqks, kgTs, decs = prep[s]
        wqS = [_dot(wqs[h], Ss[h].astype(bf16)) for h in hs]
        vnews = [us[h] - wqS[h][:C] for h in hs]
        vnbs = [v.astype(bf16) for v in vnews]
        intra = [_dot(Aqks[grp], bd_wide([vnews[h] for h in range(grp * G, (grp + 1) * G)])) for grp in range(NG)]
        os_ = [wqS[h][C:] + intra[h // G][:, (h % G) * dk:(h % G + 1) * dk] for h in hs]
        Ss = [Ss[h] * decs[h] + _dot(kgTs[h], vnbs[h]) for h in hs]
        zs = z[s * C:(s + 1) * C, :]
        ys = [(_rms(os_[h], ng) * _silu(zs[:, h * dk:(h + 1) * dk])).astype(bf16) for h in hs]
        o_ref[s * C:(s + 1) * C, :] = jnp.concatenate(ys, axis=-1)
    for h in hs:
        s_scr[h] = Ss[h]

    @pl.when(c == nsteps - 1)
    def _():
        sf_ref[...] = s_scr[...]


def _gdn_core_pre(q, k, v, ab, z, S0, a_log, dt_bias, norm_g, C, NS):
    T = q.shape[0]
    B, H, dk, dv = S0.shape
    W = H * dk
    R = NS * C
    nsteps = T // (B * R)
    pad = lambda vec: jnp.zeros((1, V7X_LANES), f32).at[0, :H].set(vec.astype(f32))
    row = lambda width: pl.BlockSpec((R, width), lambda b, c: (b * nsteps + c, 0))
    lane_row = pl.BlockSpec((1, V7X_LANES), lambda b, c: (0, 0))
    state = pl.BlockSpec((None, H, dk, dv), lambda b, c: (b, 0, 0, 0))
    est = 2 * R * (4 * W + V7X_LANES) * 4 + 3 * H * dk * dv * 4 * 2 + 16 * R * W * 4
    return pl.pallas_call(
        functools.partial(_gdn_body, C=C, NS=NS, nsteps=nsteps, nheads=H, dk=dk, pre=True),
        grid=(B, nsteps),
        in_specs=[row(W), row(W), row(W), row(V7X_LANES), row(W), state, lane_row, lane_row,
                  pl.BlockSpec((1, dv), lambda b, c: (0, 0))],
        out_specs=[row(W), state],
        out_shape=[jax.ShapeDtypeStruct((T, W), bf16), jax.ShapeDtypeStruct((B, H, dk, dv), f32)],
        scratch_shapes=[pltpu.VMEM((H, dk, dv), f32)],
        compiler_params=_params(("parallel", "arbitrary"), est),
        name="gdn_core_pre",
    )(q, k, v, ab, z, S0, pad(a_log), pad(dt_bias), norm_g.reshape(1, dv))


def _gdn_core(qkv, ab, z, conv_state, S0, conv_w, a_log, dt_bias, norm_g, C, NS):
    T = qkv.shape[0]
    B, H, dk, dv = S0.shape
    W = H * dk
    R = NS * C
    nsteps = T // (B * R)
    pad = lambda vec: jnp.zeros((1, V7X_LANES), f32).at[0, :H].set(vec.astype(f32))
    row = lambda width: pl.BlockSpec((R, width), lambda b, c: (b * nsteps + c, 0))
    est = 2 * R * (3 * W + W + V7X_LANES) * 4 + 3 * H * dk * dv * 4 * 2 + 12 * R * 3 * W * 4
    return pl.pallas_call(
        functools.partial(_gdn_body, C=C, NS=NS, nsteps=nsteps, nheads=H, dk=dk, pre=False),
        grid=(B, nsteps),
        in_specs=[row(3 * W), row(V7X_LANES), row(W),
                  pl.BlockSpec((None, CONV - 1, 3 * W), lambda b, c: (b, 0, 0)),
                  pl.BlockSpec((None, H, dk, dv), lambda b, c: (b, 0, 0, 0)),
                  pl.BlockSpec((CONV, 3 * W), lambda b, c: (0, 0)),
                  pl.BlockSpec((1, V7X_LANES), lambda b, c: (0, 0)),
                  pl.BlockSpec((1, V7X_LANES), lambda b, c: (0, 0)),
                  pl.BlockSpec((1, dv), lambda b, c: (0, 0))],
        out_specs=[row(W),
                   pl.BlockSpec((None, H, dk, dv), lambda b, c: (b, 0, 0, 0)),
                   pl.BlockSpec((None, CONV - 1, 3 * W), lambda b, c: (b, 0, 0))],
        out_shape=[jax.ShapeDtypeStruct((T, W), bf16),
                   jax.ShapeDtypeStruct((B, H, dk, dv), f32),
                   jax.ShapeDtypeStruct((B, CONV - 1, 3 * W), f32)],
        scratch_shapes=[pltpu.VMEM((GDN_HALO + R, 3 * W), f32), pltpu.VMEM((H, dk, dv), f32)],
        compiler_params=_params(("parallel", "arbitrary"), est),
        name="gdn_core",
    )(qkv, ab, z, conv_state, S0, conv_w, pad(a_log), pad(dt_bias), norm_g.reshape(1, dv))


def kernel(x_prompt, x_sample, mem_prompt, cache_a_k, cache_a_v, cache_b_k, cache_b_v, cache_b_logf, state_gdn, state_gdn_conv, cache_mem_k, cache_mem_v, norm_g, mem_norm_g, final_norm_g, ffn_w_gate, ffn_w_up, ffn_w_down, xa_w_q, xa_w_k, xa_w_v, xa_w_o, ab_w_in, ab_b_f, ab_rel_bias, ab_w_o, gdn_w_in, gdn_conv_w, gdn_a_log, gdn_dt_bias, gdn_norm_g, gdn_w_o):
    depth = norm_g.shape[0]
    BP, SEQ, D = x_prompt.shape
    BS, MS, _ = x_sample.shape
    assert BP == 1
    HA = HB = ab_b_f.shape[1]
    WA = WB = (ab_w_in.shape[2] - HB) // 6
    DHA = WA // HA
    HG = gdn_a_log.shape[1]
    WG = gdn_w_o.shape[1]
    DKG = WG // HG
    HX = cache_mem_k.shape[3]
    DHX = cache_mem_k.shape[4]
    WX = HX * DHX
    PA = cache_a_k.shape[2]
    PB = cache_b_k.shape[2]
    LANES = V7X_LANES

    xp = x_prompt.reshape(SEQ, D)
    xs = x_sample.reshape(BS * MS, D)

    wgate = ffn_w_gate.astype(bf16)
    wup = ffn_w_up.astype(bf16)
    wdown = ffn_w_down.astype(bf16)
    wxq = xa_w_q.astype(bf16)
    wxo = xa_w_o.astype(bf16)
    wxkv = jnp.concatenate([xa_w_k, xa_w_v], axis=-1).astype(bf16)

    outs = {k: [] for k in ("akp", "avp", "bkp", "bvp", "blp", "sgp", "scp", "mkp", "mvp",
                            "aks", "avs", "bks", "bvs", "bls", "sgs", "scs")}

    for l in range(depth):
        xp = _ffn(xp, norm_g[l, 0], wgate[l, 0], wup[l, 0], wdown[l, 0])
        xs = _ffn(xs, norm_g[l, 0], wgate[l, 0], wup[l, 0], wdown[l, 0])

        if l % 2 == 0:
            e = l // 2
            w_in = ab_w_in[e]
            nq = 3 * WA + 3 * WB
            w_pad = jnp.concatenate([w_in, jnp.zeros((D, LANES - HB), f32)], axis=-1).astype(bf16)
            b_pad = jnp.zeros((1, LANES), f32).at[0, :HB].set(ab_b_f[e].astype(f32))
            spec = [(0, WA, bf16, DHA ** -0.5, None),
                    (WA, WA, f32, 1.0, None), (WA, WA, bf16, 1.0, None),
                    (2 * WA, WA, f32, 1.0, None), (2 * WA, WA, bf16, 1.0, None),
                    (3 * WA, WB, bf16, (WB // HB) ** -0.5, None),
                    (3 * WA + WB, WB, f32, 1.0, None), (3 * WA + WB, WB, bf16, 1.0, None),
                    (3 * WA + 2 * WB, WB, f32, 1.0, None), (3 * WA + 2 * WB, WB, bf16, 1.0, None),
                    (nq, LANES, f32, 1.0, "logsig")]
            wo = ab_w_o[e].astype(bf16)

            spec_a = [(0, WA, bf16, DHA ** -0.5 * LOG2E, None)] + spec[1:5]
            qa, ka, kab, va, vab = _norm_proj(xp, norm_g[l, 1], w_pad[:, :3 * WA], spec_a)
            oa = _band_prompt(qa, kab, vab, _band_bias(ab_rel_bias[e]))
            DHB = WB // HB
            spread = lambda w: jnp.pad(w.reshape(D, HB, DHB), ((0, 0), (0, 0), (0, LANES - DHB))).reshape(D, HB * LANES)
            wq_b = w_in[:, 3 * WA:3 * WA + WB]
            wk_b = w_in[:, 3 * WA + WB:3 * WA + 2 * WB]
            w_b = jnp.concatenate([spread(wq_b), wk_b, spread(wk_b), w_in[:, 3 * WA + 2 * WB:nq],
                                   w_pad[:, nq:]], axis=-1).astype(bf16)
            W2 = HB * LANES
            spec_b = [(0, W2, bf16, DHB ** -0.5 * LOG2E, None),
                      (W2, WB, f32, 1.0, None), (W2 + WB, W2, bf16, 1.0, None),
                      (2 * W2 + WB, WB, f32, 1.0, None), (2 * W2 + WB, WB, bf16, 1.0, None),
                      (2 * W2 + 2 * WB, LANES, f32, 1.0, "logsig")]
            qbs, kb, kbs, vb, vbb, lf = _norm_proj(xp, norm_g[l, 1], w_b, spec_b, bias=b_pad)
            F, _ = _cumsum_rows(lf)
            qaug, kaug, vt = _fox_pack(qbs, kbs, vbb, F, HB)
            ob = _fox_prompt(qaug, kaug, vt, F, HB)
            res_p = [(oa, wo[:WA]), (ob, wo[WA:])]
            keep = min(BAND_CHUNKS * CHUNK, SEQ)
            outs["akp"].append(ka[SEQ - keep:].reshape(1, keep, HA, DHA))
            outs["avp"].append(va[SEQ - keep:].reshape(1, keep, HA, DHA))
            outs["bkp"].append(kb.reshape(1, SEQ, HB, WB // HB))
            outs["bvp"].append(vb.reshape(1, SEQ, HB, WB // HB))
            outs["blp"].append(lf[:, :HB].reshape(1, SEQ, HB))

            qa, ka, _, va, _, qb, kb, _, vb, _, lf = _norm_proj(xs, norm_g[l, 1], w_pad, spec, bias=b_pad)
            bias_s = _toeplitz_bias(ab_rel_bias[e], MS, PA + MS, PA).reshape(HA * MS, PA + MS)
            oa = _sample_attn(qa.reshape(BS, MS, WA), cache_a_k[e].reshape(BS, PA, WA), cache_a_v[e].reshape(BS, PA, WA),
                              HA, "rel", ka.reshape(BS, MS, WA), va.reshape(BS, MS, WA),
                              extra=(bias_s[:, :PA], bias_s[:, PA:]))
            lfn = lf[:, :HB].reshape(BS, MS, HB)
            lcat = jnp.concatenate([cache_b_logf[e].astype(f32), lfn], axis=1)
            LP = -(-(PB + MS) // LANES) * LANES
            lcat = jnp.pad(lcat.transpose(1, 0, 2).reshape(PB + MS, BS * HB), ((0, LP - PB - MS), (0, 0)))
            _, FTs = _cumsum_rows(lcat)
            FTs = FTs.reshape(BS, HB, LP)
            fq = FTs[:, :, PB:PB + MS].reshape(BS, HB * MS, 1)
            ob = _sample_attn(qb.reshape(BS, MS, WB), cache_b_k[e].reshape(BS, PB, WB), cache_b_v[e].reshape(BS, PB, WB),
                              HB, "fox", kb.reshape(BS, MS, WB), vb.reshape(BS, MS, WB),
                              extra=(fq, FTs[:, :, :PB], FTs[:, :, PB:PB + MS]))
            res_s = [(oa.reshape(BS * MS, WA), wo[:WA]), (ob.reshape(BS * MS, WB), wo[WA:])]
            outs["aks"].append(ka.reshape(BS, MS, HA, DHA))
            outs["avs"].append(va.reshape(BS, MS, HA, DHA))
            outs["bks"].append(kb.reshape(BS, MS, HB, WB // HB))
            outs["bvs"].append(vb.reshape(BS, MS, HB, WB // HB))
            outs["bls"].append(lfn)
        else:
            o = l // 2
            w_in = gdn_w_in[o]
            w_pad = jnp.concatenate([w_in, jnp.zeros((D, LANES - 2 * HG), f32)], axis=-1).astype(bf16)
            spec = [(0, 3 * WG, f32, 1.0, None), (3 * WG, WG, f32, 1.0, None), (4 * WG, LANES, f32, 1.0, None)]
            wo = gdn_w_o[o].astype(bf16)

            qg, kg, vg, z, ab, cvp = _gdn_proj(xp, norm_g[l, 1], w_pad, jnp.zeros((1, CONV - 1, 3 * WG), f32),
                                               gdn_conv_w[o], HG)
            y, Sp = _gdn_core_pre(qg, kg, vg, ab, z, jnp.zeros((1, HG, DKG, DKG), f32),
                                  gdn_a_log[o], gdn_dt_bias[o], gdn_norm_g[o], CHUNK, 8)
            res_p = [(y, wo)]
            outs["sgp"].append(Sp)
            outs["scp"].append(cvp)

            qkv, z, ab = _norm_proj(xs, norm_g[l, 1], w_pad, spec)
            y, Ss, cvs = _gdn_core(qkv, ab, z, state_gdn_conv[o].astype(f32), state_gdn[o].astype(f32),
                                   gdn_conv_w[o], gdn_a_log[o], gdn_dt_bias[o], gdn_norm_g[o], MS, 1)
            res_s = [(y, wo)]
            outs["sgs"].append(Ss)
            outs["scs"].append(cvs)

        mk, mkb, mv, mvb = _norm_proj(mem_prompt.reshape(MEM, D), mem_norm_g[l], wxkv[l],
                                      [(0, WX, f32, 1.0, None), (0, WX, bf16, 1.0, None),
                                       (WX, WX, f32, 1.0, None), (WX, WX, bf16, 1.0, None)])
        outs["mkp"].append(mk.reshape(1, MEM, HX, DHX))
        outs["mvp"].append(mv.reshape(1, MEM, HX, DHX))
        qspec = [(0, WX, bf16, DHX ** -0.5, None)]
        xp, ocp = _cross_prompt(xp, norm_g[l, 2], wxq[l], res_p, mkb, mvb, HX)
        xs, q = _norm_proj(xs, norm_g[l, 2], wxq[l], qspec, res=res_s)
        ocs = _cross_sample(q.reshape(BS, MS, WX), cache_mem_k, cache_mem_v, l)

        fin = final_norm_g if l == depth - 1 else None
        xp = _ffn(xp, norm_g[l, 3], wgate[l, 1], wup[l, 1], wdown[l, 1], fin, res=(ocp, wxo[l]))
        xs = _ffn(xs, norm_g[l, 3], wgate[l, 1], wup[l, 1], wdown[l, 1], fin, res=(ocs.reshape(BS * MS, WX), wxo[l]))

    st = lambda k: jnp.stack(outs[k])
    return (xp.reshape(BP, SEQ, D), xs.reshape(BS, MS, D),
            st("akp"), st("avp"), st("bkp"), st("bvp"), st("blp"),
            st("sgp"), st("scp"), st("mkp"), st("mvp"),
            st("aks"), st("avs"), st("bks"), st("bvs"), st("bls"),
            st("sgs"), st("scs"))
```

```python
import functools

import numpy as np
import jax
import jax.numpy as jnp
from jax import lax
from jax.experimental import pallas as pl
from jax.experimental.pallas import tpu as pltpu

f32 = jnp.float32
bf16 = jnp.bfloat16
EPS = 1e-6
NEG = -1e30

V7X_VMEM_BYTES = 64 * 1024 * 1024
V7X_LANES = 128
MIB = 1024 * 1024

CHUNK = 64
BAND_CHUNKS = 8
REL_CLIP = 256
CONV = 4
MEM = 256


def _params(sem, est_bytes):
    limit = int(min(V7X_VMEM_BYTES - 8 * MIB, max(32 * MIB, est_bytes + 8 * MIB)))
    return pltpu.CompilerParams(dimension_semantics=sem, vmem_limit_bytes=limit)


def _rms(x, g):
    ms = jnp.mean(x * x, axis=-1, keepdims=True)
    return x * lax.rsqrt(ms + EPS) * g


def _silu(x):
    h = 0.5 * x
    return h + h * jnp.tanh(h)


def _dot(a, b):
    return jnp.dot(a, b, preferred_element_type=f32)


def _dot_nt(a, b):
    return lax.dot_general(a, b, (((1,), (1,)), ((), ())), preferred_element_type=f32)


def _dot_hi(a, b):
    return jnp.dot(a, b, preferred_element_type=f32, precision=lax.Precision.HIGHEST)


def _split2(a):
    hi = a.astype(bf16)
    return hi, (a - hi.astype(f32)).astype(bf16)


def _split3(a):
    hi, rest = a.astype(bf16), a
    rest = rest - hi.astype(f32)
    mid = rest.astype(bf16)
    return hi, mid, (rest - mid.astype(f32)).astype(bf16)


def _ffn_body(*refs, nj, final, has_res):
    x_ref, g_ref, wg_ref, wu_ref, wd_ref = refs[:5]
    pos = 5
    if has_res:
        a_ref, wo_ref = refs[pos:pos + 2]
        pos += 2
    if final:
        gf_ref = refs[pos]
        pos += 1
    o_ref, h_scr, acc = refs[pos:]
    j = pl.program_id(1)

    @pl.when(j == 0)
    def _():
        x = x_ref[...]
        if has_res:
            x = x + _dot(a_ref[...], wo_ref[...])
        o_ref[...] = x
        h_scr[...] = _rms(x, g_ref[...]).astype(bf16)
        acc[...] = jnp.zeros_like(acc)

    h = h_scr[...]
    a = _silu(_dot(h, wg_ref[...])) * _dot(h, wu_ref[...])
    acc[...] += _dot(a.astype(bf16), wd_ref[...])

    @pl.when(j == nj - 1)
    def _():
        y = o_ref[...] + 0.5 * acc[...]
        if final:
            y = _rms(y, gf_ref[...])
        o_ref[...] = y


def _ffn(x, g, wg, wu, wd, final_g=None, res=None):
    T, D = x.shape
    FF = wg.shape[1]
    TM = min(1024, T)
    TF = min(1024, FF)
    nj = FF // TF
    final = final_g is not None
    in_specs = [pl.BlockSpec((TM, D), lambda i, j: (i, 0)),
                pl.BlockSpec((1, D), lambda i, j: (0, 0)),
                pl.BlockSpec((D, TF), lambda i, j: (0, j)),
                pl.BlockSpec((D, TF), lambda i, j: (0, j)),
                pl.BlockSpec((TF, D), lambda i, j: (j, 0))]
    args = [x, g.reshape(1, D), wg, wu, wd]
    est = 2 * (2 * TM * D * 4) + 2 * 3 * D * TF * 2 + TM * D * 6 + 3 * TM * TF * 4
    if res is not None:
        a, wo = res
        K = a.shape[1]
        in_specs += [pl.BlockSpec((TM, K), lambda i, j: (i, 0)), pl.BlockSpec((K, D), lambda i, j: (0, 0))]
        args += [a, wo]
        est += 2 * TM * K * 2 + 2 * K * D * 2
    if final:
        in_specs.append(pl.BlockSpec((1, D), lambda i, j: (0, 0)))
        args.append(final_g.reshape(1, D))
    return pl.pallas_call(
        functools.partial(_ffn_body, nj=nj, final=final, has_res=res is not None),
        grid=(T // TM, nj),
        in_specs=in_specs,
        out_specs=pl.BlockSpec((TM, D), lambda i, j: (i, 0)),
        out_shape=jax.ShapeDtypeStruct((T, D), f32),
        scratch_shapes=[pltpu.VMEM((TM, D), bf16), pltpu.VMEM((TM, D), f32)],
        compiler_params=_params(("parallel", "arbitrary"), est),
        name="ffn",
    )(*args)


def _log_sigmoid(x):
    return jnp.minimum(x, 0.0) - jnp.log(1.0 + jnp.exp(-jnp.abs(x)))


def _norm_proj_body(*refs, outs, has_bias, nres):
    x_ref, g_ref, w_ref = refs[:3]
    pos = 3
    b_ref = None
    if has_bias:
        b_ref = refs[pos]
        pos += 1
    x = x_ref[...]
    for p in range(nres):
        x = x + _dot(refs[pos][...], refs[pos + 1][...])
        pos += 2
    o_refs = refs[pos:]
    if nres:
        o_refs[0][...] = x
        o_refs = o_refs[1:]
    h = _rms(x, g_ref[...]).astype(bf16)
    cache = {}
    for o_ref, (off, n, dt, scale, act) in zip(o_refs, outs):
        if (off, n) not in cache:
            cache[(off, n)] = _dot(h, w_ref[:, off:off + n])
        r = cache[(off, n)]
        if act == "logsig":
            r = _log_sigmoid(r + b_ref[...])
        if scale != 1.0:
            r = r * scale
        o_ref[...] = r.astype(dt)


def _norm_proj(x, g, w, outs, bias=None, res=()):
    T, D = x.shape
    N = w.shape[1]
    TM = min(512, T)
    in_specs = [pl.BlockSpec((TM, D), lambda i: (i, 0)),
                pl.BlockSpec((1, D), lambda i: (0, 0)),
                pl.BlockSpec((D, N), lambda i: (0, 0))]
    args = [x, g.reshape(1, D), w]
    if bias is not None:
        in_specs.append(pl.BlockSpec((1, bias.shape[-1]), lambda i: (0, 0)))
        args.append(bias)
    est = 2 * TM * D * 4 + 2 * D * N * 2 + sum(2 * TM * n * 4 for (_, n, _, _, _) in outs) + TM * N * 4
    for a, wr in res:
        K = a.shape[1]
        in_specs += [pl.BlockSpec((TM, K), lambda i: (i, 0)), pl.BlockSpec((K, D), lambda i: (0, 0))]
        args += [a, wr]
        est += 2 * TM * K * 2 + 2 * K * D * 2
    out_specs = [pl.BlockSpec((TM, n), lambda i: (i, 0)) for (_, n, _, _, _) in outs]
    out_shape = [jax.ShapeDtypeStruct((T, n), dt) for (_, n, dt, _, _) in outs]
    if res:
        out_specs.insert(0, pl.BlockSpec((TM, D), lambda i: (i, 0)))
        out_shape.insert(0, jax.ShapeDtypeStruct((T, D), f32))
        est += 2 * TM * D * 4
    return pl.pallas_call(
        functools.partial(_norm_proj_body, outs=tuple(outs), has_bias=bias is not None, nres=len(res)),
        grid=(T // TM,),
        in_specs=in_specs,
        out_specs=out_specs,
        out_shape=out_shape,
        compiler_params=_params(("parallel",), est),
        name="norm_proj",
    )(*args)


def _cumsum_body(x_ref, f_ref, ft_ref, carry, *, nsub):
    @pl.when(pl.program_id(0) == 0)
    def _():
        carry[...] = jnp.zeros_like(carry)

    r = lax.broadcasted_iota(jnp.int32, (V7X_LANES, V7X_LANES), 0)
    c = lax.broadcasted_iota(jnp.int32, (V7X_LANES, V7X_LANES), 1)
    tri = (r >= c).astype(f32)
    run = carry[...]
    ncol = x_ref.shape[1] // V7X_LANES
    for sb in range(nsub):
        rows = slice(sb * V7X_LANES, (sb + 1) * V7X_LANES)
        blk = _dot_hi(tri, x_ref[rows, :]) + run
        f_ref[rows, :] = blk
        for cb in range(ncol):
            cols = slice(cb * V7X_LANES, (cb + 1) * V7X_LANES)
            ft_ref[cols, rows] = blk[:, cols].T
        run = blk[V7X_LANES - 1:V7X_LANES, :]
    carry[...] = run


def _cumsum_rows(x):
    L, N = x.shape
    nsub = 8 if L % (8 * V7X_LANES) == 0 else 1
    TB = nsub * V7X_LANES
    return pl.pallas_call(
        functools.partial(_cumsum_body, nsub=nsub),
        grid=(L // TB,),
        in_specs=[pl.BlockSpec((TB, N), lambda i: (i, 0))],
        out_specs=[pl.BlockSpec((TB, N), lambda i: (i, 0)), pl.BlockSpec((N, TB), lambda i: (0, i))],
        out_shape=[jax.ShapeDtypeStruct((L, N), f32), jax.ShapeDtypeStruct((N, L), f32)],
        scratch_shapes=[pltpu.VMEM((1, N), f32)],
        compiler_params=_params(("arbitrary",), 8 * TB * N * 4),
        name="cumsum_rows",
    )(x)


BAND_TQ = 256
BAND_NB = 3


def _band_body(q_ref, k0_ref, k1_ref, k2_ref, v0_ref, v1_ref, v2_ref, b_ref, o_ref, *, nheads):
    TQ = BAND_TQ
    lane = lax.broadcasted_iota(jnp.int32, (TQ, V7X_LANES), 1)
    lo = lane < 64
    krefs = (k0_ref, k1_ref, k2_ref)
    vrefs = (v0_ref, v1_ref, v2_ref)
    for hp in range(nheads // 2):
        cols = slice(hp * V7X_LANES, (hp + 1) * V7X_LANES)
        qp = q_ref[:, cols]
        zq = jnp.zeros_like(qp)
        q2 = jnp.concatenate([jnp.where(lo, qp, zq), jnp.where(lo, zq, qp)], axis=0)
        s = []
        for b in range(BAND_NB):
            blk = slice(b * TQ, (b + 1) * TQ)
            bias2 = jnp.concatenate([b_ref[2 * hp, :, blk], b_ref[2 * hp + 1, :, blk]], axis=0)
            s.append(_dot_nt(q2, krefs[b][:, cols]) + bias2)
        smax = s[0]
        for b in range(1, BAND_NB):
            smax = jnp.maximum(smax, s[b])
        m = jnp.max(smax, axis=-1, keepdims=True)
        p = [jnp.exp2(sb - m) for sb in s]
        psum = p[0]
        for b in range(1, BAND_NB):
            psum = psum + p[b]
        l = psum.sum(axis=-1, keepdims=True)
        pcat, vcat = [], []
        for b in range(BAND_NB):
            vp = vrefs[b][:, cols]
            zv = jnp.zeros_like(vp)
            vcat += [jnp.where(lo, vp, zv), jnp.where(lo, zv, vp)]
            pb = p[b].astype(bf16)
            pcat += [pb[:TQ], pb[TQ:]]
        o = _dot(jnp.concatenate(pcat, axis=1), jnp.concatenate(vcat, axis=0))
        o_ref[:, cols] = (o / jnp.where(lo, l[:TQ], l[TQ:])).astype(bf16)


def _toeplitz_bias(table, nq, nk, off):
    H = table.shape[0]
    m = np.arange(nq + nk - 1)
    idx = np.clip(off + nq - 1 - m, -REL_CLIP, REL_CLIP) + REL_CLIP
    w = jnp.concatenate([table[:, idx].astype(f32), jnp.zeros((H, 1), f32)], axis=-1)
    skew = jnp.tile(w, (1, nq))[:, :nq * (nq + nk - 1)].reshape(H, nq, nq + nk - 1)
    return skew[:, :, nq - 1:]


def _band_bias(table):
    qc = np.arange(BAND_TQ)[:, None] // CHUNK
    ki = np.arange(BAND_NB * BAND_TQ)[None, :]
    kc = ki // CHUNK
    in_band = (kc >= qc) & (kc <= qc + BAND_CHUNKS)
    valid = np.stack([in_band & (ki // BAND_TQ >= BAND_NB - 1 - n) for n in range(BAND_NB)])
    bias = _toeplitz_bias(table, BAND_TQ, BAND_NB * BAND_TQ, BAND_CHUNKS * CHUNK) * LOG2E
    return jnp.where(valid[:, None], bias[None], NEG)


def _band_prompt(q, k, v, bias):
    L, W = q.shape
    TQ = BAND_TQ
    H = bias.shape[1]
    kspec = [pl.BlockSpec((TQ, W), functools.partial(lambda i, d: (jnp.maximum(i - d, 0), 0), d=d))
             for d in (2, 1, 0)]
    est = 2 * 7 * TQ * W * 2 + 2 * H * TQ * BAND_NB * TQ * 4 + 16 * TQ * TQ * 4
    return pl.pallas_call(
        functools.partial(_band_body, nheads=H),
        grid=(L // TQ,),
        in_specs=[pl.BlockSpec((TQ, W), lambda i: (i, 0))] + kspec + kspec
                 + [pl.BlockSpec((None, H, TQ, BAND_NB * TQ), lambda i: (jnp.minimum(i, BAND_NB - 1), 0, 0, 0))],
        out_specs=pl.BlockSpec((TQ, W), lambda i: (i, 0)),
        out_shape=jax.ShapeDtypeStruct((L, W), bf16),
        compiler_params=_params(("parallel",), est),
        name="band_prompt",
    )(q, k, k, k, v, v, v, bias)


FOX_T = 1024
FOX_VROWS = 80
FOX_QPIECE = 64
FOX_KPIECE = 67
LOG2E = 1.4426950408889634


def _fox_pack_body(q_ref, k_ref, v_ref, f_ref, e_ref, qa_ref, ka_ref, vt_ref, *, nheads, dh):
    T, W2 = q_ref.shape
    pieces = _split3((f_ref[...] - f_ref[0:1, :]) * LOG2E)
    aug = _dot(jnp.concatenate(pieces, axis=1), e_ref[...])
    lane = lax.broadcasted_iota(jnp.int32, (T, W2), 1) % V7X_LANES
    ones_q = jnp.where((lane >= FOX_KPIECE) & (lane < FOX_KPIECE + 3), 1.0, 0.0)
    ones_k = jnp.where((lane >= FOX_QPIECE) & (lane < FOX_QPIECE + 3), 1.0, 0.0)
    qa_ref[...] = (q_ref[...].astype(f32) + aug[:, :W2] + ones_q).astype(bf16)
    ka_ref[...] = (k_ref[...].astype(f32) + aug[:, W2:] + ones_k).astype(bf16)
    vT = v_ref[...].astype(f32).T
    pad = FOX_VROWS - dh
    ones_row = jnp.where(lax.broadcasted_iota(jnp.int32, (pad, T), 0) == 0, 1.0, 0.0).astype(bf16)
    for h in range(nheads):
        vt_ref[h, 0:dh, :] = vT[h * dh:(h + 1) * dh, :].astype(bf16)
        vt_ref[h, dh:FOX_VROWS, :] = ones_row


def _fox_pack(q, k, v, F, nheads):
    L, W2 = q.shape
    W = v.shape[1]
    dh = W // nheads
    T = FOX_T
    e = np.zeros((3 * V7X_LANES, 2 * W2), np.float32)
    for h in range(nheads):
        for t in range(3):
            e[t * V7X_LANES + h, h * V7X_LANES + FOX_QPIECE + t] = 1.0
            e[t * V7X_LANES + h, W2 + h * V7X_LANES + FOX_KPIECE + t] = -1.0
    est = 2 * (4 * T * W2 * 2 + T * W * 2 + nheads * FOX_VROWS * T * 2) + 6 * T * W2 * 4
    return pl.pallas_call(
        functools.partial(_fox_pack_body, nheads=nheads, dh=dh),
        grid=(L // T,),
        in_specs=[pl.BlockSpec((T, W2), lambda i: (i, 0)),
                  pl.BlockSpec((T, W2), lambda i: (i, 0)),
                  pl.BlockSpec((T, W), lambda i: (i, 0)),
                  pl.BlockSpec((T, V7X_LANES), lambda i: (i, 0)),
                  pl.BlockSpec((3 * V7X_LANES, 2 * W2), lambda i: (0, 0))],
        out_specs=[pl.BlockSpec((T, W2), lambda i: (i, 0)),
                   pl.BlockSpec((T, W2), lambda i: (i, 0)),
                   pl.BlockSpec((nheads, FOX_VROWS, T), lambda i: (0, 0, i))],
        out_shape=[jax.ShapeDtypeStruct((L, W2), bf16), jax.ShapeDtypeStruct((L, W2), bf16),
                   jax.ShapeDtypeStruct((nheads, FOX_VROWS, L), bf16)],
        compiler_params=_params(("parallel",), est),
        name="fox_pack",
    )(q, k, v, F, jnp.asarray(e, bf16))


def _fox_body(qi_ref, kj_ref, qa_ref, ka_ref, vt_ref, d_ref, o_ref, m_scr, acc_scr, *, nheads, dh):
    s_id = pl.program_id(0)
    i = qi_ref[s_id]
    j = kj_ref[s_id]
    T = FOX_T

    @pl.when(j == 0)
    def _():
        m_scr[...] = jnp.full_like(m_scr, NEG)
        acc_scr[...] = jnp.zeros_like(acc_scr)

    def step(masked):
        if masked:
            keep = lax.broadcasted_iota(jnp.int32, (T, T), 0) <= lax.broadcasted_iota(jnp.int32, (T, T), 1)
        def scores(h):
            cols = slice(h * V7X_LANES, (h + 1) * V7X_LANES)
            st = _dot_nt(ka_ref[:, cols], qa_ref[:, cols])
            return jnp.where(keep, st, NEG) if masked else st

        nxt = scores(0)
        for h in range(nheads):
            st = nxt
            if h + 1 < nheads:
                nxt = scores(h + 1)
            d = d_ref[s_id * nheads + h]
            m_prev = m_scr[h:h + 1, :]
            m_new = jnp.maximum(m_prev, jnp.max(st, axis=0, keepdims=True) + d)
            p = jnp.exp2(st - (m_new - d))
            alpha = jnp.exp2(m_prev - m_new)
            acc_scr[h] = acc_scr[h] * alpha + _dot(vt_ref[h], p.astype(bf16))
            m_scr[h:h + 1, :] = m_new

    @pl.when(j < i)
    def _():
        step(False)

    @pl.when(j == i)
    def _():
        step(True)
        for hp in range(nheads // 2):
            a0 = acc_scr[2 * hp]
            a1 = acc_scr[2 * hp + 1]
            o2 = jnp.concatenate([a0[0:dh, :] / a0[dh:dh + 1, :], a1[0:dh, :] / a1[dh:dh + 1, :]], axis=0)
            o_ref[:, hp * V7X_LANES:(hp + 1) * V7X_LANES] = o2.T.astype(bf16)


def _fox_prompt(qa, ka, vt, F, nheads):
    L, W2 = qa.shape
    T = FOX_T
    n = L // T
    dh = V7X_LANES // 2
    W = nheads * dh
    pairs = [(i, j) for i in range(n) for j in range(i + 1)]
    qi = np.array([p[0] for p in pairs], np.int32)
    kj = np.array([p[1] for p in pairs], np.int32)
    fs = F[::T, :nheads]
    d = ((fs[qi] - fs[kj]) * LOG2E).reshape(-1)
    grid_spec = pltpu.PrefetchScalarGridSpec(
        num_scalar_prefetch=2,
        grid=(len(pairs),),
        in_specs=[pl.BlockSpec((T, W2), lambda s, qi, kj: (qi[s], 0)),
                  pl.BlockSpec((T, W2), lambda s, qi, kj: (kj[s], 0)),
                  pl.BlockSpec((nheads, FOX_VROWS, T), lambda s, qi, kj: (0, 0, kj[s])),
                  pl.BlockSpec(memory_space=pltpu.SMEM)],
        out_specs=pl.BlockSpec((T, W), lambda s, qi, kj: (qi[s], 0)),
        scratch_shapes=[pltpu.VMEM((nheads, T), f32), pltpu.VMEM((nheads, FOX_VROWS, T), f32)],
    )
    est = 2 * (2 * T * W2 * 2 + nheads * FOX_VROWS * T * 2 + T * W * 2) + nheads * FOX_VROWS * T * 4 + 24 * T * T * 4
    return pl.pallas_call(
        functools.partial(_fox_body, nheads=nheads, dh=dh),
        grid_spec=grid_spec,
        out_shape=jax.ShapeDtypeStruct((L, W), bf16),
        compiler_params=_params(("arbitrary",), est),
        name="fox_prompt",
    )(jnp.asarray(qi), jnp.asarray(kj), qa, ka, vt, d)


def _cross_body(*refs, nres, nheads, scale):
    x_ref, g_ref, wq_ref = refs[:3]
    x = x_ref[...]
    for p in range(nres):
        x = x + _dot(refs[3 + 2 * p][...], refs[4 + 2 * p][...])
    k_ref, v_ref, xo_ref, o_ref = refs[3 + 2 * nres:]
    xo_ref[...] = x
    q = (_dot(_rms(x, g_ref[...]).astype(bf16), wq_ref[...]) * scale).astype(bf16)
    dh = q.shape[1] // nheads
    hs = range(nheads)
    s = [_dot_nt(q[:, h * dh:(h + 1) * dh], k_ref[:, h * dh:(h + 1) * dh]) for h in hs]
    m = [jnp.max(sh, axis=-1, keepdims=True) for sh in s]
    p = [jnp.exp(s[h] - m[h]) for h in hs]
    l = [ph.sum(axis=-1, keepdims=True) for ph in p]
    o = [_dot(p[h].astype(bf16), v_ref[:, h * dh:(h + 1) * dh]) / l[h] for h in hs]
    o_ref[...] = jnp.concatenate(o, axis=-1).astype(bf16)


def _cross_prompt(x, g, wq, res, mk, mv, nheads):
    T, D = x.shape
    W = wq.shape[1]
    S = mk.shape[0]
    TM = min(512, T)
    row = lambda width: pl.BlockSpec((TM, width), lambda i: (i, 0))
    whole = lambda r, c: pl.BlockSpec((r, c), lambda i: (0, 0))
    in_specs = [row(D), whole(1, D), whole(D, W)]
    args = [x, g.reshape(1, D), wq]
    est = 4 * TM * D * 4 + 2 * D * W * 2 + 4 * S * W * 2 + 2 * TM * W * 2 + 3 * TM * W * 4 + 8 * TM * S * 4
    for a, wr in res:
        K = a.shape[1]
        in_specs += [row(K), whole(K, D)]
        args += [a, wr]
        est += 2 * TM * K * 2 + 2 * K * D * 2
    in_specs += [whole(S, W), whole(S, W)]
    args += [mk, mv]
    return pl.pallas_call(
        functools.partial(_cross_body, nres=len(res), nheads=nheads, scale=(W // nheads) ** -0.5),
        grid=(T // TM,),
        in_specs=in_specs,
        out_specs=[row(D), row(W)],
        out_shape=[jax.ShapeDtypeStruct((T, D), f32), jax.ShapeDtypeStruct((T, W), bf16)],
        compiler_params=_params(("parallel",), est),
        name="cross_prompt",
    )(*args)


SAMPLE_INNER = 2


def _sample_attn_body(*refs, nheads, mode):
    shared = (5, 6) if mode == "rel" else ()
    for g in range(SAMPLE_INNER):
        _sample_attn_one(*[r if n in shared else r.at[g] for n, r in enumerate(refs)], nheads=nheads, mode=mode)


def _sample_attn_one(*refs, nheads, mode):
    q_ref, kc_ref, vc_ref, kn_ref, vn_ref = refs[:5]
    if mode == "rel":
        bc_ref, bn_ref, o_ref = refs[5:]
    else:
        fq_ref, fkc_ref, fkn_ref, o_ref = refs[5:]

    m_q, W = q_ref.shape
    dh = W // nheads
    R = nheads * m_q
    q = q_ref[...].astype(f32)
    qt = jnp.concatenate([q] * nheads, axis=0)
    hrow = lax.broadcasted_iota(jnp.int32, (R, W), 0) // m_q
    hlane = lax.broadcasted_iota(jnp.int32, (R, W), 1) // dh
    qbd = jnp.where(hrow == hlane, qt, 0.0).astype(bf16)

    def rows_of(f_ref):
        n = f_ref.shape[-1]
        return jnp.concatenate([jnp.broadcast_to(f_ref[h:h + 1, :], (m_q, n)) for h in range(nheads)], axis=0)

    sc = _dot_nt(qbd, kc_ref[...].astype(bf16))
    sn = _dot_nt(qbd, kn_ref[...].astype(bf16))
    if mode == "rel":
        sc = sc + bc_ref[...]
        sn = sn + bn_ref[...]
    else:
        sc = sc + (fq_ref[...] - rows_of(fkc_ref))
        sn = sn + (fq_ref[...] - rows_of(fkn_ref))
        qpos = lax.broadcasted_iota(jnp.int32, (R, m_q), 0) % m_q
        kpos = lax.broadcasted_iota(jnp.int32, (R, m_q), 1)
        sn = jnp.where(kpos <= qpos, sn, NEG)
    m = jnp.maximum(jnp.max(sc, axis=-1, keepdims=True), jnp.max(sn, axis=-1, keepdims=True))
    pc = jnp.exp(sc - m)
    pn = jnp.exp(sn - m)
    l = pc.sum(axis=-1, keepdims=True) + pn.sum(axis=-1, keepdims=True)
    o = _dot(pc.astype(bf16), vc_ref[...].astype(bf16)) + _dot(pn.astype(bf16), vn_ref[...].astype(bf16))
    o = o / l
    hl = lax.broadcasted_iota(jnp.int32, (m_q, W), 1) // dh
    out = jnp.zeros((m_q, W), f32)
    for h in range(nheads):
        out = out + jnp.where(hl == h, o[h * m_q:(h + 1) * m_q, :], 0.0)
    o_ref[...] = out.astype(bf16)


def _sample_attn(q, kc, vc, nheads, mode, kn, vn, extra):
    B, m_q, W = q.shape
    P = kc.shape[1]
    R = nheads * m_q
    G = SAMPLE_INNER
    per_b = lambda *shape: pl.BlockSpec((G,) + shape, lambda b: (b,) + (0,) * len(shape))
    shared = lambda *shape: pl.BlockSpec(shape, lambda b: (0,) * len(shape))
    in_specs = [per_b(m_q, W), per_b(P, W), per_b(P, W), per_b(m_q, W), per_b(m_q, W)]
    if mode == "rel":
        in_specs += [shared(R, P), shared(R, m_q)]
    else:
        in_specs += [per_b(R, 1), per_b(nheads, P), per_b(nheads, m_q)]
    est = G * (2 * 2 * P * W * 4 + 2 * P * W * 2 + 6 * R * P * 4 + 4 * R * W * 4)
    return pl.pallas_call(
        functools.partial(_sample_attn_body, nheads=nheads, mode=mode),
        grid=(B // G,),
        in_specs=in_specs,
        out_specs=per_b(m_q, W),
        out_shape=jax.ShapeDtypeStruct((B, m_q, W), bf16),
        compiler_params=_params(("parallel",), est),
        name="sample_attn_" + mode,
    )(q, kc, vc, kn, vn, *extra)


def _cross_sample_body(q_ref, kc_ref, vc_ref, o_ref, *, nheads):
    for g in range(SAMPLE_INNER):
        _cross_sample_one(q_ref.at[g], kc_ref.at[g], vc_ref.at[g], o_ref.at[g], nheads=nheads)


def _cross_sample_one(q_ref, kc_ref, vc_ref, o_ref, *, nheads):
    m_q, W = q_ref.shape
    dh = W // nheads
    npc = dh // kc_ref.shape[1]
    P = kc_ref.shape[0] // (nheads * npc)
    hs = range(nheads)

    def head_rows(ref, h):
        pieces = [ref[pl.ds(h * npc + j, P, stride=nheads * npc), :] for j in range(npc)]
        return jnp.concatenate(pieces, axis=-1).astype(bf16)

    q = q_ref[...]
    sc = [_dot_nt(q[:, h * dh:(h + 1) * dh], head_rows(kc_ref, h)) for h in hs]
    m = [jnp.max(s, axis=-1, keepdims=True) for s in sc]
    pc = [jnp.exp(sc[h] - m[h]) for h in hs]
    l = [p.sum(axis=-1, keepdims=True) for p in pc]
    o = [_dot(pc[h].astype(bf16), head_rows(vc_ref, h)) / l[h] for h in hs]
    o_ref[...] = jnp.concatenate(o, axis=-1).astype(bf16)


def _cross_sample(q, kc_all, vc_all, layer):
    B, m_q, W = q.shape
    nl, _, P, nheads, dh = kc_all.shape
    nrows = P * nheads * (dh // V7X_LANES)
    kc_all = kc_all.reshape(nl, B, nrows, V7X_LANES)
    vc_all = vc_all.reshape(nl, B, nrows, V7X_LANES)
    G = SAMPLE_INNER
    cache = pl.BlockSpec((None, G, nrows, V7X_LANES), lambda b: (layer, b, 0, 0))
    qo = pl.BlockSpec((G, m_q, W), lambda b: (b, 0, 0))
    est = G * (2 * 2 * nrows * V7X_LANES * 4 + 4 * P * W * 2 + 8 * nheads * m_q * P * 4)
    return pl.pallas_call(
        functools.partial(_cross_sample_body, nheads=nheads),
        grid=(B // G,),
        in_specs=[qo, cache, cache],
        out_specs=qo,
        out_shape=jax.ShapeDtypeStruct((B, m_q, W), bf16),
        compiler_params=_params(("parallel",), est),
        name="cross_sample",
    )(q, kc_all, vc_all)


GDN_HALO = 8


def _softplus(x):
    return jnp.maximum(x, 0.0) + jnp.log(1.0 + jnp.exp(-jnp.abs(x)))


def _gdn_conv_qkv(xbuf, R, cw_ref, nheads, dk):
    H0 = GDN_HALO
    tail = xbuf[H0 + R - (CONV - 1):H0 + R, :]

    def act_cols(c0):
        cols = slice(c0, c0 + dk)
        full = xbuf[0:H0 + R, cols]
        conv = pltpu.roll(full, CONV - 1, 0)[H0:, :] * cw_ref[0:1, cols]
        for jj in range(1, CONV):
            sh = CONV - 1 - jj
            conv = conv + (pltpu.roll(full, sh, 0) if sh else full)[H0:, :] * cw_ref[jj:jj + 1, cols]
        return _silu(conv)

    def l2n(a):
        return a * lax.rsqrt(jnp.sum(a * a, axis=-1, keepdims=True) + EPS)

    W = nheads * dk
    qn = [l2n(act_cols(h * dk)) * (dk ** -0.5) for h in range(nheads)]
    kn = [l2n(act_cols(W + h * dk)) for h in range(nheads)]
    vn = [act_cols(2 * W + h * dk) for h in range(nheads)]
    return qn, kn, vn, tail


def _gdn_proj_body(x_ref, g_ref, w_ref, cs_ref, cw_ref, q_ref, k_ref, v_ref, z_ref, ab_ref, cf_ref, xbuf,
                   *, nsteps, nheads, dk):
    i = pl.program_id(0)
    W = nheads * dk
    TM = x_ref.shape[0]
    H0 = GDN_HALO

    @pl.when(i == 0)
    def _():
        xbuf[...] = jnp.zeros_like(xbuf)

    qn, kn, vn, tail = _gdn_conv_qkv(xbuf, TM, cw_ref, nheads, dk)
    q_ref[...] = jnp.concatenate(qn, axis=-1)
    k_ref[...] = jnp.concatenate(kn, axis=-1)
    v_ref[...] = jnp.concatenate(vn, axis=-1)

    xbuf[H0 - (CONV - 1):H0, :] = jnp.where(i == 0, cs_ref[...], tail)
    cf_ref[...] = tail

    h = _rms(x_ref[...], g_ref[...]).astype(bf16)
    xbuf[H0:H0 + TM, :] = _dot(h, w_ref[:, :3 * W])
    z_ref[...] = _dot(h, w_ref[:, 3 * W:4 * W])
    ab_ref[...] = _dot(h, w_ref[:, 4 * W:])


def _gdn_proj(x, g, w, conv_state, conv_w, nheads):
    T, D = x.shape
    N = w.shape[1]
    W = (N - V7X_LANES) // 4
    dk = W // nheads
    TM = min(512, T)
    nsteps = T // TM
    row = lambda width: pl.BlockSpec((TM, width), lambda i: (jnp.minimum(i, nsteps - 1), 0))
    late = lambda width: pl.BlockSpec((TM, width), lambda i: (jnp.maximum(i - 1, 0), 0))
    whole = lambda *shape: pl.BlockSpec(shape, lambda i: (0,) * len(shape))
    est = 2 * TM * D * 4 + 2 * D * N * 2 + 2 * TM * (4 * W + V7X_LANES) * 4 + (GDN_HALO + TM) * 3 * W * 4 + 4 * TM * 3 * W * 4
    q, k, v, z, ab, cf = pl.pallas_call(
        functools.partial(_gdn_proj_body, nsteps=nsteps, nheads=nheads, dk=dk),
        grid=(nsteps + 1,),
        in_specs=[row(D), whole(1, D), whole(D, N),
                  pl.BlockSpec((None, CONV - 1, 3 * W), lambda i: (0, 0, 0)), whole(CONV, 3 * W)],
        out_specs=[late(W), late(W), late(W), row(W), row(V7X_LANES),
                   pl.BlockSpec((None, CONV - 1, 3 * W), lambda i: (0, 0, 0))],
        out_shape=[jax.ShapeDtypeStruct((T, W), f32)] * 4
                  + [jax.ShapeDtypeStruct((T, V7X_LANES), f32), jax.ShapeDtypeStruct((1, CONV - 1, 3 * W), f32)],
        scratch_shapes=[pltpu.VMEM((GDN_HALO + TM, 3 * W), f32)],
        compiler_params=_params(("arbitrary",), est),
        name="gdn_proj",
    )(x, g.reshape(1, D), w, conv_state, conv_w)
    return q, k, v, z, ab, cf


def _gdn_body(*refs, C, NS, nsteps, nheads, dk, pre):
    if pre:
        q_ref, k_ref, v_ref, ab_ref, z_ref, s0_ref, alog_ref, dtb_ref, ng_ref, o_ref, sf_ref, s_scr = refs
    else:
        (x_ref, ab_ref, z_ref, cs_ref, s0_ref, cw_ref, alog_ref, dtb_ref, ng_ref,
         o_ref, sf_ref, cf_ref, xbuf, s_scr) = refs
    c = pl.program_id(1)
    R = NS * C
    G = V7X_LANES // C
    NG = nheads // G
    GW = G * dk
    hs = range(nheads)

    @pl.when(c == 0)
    def _():
        s_scr[...] = s0_ref[...]

    if pre:
        qn = [q_ref[:, h * dk:(h + 1) * dk] for h in hs]
        kn = [k_ref[:, h * dk:(h + 1) * dk] for h in hs]
        vn = [v_ref[:, h * dk:(h + 1) * dk] for h in hs]
    else:
        H0 = GDN_HALO

        @pl.when(c == 0)
        def _():
            xbuf[H0 - (CONV - 1):H0, :] = cs_ref[...]

        xbuf[H0:H0 + R, :] = x_ref[...]
        qn, kn, vn, tail = _gdn_conv_qkv(xbuf, R, cw_ref, nheads, dk)
        xbuf[H0 - (CONV - 1):H0, :] = tail

        @pl.when(c == nsteps - 1)
        def _():
            cf_ref[...] = tail

    ab = ab_ref[...]
    gfull = -jnp.exp(alog_ref[...]) * _softplus(ab + dtb_ref[...])
    bfull = 1.0 / (1.0 + jnp.exp(-ab))
    z = z_ref[...]
    ng = ng_ref[...]

    r_cc = lax.broadcasted_iota(jnp.int32, (C, C), 0)
    c_cc = lax.broadcasted_iota(jnp.int32, (C, C), 1)
    tri_f = (r_cc >= c_cc).astype(f32)
    ri = lax.broadcasted_iota(jnp.int32, (C, V7X_LANES), 0)
    jl = lax.broadcasted_iota(jnp.int32, (C, V7X_LANES), 1) % C
    lblk = lax.broadcasted_iota(jnp.int32, (C, V7X_LANES), 1) // C
    tri_g = ri >= jl
    strict_g = ri > jl
    eye_g = ri == jl
    r128 = lax.broadcasted_iota(jnp.int32, (V7X_LANES, V7X_LANES), 0)
    l128 = lax.broadcasted_iota(jnp.int32, (V7X_LANES, V7X_LANES), 1)
    same_blk = (r128 // C) == (l128 // C)
    wide_blk = (lax.broadcasted_iota(jnp.int32, (V7X_LANES, GW), 0) // C
                == lax.broadcasted_iota(jnp.int32, (V7X_LANES, GW), 1) // dk)

    def bd_sq(p):
        return jnp.where(same_blk, jnp.concatenate([p] * G, axis=0), jnp.zeros((), p.dtype))

    def bd_wide(xs):
        row = jnp.concatenate(xs, axis=1)
        return jnp.where(wide_blk, jnp.concatenate([row] * G, axis=0), 0.0).astype(bf16)

    def gmm3(xs, pg):
        n = len(xs)
        parts = [_split2(x) for x in xs]
        ph, plo = _split2(pg)
        top = _dot(jnp.concatenate([p for hl in parts for p in hl], axis=0), bd_sq(ph))
        bot = _dot(jnp.concatenate([hl[0] for hl in parts], axis=0), bd_sq(plo)) if n > 1 else _dot(parts[0][0], bd_sq(plo))
        return [top[2 * k * C:(2 * k + 1) * C] + (bot[k * C:(k + 1) * C] + top[(2 * k + 1) * C:(2 * k + 2) * C])
                for k in range(n)]

    sls = [slice(s * C, (s + 1) * C) for s in range(NS)]
    gcs = [_dot_hi(tri_f, gfull[sl]) for sl in sls]
    egcs = [jnp.exp(gc) for gc in gcs]
    kdecs = [jnp.exp(gc[C - 1:C, :] - gc) for gc in gcs]
    gcTs = [jnp.concatenate([gc] * G, axis=0).T for gc in gcs]
    bcols_s = [[bfull[sl, nheads + h:nheads + h + 1] for h in hs] for sl in sls]
    ecols_s = [[egc[:, h:h + 1] for h in hs] for egc in egcs]
    kbs_s = [[kn[h][sls[s]] * bcols_s[s][h] for h in hs] for s in range(NS)]
    Ms, Aqks = [], []
    for s in range(NS):
        sl, gc, gcT, kbs = sls[s], gcs[s], gcTs[s], kbs_s[s]
        for grp in range(NG):
            heads = range(grp * G, (grp + 1) * G)
            gcol = jnp.broadcast_to(gc[:, grp * G:grp * G + 1], (C, V7X_LANES))
            grow = gcT[grp * G:grp * G + 1, :]
            for g in range(1, G):
                h = grp * G + g
                gcol = jnp.where(lblk == g, gc[:, h:h + 1], gcol)
                grow = jnp.where(lblk[0:1, :] == g, gcT[h:h + 1, :], grow)
            Lm = jnp.where(tri_g, jnp.exp(jnp.where(tri_g, gcol - grow, 0.0)), 0.0)
            kbd = bd_wide([kn[h][sl] for h in heads])
            kb_row = jnp.concatenate([kbs[h] for h in heads], axis=1)
            q_row = jnp.concatenate([qn[h][sl] for h in heads], axis=1)
            kq = _dot_nt(jnp.concatenate([kb_row, q_row], axis=0).astype(bf16), kbd)
            Ms.append(jnp.where(strict_g, kq[:C] * Lm, 0.0))
            Aqks.append(jnp.where(tri_g, kq[C:] * Lm, 0.0).astype(bf16))
    Xs = [jnp.where(eye_g, 1.0, 0.0) - M for M in Ms]
    Pws = [gmm3([M], M)[0] for M in Ms]
    e = 2
    while e < C:
        e *= 2
        if e < C:
            nxt = [gmm3([X, Pw], Pw) for X, Pw in zip(Xs, Pws)]
            Xs = [X + r[0] for X, r in zip(Xs, nxt)]
            Pws = [r[1] for r in nxt]
        else:
            Xs = [X + gmm3([X], Pw)[0] for X, Pw in zip(Xs, Pws)]
    prep = []
    for s in range(NS):
        sl, bcols, ecols, kbs = sls[s], bcols_s[s], ecols_s[s], kbs_s[s]
        us, ws = [], []
        for grp in range(NG):
            heads = range(grp * G, (grp + 1) * G)
            Tm = Xs[s * NG + grp].astype(bf16)
            U = _dot(Tm, bd_wide([vn[h][sl] * bcols[h] for h in heads]))
            Wm = _dot(Tm, bd_wide([kbs[h] * ecols[h] for h in heads]))
            for g, h in enumerate(heads):
                us.append(U[:, g * dk:(g + 1) * dk])
                ws.append(Wm[:, g * dk:(g + 1) * dk].astype(bf16))
        qgs = [(qn[h][sl] * ecols[h]).astype(bf16) for h in hs]
        kgTs = [(kn[h][sl] * kdecs[s][:, h:h + 1]).T.astype(bf16) for h in hs]
        decs = [egcs[s][C - 1:C, h:h + 1] for h in hs]
        wqs = [jnp.concatenate([ws[h], qgs[h]], axis=0) for h in hs]
        prep.append((us, wqs, Aqks[s * NG:(s + 1) * NG], kgTs, decs))

    Ss = [s_scr[h] for h in hs]
    for s in range(NS):
        us, wqs, Aqks, kgTs, decs = prep[s]
        wqS = [_dot(wqs[h], Ss[h].astype(bf16)) for h in hs]
        vnews = [us[h] - wqS[h][:C] for h in hs]
        vnbs = [v.astype(bf16) for v in vnews]
        intra = [_dot(Aqks[grp], bd_wide([vnews[h] for h in range(grp * G, (grp + 1) * G)])) for grp in range(NG)]
        os_ = [wqS[h][C:] + intra[h // G][:, (h % G) * dk:(h % G + 1) * dk] for h in hs]
        Ss = [Ss[h] * decs[h] + _dot(kgTs[h], vnbs[h]) for h in hs]
        zs = z[s * C:(s + 1) * C, :]
        ys = [(_rms(os_[h], ng) * _silu(zs[:, h * dk:(h + 1) * dk])).astype(bf16) for h in hs]
        o_ref[s * C:(s + 1) * C, :] = jnp.concatenate(ys, axis=-1)
    for h in hs:
        s_scr[h] = Ss[h]

    @pl.when(c == nsteps - 1)
    def _():
        sf_ref[...] = s_scr[...]


def _gdn_core_pre(q, k, v, ab, z, S0, a_log, dt_bias, norm_g, C, NS):
    T = q.shape[0]
    B, H, dk, dv = S0.shape
    W = H * dk
    R = NS * C
    nsteps = T // (B * R)
    pad = lambda vec: jnp.zeros((1, V7X_LANES), f32).at[0, :H].set(vec.astype(f32))
    row = lambda width: pl.BlockSpec((R, width), lambda b, c: (b * nsteps + c, 0))
    lane_row = pl.BlockSpec((1, V7X_LANES), lambda b, c: (0, 0))
    state = pl.BlockSpec((None, H, dk, dv), lambda b, c: (b, 0, 0, 0))
    est = 2 * R * (4 * W + V7X_LANES) * 4 + 3 * H * dk * dv * 4 * 2 + 16 * R * W * 4
    return pl.pallas_call(
        functools.partial(_gdn_body, C=C, NS=NS, nsteps=nsteps, nheads=H, dk=dk, pre=True),
        grid=(B, nsteps),
        in_specs=[row(W), row(W), row(W), row(V7X_LANES), row(W), state, lane_row, lane_row,
                  pl.BlockSpec((1, dv), lambda b, c: (0, 0))],
        out_specs=[row(W), state],
        out_shape=[jax.ShapeDtypeStruct((T, W), bf16), jax.ShapeDtypeStruct((B, H, dk, dv), f32)],
        scratch_shapes=[pltpu.VMEM((H, dk, dv), f32)],
        compiler_params=_params(("parallel", "arbitrary"), est),
        name="gdn_core_pre",
    )(q, k, v, ab, z, S0, pad(a_log), pad(dt_bias), norm_g.reshape(1, dv))


def _gdn_core(qkv, ab, z, conv_state, S0, conv_w, a_log, dt_bias, norm_g, C, NS):
    T = qkv.shape[0]
    B, H, dk, dv = S0.shape
    W = H * dk
    R = NS * C
    nsteps = T // (B * R)
    pad = lambda vec: jnp.zeros((1, V7X_LANES), f32).at[0, :H].set(vec.astype(f32))
    row = lambda width: pl.BlockSpec((R, width), lambda b, c: (b * nsteps + c, 0))
    est = 2 * R * (3 * W + W + V7X_LANES) * 4 + 3 * H * dk * dv * 4 * 2 + 12 * R * 3 * W * 4
    return pl.pallas_call(
        functools.partial(_gdn_body, C=C, NS=NS, nsteps=nsteps, nheads=H, dk=dk, pre=False),
        grid=(B, nsteps),
        in_specs=[row(3 * W), row(V7X_LANES), row(W),
                  pl.BlockSpec((None, CONV - 1, 3 * W), lambda b, c: (b, 0, 0)),
                  pl.BlockSpec((None, H, dk, dv), lambda b, c: (b, 0, 0, 0)),
                  pl.BlockSpec((CONV, 3 * W), lambda b, c: (0, 0)),
                  pl.BlockSpec((1, V7X_LANES), lambda b, c: (0, 0)),
                  pl.BlockSpec((1, V7X_LANES), lambda b, c: (0, 0)),
                  pl.BlockSpec((1, dv), lambda b, c: (0, 0))],
        out_specs=[row(W),
                   pl.BlockSpec((None, H, dk, dv), lambda b, c: (b, 0, 0, 0)),
                   pl.BlockSpec((None, CONV - 1, 3 * W), lambda b, c: (b, 0, 0))],
        out_shape=[jax.ShapeDtypeStruct((T, W), bf16),
                   jax.ShapeDtypeStruct((B, H, dk, dv), f32),
                   jax.ShapeDtypeStruct((B, CONV - 1, 3 * W), f32)],
        scratch_shapes=[pltpu.VMEM((GDN_HALO + R, 3 * W), f32), pltpu.VMEM((H, dk, dv), f32)],
        compiler_params=_params(("parallel", "arbitrary"), est),
        name="gdn_core",
    )(qkv, ab, z, conv_state, S0, conv_w, pad(a_log), pad(dt_bias), norm_g.reshape(1, dv))


def kernel(x_prompt, x_sample, mem_prompt, cache_a_k, cache_a_v, cache_b_k, cache_b_v, cache_b_logf, state_gdn, state_gdn_conv, cache_mem_k, cache_mem_v, norm_g, mem_norm_g, final_norm_g, ffn_w_gate, ffn_w_up, ffn_w_down, xa_w_q, xa_w_k, xa_w_v, xa_w_o, ab_w_in, ab_b_f, ab_rel_bias, ab_w_o, gdn_w_in, gdn_conv_w, gdn_a_log, gdn_dt_bias, gdn_norm_g, gdn_w_o):
    depth = norm_g.shape[0]
    BP, SEQ, D = x_prompt.shape
    BS, MS, _ = x_sample.shape
    assert BP == 1
    HA = HB = ab_b_f.shape[1]
    WA = WB = (ab_w_in.shape[2] - HB) // 6
    DHA = WA // HA
    HG = gdn_a_log.shape[1]
    WG = gdn_w_o.shape[1]
    DKG = WG // HG
    HX = cache_mem_k.shape[3]
    DHX = cache_mem_k.shape[4]
    WX = HX * DHX
    PA = cache_a_k.shape[2]
    PB = cache_b_k.shape[2]
    LANES = V7X_LANES

    xp = x_prompt.reshape(SEQ, D)
    xs = x_sample.reshape(BS * MS, D)

    wgate = ffn_w_gate.astype(bf16)
    wup = ffn_w_up.astype(bf16)
    wdown = ffn_w_down.astype(bf16)
    wxq = xa_w_q.astype(bf16)
    wxo = xa_w_o.astype(bf16)
    wxkv = jnp.concatenate([xa_w_k, xa_w_v], axis=-1).astype(bf16)

    outs = {k: [] for k in ("akp", "avp", "bkp", "bvp", "blp", "sgp", "scp", "mkp", "mvp",
                            "aks", "avs", "bks", "bvs", "bls", "sgs", "scs")}

    for l in range(depth):
        xp = _ffn(xp, norm_g[l, 0], wgate[l, 0], wup[l, 0], wdown[l, 0])
        xs = _ffn(xs, norm_g[l, 0], wgate[l, 0], wup[l, 0], wdown[l, 0])

        if l % 2 == 0:
            e = l // 2
            w_in = ab_w_in[e]
            nq = 3 * WA + 3 * WB
            w_pad = jnp.concatenate([w_in, jnp.zeros((D, LANES - HB), f32)], axis=-1).astype(bf16)
            b_pad = jnp.zeros((1, LANES), f32).at[0, :HB].set(ab_b_f[e].astype(f32))
            spec = [(0, WA, bf16, DHA ** -0.5, None),
                    (WA, WA, f32, 1.0, None), (WA, WA, bf16, 1.0, None),
                    (2 * WA, WA, f32, 1.0, None), (2 * WA, WA, bf16, 1.0, None),
                    (3 * WA, WB, bf16, (WB // HB) ** -0.5, None),
                    (3 * WA + WB, WB, f32, 1.0, None), (3 * WA + WB, WB, bf16, 1.0, None),
                    (3 * WA + 2 * WB, WB, f32, 1.0, None), (3 * WA + 2 * WB, WB, bf16, 1.0, None),
                    (nq, LANES, f32, 1.0, "logsig")]
            wo = ab_w_o[e].astype(bf16)

            spec_a = [(0, WA, bf16, DHA ** -0.5 * LOG2E, None)] + spec[1:5]
            qa, ka, kab, va, vab = _norm_proj(xp, norm_g[l, 1], w_pad[:, :3 * WA], spec_a)
            oa = _band_prompt(qa, kab, vab, _band_bias(ab_rel_bias[e]))
            DHB = WB // HB
            spread = lambda w: jnp.pad(w.reshape(D, HB, DHB), ((0, 0), (0, 0), (0, LANES - DHB))).reshape(D, HB * LANES)
            wq_b = w_in[:, 3 * WA:3 * WA + WB]
            wk_b = w_in[:, 3 * WA + WB:3 * WA + 2 * WB]
            w_b = jnp.concatenate([spread(wq_b), wk_b, spread(wk_b), w_in[:, 3 * WA + 2 * WB:nq],
                                   w_pad[:, nq:]], axis=-1).astype(bf16)
            W2 = HB * LANES
            spec_b = [(0, W2, bf16, DHB ** -0.5 * LOG2E, None),
                      (W2, WB, f32, 1.0, None), (W2 + WB, W2, bf16, 1.0, None),
                      (2 * W2 + WB, WB, f32, 1.0, None), (2 * W2 + WB, WB, bf16, 1.0, None),
                      (2 * W2 + 2 * WB, LANES, f32, 1.0, "logsig")]
            qbs, kb, kbs, vb, vbb, lf = _norm_proj(xp, norm_g[l, 1], w_b, spec_b, bias=b_pad)
            F, _ = _cumsum_rows(lf)
            qaug, kaug, vt = _fox_pack(qbs, kbs, vbb, F, HB)
            ob = _fox_prompt(qaug, kaug, vt, F, HB)
            res_p = [(oa, wo[:WA]), (ob, wo[WA:])]
            keep = min(BAND_CHUNKS * CHUNK, SEQ)
            outs["akp"].append(ka[SEQ - keep:].reshape(1, keep, HA, DHA))
            outs["avp"].append(va[SEQ - keep:].reshape(1, keep, HA, DHA))
            outs["bkp"].append(kb.reshape(1, SEQ, HB, WB // HB))
            outs["bvp"].append(vb.reshape(1, SEQ, HB, WB // HB))
            outs["blp"].append(lf[:, :HB].reshape(1, SEQ, HB))

            qa, ka, _, va, _, qb, kb, _, vb, _, lf = _norm_proj(xs, norm_g[l, 1], w_pad, spec, bias=b_pad)
            bias_s = _toeplitz_bias(ab_rel_bias[e], MS, PA + MS, PA).reshape(HA * MS, PA + MS)
            oa = _sample_attn(qa.reshape(BS, MS, WA), cache_a_k[e].reshape(BS, PA, WA), cache_a_v[e].reshape(BS, PA, WA),
                              HA, "rel", ka.reshape(BS, MS, WA), va.reshape(BS, MS, WA),
                              extra=(bias_s[:, :PA], bias_s[:, PA:]))
            lfn = lf[:, :HB].reshape(BS, MS, HB)
            lcat = jnp.concatenate([cache_b_logf[e].astype(f32), lfn], axis=1)
            LP = -(-(PB + MS) // LANES) * LANES
            lcat = jnp.pad(lcat.transpose(1, 0, 2).reshape(PB + MS, BS * HB), ((0, LP - PB - MS), (0, 0)))
            _, FTs = _cumsum_rows(lcat)
            FTs = FTs.reshape(BS, HB, LP)
            fq = FTs[:, :, PB:PB + MS].reshape(BS, HB * MS, 1)
            ob = _sample_attn(qb.reshape(BS, MS, WB), cache_b_k[e].reshape(BS, PB, WB), cache_b_v[e].reshape(BS, PB, WB),
                              HB, "fox", kb.reshape(BS, MS, WB), vb.reshape(BS, MS, WB),
                              extra=(fq, FTs[:, :, :PB], FTs[:, :, PB:PB + MS]))
            res_s = [(oa.reshape(BS * MS, WA), wo[:WA]), (ob.reshape(BS * MS, WB), wo[WA:])]
            outs["aks"].append(ka.reshape(BS, MS, HA, DHA))
            outs["avs"].append(va.reshape(BS, MS, HA, DHA))
            outs["bks"].append(kb.reshape(BS, MS, HB, WB // HB))
            outs["bvs"].append(vb.reshape(BS, MS, HB, WB // HB))
            outs["bls"].append(lfn)
        else:
            o = l // 2
            w_in = gdn_w_in[o]
            w_pad = jnp.concatenate([w_in, jnp.zeros((D, LANES - 2 * HG), f32)], axis=-1).astype(bf16)
            spec = [(0, 3 * WG, f32, 1.0, None), (3 * WG, WG, f32, 1.0, None), (4 * WG, LANES, f32, 1.0, None)]
            wo = gdn_w_o[o].astype(bf16)

            qg, kg, vg, z, ab, cvp = _gdn_proj(xp, norm_g[l, 1], w_pad, jnp.zeros((1, CONV - 1, 3 * WG), f32),
                                               gdn_conv_w[o], HG)
            y, Sp = _gdn_core_pre(qg, kg, vg, ab, z, jnp.zeros((1, HG, DKG, DKG), f32),
                                  gdn_a_log[o], gdn_dt_bias[o], gdn_norm_g[o], CHUNK, 8)
            res_p = [(y, wo)]
            outs["sgp"].append(Sp)
            outs["scp"].append(cvp)

            qkv, z, ab = _norm_proj(xs, norm_g[l, 1], w_pad, spec)
            y, Ss, cvs = _gdn_core(qkv, ab, z, state_gdn_conv[o].astype(f32), state_gdn[o].astype(f32),
                                   gdn_conv_w[o], gdn_a_log[o], gdn_dt_bias[o], gdn_norm_g[o], MS, 1)
            res_s = [(y, wo)]
            outs["sgs"].append(Ss)
            outs["scs"].append(cvs)

        mk, mkb, mv, mvb = _norm_proj(mem_prompt.reshape(MEM, D), mem_norm_g[l], wxkv[l],
                                      [(0, WX, f32, 1.0, None), (0, WX, bf16, 1.0, None),
                                       (WX, WX, f32, 1.0, None), (WX, WX, bf16, 1.0, None)])
        outs["mkp"].append(mk.reshape(1, MEM, HX, DHX))
        outs["mvp"].append(mv.reshape(1, MEM, HX, DHX))
        qspec = [(0, WX, bf16, DHX ** -0.5, None)]
        xp, ocp = _cross_prompt(xp, norm_g[l, 2], wxq[l], res_p, mkb, mvb, HX)
        xs, q = _norm_proj(xs, norm_g[l, 2], wxq[l], qspec, res=res_s)
        ocs = _cross_sample(q.reshape(BS, MS, WX), cache_mem_k, cache_mem_v, l)

        fin = final_norm_g if l == depth - 1 else None
        xp = _ffn(xp, norm_g[l, 3], wgate[l, 1], wup[l, 1], wdown[l, 1], fin, res=(ocp, wxo[l]))
        xs = _ffn(xs, norm_g[l, 3], wgate[l, 1], wup[l, 1], wdown[l, 1], fin, res=(ocs.reshape(BS * MS, WX), wxo[l]))

    st = lambda k: jnp.stack(outs[k])
    return (xp.reshape(BP, SEQ, D), xs.reshape(BS, MS, D),
            st("akp"), st("avp"), st("bkp"), st("bvp"), st("blp"),
            st("sgp"), st("scp"), st("mkp"), st("mvp"),
            st("aks"), st("avs"), st("bks"), st("bvs"), st("bls"),
            st("sgs"), st("scs"))
```
